```python
import math
import jax
import jax.numpy as jnp
from jax import lax
import numpy as np

D_MODEL = 1024
BATCH = 8
SEQ = 4096
DEPTH = 1

D_MIX = 2 * D_MODEL
ML_HEADS = 4
ML_DV = D_MODEL // ML_HEADS
ML_DQK = ML_DV // 2
ML_QK = ML_HEADS * ML_DQK
ML_WIDTH = ML_HEADS * ML_DV
SSM_WIDTH = D_MIX - ML_WIDTH
SSM_HEADDIM = 64
SSM_HEADS = SSM_WIDTH // SSM_HEADDIM
SSM_GROUPS = 4
SSM_STATE = 128
SSM_CONV_DIM = SSM_WIDTH + 2 * SSM_GROUPS * SSM_STATE
CONV_K = 5
CHUNK = 128
N_EXPERTS = 64
TOP_K = 8
N_EXPERT_GROUPS = 8
TOPK_GROUPS = 4
D_EXPERT = D_MODEL // 4
ROUTED_SCALE = 2.5
MOE_BLOCK = 512
LN_EPS = 1e-5
RMS_EPS = 1e-5
ALPHA = (2 * DEPTH) ** 0.25
BETA = (8 * DEPTH) ** -0.25
IN_SIZES = (ML_QK, ML_QK, ML_WIDTH, ML_WIDTH, 2 * ML_HEADS, 2 * ML_HEADS, SSM_WIDTH, SSM_CONV_DIM, 2 * SSM_HEADS)
IN_SCALES = (D_MODEL ** -0.5,) * 8 + (0.1 * D_MODEL ** -0.5,)
IN_SPLITS = tuple(int(s) for s in np.cumsum(IN_SIZES)[:-1])
D_IN_PROJ = sum(IN_SIZES)

kernel_name = 'hymba_mlstm_mamba2_moe_encoder_block'


def _layer_norm(x, g, b):
    xf = x.astype(jnp.float32)
    mu = xf.mean(-1, keepdims=True)
    var = jnp.square(xf - mu).mean(-1, keepdims=True)
    return ((xf - mu) * lax.rsqrt(var + LN_EPS) * g + b).astype(x.dtype)


def _mlstm_chunkwise(q, k, v, i_pre, f_pre):
    Bb, H, S, Dk = q.shape
    Dv = v.shape[-1]
    nc = S // CHUNK
    q = q.reshape(Bb, H, nc, CHUNK, Dk)
    k = k.reshape(Bb, H, nc, CHUNK, Dk)
    v = v.reshape(Bb, H, nc, CHUNK, Dv)
    ig = i_pre.reshape(Bb, H, nc, CHUNK)
    b = jnp.cumsum(jax.nn.log_sigmoid(f_pre).reshape(Bb, H, nc, CHUNK), axis=-1)
    g = b[..., -1]
    a = g[..., None] - b + ig
    m_loc = a.max(-1)
    kw = k * jnp.exp(a - m_loc[..., None])[..., None]
    U = jnp.einsum('bhcsk,bhcsv->bhckv', kw, v)
    u = kw.sum(3)

    def step(carry, inp):
        C, n, m = carry
        U_c, u_c, g_c, ml_c = inp
        m_new = jnp.maximum(g_c + m, ml_c)
        decay = jnp.exp(g_c + m - m_new)
        scale = jnp.exp(ml_c - m_new)
        C_new = decay[..., None, None] * C + scale[..., None, None] * U_c
        n_new = decay[..., None] * n + scale[..., None] * u_c
        return (C_new, n_new, m_new), (C, n, m)

    init = (jnp.zeros((Bb, H, Dk, Dv), jnp.float32), jnp.zeros((Bb, H, Dk), jnp.float32),
            jnp.full((Bb, H), -jnp.inf, jnp.float32))
    mv = lambda t: jnp.moveaxis(t, 2, 0)
    _, (Cp, n_p, m_p) = lax.scan(step, init, (mv(U), mv(u), mv(g), mv(m_loc)))
    Cp = jnp.moveaxis(Cp, 0, 2)
    n_p = jnp.moveaxis(n_p, 0, 2)
    m_p = jnp.moveaxis(m_p, 0, 2)

    tril = jnp.tril(jnp.ones((CHUNK, CHUNK), bool))
    dmat = jnp.where(tril, b[..., :, None] - b[..., None, :] + ig[..., None, :], -jnp.inf)
    m_inter = b + m_p[..., None]
    m_t = jnp.maximum(m_inter, dmat.max(-1))
    s_mat = jnp.einsum('bhctk,bhcsk->bhcts', q, k) * jnp.exp(dmat - m_t[..., None])
    inter_w = jnp.exp(m_inter - m_t)
    num = jnp.einsum('bhcts,bhcsv->bhctv', s_mat, v) + inter_w[..., None] * jnp.einsum('bhctk,bhckv->bhctv', q, Cp)
    den = s_mat.sum(-1) + inter_w * jnp.einsum('bhctk,bhck->bhct', q, n_p)
    h = num / jnp.maximum(jnp.abs(den), jnp.exp(-m_t))[..., None]
    return h.reshape(Bb, H, S, Dv)


def _mlstm_group(q, k, v, o, i_pre, f_pre, i_bias, f_bias, norm_w):
    Bb, S, _ = q.shape
    H = ML_HEADS
    heads = lambda t, d: t.reshape(Bb, S, H, d).transpose(0, 2, 1, 3).astype(jnp.float32)
    qh = heads(q, ML_DQK)
    kh = heads(k, ML_DQK) * (ML_DQK ** -0.5)
    vh = heads(v, ML_DV)
    ig = (i_pre.astype(jnp.float32) + i_bias.astype(jnp.float32)).transpose(0, 2, 1)
    fg = (f_pre.astype(jnp.float32) + f_bias.astype(jnp.float32)).transpose(0, 2, 1)
    flip = lambda t: jnp.flip(t, axis=2)
    h_fwd = _mlstm_chunkwise(qh, kh, vh, ig[:, :H], fg[:, :H])
    h_bwd = flip(_mlstm_chunkwise(flip(qh), flip(kh), flip(vh), flip(ig[:, H:]), flip(fg[:, H:])))
    h = h_fwd + h_bwd
    mu = h.mean(-1, keepdims=True)
    var = jnp.square(h - mu).mean(-1, keepdims=True)
    h = (h - mu) * lax.rsqrt(var + LN_EPS) * norm_w.astype(jnp.float32).reshape(H, 1, ML_DV)
    h = h.transpose(0, 2, 1, 3).reshape(Bb, S, ML_WIDTH)
    return (jax.nn.sigmoid(o.astype(jnp.float32)) * h).astype(q.dtype)


def _ssd_chunked(x, dt, A, Bm, Cm):
    Bb, S, H, P = x.shape
    G, N = Bm.shape[2], Bm.shape[3]
    hg = H // G
    nc = S // CHUNK
    x = x.reshape(Bb, nc, CHUNK, G, hg, P)
    dt = dt.reshape(Bb, nc, CHUNK, G, hg)
    Bm = Bm.reshape(Bb, nc, CHUNK, G, N)
    Cm = Cm.reshape(Bb, nc, CHUNK, G, N)
    acum = jnp.cumsum(dt * A.reshape(G, hg), axis=2)
    xdt = x * dt[..., None]
    tril = jnp.tril(jnp.ones((CHUNK, CHUNK), bool))
    seg = acum[:, :, :, None] - acum[:, :, None, :]
    decay_ts = jnp.exp(jnp.where(tril[:, :, None, None], seg, -jnp.inf))
    cb = jnp.einsum('bctgn,bcsgn->bctsg', Cm, Bm)
    y_diag = jnp.einsum('bctsgh,bcsghp->bctghp', cb[..., None] * decay_ts, xdt)
    decay_s = jnp.exp(acum[:, :, -1:] - acum)
    states = jnp.einsum('bcsgn,bcsghp->bcghpn', Bm, xdt * decay_s[..., None])
    chunk_decay = jnp.exp(acum[:, :, -1])

    def step(h, inp):
        st, dc = inp
        return dc[..., None, None] * h + st, h

    _, h_prev = lax.scan(step, jnp.zeros((Bb, G, hg, P, N), jnp.float32),
                         (jnp.moveaxis(states, 1, 0), jnp.moveaxis(chunk_decay, 1, 0)))
    h_prev = jnp.moveaxis(h_prev, 0, 1)
    y_off = jnp.einsum('bctgn,bcghpn->bctghp', Cm, h_prev) * jnp.exp(acum)[..., None]
    return (y_diag + y_off).reshape(Bb, S, H, P)


def _mamba2_group(z, xbc, dt_pre, conv_w, conv_b, dt_bias, a_log, d_skip, norm_w):
    Bb, S, _ = z.shape
    H = SSM_HEADS
    xbc = lax.conv_general_dilated(xbc, conv_w[:, None, :], window_strides=(1,),
                                   padding=[(CONV_K // 2, CONV_K // 2)],
                                   dimension_numbers=('NWC', 'WIO', 'NWC'),
                                   feature_group_count=SSM_CONV_DIM) + conv_b
    xbc = jax.nn.silu(xbc)
    xs, Bm, Cm = jnp.split(xbc, [SSM_WIDTH, SSM_WIDTH + SSM_GROUPS * SSM_STATE], axis=-1)
    xs = xs.reshape(Bb, S, H, SSM_HEADDIM).astype(jnp.float32)
    Bm = Bm.reshape(Bb, S, SSM_GROUPS, SSM_STATE).astype(jnp.float32)
    Cm = Cm.reshape(Bb, S, SSM_GROUPS, SSM_STATE).astype(jnp.float32)
    dt = jax.nn.softplus(dt_pre.astype(jnp.float32) + dt_bias.astype(jnp.float32))
    A = -jnp.exp(a_log.astype(jnp.float32))
    fl = lambda t: jnp.flip(t, axis=1)
    y_f = _ssd_chunked(xs, dt[..., :H], A[:H], Bm, Cm)
    y_b = fl(_ssd_chunked(fl(xs), fl(dt[..., H:]), A[H:], fl(Bm), fl(Cm)))
    y = y_f + y_b + d_skip.astype(jnp.float32)[:, None] * xs
    y = y.reshape(Bb, S, SSM_WIDTH) * jax.nn.silu(z.astype(jnp.float32))
    yg = y.reshape(Bb, S, SSM_GROUPS, SSM_WIDTH // SSM_GROUPS)
    yg = yg * lax.rsqrt(jnp.mean(yg * yg, -1, keepdims=True) + RMS_EPS)
    return (yg.reshape(Bb, S, SSM_WIDTH) * norm_w.astype(jnp.float32)).astype(z.dtype)


def _moe(x2d, w_router, router_bias, w_gate, w_up, w_down, ws_gate, ws_up, ws_down):
    T, D = x2d.shape
    E = N_EXPERTS
    scores = jax.nn.sigmoid((x2d @ w_router).astype(jnp.float32))
    choice = scores + router_bias.astype(jnp.float32)
    grp_score = lax.top_k(choice.reshape(T, N_EXPERT_GROUPS, E // N_EXPERT_GROUPS), 2)[0].sum(-1)
    _, top_groups = lax.top_k(grp_score, TOPK_GROUPS)
    gmask = jnp.any(top_groups[..., None] == jnp.arange(N_EXPERT_GROUPS), axis=1)
    emask = jnp.repeat(gmask, E // N_EXPERT_GROUPS, axis=1)
    _, idx = lax.top_k(jnp.where(emask, choice, -jnp.inf), TOP_K)
    wsel = jnp.take_along_axis(scores, idx, axis=1)
    wsel = wsel / wsel.sum(-1, keepdims=True) * ROUTED_SCALE

    M = T * TOP_K
    e_flat = idx.reshape(M)
    tok_flat = jnp.arange(M, dtype=jnp.int32) // TOP_K
    g_flat = wsel.reshape(M)
    order = jnp.argsort(e_flat)
    e_sorted = e_flat[order]
    counts = jnp.bincount(e_flat, length=E)
    starts = jnp.cumsum(counts) - counts
    pcounts = (counts + MOE_BLOCK - 1) // MOE_BLOCK * MOE_BLOCK
    pends = jnp.cumsum(pcounts)
    pstarts = pends - pcounts
    dest = pstarts[e_sorted] + (jnp.arange(M, dtype=jnp.int32) - starts[e_sorted])
    nb = (M + E * (MOE_BLOCK - 1)) // MOE_BLOCK
    slot_tok = jnp.zeros((nb * MOE_BLOCK,), jnp.int32).at[dest].set(tok_flat[order])
    slot_gate = jnp.zeros((nb * MOE_BLOCK,), jnp.float32).at[dest].set(g_flat[order])
    block_expert = jnp.minimum(jnp.searchsorted(pends, jnp.arange(nb) * MOE_BLOCK, side='right'), E - 1)

    def body(y, blk):
        tok, gate, e = blk
        xb = x2d[tok]
        h = jax.nn.silu(xb @ w_gate[e]) * (xb @ w_up[e])
        out = (h @ w_down[e]) * gate[:, None].astype(x2d.dtype)
        return y.at[tok].add(out), None

    routed, _ = lax.scan(body, jnp.zeros_like(x2d),
                         (slot_tok.reshape(nb, MOE_BLOCK), slot_gate.reshape(nb, MOE_BLOCK), block_expert))
    shared = (jax.nn.silu(x2d @ ws_gate) * (x2d @ ws_up)) @ ws_down
    return routed + shared


def _layer(x, w_in, ml_i_bias, ml_f_bias, ml_norm_w, conv_w, conv_b, dt_bias, a_log, d_skip, ssm_norm_w,
           w_out, ln1_g, ln1_b, w_router, router_bias, w_exp_gate, w_exp_up, w_exp_down,
           w_sh_gate, w_sh_up, w_sh_down, ln2_g, ln2_b):
    Bb, S, D = x.shape
    proj = jnp.einsum('bsd,de->bse', x, w_in)
    q, k, v, o, ig, fg, z, xbc, dt = jnp.split(proj, IN_SPLITS, axis=-1)
    y_ml = _mlstm_group(q, k, v, o, ig, fg, ml_i_bias, ml_f_bias, ml_norm_w)
    y_ssm = _mamba2_group(z, xbc, dt, conv_w, conv_b, dt_bias, a_log, d_skip, ssm_norm_w)
    mix = jnp.einsum('bse,ed->bsd', jnp.concatenate([y_ml, y_ssm], axis=-1), w_out)
    h = _layer_norm(ALPHA * x + mix, ln1_g, ln1_b)
    ffn = _moe(h.reshape(Bb * S, D), w_router, router_bias, w_exp_gate, w_exp_up, w_exp_down,
               w_sh_gate, w_sh_up, w_sh_down).reshape(Bb, S, D)
    return _layer_norm(ALPHA * h + ffn, ln2_g, ln2_b)


def setup_inputs(seed: int = 0) -> dict:
    key = jax.random.key(seed)
    ks = jax.random.split(key, 32)
    f32 = jnp.float32
    L = DEPTH
    nrm = lambda kk, shape, scale: jax.random.normal(kk, shape, f32) * scale
    x = nrm(ks[0], (BATCH, SEQ, D_MODEL), 1.0)
    wk = jax.random.split(ks[1], len(IN_SIZES))
    w_in = jnp.concatenate([nrm(wk[j], (L, D_MODEL, IN_SIZES[j]), IN_SCALES[j]) for j in range(len(IN_SIZES))], axis=-1)
    ml_i_bias = nrm(ks[2], (L, 2 * ML_HEADS), 0.1)
    ml_f_bias = 3.0 + 3.0 * jax.random.uniform(ks[3], (L, 2 * ML_HEADS), f32)
    ml_norm_w = 1.0 + nrm(ks[4], (L, ML_WIDTH), 0.02)
    conv_w = nrm(ks[5], (L, CONV_K, SSM_CONV_DIM), CONV_K ** -0.5)
    conv_b = nrm(ks[6], (L, SSM_CONV_DIM), 0.02)
    dt0 = jnp.exp(jax.random.uniform(ks[7], (L, 2 * SSM_HEADS), f32, math.log(1e-3), math.log(1e-1)))
    dt_bias = dt0 + jnp.log(-jnp.expm1(-dt0))
    a_log = jnp.log(jax.random.uniform(ks[8], (L, 2 * SSM_HEADS), f32, 1.0, 16.0))
    d_skip = 1.0 + nrm(ks[9], (L, SSM_HEADS), 0.02)
    ssm_norm_w = 1.0 + nrm(ks[10], (L, SSM_WIDTH), 0.02)
    w_out = nrm(ks[11], (L, D_MIX, D_MODEL), D_MIX ** -0.5 * BETA)
    ln1_g = 1.0 + nrm(ks[12], (L, D_MODEL), 0.02)
    ln1_b = nrm(ks[13], (L, D_MODEL), 0.02)
    w_router = nrm(ks[14], (L, D_MODEL, N_EXPERTS), D_MODEL ** -0.5)
    router_bias = nrm(ks[15], (L, N_EXPERTS), 0.01)
    w_exp_gate = nrm(ks[16], (L, N_EXPERTS, D_MODEL, D_EXPERT), D_MODEL ** -0.5)
    w_exp_up = nrm(ks[17], (L, N_EXPERTS, D_MODEL, D_EXPERT), D_MODEL ** -0.5)
    w_exp_down = nrm(ks[18], (L, N_EXPERTS, D_EXPERT, D_MODEL), D_EXPERT ** -0.5 * BETA)
    w_sh_gate = nrm(ks[19], (L, D_MODEL, D_EXPERT), D_MODEL ** -0.5)
    w_sh_up = nrm(ks[20], (L, D_MODEL, D_EXPERT), D_MODEL ** -0.5)
    w_sh_down = nrm(ks[21], (L, D_EXPERT, D_MODEL), D_EXPERT ** -0.5 * BETA)
    ln2_g = 1.0 + nrm(ks[22], (L, D_MODEL), 0.02)
    ln2_b = nrm(ks[23], (L, D_MODEL), 0.02)
    return {'x': x, 'w_in': w_in, 'ml_i_bias': ml_i_bias, 'ml_f_bias': ml_f_bias, 'ml_norm_w': ml_norm_w,
            'conv_w': conv_w, 'conv_b': conv_b, 'dt_bias': dt_bias, 'a_log': a_log, 'd_skip': d_skip,
            'ssm_norm_w': ssm_norm_w, 'w_out': w_out, 'ln1_g': ln1_g, 'ln1_b': ln1_b,
            'w_router': w_router, 'router_bias': router_bias, 'w_exp_gate': w_exp_gate,
            'w_exp_up': w_exp_up, 'w_exp_down': w_exp_down, 'w_sh_gate': w_sh_gate,
            'w_sh_up': w_sh_up, 'w_sh_down': w_sh_down, 'ln2_g': ln2_g, 'ln2_b': ln2_b}


def reference(x, w_in, ml_i_bias, ml_f_bias, ml_norm_w, conv_w, conv_b, dt_bias, a_log, d_skip,
              ssm_norm_w, w_out, ln1_g, ln1_b, w_router, router_bias, w_exp_gate, w_exp_up, w_exp_down,
              w_sh_gate, w_sh_up, w_sh_down, ln2_g, ln2_b):
    for l in range(DEPTH):
        x = _layer(x, w_in[l], ml_i_bias[l], ml_f_bias[l], ml_norm_w[l], conv_w[l], conv_b[l], dt_bias[l],
                   a_log[l], d_skip[l], ssm_norm_w[l], w_out[l], ln1_g[l], ln1_b[l], w_router[l],
                   router_bias[l], w_exp_gate[l], w_exp_up[l], w_exp_down[l], w_sh_gate[l], w_sh_up[l],
                   w_sh_down[l], ln2_g[l], ln2_b[l])
    return x
```

```python
import functools

import jax
import jax.numpy as jnp
from jax import lax
from jax.experimental import pallas as pl
from jax.experimental.pallas import tpu as pltpu

D_MODEL = 1024
ML_HEADS = 4
ML_DV = 256
ML_DQK = 128
ML_QK = ML_HEADS * ML_DQK
ML_WIDTH = ML_HEADS * ML_DV
SSM_WIDTH = 1024
SSM_HEADDIM = 64
SSM_HEADS = 16
SSM_GROUPS = 4
SSM_STATE = 128
SSM_CONV_DIM = SSM_WIDTH + 2 * SSM_GROUPS * SSM_STATE
CONV_K = 5
CHUNK = 128
N_EXPERTS = 64
TOP_K = 8
N_EXPERT_GROUPS = 8
TOPK_GROUPS = 4
D_EXPERT = 256
ROUTED_SCALE = 2.5
LN_EPS = 1e-5
RMS_EPS = 1e-5
DEPTH = 1
ALPHA = (2 * DEPTH) ** 0.25

PA_COLS = 2 * ML_QK + 2 * ML_WIDTH + SSM_WIDTH + SSM_CONV_DIM
PB_COLS = 128
N_GATE_ROWS = 2 * ML_HEADS + 2 * ML_HEADS + 2 * SSM_HEADS
N_PREP_ROWS = 80

MOE_BLOCK = 512
GATHER_ROWS = 512
VMEM_LIMIT = 56 * 1024 * 1024

BF16 = jnp.bfloat16
F32 = jnp.float32
NEG_INF = float("-inf")


def _cparams(sem, vmem=VMEM_LIMIT):
    return pltpu.CompilerParams(dimension_semantics=sem, vmem_limit_bytes=vmem)


def _dot(a, b):
    return jnp.dot(a, b, preferred_element_type=F32)


def _dot_nt(a, b):
    return lax.dot_general(a, b, (((1,), (1,)), ((), ())), preferred_element_type=F32)


def _dot_tn(a, b):
    return lax.dot_general(a, b, (((0,), (0,)), ((), ())), preferred_element_type=F32)


def _sigmoid(x):
    return 1.0 / (1.0 + jnp.exp(-x))


def _softplus(x):
    return jnp.maximum(x, 0.0) + jnp.log1p(jnp.exp(-jnp.abs(x)))


def _proj_kernel(x_ref, wb_ref, ws_ref, oa_ref, ob_ref, xb_scr):
    @pl.when(pl.program_id(1) == 0)
    def _():
        xb = x_ref[...].astype(BF16)
        xb_scr[...] = xb
        ob_ref[...] = _dot(xb, ws_ref[...])

    oa_ref[...] = _dot(xb_scr[...], wb_ref[...]).astype(oa_ref.dtype)


def _proj(x2d, w_big, w_small, tm=1024, tn=1536):
    T, D = x2d.shape
    N = w_big.shape[1]
    tm = min(tm, T)
    return pl.pallas_call(
        _proj_kernel,
        grid=(T // tm, N // tn),
        in_specs=[
            pl.BlockSpec((tm, D), lambda i, j: (i, 0)),
            pl.BlockSpec((D, tn), lambda i, j: (0, j)),
            pl.BlockSpec((D, PB_COLS), lambda i, j: (0, 0)),
        ],
        out_specs=[
            pl.BlockSpec((tm, tn), lambda i, j: (i, j)),
            pl.BlockSpec((tm, PB_COLS), lambda i, j: (i, 0)),
        ],
        out_shape=[jax.ShapeDtypeStruct((T, N), BF16), jax.ShapeDtypeStruct((T, PB_COLS), F32)],
        scratch_shapes=[pltpu.VMEM((tm, D), BF16)],
        compiler_params=_cparams(("parallel", "arbitrary")),
        name="in_proj",
    )(x2d, w_big, w_small)


def _split3_dot(x, u):
    hi = x.astype(BF16)
    r1 = x - hi.astype(F32)
    mid = r1.astype(BF16)
    lo = (r1 - mid.astype(F32)).astype(BF16)
    return _dot(hi, u) + _dot(mid, u) + _dot(lo, u)


def _gate_kernel(g_ref, bias_ref, aneg_ref, o_ref, *, nc, L):
    g = g_ref[0] + bias_ref[...]
    nh = ML_HEADS
    ig = g[0:2 * nh]
    fpre = g[2 * nh:4 * nh]
    lf = -_softplus(-fpre)
    dt = _softplus(g[4 * nh:])
    dA = dt * aneg_ref[...]
    r = lax.broadcasted_iota(jnp.int32, (L, L), 0)
    c = lax.broadcasted_iota(jnp.int32, (L, L), 1)
    u_pre = (r <= c).astype(BF16)
    u_suf = (r >= c).astype(BF16)
    nf = nh + SSM_HEADS
    xf = jnp.concatenate([lf[0:nh], dA[0:SSM_HEADS]], axis=0).reshape(nf * nc, L)
    xb = jnp.concatenate([lf[nh:], dA[SSM_HEADS:]], axis=0).reshape(nf * nc, L)
    cf = _split3_dot(xf, u_pre).reshape(nf, nc, L)
    cb = _split3_dot(xb, u_suf).reshape(nf, nc, L)
    o_ref[0, 0:nh] = cf[0:nh]
    o_ref[0, nh:2 * nh] = cb[0:nh]
    o_ref[0, 2 * nh:4 * nh] = ig
    o_ref[0, 16:32] = cf[nh:]
    o_ref[0, 32:48] = cb[nh:]
    o_ref[0, 48:80] = dt


def _gate_prep(g_rows, bias_rows, aneg_rows):
    B, R, nc, L = g_rows.shape
    return pl.pallas_call(
        functools.partial(_gate_kernel, nc=nc, L=L),
        grid=(B,),
        in_specs=[
            pl.BlockSpec((1, R, nc, L), lambda b: (b, 0, 0, 0)),
            pl.BlockSpec((R, 1, L), lambda b: (0, 0, 0)),
            pl.BlockSpec((2 * SSM_HEADS, 1, L), lambda b: (0, 0, 0)),
        ],
        out_specs=pl.BlockSpec((1, N_PREP_ROWS, nc, L), lambda b: (b, 0, 0, 0)),
        out_shape=jax.ShapeDtypeStruct((B, N_PREP_ROWS, nc, L), F32),
        compiler_params=_cparams(("parallel",)),
        name="gate_prep",
    )(g_rows, bias_rows, aneg_rows)


def _mlstm_kernel(q_ref, k_ref, v_ref, o_ref, mg_ref, nw_ref, y_ref, hf_scr, c_scr, n_scr, m_scr, *, nc, L):
    scale = ML_DQK ** -0.5
    row_i = lax.broadcasted_iota(jnp.int32, (L, L), 0)
    col_i = lax.broadcasted_iota(jnp.int32, (L, L), 1)

    def reset():
        c_scr[...] = jnp.zeros_like(c_scr)
        n_scr[...] = jnp.zeros_like(n_scr)
        m_scr[...] = jnp.full_like(m_scr, NEG_INF)

    def chunk(c, fwd):
        sl = pl.ds(pl.multiple_of(c * L, L), L)
        q = q_ref[0, sl, :]
        kf = k_ref[0, sl, :].astype(F32) * scale
        v = v_ref[0, sl, :]
        g8 = mg_ref[0, 0, c]
        gt = g8.T
        if fwd:
            b_r, i_r, b_c, i_c = g8[0:1], g8[1:2], gt[:, 0:1], gt[:, 1:2]
            gtot = b_r[:, L - 1:L]
            mask = col_i <= row_i
        else:
            b_r, i_r, b_c, i_c = g8[2:3], g8[3:4], gt[:, 2:3], gt[:, 3:4]
            gtot = b_r[:, 0:1]
            mask = col_i >= row_i
        m_prev = m_scr[...]
        a_r = gtot - b_r + i_r
        m_loc = jnp.max(a_r, axis=1, keepdims=True)
        a_c = gtot - b_c + i_c
        kw = kf * jnp.exp(a_c - m_loc)
        u_mat = _dot_tn(kw.astype(BF16), v)
        u_vec = jnp.sum(kw, axis=0, keepdims=True)
        dmat = jnp.where(mask, b_c - b_r + i_r, NEG_INF)
        m_inter = b_c + m_prev
        m_t = jnp.maximum(m_inter, jnp.max(dmat, axis=1, keepdims=True))
        s_mat = _dot_nt(q, kf.astype(BF16)) * jnp.exp(dmat - m_t)
        inter_w = jnp.exp(m_inter - m_t)
        num = _dot(s_mat.astype(BF16), v) + inter_w * _dot(q, c_scr[...].astype(BF16))
        qn = jnp.sum(q.astype(F32) * n_scr[...], axis=1, keepdims=True)
        den = jnp.sum(s_mat, axis=1, keepdims=True) + inter_w * qn
        h = num / jnp.maximum(jnp.abs(den), jnp.exp(-m_t))
        m_new = jnp.maximum(gtot + m_prev, m_loc)
        decay = jnp.exp(gtot + m_prev - m_new)
        sc = jnp.exp(m_loc - m_new)
        c_scr[...] = decay * c_scr[...] + sc * u_mat
        n_scr[...] = decay * n_scr[...] + sc * u_vec
        m_scr[...] = m_new
        return sl, h

    reset()

    def fwd_body(c, carry):
        sl, h = chunk(c, True)
        hf_scr[sl, :] = h
        return carry

    lax.fori_loop(0, nc, fwd_body, 0)
    reset()

    def bwd_body(i, carry):
        sl, hb = chunk(nc - 1 - i, False)
        h = hf_scr[sl, :] + hb
        mu = jnp.mean(h, axis=1, keepdims=True)
        d = h - mu
        var = jnp.mean(d * d, axis=1, keepdims=True)
        y = d * lax.rsqrt(var + LN_EPS) * nw_ref[...]
        y = y * _sigmoid(o_ref[0, sl, :].astype(F32))
        y_ref[0, sl, :] = y.astype(y_ref.dtype)
        return carry

    lax.fori_loop(0, nc, bwd_body, 0)


def _mlstm(pa3, mg, norm_w):
    B, S, _ = pa3.shape
    nc = S // CHUNK
    H = ML_HEADS
    return pl.pallas_call(
        functools.partial(_mlstm_kernel, nc=nc, L=CHUNK),
        grid=(B, H),
        in_specs=[
            pl.BlockSpec((1, S, ML_DQK), lambda b, h: (b, 0, h)),
            pl.BlockSpec((1, S, ML_DQK), lambda b, h: (b, 0, H + h)),
            pl.BlockSpec((1, S, ML_DV), lambda b, h: (b, 0, H + h)),
            pl.BlockSpec((1, S, ML_DV), lambda b, h: (b, 0, 2 * H + h)),
            pl.BlockSpec((1, 1, nc, 8, CHUNK), lambda b, h: (b, h, 0, 0, 0)),
            pl.BlockSpec((1, ML_DV), lambda b, h: (0, h)),
        ],
        out_specs=pl.BlockSpec((1, S, ML_DV), lambda b, h: (b, 0, h)),
        out_shape=jax.ShapeDtypeStruct((B, S, ML_WIDTH), BF16),
        scratch_shapes=[
            pltpu.VMEM((S, ML_DV), F32),
            pltpu.VMEM((ML_DQK, ML_DV), F32),
            pltpu.VMEM((1, ML_DQK), F32),
            pltpu.VMEM((1, 1), F32),
        ],
        compiler_params=_cparams(("parallel", "parallel")),
        name="mlstm",
    )(pa3, pa3, pa3, pa3, mg, norm_w)


def _conv_kernel(x_ref, halo_ref, w_ref, b_ref, o_ref, ext_scr, *, Lc):
    pad = CONV_K // 2
    halo = halo_ref[0, 0].astype(F32)
    ext_scr[8 - pad:8, :] = halo[0:pad]
    ext_scr[8:8 + Lc, :] = x_ref[0].astype(F32)
    ext_scr[8 + Lc:8 + Lc + pad, :] = halo[pad:2 * pad]
    acc = b_ref[...] + w_ref[0:1, :] * ext_scr[8 - pad:8 - pad + Lc, :]
    for j in range(1, CONV_K):
        acc = acc + w_ref[j:j + 1, :] * ext_scr[8 - pad + j:8 - pad + j + Lc, :]
    o_ref[0] = (acc * _sigmoid(acc)).astype(o_ref.dtype)


def _conv(pa3, halo, conv_w8, conv_b, Lc=512):
    B, S, _ = pa3.shape
    C = SSM_CONV_DIM
    xbc_blk = (PA_COLS - C) // C
    return pl.pallas_call(
        functools.partial(_conv_kernel, Lc=Lc),
        grid=(B, S // Lc),
        in_specs=[
            pl.BlockSpec((1, Lc, C), lambda b, i: (b, i, xbc_blk)),
            pl.BlockSpec((1, 1, 8, C), lambda b, i: (b, i, 0, 0)),
            pl.BlockSpec((8, C), lambda b, i: (0, 0)),
            pl.BlockSpec((1, C), lambda b, i: (0, 0)),
        ],
        out_specs=pl.BlockSpec((1, Lc, C), lambda b, i: (b, i, 0)),
        out_shape=jax.ShapeDtypeStruct((B, S, C), BF16),
        scratch_shapes=[pltpu.VMEM((Lc + 16, C), F32)],
        compiler_params=_cparams(("parallel", "parallel")),
        name="ssm_conv",
    )(pa3, halo, conv_w8, conv_b)


def _lane_groups(cols, width):
    n = len(cols)
    rows = cols[0].shape[0]
    lane = lax.broadcasted_iota(jnp.int32, (rows, n * width), 1)
    out = jnp.broadcast_to(cols[n - 1], (rows, n * width))
    for j in range(n - 2, -1, -1):
        out = jnp.where(lane < (j + 1) * width, jnp.broadcast_to(cols[j], (rows, n * width)), out)
    return out


def _ssd_bwd_kernel(x_ref, b_ref, sg_ref, hb_ref, h_scr, *, L):
    G, hg, P, N = SSM_GROUPS, SSM_HEADS // SSM_GROUPS, SSM_HEADDIM, SSM_STATE
    H = SSM_HEADS

    @pl.when(pl.program_id(1) == 0)
    def _():
        h_scr[...] = jnp.zeros_like(h_scr)

    sg = sg_ref[0, 0]
    sgt = sg.T
    for g in range(G):
        wcols, dcols = [], []
        for j in range(hg):
            hd = g * hg + j
            tot = sg[H + hd:H + hd + 1, 0:1]
            wcols.append(sgt[:, 3 * H + hd:3 * H + hd + 1] * jnp.exp(tot - sgt[:, H + hd:H + hd + 1]))
            dcols.append(jnp.exp(tot))
        w = _lane_groups(wcols, P)
        dec = _lane_groups(dcols, P)
        xs = (x_ref[0, :, g * hg * P:(g + 1) * hg * P].astype(F32) * w).astype(BF16)
        st = _dot_tn(b_ref[0, :, g * N:(g + 1) * N], xs)
        h_old = h_scr[g]
        hb_ref[0, 0, g] = h_old.astype(hb_ref.dtype)
        h_scr[g] = dec * h_old + st


def _ssd_bwd(xc, sg):
    B, S, _ = xc.shape
    nc = S // CHUNK
    G, N, W = SSM_GROUPS, SSM_STATE, SSM_WIDTH
    return pl.pallas_call(
        functools.partial(_ssd_bwd_kernel, L=CHUNK),
        grid=(B, nc),
        in_specs=[
            pl.BlockSpec((1, CHUNK, W), lambda b, c: (b, nc - 1 - c, 0)),
            pl.BlockSpec((1, CHUNK, G * N), lambda b, c: (b, nc - 1 - c, W // (G * N))),
            pl.BlockSpec((1, 1, 4 * SSM_HEADS, CHUNK), lambda b, c: (b, nc - 1 - c, 0, 0)),
        ],
        out_specs=pl.BlockSpec((1, 1, G, N, W // G), lambda b, c: (b, nc - 1 - c, 0, 0, 0)),
        out_shape=jax.ShapeDtypeStruct((B, nc, G, N, W // G), BF16),
        scratch_shapes=[pltpu.VMEM((G, N, W // G), F32)],
        compiler_params=_cparams(("parallel", "arbitrary")),
        name="ssd_bwd_states",
    )(xc, xc, sg)


def _ssd_main_kernel(x_ref, b_ref, c_ref, z_ref, sg_ref, hb_ref, dsk_ref, nw_ref, y_ref, h_scr, *, L):
    G, hg, P, N = SSM_GROUPS, SSM_HEADS // SSM_GROUPS, SSM_HEADDIM, SSM_STATE
    H = SSM_HEADS
    GW = hg * P

    @pl.when(pl.program_id(1) == 0)
    def _():
        h_scr[...] = jnp.zeros_like(h_scr)

    row_i = lax.broadcasted_iota(jnp.int32, (L, L), 0)
    col_i = lax.broadcasted_iota(jnp.int32, (L, L), 1)
    lower = col_i <= row_i
    upper = col_i >= row_i
    lane_g = lax.broadcasted_iota(jnp.int32, (L, GW), 1)

    sg = sg_ref[0, 0]
    sgt = sg.T
    for g in range(G):
        xg = x_ref[0, :, g * GW:(g + 1) * GW]
        bg = b_ref[0, :, g * N:(g + 1) * N]
        cg = c_ref[0, :, g * N:(g + 1) * N]
        cb = _dot_nt(cg, bg)
        ms, xms, efc, ebc, wcols, dcols = [], [], [], [], [], []
        for j in range(hg):
            hd = g * hg + j
            af_r, ab_r = sg[hd:hd + 1], sg[H + hd:H + hd + 1]
            dtf_r, dtb_r = sg[2 * H + hd:2 * H + hd + 1], sg[3 * H + hd:3 * H + hd + 1]
            af_c, ab_c = sgt[:, hd:hd + 1], sgt[:, H + hd:H + hd + 1]
            dtf_c = sgt[:, 2 * H + hd:2 * H + hd + 1]
            fm = jnp.exp(jnp.where(lower, af_c - af_r, NEG_INF)) * dtf_r
            bm = jnp.exp(jnp.where(upper, ab_c - ab_r, NEG_INF)) * dtb_r
            ms.append((cb * (fm + bm)).astype(BF16))
            xms.append(jnp.where((lane_g >= j * P) & (lane_g < (j + 1) * P), xg, jnp.zeros_like(xg)))
            efc.append(jnp.exp(af_c))
            ebc.append(jnp.exp(ab_c))
            tot = af_r[:, L - 1:L]
            wcols.append(dtf_c * jnp.exp(tot - af_c))
            dcols.append(jnp.exp(tot))
        y = _dot(jnp.concatenate(ms, axis=1), jnp.concatenate(xms, axis=0))
        h_old = h_scr[g]
        y = y + _lane_groups(efc, P) * _dot(cg, h_old.astype(BF16))
        y = y + _lane_groups(ebc, P) * _dot(cg, hb_ref[0, 0, g])
        xgf = xg.astype(F32)
        y = y + dsk_ref[:, g * GW:(g + 1) * GW] * xgf
        xs = (xgf * _lane_groups(wcols, P)).astype(BF16)
        h_scr[g] = _lane_groups(dcols, P) * h_old + _dot_tn(bg, xs)
        zg = z_ref[0, :, g * GW:(g + 1) * GW].astype(F32)
        y = y * (zg * _sigmoid(zg))
        y = y * lax.rsqrt(jnp.mean(y * y, axis=1, keepdims=True) + RMS_EPS)
        y_ref[0, :, g * GW:(g + 1) * GW] = (y * nw_ref[:, g * GW:(g + 1) * GW]).astype(y_ref.dtype)


def _ssd_main(xc, pa3, sg, hb, dskip_row, norm_w):
    B, S, _ = xc.shape
    nc = S // CHUNK
    G, N, W = SSM_GROUPS, SSM_STATE, SSM_WIDTH
    z_blk = (2 * ML_QK + 2 * ML_WIDTH) // W
    return pl.pallas_call(
        functools.partial(_ssd_main_kernel, L=CHUNK),
        grid=(B, nc),
        in_specs=[
            pl.BlockSpec((1, CHUNK, W), lambda b, c: (b, c, 0)),
            pl.BlockSpec((1, CHUNK, G * N), lambda b, c: (b, c, W // (G * N))),
            pl.BlockSpec((1, CHUNK, G * N), lambda b, c: (b, c, W // (G * N) + 1)),
            pl.BlockSpec((1, CHUNK, W), lambda b, c: (b, c, z_blk)),
            pl.BlockSpec((1, 1, 4 * SSM_HEADS, CHUNK), lambda b, c: (b, c, 0, 0)),
            pl.BlockSpec((1, 1, G, N, W // G), lambda b, c: (b, c, 0, 0, 0)),
            pl.BlockSpec((1, W), lambda b, c: (0, 0)),
            pl.BlockSpec((1, W), lambda b, c: (0, 0)),
        ],
        out_specs=pl.BlockSpec((1, CHUNK, W), lambda b, c: (b, c, 0)),
        out_shape=jax.ShapeDtypeStruct((B, S, W), BF16),
        scratch_shapes=[pltpu.VMEM((G, N, W // G), F32)],
        compiler_params=_cparams(("parallel", "arbitrary")),
        name="ssd_main",
    )(xc, xc, xc, pa3, sg, hb, dskip_row, norm_w)


def _pack_bf16_pairs(v):
    n = v.shape[1] // 2
    hi = lax.bitcast_convert_type(v[:, :n].astype(BF16).astype(F32), jnp.uint32)
    lo = lax.bitcast_convert_type(v[:, n:].astype(BF16).astype(F32), jnp.uint32)
    return (hi & jnp.uint32(0xFFFF0000)) | (lo >> 16)


def _unpack_bf16_pairs(u):
    hi = lax.bitcast_convert_type(u & jnp.uint32(0xFFFF0000), F32)
    lo = lax.bitcast_convert_type(u << 16, F32)
    return hi, lo


def _layer_norm(v, g, b):
    mu = jnp.mean(v, axis=1, keepdims=True)
    d = v - mu
    var = jnp.mean(d * d, axis=1, keepdims=True)
    return d * lax.rsqrt(var + LN_EPS) * g + b


def _outproj_kernel(yml_ref, yssm_ref, x_ref, wo_ref, g_ref, b_ref, wr_ref, h_ref, hp_ref, sc_ref):
    mix = _dot(yml_ref[...], wo_ref[0:ML_WIDTH, :]) + _dot(yssm_ref[...], wo_ref[ML_WIDTH:, :])
    h = _layer_norm(ALPHA * x_ref[...] + mix, g_ref[...], b_ref[...])
    h_ref[...] = h
    hp_ref[...] = _pack_bf16_pairs(h)
    sc_ref[...] = _sigmoid(_dot(h.astype(BF16), wr_ref[...]))


def _outproj(yml, yssm, x2d, w_out, ln_g, ln_b, w_router, tm=512):
    T, D = x2d.shape
    tm = min(tm, T)
    row = lambda i: (i, 0)
    fixed = lambda i: (0, 0)
    return pl.pallas_call(
        _outproj_kernel,
        grid=(T // tm,),
        in_specs=[
            pl.BlockSpec((tm, ML_WIDTH), row),
            pl.BlockSpec((tm, SSM_WIDTH), row),
            pl.BlockSpec((tm, D), row),
            pl.BlockSpec((ML_WIDTH + SSM_WIDTH, D), fixed),
            pl.BlockSpec((1, D), fixed),
            pl.BlockSpec((1, D), fixed),
            pl.BlockSpec((D, 128), fixed),
        ],
        out_specs=[pl.BlockSpec((tm, D), row), pl.BlockSpec((tm, D // 2), row), pl.BlockSpec((tm, 128), row)],
        out_shape=[
            jax.ShapeDtypeStruct((T, D), F32),
            jax.ShapeDtypeStruct((T, D // 2), jnp.uint32),
            jax.ShapeDtypeStruct((T, 128), F32),
        ],
        compiler_params=_cparams(("parallel",)),
        name="out_proj_ln1_router",
    )(yml, yssm, x2d, w_out, ln_g, ln_b, w_router)


def _route_kernel(sc_ref, rb_ref, idx_ref, w_ref, rank_ref, cnt_ref, carry_scr, *, Tt):
    E, NG = N_EXPERTS, N_EXPERT_GROUPS
    EG = E // NG

    @pl.when(pl.program_id(0) == 0)
    def _():
        carry_scr[...] = jnp.zeros_like(carry_scr)

    s = sc_ref[...].T[0:E]
    s3 = s.reshape(NG, EG, Tt)
    ch3 = (s + rb_ref[:, 0:1]).reshape(NG, EG, Tt)
    e_in = lax.broadcasted_iota(jnp.int32, (NG, EG, Tt), 1)
    g_in = lax.broadcasted_iota(jnp.int32, (NG, EG, Tt), 0)
    f_i = g_in * EG + e_in
    m1 = jnp.max(ch3, axis=1, keepdims=True)
    i1 = jnp.min(jnp.where(ch3 == m1, e_in, EG), axis=1, keepdims=True)
    m2 = jnp.max(jnp.where(e_in == i1, NEG_INF, ch3), axis=1, keepdims=True)
    gs = m1 + m2
    g_i = lax.broadcasted_iota(jnp.int32, (NG, 1, Tt), 0)
    gsel = jnp.zeros((NG, 1, Tt), jnp.bool_)
    cur = gs
    for _ in range(TOPK_GROUPS):
        mx = jnp.max(cur, axis=0, keepdims=True)
        pick = jnp.min(jnp.where(cur == mx, g_i, NG), axis=0, keepdims=True)
        hit = g_i == pick
        gsel = gsel | hit
        cur = jnp.where(hit, NEG_INF, cur)
    cur = jnp.where(gsel, ch3, NEG_INF)
    selmask = jnp.zeros((NG, EG, Tt), jnp.bool_)
    idxs, ws = [], []
    for _ in range(TOP_K):
        mx = jnp.max(jnp.max(cur, axis=1, keepdims=True), axis=0, keepdims=True)
        cand = jnp.where(cur == mx, f_i, E)
        pick = jnp.min(jnp.min(cand, axis=1, keepdims=True), axis=0, keepdims=True)
        hit = f_i == pick
        wk = jnp.sum(jnp.sum(jnp.where(hit, s3, 0.0), axis=1, keepdims=True), axis=0, keepdims=True)
        idxs.append(pick.reshape(1, Tt))
        ws.append(wk.reshape(1, Tt))
        selmask = selmask | hit
        cur = jnp.where(hit, NEG_INF, cur)
    wsum = ws[0]
    for k in range(1, TOP_K):
        wsum = wsum + ws[k]
    r = lax.broadcasted_iota(jnp.int32, (Tt, Tt), 0)
    c = lax.broadcasted_iota(jnp.int32, (Tt, Tt), 1)
    ustrict = (r < c).astype(BF16)
    self32 = jnp.where(selmask, 1.0, 0.0).reshape(E, Tt)
    excl = _dot(self32.astype(BF16), ustrict) + carry_scr[:, 0:1]
    excl3 = excl.reshape(NG, EG, Tt)
    for k in range(TOP_K):
        hit = f_i == idxs[k].reshape(1, 1, Tt)
        rk = jnp.sum(jnp.sum(jnp.where(hit, excl3, 0.0), axis=1, keepdims=True), axis=0, keepdims=True)
        idx_ref[k:k + 1, :] = idxs[k]
        w_ref[k:k + 1, :] = ws[k] / wsum * ROUTED_SCALE
        rank_ref[k:k + 1, :] = rk.reshape(1, Tt).astype(jnp.int32)
    carry_scr[...] = carry_scr[...] + jnp.sum(self32, axis=1, keepdims=True)
    cnt_ref[...] = carry_scr[...]


def _route(scores, rbias, Tt=512):
    T = scores.shape[0]
    kt = lambda i: (0, i)
    return pl.pallas_call(
        functools.partial(_route_kernel, Tt=Tt),
        grid=(T // Tt,),
        in_specs=[pl.BlockSpec((Tt, 128), lambda i: (i, 0)), pl.BlockSpec((N_EXPERTS, 128), lambda i: (0, 0))],
        out_specs=[
            pl.BlockSpec((TOP_K, Tt), kt),
            pl.BlockSpec((TOP_K, Tt), kt),
            pl.BlockSpec((TOP_K, Tt), kt),
            pl.BlockSpec((N_EXPERTS, 128), lambda i: (0, 0)),
        ],
        out_shape=[
            jax.ShapeDtypeStruct((TOP_K, T), jnp.int32),
            jax.ShapeDtypeStruct((TOP_K, T), F32),
            jax.ShapeDtypeStruct((TOP_K, T), jnp.int32),
            jax.ShapeDtypeStruct((N_EXPERTS, 128), F32),
        ],
        scratch_shapes=[pltpu.VMEM((N_EXPERTS, 128), F32)],
        compiler_params=_cparams(("arbitrary",)),
        name="route_topk",
    )(scores, rbias)


def _gather_kernel(idx_hbm, src_ref, dst_ref, idx_smem, isem, sem, *, R):
    i = pl.program_id(0)
    cp = pltpu.make_async_copy(idx_hbm.at[i], idx_smem, isem)
    cp.start()
    cp.wait()
    base = i * R

    def body(r, carry):
        t = idx_smem[0, r]
        pltpu.make_async_copy(src_ref.at[pl.ds(t, 1)], dst_ref.at[pl.ds(base + r, 1)], sem).start()
        return carry

    lax.fori_loop(0, R, body, 0, unroll=8)
    pltpu.make_async_copy(src_ref.at[pl.ds(0, R)], dst_ref.at[pl.ds(base, R)], sem).wait()


def _gather_rows(src, idx, R=GATHER_ROWS):
    n = idx.shape[0]
    W = src.shape[1]
    return pl.pallas_call(
        functools.partial(_gather_kernel, R=R),
        grid=(n // R,),
        in_specs=[pl.BlockSpec(memory_space=pl.ANY), pl.BlockSpec(memory_space=pl.ANY)],
        out_specs=pl.BlockSpec(memory_space=pl.ANY),
        out_shape=jax.ShapeDtypeStruct((n, W), src.dtype),
        scratch_shapes=[pltpu.SMEM((1, R), jnp.int32), pltpu.SemaphoreType.DMA, pltpu.SemaphoreType.DMA],
        compiler_params=_cparams(("arbitrary",)),
        name="gather_rows",
    )(idx.reshape(n // R, 1, R), src)


def _expert_kernel(be_ref, nu_ref, x_ref, wg_ref, wu_ref, wd_ref, o_ref):
    half = D_MODEL // 2

    @pl.when(pl.program_id(0) < nu_ref[0])
    def _():
        hi, lo = _unpack_bf16_pairs(x_ref[...])
        hi, lo = hi.astype(BF16), lo.astype(BF16)
        gt = _dot(hi, wg_ref[0, 0:half, :]) + _dot(lo, wg_ref[0, half:, :])
        up = _dot(hi, wu_ref[0, 0:half, :]) + _dot(lo, wu_ref[0, half:, :])
        hmid = (gt * _sigmoid(gt) * up).astype(BF16)
        o_ref[...] = _pack_bf16_pairs(_dot(hmid, wd_ref[0]))

    @pl.when(pl.program_id(0) >= nu_ref[0])
    def _():
        o_ref[...] = jnp.zeros_like(o_ref)


def _experts(x_sorted, block_expert, n_used, w_gate, w_up, w_down, blk=MOE_BLOCK):
    P, half = x_sorted.shape
    nb = P // blk
    D, F = D_MODEL, D_EXPERT
    cur = lambda i, be, nu: jnp.minimum(i, nu[0] - 1)
    return pl.pallas_call(
        _expert_kernel,
        grid_spec=pltpu.PrefetchScalarGridSpec(
            num_scalar_prefetch=2,
            grid=(nb,),
            in_specs=[
                pl.BlockSpec((blk, half), lambda i, be, nu: (cur(i, be, nu), 0)),
                pl.BlockSpec((1, D, F), lambda i, be, nu: (be[cur(i, be, nu)], 0, 0)),
                pl.BlockSpec((1, D, F), lambda i, be, nu: (be[cur(i, be, nu)], 0, 0)),
                pl.BlockSpec((1, F, D), lambda i, be, nu: (be[cur(i, be, nu)], 0, 0)),
            ],
            out_specs=pl.BlockSpec((blk, half), lambda i, be, nu: (i, 0)),
        ),
        out_shape=jax.ShapeDtypeStruct((P, half), jnp.uint32),
        compiler_params=_cparams(("arbitrary",)),
        name="expert_mlp",
    )(block_expert, n_used, x_sorted, w_gate, w_up, w_down)


def _final_kernel(h_ref, gk_ref, wt_ref, wsg_ref, wsu_ref, wsd_ref, g_ref, b_ref, o_ref):
    h = h_ref[...]
    hb = h.astype(BF16)
    gt = _dot(hb, wsg_ref[...])
    up = _dot(hb, wsu_ref[...])
    shared = _dot((gt * _sigmoid(gt) * up).astype(BF16), wsd_ref[...])
    acc_hi = None
    for k in range(TOP_K):
        hi, lo = _unpack_bf16_pairs(gk_ref[k])
        wk = wt_ref[:, k:k + 1]
        acc_hi = wk * hi if acc_hi is None else acc_hi + wk * hi
        acc_lo = wk * lo if k == 0 else acc_lo + wk * lo
    routed = jnp.concatenate([acc_hi, acc_lo], axis=1)
    o_ref[...] = _layer_norm(ALPHA * h + (routed + shared), g_ref[...], b_ref[...])


def _final(h, gk3, wt, ws_gate, ws_up, ws_down, ln_g, ln_b, tm=512):
    T, D = h.shape
    tm = min(tm, T)
    F = D_EXPERT
    fixed = lambda i: (0, 0)
    return pl.pallas_call(
        _final_kernel,
        grid=(T // tm,),
        in_specs=[
            pl.BlockSpec((tm, D), lambda i: (i, 0)),
            pl.BlockSpec((TOP_K, tm, D // 2), lambda i: (0, i, 0)),
            pl.BlockSpec((tm, TOP_K), lambda i: (i, 0)),
            pl.BlockSpec((D, F), fixed),
            pl.BlockSpec((D, F), fixed),
            pl.BlockSpec((F, D), fixed),
            pl.BlockSpec((1, D), fixed),
            pl.BlockSpec((1, D), fixed),
        ],
        out_specs=pl.BlockSpec((tm, D), lambda i: (i, 0)),
        out_shape=jax.ShapeDtypeStruct((T, D), F32),
        compiler_params=_cparams(("parallel",)),
        name="shared_combine_ln2",
    )(h, gk3, wt, ws_gate, ws_up, ws_down, ln_g, ln_b)


def _mixers(x, w_in, ml_i_bias, ml_f_bias, ml_norm_w, conv_w, conv_b, dt_bias, a_log, d_skip, ssm_norm_w):
    B, S, D = x.shape
    T = B * S
    nc = S // CHUNK
    H = ML_HEADS
    o0 = 2 * ML_QK + 2 * ML_WIDTH
    g0, z0 = o0, o0 + 4 * H
    x0 = z0 + SSM_WIDTH
    d0 = x0 + SSM_CONV_DIM
    w_big = jnp.concatenate([w_in[:, :o0], w_in[:, z0:d0]], axis=1).astype(BF16)
    w_small = jnp.concatenate(
        [w_in[:, g0:z0], w_in[:, d0:], jnp.zeros((D, PB_COLS - N_GATE_ROWS), w_in.dtype)], axis=1).astype(BF16)
    pa, pb = _proj(x.reshape(T, D), w_big, w_small)
    pa3 = pa.reshape(B, S, PA_COLS)

    g_rows = pb.reshape(B, S, PB_COLS)[:, :, :N_GATE_ROWS].transpose(0, 2, 1).reshape(B, N_GATE_ROWS, nc, CHUNK)
    bias = jnp.concatenate([ml_i_bias, ml_f_bias, dt_bias]).astype(F32)
    bias_rows = jnp.broadcast_to(bias[:, None, None], (N_GATE_ROWS, 1, CHUNK))
    aneg_rows = jnp.broadcast_to((-jnp.exp(a_log.astype(F32)))[:, None, None], (2 * SSM_HEADS, 1, CHUNK))
    prep = _gate_prep(g_rows, bias_rows, aneg_rows)
    mg = jnp.stack([prep[:, 0:H], prep[:, 2 * H:3 * H], prep[:, H:2 * H], prep[:, 3 * H:4 * H]], axis=2)
    mg = jnp.concatenate([mg, jnp.zeros_like(mg)], axis=2).transpose(0, 1, 3, 2, 4)
    sg = prep[:, 16:80].transpose(0, 2, 1, 3)

    y_ml = _mlstm(pa3, mg, ml_norm_w.astype(F32).reshape(1, ML_WIDTH))

    Lc = 512 if S % 512 == 0 else CHUNK
    nt = S // Lc
    pad = CONV_K // 2
    xbc0 = PA_COLS - SSM_CONV_DIM
    pa4 = pa3.reshape(B, nt, Lc, PA_COLS)
    top = pa4[:, :, Lc - pad:, xbc0:]
    bot = pa4[:, :, :pad, xbc0:]
    zrow = jnp.zeros((B, 1, pad, SSM_CONV_DIM), pa.dtype)
    top = jnp.concatenate([zrow, top[:, :-1]], axis=1)
    bot = jnp.concatenate([bot[:, 1:], zrow], axis=1)
    halo = jnp.concatenate([top, bot, jnp.zeros((B, nt, 8 - 2 * pad, SSM_CONV_DIM), pa.dtype)], axis=2)
    conv_w8 = jnp.concatenate([conv_w.astype(F32), jnp.zeros((8 - CONV_K, SSM_CONV_DIM), F32)], axis=0)
    xc = _conv(pa3, halo, conv_w8, conv_b.astype(F32).reshape(1, SSM_CONV_DIM), Lc=Lc)

    hb = _ssd_bwd(xc, sg)
    dsk = jnp.repeat(d_skip.astype(F32), SSM_HEADDIM).reshape(1, SSM_WIDTH)
    y_ssm = _ssd_main(xc, pa3, sg, hb, dsk, ssm_norm_w.astype(F32).reshape(1, SSM_WIDTH))
    return y_ml.reshape(T, ML_WIDTH), y_ssm.reshape(T, SSM_WIDTH)


def _moe_dispatch(idx, rank, counts, T, blk):
    E = N_EXPERTS
    M = T * TOP_K
    nb = (M + E * (blk - 1)) // blk
    pcounts = (counts + blk - 1) // blk * blk
    pends = jnp.cumsum(pcounts)
    pstarts = pends - pcounts
    dest = pstarts[idx] + rank
    P = nb * blk
    tok = jnp.broadcast_to(jnp.arange(T, dtype=jnp.int32)[None, :], (TOP_K, T))
    slot_tok = (jnp.arange(P, dtype=jnp.int32) % T).at[dest.reshape(-1)].set(tok.reshape(-1))
    block_expert = jnp.minimum(
        jnp.searchsorted(pends, jnp.arange(nb, dtype=jnp.int32) * blk, side="right"), E - 1).astype(jnp.int32)
    n_used = (pends[-1] // blk).astype(jnp.int32).reshape(1)
    return dest, slot_tok, block_expert, n_used


def _layer(x, w_in, ml_i_bias, ml_f_bias, ml_norm_w, conv_w, conv_b, dt_bias, a_log, d_skip, ssm_norm_w,
           w_out, ln1_g, ln1_b, w_router, router_bias, w_exp_gate, w_exp_up, w_exp_down,
           w_sh_gate, w_sh_up, w_sh_down, ln2_g, ln2_b):
    B, S, D = x.shape
    T = B * S
    y_ml, y_ssm = _mixers(x, w_in, ml_i_bias, ml_f_bias, ml_norm_w, conv_w, conv_b, dt_bias, a_log, d_skip,
                          ssm_norm_w)
    w_r = jnp.concatenate([w_router, jnp.zeros((D, 128 - N_EXPERTS), w_router.dtype)], axis=1).astype(BF16)
    h, h_pk, scores = _outproj(y_ml, y_ssm, x.reshape(T, D), w_out.astype(BF16),
                               ln1_g.astype(F32).reshape(1, D), ln1_b.astype(F32).reshape(1, D), w_r)
    rbias = jnp.broadcast_to(router_bias.astype(F32)[:, None], (N_EXPERTS, 128))
    idx, wgt, rank, cnt = _route(scores, rbias)
    counts = cnt[:, 0].astype(jnp.int32)
    dest, slot_tok, block_expert, n_used = _moe_dispatch(idx, rank, counts, T, MOE_BLOCK)
    x_sorted = _gather_rows(h_pk, slot_tok)
    out_sorted = _experts(x_sorted, block_expert, n_used, w_exp_gate.astype(BF16), w_exp_up.astype(BF16),
                          w_exp_down.astype(BF16))
    gk = _gather_rows(out_sorted, dest.reshape(-1))
    out = _final(h, gk.reshape(TOP_K, T, D // 2), wgt.T, w_sh_gate.astype(BF16), w_sh_up.astype(BF16),
                 w_sh_down.astype(BF16), ln2_g.astype(F32).reshape(1, D), ln2_b.astype(F32).reshape(1, D))
    return out.reshape(B, S, D)


def kernel(x, w_in, ml_i_bias, ml_f_bias, ml_norm_w, conv_w, conv_b, dt_bias, a_log, d_skip, ssm_norm_w, w_out,
           ln1_g, ln1_b, w_router, router_bias, w_exp_gate, w_exp_up, w_exp_down, w_sh_gate, w_sh_up, w_sh_down,
           ln2_g, ln2_b):
    for l in range(w_in.shape[0]):
        x = _layer(x, w_in[l], ml_i_bias[l], ml_f_bias[l], ml_norm_w[l], conv_w[l], conv_b[l], dt_bias[l],
                   a_log[l], d_skip[l], ssm_norm_w[l], w_out[l], ln1_g[l], ln1_b[l], w_router[l],
                   router_bias[l], w_exp_gate[l], w_exp_up[l], w_exp_down[l], w_sh_gate[l], w_sh_up[l],
                   w_sh_down[l], ln2_g[l], ln2_b[l])
    return x
```

```python
import functools

import jax
import jax.numpy as jnp
from jax import lax
from jax.experimental import pallas as pl
from jax.experimental.pallas import tpu as pltpu
from jax.experimental.pallas import tpu_sc as plsc

D_MODEL = 1024
ML_HEADS = 4
ML_DV = 256
ML_DQK = 128
ML_QK = ML_HEADS * ML_DQK
ML_WIDTH = ML_HEADS * ML_DV
SSM_WIDTH = 1024
SSM_HEADDIM = 64
SSM_HEADS = 16
SSM_GROUPS = 4
SSM_STATE = 128
SSM_CONV_DIM = SSM_WIDTH + 2 * SSM_GROUPS * SSM_STATE
CONV_K = 5
CHUNK = 128
N_EXPERTS = 64
TOP_K = 8
N_EXPERT_GROUPS = 8
TOPK_GROUPS = 4
D_EXPERT = 256
ROUTED_SCALE = 2.5
LN_EPS = 1e-5
RMS_EPS = 1e-5
DEPTH = 1
ALPHA = (2 * DEPTH) ** 0.25

PA_COLS = 2 * ML_QK + 2 * ML_WIDTH + SSM_WIDTH + SSM_CONV_DIM
PB_COLS = 128
N_GATE_ROWS = 2 * ML_HEADS + 2 * ML_HEADS + 2 * SSM_HEADS
N_PREP_ROWS = 80

MOE_BLOCK = 512
SC_WINDOW = 128
VMEM_LIMIT = 56 * 1024 * 1024

BF16 = jnp.bfloat16
F32 = jnp.float32
NEG_INF = float("-inf")


def _cparams(sem, vmem=VMEM_LIMIT):
    return pltpu.CompilerParams(dimension_semantics=sem, vmem_limit_bytes=vmem)


def _dot(a, b):
    return jnp.dot(a, b, preferred_element_type=F32)


def _dot_nt(a, b):
    return lax.dot_general(a, b, (((1,), (1,)), ((), ())), preferred_element_type=F32)


def _dot_tn(a, b):
    return lax.dot_general(a, b, (((0,), (0,)), ((), ())), preferred_element_type=F32)


def _sigmoid(x):
    return 1.0 / (1.0 + jnp.exp(-x))


def _softplus(x):
    return jnp.maximum(x, 0.0) + jnp.log1p(jnp.exp(-jnp.abs(x)))


def _proj_kernel(x_ref, wb_ref, ws_ref, oa_ref, ob_ref, xb_scr):
    @pl.when(pl.program_id(1) == 0)
    def _():
        xb = x_ref[...].astype(BF16)
        xb_scr[...] = xb
        ob_ref[...] = _dot(xb, ws_ref[...])

    oa_ref[...] = _dot(xb_scr[...], wb_ref[...]).astype(oa_ref.dtype)


def _proj(x2d, w_big, w_small, tm=1024, tn=1536):
    T, D = x2d.shape
    N = w_big.shape[1]
    tm = min(tm, T)
    return pl.pallas_call(
        _proj_kernel,
        grid=(T // tm, N // tn),
        in_specs=[
            pl.BlockSpec((tm, D), lambda i, j: (i, 0)),
            pl.BlockSpec((D, tn), lambda i, j: (0, j)),
            pl.BlockSpec((D, PB_COLS), lambda i, j: (0, 0)),
        ],
        out_specs=[
            pl.BlockSpec((tm, tn), lambda i, j: (i, j)),
            pl.BlockSpec((tm, PB_COLS), lambda i, j: (i, 0)),
        ],
        out_shape=[jax.ShapeDtypeStruct((T, N), BF16), jax.ShapeDtypeStruct((T, PB_COLS), F32)],
        scratch_shapes=[pltpu.VMEM((tm, D), BF16)],
        compiler_params=_cparams(("parallel", "arbitrary")),
        name="in_proj",
    )(x2d, w_big, w_small)


def _split3_dot(x, u):
    hi = x.astype(BF16)
    r1 = x - hi.astype(F32)
    mid = r1.astype(BF16)
    lo = (r1 - mid.astype(F32)).astype(BF16)
    return _dot(hi, u) + _dot(mid, u) + _dot(lo, u)


def _gate_kernel(g_ref, bias_ref, aneg_ref, o_ref, *, nc, L):
    g = g_ref[0] + bias_ref[...]
    nh = ML_HEADS
    ig = g[0:2 * nh]
    fpre = g[2 * nh:4 * nh]
    lf = -_softplus(-fpre)
    dt = _softplus(g[4 * nh:])
    dA = dt * aneg_ref[...]
    r = lax.broadcasted_iota(jnp.int32, (L, L), 0)
    c = lax.broadcasted_iota(jnp.int32, (L, L), 1)
    u_pre = (r <= c).astype(BF16)
    u_suf = (r >= c).astype(BF16)
    nf = nh + SSM_HEADS
    xf = jnp.concatenate([lf[0:nh], dA[0:SSM_HEADS]], axis=0).reshape(nf * nc, L)
    xb = jnp.concatenate([lf[nh:], dA[SSM_HEADS:]], axis=0).reshape(nf * nc, L)
    cf = _split3_dot(xf, u_pre).reshape(nf, nc, L)
    cb = _split3_dot(xb, u_suf).reshape(nf, nc, L)
    o_ref[0, 0:nh] = cf[0:nh]
    o_ref[0, nh:2 * nh] = cb[0:nh]
    o_ref[0, 2 * nh:4 * nh] = ig
    o_ref[0, 16:32] = cf[nh:]
    o_ref[0, 32:48] = cb[nh:]
    o_ref[0, 48:80] = dt


def _gate_prep(g_rows, bias_rows, aneg_rows):
    B, R, nc, L = g_rows.shape
    return pl.pallas_call(
        functools.partial(_gate_kernel, nc=nc, L=L),
        grid=(B,),
        in_specs=[
            pl.BlockSpec((1, R, nc, L), lambda b: (b, 0, 0, 0)),
            pl.BlockSpec((R, 1, L), lambda b: (0, 0, 0)),
            pl.BlockSpec((2 * SSM_HEADS, 1, L), lambda b: (0, 0, 0)),
        ],
        out_specs=pl.BlockSpec((1, N_PREP_ROWS, nc, L), lambda b: (b, 0, 0, 0)),
        out_shape=jax.ShapeDtypeStruct((B, N_PREP_ROWS, nc, L), F32),
        compiler_params=_cparams(("parallel",)),
        name="gate_prep",
    )(g_rows, bias_rows, aneg_rows)


def _mlstm_kernel(q_ref, k_ref, v_ref, o_ref, mg_ref, nw_ref, y_ref, hf_scr, c_scr, n_scr, m_scr, *, nc, L):
    scale = ML_DQK ** -0.5
    row_i = lax.broadcasted_iota(jnp.int32, (L, L), 0)
    col_i = lax.broadcasted_iota(jnp.int32, (L, L), 1)

    def reset():
        c_scr[...] = jnp.zeros_like(c_scr)
        n_scr[...] = jnp.zeros_like(n_scr)
        m_scr[...] = jnp.full_like(m_scr, NEG_INF)

    def chunk(c, fwd):
        sl = pl.ds(pl.multiple_of(c * L, L), L)
        q = q_ref[0, sl, :]
        kf = k_ref[0, sl, :].astype(F32) * scale
        v = v_ref[0, sl, :]
        g8 = mg_ref[0, 0, c]
        gt = g8.T
        if fwd:
            b_r, i_r, b_c, i_c = g8[0:1], g8[1:2], gt[:, 0:1], gt[:, 1:2]
            gtot = b_r[:, L - 1:L]
            mask = col_i <= row_i
        else:
            b_r, i_r, b_c, i_c = g8[2:3], g8[3:4], gt[:, 2:3], gt[:, 3:4]
            gtot = b_r[:, 0:1]
            mask = col_i >= row_i
        m_prev = m_scr[...]
        a_r = gtot - b_r + i_r
        m_loc = jnp.max(a_r, axis=1, keepdims=True)
        a_c = gtot - b_c + i_c
        kw = kf * jnp.exp(a_c - m_loc)
        u_mat = _dot_tn(kw.astype(BF16), v)
        u_vec = jnp.sum(kw, axis=0, keepdims=True)
        dmat = jnp.where(mask, b_c - b_r + i_r, NEG_INF)
        m_inter = b_c + m_prev
        m_t = jnp.maximum(m_inter, jnp.max(dmat, axis=1, keepdims=True))
        s_mat = _dot_nt(q, kf.astype(BF16)) * jnp.exp(dmat - m_t)
        inter_w = jnp.exp(m_inter - m_t)
        num = _dot(s_mat.astype(BF16), v) + inter_w * _dot(q, c_scr[...].astype(BF16))
        qn = jnp.sum(q.astype(F32) * n_scr[...], axis=1, keepdims=True)
        den = jnp.sum(s_mat, axis=1, keepdims=True) + inter_w * qn
        h = num / jnp.maximum(jnp.abs(den), jnp.exp(-m_t))
        m_new = jnp.maximum(gtot + m_prev, m_loc)
        decay = jnp.exp(gtot + m_prev - m_new)
        sc = jnp.exp(m_loc - m_new)
        c_scr[...] = decay * c_scr[...] + sc * u_mat
        n_scr[...] = decay * n_scr[...] + sc * u_vec
        m_scr[...] = m_new
        return sl, h

    reset()

    def fwd_body(c, carry):
        sl, h = chunk(c, True)
        hf_scr[sl, :] = h
        return carry

    lax.fori_loop(0, nc, fwd_body, 0)
    reset()

    def bwd_body(i, carry):
        sl, hb = chunk(nc - 1 - i, False)
        h = hf_scr[sl, :] + hb
        mu = jnp.mean(h, axis=1, keepdims=True)
        d = h - mu
        var = jnp.mean(d * d, axis=1, keepdims=True)
        y = d * lax.rsqrt(var + LN_EPS) * nw_ref[...]
        y = y * _sigmoid(o_ref[0, sl, :].astype(F32))
        y_ref[0, sl, :] = y.astype(y_ref.dtype)
        return carry

    lax.fori_loop(0, nc, bwd_body, 0)


def _mlstm(pa3, mg, norm_w):
    B, S, _ = pa3.shape
    nc = S // CHUNK
    H = ML_HEADS
    return pl.pallas_call(
        functools.partial(_mlstm_kernel, nc=nc, L=CHUNK),
        grid=(B, H),
        in_specs=[
            pl.BlockSpec((1, S, ML_DQK), lambda b, h: (b, 0, h)),
            pl.BlockSpec((1, S, ML_DQK), lambda b, h: (b, 0, H + h)),
            pl.BlockSpec((1, S, ML_DV), lambda b, h: (b, 0, H + h)),
            pl.BlockSpec((1, S, ML_DV), lambda b, h: (b, 0, 2 * H + h)),
            pl.BlockSpec((1, 1, nc, 8, CHUNK), lambda b, h: (b, h, 0, 0, 0)),
            pl.BlockSpec((1, ML_DV), lambda b, h: (0, h)),
        ],
        out_specs=pl.BlockSpec((1, S, ML_DV), lambda b, h: (b, 0, h)),
        out_shape=jax.ShapeDtypeStruct((B, S, ML_WIDTH), BF16),
        scratch_shapes=[
            pltpu.VMEM((S, ML_DV), F32),
            pltpu.VMEM((ML_DQK, ML_DV), F32),
            pltpu.VMEM((1, ML_DQK), F32),
            pltpu.VMEM((1, 1), F32),
        ],
        compiler_params=_cparams(("parallel", "parallel")),
        name="mlstm",
    )(pa3, pa3, pa3, pa3, mg, norm_w)


def _conv_kernel(x_ref, halo_ref, w_ref, b_ref, o_ref, ext_scr, *, Lc):
    pad = CONV_K // 2
    halo = halo_ref[0, 0].astype(F32)
    ext_scr[8 - pad:8, :] = halo[0:pad]
    ext_scr[8:8 + Lc, :] = x_ref[0].astype(F32)
    ext_scr[8 + Lc:8 + Lc + pad, :] = halo[pad:2 * pad]
    acc = b_ref[...] + w_ref[0:1, :] * ext_scr[8 - pad:8 - pad + Lc, :]
    for j in range(1, CONV_K):
        acc = acc + w_ref[j:j + 1, :] * ext_scr[8 - pad + j:8 - pad + j + Lc, :]
    o_ref[0] = (acc * _sigmoid(acc)).astype(o_ref.dtype)


def _conv(pa3, halo, conv_w8, conv_b, Lc=512):
    B, S, _ = pa3.shape
    C = SSM_CONV_DIM
    xbc_blk = (PA_COLS - C) // C
    return pl.pallas_call(
        functools.partial(_conv_kernel, Lc=Lc),
        grid=(B, S // Lc),
        in_specs=[
            pl.BlockSpec((1, Lc, C), lambda b, i: (b, i, xbc_blk)),
            pl.BlockSpec((1, 1, 8, C), lambda b, i: (b, i, 0, 0)),
            pl.BlockSpec((8, C), lambda b, i: (0, 0)),
            pl.BlockSpec((1, C), lambda b, i: (0, 0)),
        ],
        out_specs=pl.BlockSpec((1, Lc, C), lambda b, i: (b, i, 0)),
        out_shape=jax.ShapeDtypeStruct((B, S, C), BF16),
        scratch_shapes=[pltpu.VMEM((Lc + 16, C), F32)],
        compiler_params=_cparams(("parallel", "parallel")),
        name="ssm_conv",
    )(pa3, halo, conv_w8, conv_b)


def _lane_groups(cols, width):
    n = len(cols)
    rows = cols[0].shape[0]
    lane = lax.broadcasted_iota(jnp.int32, (rows, n * width), 1)
    out = jnp.broadcast_to(cols[n - 1], (rows, n * width))
    for j in range(n - 2, -1, -1):
        out = jnp.where(lane < (j + 1) * width, jnp.broadcast_to(cols[j], (rows, n * width)), out)
    return out


def _ssd_bwd_kernel(x_ref, b_ref, sg_ref, hb_ref, h_scr, *, L):
    G, hg, P, N = SSM_GROUPS, SSM_HEADS // SSM_GROUPS, SSM_HEADDIM, SSM_STATE
    H = SSM_HEADS

    @pl.when(pl.program_id(1) == 0)
    def _():
        h_scr[...] = jnp.zeros_like(h_scr)

    sg = sg_ref[0, 0]
    sgt = sg.T
    for g in range(G):
        wcols, dcols = [], []
        for j in range(hg):
            hd = g * hg + j
            tot = sg[H + hd:H + hd + 1, 0:1]
            wcols.append(sgt[:, 3 * H + hd:3 * H + hd + 1] * jnp.exp(tot - sgt[:, H + hd:H + hd + 1]))
            dcols.append(jnp.exp(tot))
        w = _lane_groups(wcols, P)
        dec = _lane_groups(dcols, P)
        xs = (x_ref[0, :, g * hg * P:(g + 1) * hg * P].astype(F32) * w).astype(BF16)
        st = _dot_tn(b_ref[0, :, g * N:(g + 1) * N], xs)
        h_old = h_scr[g]
        hb_ref[0, 0, g] = h_old.astype(hb_ref.dtype)
        h_scr[g] = dec * h_old + st


def _ssd_bwd(xc, sg):
    B, S, _ = xc.shape
    nc = S // CHUNK
    G, N, W = SSM_GROUPS, SSM_STATE, SSM_WIDTH
    return pl.pallas_call(
        functools.partial(_ssd_bwd_kernel, L=CHUNK),
        grid=(B, nc),
        in_specs=[
            pl.BlockSpec((1, CHUNK, W), lambda b, c: (b, nc - 1 - c, 0)),
            pl.BlockSpec((1, CHUNK, G * N), lambda b, c: (b, nc - 1 - c, W // (G * N))),
            pl.BlockSpec((1, 1, 4 * SSM_HEADS, CHUNK), lambda b, c: (b, nc - 1 - c, 0, 0)),
        ],
        out_specs=pl.BlockSpec((1, 1, G, N, W // G), lambda b, c: (b, nc - 1 - c, 0, 0, 0)),
        out_shape=jax.ShapeDtypeStruct((B, nc, G, N, W // G), BF16),
        scratch_shapes=[pltpu.VMEM((G, N, W // G), F32)],
        compiler_params=_cparams(("parallel", "arbitrary")),
        name="ssd_bwd_states",
    )(xc, xc, sg)


def _ssd_main_kernel(x_ref, b_ref, c_ref, z_ref, sg_ref, hb_ref, dsk_ref, nw_ref, y_ref, h_scr, *, L):
    G, hg, P, N = SSM_GROUPS, SSM_HEADS // SSM_GROUPS, SSM_HEADDIM, SSM_STATE
    H = SSM_HEADS
    GW = hg * P

    @pl.when(pl.program_id(1) == 0)
    def _():
        h_scr[...] = jnp.zeros_like(h_scr)

    row_i = lax.broadcasted_iota(jnp.int32, (L, L), 0)
    col_i = lax.broadcasted_iota(jnp.int32, (L, L), 1)
    lower = col_i <= row_i
    upper = col_i >= row_i
    lane_g = lax.broadcasted_iota(jnp.int32, (L, GW), 1)

    sg = sg_ref[0, 0]
    sgt = sg.T
    for g in range(G):
        xg = x_ref[0, :, g * GW:(g + 1) * GW]
        bg = b_ref[0, :, g * N:(g + 1) * N]
        cg = c_ref[0, :, g * N:(g + 1) * N]
        cb = _dot_nt(cg, bg)
        ms, xms, efc, ebc, wcols, dcols = [], [], [], [], [], []
        for j in range(hg):
            hd = g * hg + j
            af_r, ab_r = sg[hd:hd + 1], sg[H + hd:H + hd + 1]
            dtf_r, dtb_r = sg[2 * H + hd:2 * H + hd + 1], sg[3 * H + hd:3 * H + hd + 1]
            af_c, ab_c = sgt[:, hd:hd + 1], sgt[:, H + hd:H + hd + 1]
            dtf_c = sgt[:, 2 * H + hd:2 * H + hd + 1]
            fm = jnp.exp(jnp.where(lower, af_c - af_r, NEG_INF)) * dtf_r
            bm = jnp.exp(jnp.where(upper, ab_c - ab_r, NEG_INF)) * dtb_r
            ms.append((cb * (fm + bm)).astype(BF16))
            xms.append(jnp.where((lane_g >= j * P) & (lane_g < (j + 1) * P), xg, jnp.zeros_like(xg)))
            efc.append(jnp.exp(af_c))
            ebc.append(jnp.exp(ab_c))
            tot = af_r[:, L - 1:L]
            wcols.append(dtf_c * jnp.exp(tot - af_c))
            dcols.append(jnp.exp(tot))
        y = _dot(jnp.concatenate(ms, axis=1), jnp.concatenate(xms, axis=0))
        h_old = h_scr[g]
        y = y + _lane_groups(efc, P) * _dot(cg, h_old.astype(BF16))
        y = y + _lane_groups(ebc, P) * _dot(cg, hb_ref[0, 0, g])
        xgf = xg.astype(F32)
        y = y + dsk_ref[:, g * GW:(g + 1) * GW] * xgf
        xs = (xgf * _lane_groups(wcols, P)).astype(BF16)
        h_scr[g] = _lane_groups(dcols, P) * h_old + _dot_tn(bg, xs)
        zg = z_ref[0, :, g * GW:(g + 1) * GW].astype(F32)
        y = y * (zg * _sigmoid(zg))
        y = y * lax.rsqrt(jnp.mean(y * y, axis=1, keepdims=True) + RMS_EPS)
        y_ref[0, :, g * GW:(g + 1) * GW] = (y * nw_ref[:, g * GW:(g + 1) * GW]).astype(y_ref.dtype)


def _ssd_main(xc, pa3, sg, hb, dskip_row, norm_w):
    B, S, _ = xc.shape
    nc = S // CHUNK
    G, N, W = SSM_GROUPS, SSM_STATE, SSM_WIDTH
    z_blk = (2 * ML_QK + 2 * ML_WIDTH) // W
    return pl.pallas_call(
        functools.partial(_ssd_main_kernel, L=CHUNK),
        grid=(B, nc),
        in_specs=[
            pl.BlockSpec((1, CHUNK, W), lambda b, c: (b, c, 0)),
            pl.BlockSpec((1, CHUNK, G * N), lambda b, c: (b, c, W // (G * N))),
            pl.BlockSpec((1, CHUNK, G * N), lambda b, c: (b, c, W // (G * N) + 1)),
            pl.BlockSpec((1, CHUNK, W), lambda b, c: (b, c, z_blk)),
            pl.BlockSpec((1, 1, 4 * SSM_HEADS, CHUNK), lambda b, c: (b, c, 0, 0)),
            pl.BlockSpec((1, 1, G, N, W // G), lambda b, c: (b, c, 0, 0, 0)),
            pl.BlockSpec((1, W), lambda b, c: (0, 0)),
            pl.BlockSpec((1, W), lambda b, c: (0, 0)),
        ],
        out_specs=pl.BlockSpec((1, CHUNK, W), lambda b, c: (b, c, 0)),
        out_shape=jax.ShapeDtypeStruct((B, S, W), BF16),
        scratch_shapes=[pltpu.VMEM((G, N, W // G), F32)],
        compiler_params=_cparams(("parallel", "arbitrary")),
        name="ssd_main",
    )(xc, xc, xc, pa3, sg, hb, dskip_row, norm_w)


def _pack_bf16_pairs(v):
    n = v.shape[1] // 2
    hi = lax.bitcast_convert_type(v[:, :n].astype(BF16).astype(F32), jnp.uint32)
    lo = lax.bitcast_convert_type(v[:, n:].astype(BF16).astype(F32), jnp.uint32)
    return (hi & jnp.uint32(0xFFFF0000)) | (lo >> 16)


def _unpack_bf16_pairs(u):
    hi = lax.bitcast_convert_type(u & jnp.uint32(0xFFFF0000), F32)
    lo = lax.bitcast_convert_type(u << 16, F32)
    return hi, lo


def _layer_norm(v, g, b):
    mu = jnp.mean(v, axis=1, keepdims=True)
    d = v - mu
    var = jnp.mean(d * d, axis=1, keepdims=True)
    return d * lax.rsqrt(var + LN_EPS) * g + b


def _outproj_kernel(yml_ref, yssm_ref, x_ref, wo_ref, g_ref, b_ref, wr_ref, h_ref, hp_ref, sc_ref):
    mix = _dot(yml_ref[...], wo_ref[0:ML_WIDTH, :]) + _dot(yssm_ref[...], wo_ref[ML_WIDTH:, :])
    h = _layer_norm(ALPHA * x_ref[...] + mix, g_ref[...], b_ref[...])
    h_ref[...] = h
    hp = _pack_bf16_pairs(h)
    q = hp.shape[1] // 2
    hp_ref[0] = hp[:, :q]
    hp_ref[1] = hp[:, q:]
    sc_ref[...] = _sigmoid(_dot(h.astype(BF16), wr_ref[...]))


def _outproj(yml, yssm, x2d, w_out, ln_g, ln_b, w_router, tm=512):
    T, D = x2d.shape
    tm = min(tm, T)
    row = lambda i: (i, 0)
    fixed = lambda i: (0, 0)
    return pl.pallas_call(
        _outproj_kernel,
        grid=(T // tm,),
        in_specs=[
            pl.BlockSpec((tm, ML_WIDTH), row),
            pl.BlockSpec((tm, SSM_WIDTH), row),
            pl.BlockSpec((tm, D), row),
            pl.BlockSpec((ML_WIDTH + SSM_WIDTH, D), fixed),
            pl.BlockSpec((1, D), fixed),
            pl.BlockSpec((1, D), fixed),
            pl.BlockSpec((D, 128), fixed),
        ],
        out_specs=[pl.BlockSpec((tm, D), row), pl.BlockSpec((2, tm, D // 4), lambda i: (0, i, 0)),
                   pl.BlockSpec((tm, 128), row)],
        out_shape=[
            jax.ShapeDtypeStruct((T, D), F32),
            jax.ShapeDtypeStruct((2, T, D // 4), jnp.uint32),
            jax.ShapeDtypeStruct((T, 128), F32),
        ],
        compiler_params=_cparams(("parallel",)),
        name="out_proj_ln1_router",
    )(yml, yssm, x2d, w_out, ln_g, ln_b, w_router)


def _route_kernel(sc_ref, rb_ref, idx_ref, w_ref, rank_ref, cnt_ref, carry_scr, *, Tt):
    E, NG = N_EXPERTS, N_EXPERT_GROUPS
    EG = E // NG

    @pl.when(pl.program_id(0) == 0)
    def _():
        carry_scr[...] = jnp.zeros_like(carry_scr)

    s = sc_ref[...].T[0:E]
    s3 = s.reshape(NG, EG, Tt)
    ch3 = (s + rb_ref[:, 0:1]).reshape(NG, EG, Tt)
    e_in = lax.broadcasted_iota(jnp.int32, (NG, EG, Tt), 1)
    g_in = lax.broadcasted_iota(jnp.int32, (NG, EG, Tt), 0)
    f_i = g_in * EG + e_in
    m1 = jnp.max(ch3, axis=1, keepdims=True)
    i1 = jnp.min(jnp.where(ch3 == m1, e_in, EG), axis=1, keepdims=True)
    m2 = jnp.max(jnp.where(e_in == i1, NEG_INF, ch3), axis=1, keepdims=True)
    gs = m1 + m2
    g_i = lax.broadcasted_iota(jnp.int32, (NG, 1, Tt), 0)
    gsel = jnp.zeros((NG, 1, Tt), jnp.bool_)
    cur = gs
    for _ in range(TOPK_GROUPS):
        mx = jnp.max(cur, axis=0, keepdims=True)
        pick = jnp.min(jnp.where(cur == mx, g_i, NG), axis=0, keepdims=True)
        hit = g_i == pick
        gsel = gsel | hit
        cur = jnp.where(hit, NEG_INF, cur)
    cur = jnp.where(gsel, ch3, NEG_INF)
    selmask = jnp.zeros((NG, EG, Tt), jnp.bool_)
    idxs, ws = [], []
    for _ in range(TOP_K):
        mx = jnp.max(jnp.max(cur, axis=1, keepdims=True), axis=0, keepdims=True)
        cand = jnp.where(cur == mx, f_i, E)
        pick = jnp.min(jnp.min(cand, axis=1, keepdims=True), axis=0, keepdims=True)
        hit = f_i == pick
        wk = jnp.sum(jnp.sum(jnp.where(hit, s3, 0.0), axis=1, keepdims=True), axis=0, keepdims=True)
        idxs.append(pick.reshape(1, Tt))
        ws.append(wk.reshape(1, Tt))
        selmask = selmask | hit
        cur = jnp.where(hit, NEG_INF, cur)
    wsum = ws[0]
    for k in range(1, TOP_K):
        wsum = wsum + ws[k]
    r = lax.broadcasted_iota(jnp.int32, (Tt, Tt), 0)
    c = lax.broadcasted_iota(jnp.int32, (Tt, Tt), 1)
    ustrict = (r < c).astype(BF16)
    self32 = jnp.where(selmask, 1.0, 0.0).reshape(E, Tt)
    excl = _dot(self32.astype(BF16), ustrict) + carry_scr[:, 0:1]
    excl3 = excl.reshape(NG, EG, Tt)
    for k in range(TOP_K):
        hit = f_i == idxs[k].reshape(1, 1, Tt)
        rk = jnp.sum(jnp.sum(jnp.where(hit, excl3, 0.0), axis=1, keepdims=True), axis=0, keepdims=True)
        idx_ref[k:k + 1, :] = idxs[k]
        w_ref[k:k + 1, :] = ws[k] / wsum * ROUTED_SCALE
        rank_ref[k:k + 1, :] = rk.reshape(1, Tt).astype(jnp.int32)
    carry_scr[...] = carry_scr[...] + jnp.sum(self32, axis=1, keepdims=True)
    cnt_ref[...] = carry_scr[...]


def _route(scores, rbias, Tt=512):
    T = scores.shape[0]
    kt = lambda i: (0, i)
    return pl.pallas_call(
        functools.partial(_route_kernel, Tt=Tt),
        grid=(T // Tt,),
        in_specs=[pl.BlockSpec((Tt, 128), lambda i: (i, 0)), pl.BlockSpec((N_EXPERTS, 128), lambda i: (0, 0))],
        out_specs=[
            pl.BlockSpec((TOP_K, Tt), kt),
            pl.BlockSpec((TOP_K, Tt), kt),
            pl.BlockSpec((TOP_K, Tt), kt),
            pl.BlockSpec((N_EXPERTS, 128), lambda i: (0, 0)),
        ],
        out_shape=[
            jax.ShapeDtypeStruct((TOP_K, T), jnp.int32),
            jax.ShapeDtypeStruct((TOP_K, T), F32),
            jax.ShapeDtypeStruct((TOP_K, T), jnp.int32),
            jax.ShapeDtypeStruct((N_EXPERTS, 128), F32),
        ],
        scratch_shapes=[pltpu.VMEM((N_EXPERTS, 128), F32)],
        compiler_params=_cparams(("arbitrary",)),
        name="route_topk",
    )(scores, rbias)


def _dest_kernel(ps_ref, idx_ref, rank_ref, o_ref):
    idx = idx_ref[...]
    start = jnp.zeros(idx.shape, jnp.int32)
    for e in range(N_EXPERTS):
        start = jnp.where(idx == e, ps_ref[e], start)
    o_ref[...] = start + rank_ref[...]


def _dest(pstarts, idx, rank, Tt=2048):
    K, T = idx.shape
    Tt = min(Tt, T)
    blk = pl.BlockSpec((K, Tt), lambda i, ps: (0, i))
    return pl.pallas_call(
        _dest_kernel,
        grid_spec=pltpu.PrefetchScalarGridSpec(num_scalar_prefetch=1, grid=(T // Tt,), in_specs=[blk, blk],
                                               out_specs=blk),
        out_shape=jax.ShapeDtypeStruct((K, T), jnp.int32),
        compiler_params=_cparams(("parallel",)),
        name="dispatch_slots",
    )(pstarts, idx, rank)


def _sc_mesh():
    return plsc.VectorSubcoreMesh(core_axis_name="c", subcore_axis_name="s")


def _sc_scatter_rows(src, idx, n_out, win=SC_WINDOW):
    n = idx.shape[0]
    T, W = src.shape
    nt = T // win

    @functools.partial(pl.kernel, out_type=jax.ShapeDtypeStruct((n_out, W), src.dtype), mesh=_sc_mesh(),
                       scratch_types=[], name="sc_scatter_rows")
    def k(x_hbm, i_hbm, o_hbm):
        def body(x_vmem, i_vmem):
            pltpu.sync_copy(x_vmem, o_hbm.at[i_vmem.at[0]])

        pltpu.emit_pipeline(
            body,
            grid=(n // win,),
            in_specs=[pl.BlockSpec((win, W), lambda i: (i % nt, 0)), pl.BlockSpec((1, win), lambda i: (0, i))],
            out_specs=[],
            core_axis_name=("c", "s"),
            dimension_semantics=(pltpu.PARALLEL,),
        )(x_hbm, i_hbm)

    return k(src, idx.reshape(1, n))


def _sc_gather_rows(table, idx, win=SC_WINDOW):
    n = idx.shape[0]
    W = table.shape[1]

    @functools.partial(pl.kernel, out_type=jax.ShapeDtypeStruct((n, W), table.dtype), mesh=_sc_mesh(),
                       scratch_types=[], name="sc_gather_rows")
    def k(t_hbm, i_hbm, o_hbm):
        def body(i_vmem, o_vmem):
            pltpu.sync_copy(t_hbm.at[i_vmem.at[0]], o_vmem)

        pltpu.emit_pipeline(
            body,
            grid=(n // win,),
            in_specs=[pl.BlockSpec((1, win), lambda i: (0, i))],
            out_specs=[pl.BlockSpec((win, W), lambda i: (i, 0))],
            core_axis_name=("c", "s"),
            dimension_semantics=(pltpu.PARALLEL,),
        )(i_hbm, o_hbm)

    return k(table, idx.reshape(1, n))


def _expert_kernel(be_ref, nu_ref, xa_ref, xb_ref, wg_ref, wu_ref, wd_ref, o_ref):
    half = D_MODEL // 2

    @pl.when(pl.program_id(0) < nu_ref[0])
    def _():
        hi, lo = _unpack_bf16_pairs(jnp.concatenate([xa_ref[...], xb_ref[...]], axis=1))
        hi, lo = hi.astype(BF16), lo.astype(BF16)
        gt = _dot(hi, wg_ref[0, 0:half, :]) + _dot(lo, wg_ref[0, half:, :])
        up = _dot(hi, wu_ref[0, 0:half, :]) + _dot(lo, wu_ref[0, half:, :])
        hmid = (gt * _sigmoid(gt) * up).astype(BF16)
        packed = _pack_bf16_pairs(_dot(hmid, wd_ref[0]))
        q = packed.shape[1] // 2
        o_ref[0] = packed[:, :q]
        o_ref[1] = packed[:, q:]

    @pl.when(pl.program_id(0) >= nu_ref[0])
    def _():
        o_ref[...] = jnp.zeros_like(o_ref)


def _experts(xs_a, xs_b, block_expert, n_used, w_gate, w_up, w_down, blk=MOE_BLOCK):
    P, quarter = xs_a.shape
    nb = P // blk
    D, F = D_MODEL, D_EXPERT
    cur = lambda i, be, nu: jnp.minimum(i, nu[0] - 1)
    return pl.pallas_call(
        _expert_kernel,
        grid_spec=pltpu.PrefetchScalarGridSpec(
            num_scalar_prefetch=2,
            grid=(nb,),
            in_specs=[
                pl.BlockSpec((blk, quarter), lambda i, be, nu: (cur(i, be, nu), 0)),
                pl.BlockSpec((blk, quarter), lambda i, be, nu: (cur(i, be, nu), 0)),
                pl.BlockSpec((1, D, F), lambda i, be, nu: (be[cur(i, be, nu)], 0, 0)),
                pl.BlockSpec((1, D, F), lambda i, be, nu: (be[cur(i, be, nu)], 0, 0)),
                pl.BlockSpec((1, F, D), lambda i, be, nu: (be[cur(i, be, nu)], 0, 0)),
            ],
            out_specs=pl.BlockSpec((2, blk, quarter), lambda i, be, nu: (0, i, 0)),
        ),
        out_shape=jax.ShapeDtypeStruct((2, P, quarter), jnp.uint32),
        compiler_params=_cparams(("arbitrary",)),
        name="expert_mlp",
    )(block_expert, n_used, xs_a, xs_b, w_gate, w_up, w_down)


def _final_kernel(h_ref, ga_ref, gb_ref, wt_ref, wsg_ref, wsu_ref, wsd_ref, g_ref, b_ref, o_ref):
    h = h_ref[...]
    hb = h.astype(BF16)
    gt = _dot(hb, wsg_ref[...])
    up = _dot(hb, wsu_ref[...])
    shared = _dot((gt * _sigmoid(gt) * up).astype(BF16), wsd_ref[...])
    acc_hi = None
    for k in range(TOP_K):
        hi, lo = _unpack_bf16_pairs(jnp.concatenate([ga_ref[k], gb_ref[k]], axis=1))
        wk = wt_ref[:, k:k + 1]
        acc_hi = wk * hi if acc_hi is None else acc_hi + wk * hi
        acc_lo = wk * lo if k == 0 else acc_lo + wk * lo
    routed = jnp.concatenate([acc_hi, acc_lo], axis=1)
    o_ref[...] = _layer_norm(ALPHA * h + (routed + shared), g_ref[...], b_ref[...])


def _final(h, gk_a, gk_b, wt, ws_gate, ws_up, ws_down, ln_g, ln_b, tm=512):
    T, D = h.shape
    tm = min(tm, T)
    F = D_EXPERT
    fixed = lambda i: (0, 0)
    return pl.pallas_call(
        _final_kernel,
        grid=(T // tm,),
        in_specs=[
            pl.BlockSpec((tm, D), lambda i: (i, 0)),
            pl.BlockSpec((TOP_K, tm, D // 4), lambda i: (0, i, 0)),
            pl.BlockSpec((TOP_K, tm, D // 4), lambda i: (0, i, 0)),
            pl.BlockSpec((tm, TOP_K), lambda i: (i, 0)),
            pl.BlockSpec((D, F), fixed),
            pl.BlockSpec((D, F), fixed),
            pl.BlockSpec((F, D), fixed),
            pl.BlockSpec((1, D), fixed),
            pl.BlockSpec((1, D), fixed),
        ],
        out_specs=pl.BlockSpec((tm, D), lambda i: (i, 0)),
        out_shape=jax.ShapeDtypeStruct((T, D), F32),
        compiler_params=_cparams(("parallel",)),
        name="shared_combine_ln2",
    )(h, gk_a, gk_b, wt, ws_gate, ws_up, ws_down, ln_g, ln_b)


def _mixers(x, w_in, ml_i_bias, ml_f_bias, ml_norm_w, conv_w, conv_b, dt_bias, a_log, d_skip, ssm_norm_w):
    B, S, D = x.shape
    T = B * S
    nc = S // CHUNK
    H = ML_HEADS
    o0 = 2 * ML_QK + 2 * ML_WIDTH
    g0, z0 = o0, o0 + 4 * H
    x0 = z0 + SSM_WIDTH
    d0 = x0 + SSM_CONV_DIM
    w_big = jnp.concatenate([w_in[:, :o0], w_in[:, z0:d0]], axis=1).astype(BF16)
    w_small = jnp.concatenate(
        [w_in[:, g0:z0], w_in[:, d0:], jnp.zeros((D, PB_COLS - N_GATE_ROWS), w_in.dtype)], axis=1).astype(BF16)
    pa, pb = _proj(x.reshape(T, D), w_big, w_small)
    pa3 = pa.reshape(B, S, PA_COLS)

    g_rows = pb.reshape(B, S, PB_COLS)[:, :, :N_GATE_ROWS].transpose(0, 2, 1).reshape(B, N_GATE_ROWS, nc, CHUNK)
    bias = jnp.concatenate([ml_i_bias, ml_f_bias, dt_bias]).astype(F32)
    bias_rows = jnp.broadcast_to(bias[:, None, None], (N_GATE_ROWS, 1, CHUNK))
    aneg_rows = jnp.broadcast_to((-jnp.exp(a_log.astype(F32)))[:, None, None], (2 * SSM_HEADS, 1, CHUNK))
    prep = _gate_prep(g_rows, bias_rows, aneg_rows)
    mg = jnp.stack([prep[:, 0:H], prep[:, 2 * H:3 * H], prep[:, H:2 * H], prep[:, 3 * H:4 * H]], axis=2)
    mg = jnp.concatenate([mg, jnp.zeros_like(mg)], axis=2).transpose(0, 1, 3, 2, 4)
    sg = prep[:, 16:80].transpose(0, 2, 1, 3)

    y_ml = _mlstm(pa3, mg, ml_norm_w.astype(F32).reshape(1, ML_WIDTH))

    Lc = 512 if S % 512 == 0 else CHUNK
    nt = S // Lc
    pad = CONV_K // 2
    xbc0 = PA_COLS - SSM_CONV_DIM
    pa4 = pa3.reshape(B, nt, Lc, PA_COLS)
    top = pa4[:, :, Lc - pad:, xbc0:]
    bot = pa4[:, :, :pad, xbc0:]
    zrow = jnp.zeros((B, 1, pad, SSM_CONV_DIM), pa.dtype)
    top = jnp.concatenate([zrow, top[:, :-1]], axis=1)
    bot = jnp.concatenate([bot[:, 1:], zrow], axis=1)
    halo = jnp.concatenate([top, bot, jnp.zeros((B, nt, 8 - 2 * pad, SSM_CONV_DIM), pa.dtype)], axis=2)
    conv_w8 = jnp.concatenate([conv_w.astype(F32), jnp.zeros((8 - CONV_K, SSM_CONV_DIM), F32)], axis=0)
    xc = _conv(pa3, halo, conv_w8, conv_b.astype(F32).reshape(1, SSM_CONV_DIM), Lc=Lc)

    hb = _ssd_bwd(xc, sg)
    dsk = jnp.repeat(d_skip.astype(F32), SSM_HEADDIM).reshape(1, SSM_WIDTH)
    y_ssm = _ssd_main(xc, pa3, sg, hb, dsk, ssm_norm_w.astype(F32).reshape(1, SSM_WIDTH))
    return y_ml.reshape(T, ML_WIDTH), y_ssm.reshape(T, SSM_WIDTH)


def _moe_dispatch(idx, rank, counts, T, blk):
    E = N_EXPERTS
    M = T * TOP_K
    nb = (M + E * (blk - 1)) // blk
    pcounts = (counts + blk - 1) // blk * blk
    pends = jnp.cumsum(pcounts)
    pstarts = (pends - pcounts).astype(jnp.int32)
    dest = _dest(pstarts, idx, rank)
    block_start = jnp.arange(nb, dtype=jnp.int32) * blk
    block_expert = jnp.minimum(jnp.sum(pends[None, :] <= block_start[:, None], axis=1), E - 1).astype(jnp.int32)
    n_used = (pends[-1] // blk).astype(jnp.int32).reshape(1)
    return dest, nb * blk, block_expert, n_used


def _layer(x, w_in, ml_i_bias, ml_f_bias, ml_norm_w, conv_w, conv_b, dt_bias, a_log, d_skip, ssm_norm_w,
           w_out, ln1_g, ln1_b, w_router, router_bias, w_exp_gate, w_exp_up, w_exp_down,
           w_sh_gate, w_sh_up, w_sh_down, ln2_g, ln2_b):
    B, S, D = x.shape
    T = B * S
    y_ml, y_ssm = _mixers(x, w_in, ml_i_bias, ml_f_bias, ml_norm_w, conv_w, conv_b, dt_bias, a_log, d_skip,
                          ssm_norm_w)
    w_r = jnp.concatenate([w_router, jnp.zeros((D, 128 - N_EXPERTS), w_router.dtype)], axis=1).astype(BF16)
    h, h_pk, scores = _outproj(y_ml, y_ssm, x.reshape(T, D), w_out.astype(BF16),
                               ln1_g.astype(F32).reshape(1, D), ln1_b.astype(F32).reshape(1, D), w_r)
    rbias = jnp.broadcast_to(router_bias.astype(F32)[:, None], (N_EXPERTS, 128))
    idx, wgt, rank, cnt = _route(scores, rbias)
    counts = cnt[:, 0].astype(jnp.int32)
    dest, n_slots, block_expert, n_used = _moe_dispatch(idx, rank, counts, T, MOE_BLOCK)
    dest_flat = dest.reshape(-1)
    xs_a = _sc_scatter_rows(h_pk[0], dest_flat, n_slots)
    xs_b = _sc_scatter_rows(h_pk[1], dest_flat, n_slots)
    out_sorted = _experts(xs_a, xs_b, block_expert, n_used, w_exp_gate.astype(BF16), w_exp_up.astype(BF16),
                          w_exp_down.astype(BF16))
    gk_a = _sc_gather_rows(out_sorted[0], dest_flat).reshape(TOP_K, T, D // 4)
    gk_b = _sc_gather_rows(out_sorted[1], dest_flat).reshape(TOP_K, T, D // 4)
    out = _final(h, gk_a, gk_b, wgt.T, w_sh_gate.astype(BF16), w_sh_up.astype(BF16),
                 w_sh_down.astype(BF16), ln2_g.astype(F32).reshape(1, D), ln2_b.astype(F32).reshape(1, D))
    return out.reshape(B, S, D)


def kernel(x, w_in, ml_i_bias, ml_f_bias, ml_norm_w, conv_w, conv_b, dt_bias, a_log, d_skip, ssm_norm_w, w_out,
           ln1_g, ln1_b, w_router, router_bias, w_exp_gate, w_exp_up, w_exp_down, w_sh_gate, w_sh_up, w_sh_down,
           ln2_g, ln2_b):
    for l in range(w_in.shape[0]):
        x = _layer(x, w_in[l], ml_i_bias[l], ml_f_bias[l], ml_norm_w[l], conv_w[l], conv_b[l], dt_bias[l],
                   a_log[l], d_skip[l], ssm_norm_w[l], w_out[l], ln1_g[l], ln1_b[l], w_router[l],
                   router_bias[l], w_exp_gate[l], w_exp_up[l], w_exp_down[l], w_sh_gate[l], w_sh_up[l],
                   w_sh_down[l], ln2_g[l], ln2_b[l])
    return x
```

```python
import functools

import jax
import jax.numpy as jnp
from jax import lax
from jax.experimental import pallas as pl
from jax.experimental.pallas import tpu as pltpu
from jax.experimental.pallas import tpu_sc as plsc

D_MODEL = 1024
ML_HEADS = 4
ML_DV = 256
ML_DQK = 128
ML_QK = ML_HEADS * ML_DQK
ML_WIDTH = ML_HEADS * ML_DV
SSM_WIDTH = 1024
SSM_HEADDIM = 64
SSM_HEADS = 16
SSM_GROUPS = 4
SSM_STATE = 128
SSM_CONV_DIM = SSM_WIDTH + 2 * SSM_GROUPS * SSM_STATE
CONV_K = 5
CHUNK = 128
N_EXPERTS = 64
TOP_K = 8
N_EXPERT_GROUPS = 8
TOPK_GROUPS = 4
D_EXPERT = 256
ROUTED_SCALE = 2.5
LN_EPS = 1e-5
RMS_EPS = 1e-5
DEPTH = 1
ALPHA = (2 * DEPTH) ** 0.25

PA_XBC0 = 0
PA_V0 = PA_XBC0 + SSM_CONV_DIM
PA_O0 = PA_V0 + ML_WIDTH
PA_Z0 = PA_O0 + ML_WIDTH
PA_Q0 = PA_Z0 + SSM_WIDTH
PA_COLS = PA_Q0 + ML_QK
PB_COLS = 128
N_GATE_ROWS = 2 * ML_HEADS + 2 * ML_HEADS + 2 * SSM_HEADS
PREP_SSM0 = 32
N_PREP_ROWS = PREP_SSM0 + 10 * SSM_HEADS

MOE_BLOCK = 512
SC_WINDOW = 128
VMEM_LIMIT = 56 * 1024 * 1024

BF16 = jnp.bfloat16
F32 = jnp.float32
NEG_INF = float("-inf")


def _cparams(sem, vmem=VMEM_LIMIT):
    return pltpu.CompilerParams(dimension_semantics=sem, vmem_limit_bytes=vmem)


def _dot(a, b):
    return jnp.dot(a, b, preferred_element_type=F32)


def _dot_nt(a, b):
    return lax.dot_general(a, b, (((1,), (1,)), ((), ())), preferred_element_type=F32)


def _dot_tn(a, b):
    return lax.dot_general(a, b, (((0,), (0,)), ((), ())), preferred_element_type=F32)


def _sigmoid(x):
    return 1.0 / (1.0 + jnp.exp(-x))


def _softplus(x):
    return jnp.maximum(x, 0.0) + jnp.log1p(jnp.exp(-jnp.abs(x)))


def _proj_kernel(x_ref, wb_ref, ws_ref, wkt_ref, oa_ref, ob_ref, kt_ref, xb_scr, *, L):
    @pl.when(pl.program_id(1) == 0)
    def _():
        xb = x_ref[...].astype(BF16)
        xb_scr[...] = xb
        ob_ref[...] = _dot(xb, ws_ref[...])
        kt = _dot_nt(wkt_ref[...], xb).astype(kt_ref.dtype)
        for c in range(kt_ref.shape[0]):
            kt_ref[c] = kt[:, c * L:(c + 1) * L]

    oa_ref[...] = _dot(xb_scr[...], wb_ref[...]).astype(oa_ref.dtype)


def _proj(x2d, w_big, w_small, w_kt, tm=1024, tn=1408):
    T, D = x2d.shape
    N = w_big.shape[1]
    tm = min(tm, T)
    L = CHUNK
    return pl.pallas_call(
        functools.partial(_proj_kernel, L=L),
        grid=(T // tm, N // tn),
        in_specs=[
            pl.BlockSpec((tm, D), lambda i, j: (i, 0)),
            pl.BlockSpec((D, tn), lambda i, j: (0, j)),
            pl.BlockSpec((D, PB_COLS), lambda i, j: (0, 0)),
            pl.BlockSpec((ML_QK, D), lambda i, j: (0, 0)),
        ],
        out_specs=[
            pl.BlockSpec((tm, tn), lambda i, j: (i, j)),
            pl.BlockSpec((tm, PB_COLS), lambda i, j: (i, 0)),
            pl.BlockSpec((tm // L, ML_QK, L), lambda i, j: (i, 0, 0)),
        ],
        out_shape=[jax.ShapeDtypeStruct((T, N), BF16), jax.ShapeDtypeStruct((T, PB_COLS), F32),
                   jax.ShapeDtypeStruct((T // L, ML_QK, L), BF16)],
        scratch_shapes=[pltpu.VMEM((tm, D), BF16)],
        compiler_params=_cparams(("parallel", "arbitrary")),
        name="in_proj",
    )(x2d, w_big, w_small, w_kt)


def _split3_dot(x, u):
    hi = x.astype(BF16)
    r1 = x - hi.astype(F32)
    mid = r1.astype(BF16)
    lo = (r1 - mid.astype(F32)).astype(BF16)
    return _dot(hi, u) + _dot(mid, u) + _dot(lo, u)


def _gate_kernel(g_ref, bias_ref, aneg_ref, o_ref, *, nc, L):
    g = g_ref[0] + bias_ref[...]
    nh = ML_HEADS
    ig = g[0:2 * nh]
    fpre = g[2 * nh:4 * nh]
    lf = -_softplus(-fpre)
    dt = _softplus(g[4 * nh:])
    dA = dt * aneg_ref[...]
    r = lax.broadcasted_iota(jnp.int32, (L, L), 0)
    c = lax.broadcasted_iota(jnp.int32, (L, L), 1)
    u_pre = (r <= c).astype(BF16)
    u_suf = (r >= c).astype(BF16)
    nf = nh + SSM_HEADS
    xf = jnp.concatenate([lf[0:nh], dA[0:SSM_HEADS]], axis=0).reshape(nf * nc, L)
    xb = jnp.concatenate([lf[nh:], dA[SSM_HEADS:]], axis=0).reshape(nf * nc, L)
    cf = _split3_dot(xf, u_pre).reshape(nf, nc, L)
    cb = _split3_dot(xb, u_suf).reshape(nf, nc, L)
    totf = jnp.broadcast_to(cf[:, :, L - 1:L], cf.shape)
    totb = jnp.broadcast_to(cb[:, :, 0:1], cb.shape)
    o_ref[0, 0:nh] = cf[0:nh]
    o_ref[0, nh:2 * nh] = cb[0:nh]
    o_ref[0, 2 * nh:4 * nh] = ig
    o_ref[0, 4 * nh:5 * nh] = totf[0:nh] - cf[0:nh] + ig[0:nh]
    o_ref[0, 5 * nh:6 * nh] = totb[0:nh] - cb[0:nh] + ig[nh:]
    lane = lax.broadcasted_iota(jnp.int32, (nh * nc, L), 1)
    pf = (ig[0:nh] - cf[0:nh]).reshape(nh * nc, L)
    pb = (ig[nh:] - cb[0:nh]).reshape(nh * nc, L)
    sh = 1
    while sh < L:
        pf = jnp.where(lane >= sh, jnp.maximum(pf, pltpu.roll(pf, sh, 1)), pf)
        pb = jnp.where(lane < L - sh, jnp.maximum(pb, pltpu.roll(pb, L - sh, 1)), pb)
        sh *= 2
    o_ref[0, 6 * nh:7 * nh] = cf[0:nh] + pf.reshape(nh, nc, L)
    o_ref[0, 7 * nh:8 * nh] = cb[0:nh] + pb.reshape(nh, nc, L)
    H = SSM_HEADS
    af, ab, dtf, dtb = cf[nh:], cb[nh:], dt[0:H], dt[H:]
    s0 = PREP_SSM0
    o_ref[0, s0:s0 + H] = af
    o_ref[0, s0 + H:s0 + 2 * H] = ab
    o_ref[0, s0 + 2 * H:s0 + 3 * H] = dtf
    o_ref[0, s0 + 3 * H:s0 + 4 * H] = dtb
    o_ref[0, s0 + 4 * H:s0 + 5 * H] = jnp.exp(af)
    o_ref[0, s0 + 5 * H:s0 + 6 * H] = jnp.exp(ab)
    o_ref[0, s0 + 6 * H:s0 + 7 * H] = dtf * jnp.exp(totf[nh:] - af)
    o_ref[0, s0 + 7 * H:s0 + 8 * H] = dtb * jnp.exp(totb[nh:] - ab)
    o_ref[0, s0 + 8 * H:s0 + 9 * H] = jnp.exp(totf[nh:])
    o_ref[0, s0 + 9 * H:s0 + 10 * H] = jnp.exp(totb[nh:])


def _gate_prep(g_rows, bias_rows, aneg_rows):
    B, R, nc, L = g_rows.shape
    return pl.pallas_call(
        functools.partial(_gate_kernel, nc=nc, L=L),
        grid=(B,),
        in_specs=[
            pl.BlockSpec((1, R, nc, L), lambda b: (b, 0, 0, 0)),
            pl.BlockSpec((R, 1, L), lambda b: (0, 0, 0)),
            pl.BlockSpec((2 * SSM_HEADS, 1, L), lambda b: (0, 0, 0)),
        ],
        out_specs=pl.BlockSpec((1, N_PREP_ROWS, nc, L), lambda b: (b, 0, 0, 0)),
        out_shape=jax.ShapeDtypeStruct((B, N_PREP_ROWS, nc, L), F32),
        compiler_params=_cparams(("parallel",)),
        name="gate_prep",
    )(g_rows, bias_rows, aneg_rows)


def _mlstm_kernel(q_ref, kt_ref, v_ref, o_ref, mg_ref, mc_ref, nw_ref, y_ref, hf_scr, hb_scr, c_scr, n_scr, m_scr,
                  *, nc, L):
    row_i = lax.broadcasted_iota(jnp.int32, (L, L), 0)
    col_i = lax.broadcasted_iota(jnp.int32, (L, L), 1)
    ones = jnp.ones((L, L), BF16)

    c_scr[...] = jnp.zeros_like(c_scr)
    n_scr[...] = jnp.zeros_like(n_scr)
    m_scr[...] = jnp.full_like(m_scr, NEG_INF)

    def chunk(c, d):
        sl = pl.ds(pl.multiple_of(c * L, L), L)
        q = q_ref[0, sl, :]
        kt = kt_ref[0, c, 0]
        vo = jnp.concatenate([v_ref[0, sl, :], ones], axis=1)
        g8 = mg_ref[0, 0, c]
        ct = mc_ref[0, 0, sl, :]
        b_r, i_r, a_r = g8[2 * d:2 * d + 1], g8[2 * d + 1:2 * d + 2], g8[4 + d:5 + d]
        b_c = jnp.broadcast_to(ct[:, 2 * d:2 * d + 1], (L, L))
        mi_c = jnp.broadcast_to(ct[:, 6 + d:7 + d], (L, L))
        if d == 0:
            gtot = b_r[:, L - 1:L]
            mask = col_i <= row_i
        else:
            gtot = b_r[:, 0:1]
            mask = col_i >= row_i
        c_ref, n_ref, m_ref = c_scr.at[d], n_scr.at[d], m_scr.at[d]
        m_prev = m_ref[...]
        m_loc = jnp.max(a_r, axis=1, keepdims=True)
        kwt = (kt.astype(F32) * jnp.exp(a_r - m_loc)).astype(BF16)
        uu = _dot(kwt, vo)
        qq = _dot(q, jnp.concatenate([kt, c_ref[...].astype(BF16), n_ref[...].astype(BF16)], axis=1))
        s, q_c, q_n = qq[:, 0:L], qq[:, L:L + ML_DV], qq[:, L + ML_DV:]
        dmat = jnp.where(mask, b_c - b_r + i_r, NEG_INF)
        m_inter = b_c + m_prev
        m_t = jnp.maximum(m_inter, mi_c)
        s_mat = s * jnp.exp(dmat - m_t)
        inter_w = jnp.exp(m_inter - m_t)
        ss = _dot(s_mat.astype(BF16), vo)
        den = ss[:, ML_DV:] + inter_w * q_n
        r = 1.0 / jnp.maximum(jnp.abs(den), jnp.exp(-m_t))
        num = ss[:, 0:ML_DV] + jnp.concatenate([inter_w, inter_w], axis=1) * q_c
        h = num * jnp.concatenate([r, r], axis=1)
        m_new = jnp.maximum(gtot + m_prev, m_loc)
        decay = jnp.exp(gtot + m_prev - m_new)
        sc = jnp.exp(m_loc - m_new)
        c_ref[...] = decay * c_ref[...] + sc * uu[:, 0:ML_DV]
        n_ref[...] = decay * n_ref[...] + sc * uu[:, ML_DV:]
        m_ref[...] = m_new
        return sl, h

    def scan_body(i, carry):
        sl, h = chunk(i, 0)
        hf_scr[sl, :] = h
        sl, h = chunk(nc - 1 - i, 1)
        hb_scr[sl, :] = h
        return carry

    lax.fori_loop(0, nc, scan_body, 0, unroll=2)

    def out_body(c, carry):
        sl = pl.ds(pl.multiple_of(c * L, L), L)
        h = hf_scr[sl, :] + hb_scr[sl, :]
        mu = jnp.mean(h, axis=1, keepdims=True)
        d = h - mu
        var = jnp.mean(d * d, axis=1, keepdims=True)
        y = d * lax.rsqrt(var + LN_EPS) * nw_ref[...]
        y = y * _sigmoid(o_ref[0, sl, :].astype(F32))
        y_ref[0, sl, :] = y.astype(y_ref.dtype)
        return carry

    lax.fori_loop(0, nc, out_body, 0, unroll=2)


def _mlstm(pa3, kt5, mg, mc, norm_w):
    B, S, _ = pa3.shape
    nc = S // CHUNK
    H = ML_HEADS
    q_blk, v_blk, o_blk = PA_Q0 // ML_DQK, PA_V0 // ML_DV, PA_O0 // ML_DV
    return pl.pallas_call(
        functools.partial(_mlstm_kernel, nc=nc, L=CHUNK),
        grid=(B, H),
        in_specs=[
            pl.BlockSpec((1, S, ML_DQK), lambda b, h: (b, 0, q_blk + h)),
            pl.BlockSpec((1, nc, 1, ML_DQK, CHUNK), lambda b, h: (b, 0, h, 0, 0)),
            pl.BlockSpec((1, S, ML_DV), lambda b, h: (b, 0, v_blk + h)),
            pl.BlockSpec((1, S, ML_DV), lambda b, h: (b, 0, o_blk + h)),
            pl.BlockSpec((1, 1, nc, 8, CHUNK), lambda b, h: (b, h, 0, 0, 0)),
            pl.BlockSpec((1, 1, S, 8), lambda b, h: (b, h, 0, 0)),
            pl.BlockSpec((1, ML_DV), lambda b, h: (0, h)),
        ],
        out_specs=pl.BlockSpec((1, S, ML_DV), lambda b, h: (b, 0, h)),
        out_shape=jax.ShapeDtypeStruct((B, S, ML_WIDTH), BF16),
        scratch_shapes=[
            pltpu.VMEM((S, ML_DV), F32),
            pltpu.VMEM((S, ML_DV), F32),
            pltpu.VMEM((2, ML_DQK, ML_DV), F32),
            pltpu.VMEM((2, ML_DQK, CHUNK), F32),
            pltpu.VMEM((2, 1, 1), F32),
        ],
        compiler_params=_cparams(("parallel", "parallel")),
        name="mlstm",
    )(pa3, kt5, pa3, pa3, mg, mc, norm_w)


def _conv_kernel(x_ref, halo_ref, w_ref, b_ref, o_ref, ext_scr, *, Lc):
    pad = CONV_K // 2
    halo = halo_ref[0, 0].astype(F32)
    ext_scr[8 - pad:8, :] = halo[0:pad]
    ext_scr[8:8 + Lc, :] = x_ref[0].astype(F32)
    ext_scr[8 + Lc:8 + Lc + pad, :] = halo[pad:2 * pad]
    acc = b_ref[...] + w_ref[0:1, :] * ext_scr[8 - pad:8 - pad + Lc, :]
    for j in range(1, CONV_K):
        acc = acc + w_ref[j:j + 1, :] * ext_scr[8 - pad + j:8 - pad + j + Lc, :]
    o_ref[0] = (acc * _sigmoid(acc)).astype(o_ref.dtype)


def _conv(pa3, halo, conv_w8, conv_b, Lc=512):
    B, S, _ = pa3.shape
    C = SSM_CONV_DIM
    xbc_blk = PA_XBC0 // C
    return pl.pallas_call(
        functools.partial(_conv_kernel, Lc=Lc),
        grid=(B, S // Lc),
        in_specs=[
            pl.BlockSpec((1, Lc, C), lambda b, i: (b, i, xbc_blk)),
            pl.BlockSpec((1, 1, 8, C), lambda b, i: (b, i, 0, 0)),
            pl.BlockSpec((8, C), lambda b, i: (0, 0)),
            pl.BlockSpec((1, C), lambda b, i: (0, 0)),
        ],
        out_specs=pl.BlockSpec((1, Lc, C), lambda b, i: (b, i, 0)),
        out_shape=jax.ShapeDtypeStruct((B, S, C), BF16),
        scratch_shapes=[pltpu.VMEM((Lc + 16, C), F32)],
        compiler_params=_cparams(("parallel", "parallel")),
        name="ssm_conv",
    )(pa3, halo, conv_w8, conv_b)


def _lane_groups(cols, width):
    n = len(cols)
    rows = cols[0].shape[0]
    lane = lax.broadcasted_iota(jnp.int32, (rows, n * width), 1)
    out = jnp.broadcast_to(cols[n - 1], (rows, n * width))
    for j in range(n - 2, -1, -1):
        out = jnp.where(lane < (j + 1) * width, jnp.broadcast_to(cols[j], (rows, n * width)), out)
    return out


def _ssd_bwd_kernel(x_ref, b_ref, sc_ref, dec_ref, hb_ref, h_scr, *, L):
    G, hg, P, N = SSM_GROUPS, SSM_HEADS // SSM_GROUPS, SSM_HEADDIM, SSM_STATE
    H = SSM_HEADS
    GW = hg * P

    @pl.when(pl.program_id(1) == 0)
    def _():
        h_scr[...] = jnp.zeros_like(h_scr)

    sc = sc_ref[0]
    for g in range(G):
        w = _lane_groups([sc[:, 5 * H + g * hg + j:5 * H + g * hg + j + 1] for j in range(hg)], P)
        xs = (x_ref[0, :, g * GW:(g + 1) * GW].astype(F32) * w).astype(BF16)
        st = _dot_tn(b_ref[0, :, g * N:(g + 1) * N], xs)
        h_old = h_scr[g]
        hb_ref[0, 0, g] = h_old.astype(hb_ref.dtype)
        h_scr[g] = dec_ref[0, 0, 1:2, g * GW:(g + 1) * GW] * h_old + st


def _ssd_bwd(xc, sc, decs):
    B, S, _ = xc.shape
    nc = S // CHUNK
    G, N, W = SSM_GROUPS, SSM_STATE, SSM_WIDTH
    return pl.pallas_call(
        functools.partial(_ssd_bwd_kernel, L=CHUNK),
        grid=(B, nc),
        in_specs=[
            pl.BlockSpec((1, CHUNK, W), lambda b, c: (b, nc - 1 - c, 0)),
            pl.BlockSpec((1, CHUNK, G * N), lambda b, c: (b, nc - 1 - c, W // (G * N))),
            pl.BlockSpec((1, CHUNK, 128), lambda b, c: (b, nc - 1 - c, 0)),
            pl.BlockSpec((1, 1, 2, W), lambda b, c: (b, nc - 1 - c, 0, 0)),
        ],
        out_specs=pl.BlockSpec((1, 1, G, N, W // G), lambda b, c: (b, nc - 1 - c, 0, 0, 0)),
        out_shape=jax.ShapeDtypeStruct((B, nc, G, N, W // G), BF16),
        scratch_shapes=[pltpu.VMEM((G, N, W // G), F32)],
        compiler_params=_cparams(("parallel", "arbitrary")),
        name="ssd_bwd_states",
    )(xc, xc, sc, decs)


def _ssd_main_kernel(x_ref, b_ref, c_ref, z_ref, sg_ref, sc_ref, dec_ref, hb_ref, dsk_ref, nw_ref, y_ref, h_scr,
                     *, L):
    G, hg, P, N = SSM_GROUPS, SSM_HEADS // SSM_GROUPS, SSM_HEADDIM, SSM_STATE
    H = SSM_HEADS
    GW = hg * P

    @pl.when(pl.program_id(1) == 0)
    def _():
        h_scr[...] = jnp.zeros_like(h_scr)

    row_i = lax.broadcasted_iota(jnp.int32, (L, L), 0)
    col_i = lax.broadcasted_iota(jnp.int32, (L, L), 1)
    lower = col_i <= row_i
    upper = col_i >= row_i
    lane_g = lax.broadcasted_iota(jnp.int32, (L, GW), 1)

    sg = sg_ref[0, 0]
    sc = sc_ref[0]
    for g in range(G):
        xg = x_ref[0, :, g * GW:(g + 1) * GW]
        bg = b_ref[0, :, g * N:(g + 1) * N]
        cg = c_ref[0, :, g * N:(g + 1) * N]
        cb = _dot_nt(cg, bg)
        ms, xms = [], []
        for j in range(hg):
            hd = g * hg + j
            af_r, ab_r = sg[hd:hd + 1], sg[H + hd:H + hd + 1]
            dtf_r, dtb_r = sg[2 * H + hd:2 * H + hd + 1], sg[3 * H + hd:3 * H + hd + 1]
            af_c, ab_c = sc[:, hd:hd + 1], sc[:, H + hd:H + hd + 1]
            fm = jnp.exp(jnp.where(lower, af_c - af_r, NEG_INF)) * dtf_r
            bm = jnp.exp(jnp.where(upper, ab_c - ab_r, NEG_INF)) * dtb_r
            ms.append((cb * (fm + bm)).astype(BF16))
            xms.append(jnp.where((lane_g >= j * P) & (lane_g < (j + 1) * P), xg, jnp.zeros_like(xg)))
        hds = range(g * hg, (g + 1) * hg)
        efc = _lane_groups([sc[:, 2 * H + hd:2 * H + hd + 1] for hd in hds], P)
        ebc = _lane_groups([sc[:, 3 * H + hd:3 * H + hd + 1] for hd in hds], P)
        wfc = _lane_groups([sc[:, 4 * H + hd:4 * H + hd + 1] for hd in hds], P)
        y = _dot(jnp.concatenate(ms, axis=1), jnp.concatenate(xms, axis=0))
        h_old = h_scr[g]
        y = y + efc * _dot(cg, h_old.astype(BF16))
        y = y + ebc * _dot(cg, hb_ref[0, 0, g])
        xgf = xg.astype(F32)
        y = y + dsk_ref[:, g * GW:(g + 1) * GW] * xgf
        xs = (xgf * wfc).astype(BF16)
        h_scr[g] = dec_ref[0, 0, 0:1, g * GW:(g + 1) * GW] * h_old + _dot_tn(bg, xs)
        zg = z_ref[0, :, g * GW:(g + 1) * GW].astype(F32)
        y = y * (zg * _sigmoid(zg))
        y = y * lax.rsqrt(jnp.mean(y * y, axis=1, keepdims=True) + RMS_EPS)
        y_ref[0, :, g * GW:(g + 1) * GW] = (y * nw_ref[:, g * GW:(g + 1) * GW]).astype(y_ref.dtype)


def _ssd_main(xc, pa3, sg, sc, decs, hb, dskip_row, norm_w):
    B, S, _ = xc.shape
    nc = S // CHUNK
    G, N, W = SSM_GROUPS, SSM_STATE, SSM_WIDTH
    z_blk = PA_Z0 // W
    return pl.pallas_call(
        functools.partial(_ssd_main_kernel, L=CHUNK),
        grid=(B, nc),
        in_specs=[
            pl.BlockSpec((1, CHUNK, W), lambda b, c: (b, c, 0)),
            pl.BlockSpec((1, CHUNK, G * N), lambda b, c: (b, c, W // (G * N))),
            pl.BlockSpec((1, CHUNK, G * N), lambda b, c: (b, c, W // (G * N) + 1)),
            pl.BlockSpec((1, CHUNK, W), lambda b, c: (b, c, z_blk)),
            pl.BlockSpec((1, 1, 4 * SSM_HEADS, CHUNK), lambda b, c: (b, c, 0, 0)),
            pl.BlockSpec((1, CHUNK, 128), lambda b, c: (b, c, 0)),
            pl.BlockSpec((1, 1, 2, W), lambda b, c: (b, c, 0, 0)),
            pl.BlockSpec((1, 1, G, N, W // G), lambda b, c: (b, c, 0, 0, 0)),
            pl.BlockSpec((1, W), lambda b, c: (0, 0)),
            pl.BlockSpec((1, W), lambda b, c: (0, 0)),
        ],
        out_specs=pl.BlockSpec((1, CHUNK, W), lambda b, c: (b, c, 0)),
        out_shape=jax.ShapeDtypeStruct((B, S, W), BF16),
        scratch_shapes=[pltpu.VMEM((G, N, W // G), F32)],
        compiler_params=_cparams(("parallel", "arbitrary")),
        name="ssd_main",
    )(xc, xc, xc, pa3, sg, sc, decs, hb, dskip_row, norm_w)


def _pack_bf16_pairs(v):
    n = v.shape[1] // 2
    hi = lax.bitcast_convert_type(v[:, :n].astype(BF16).astype(F32), jnp.uint32)
    lo = lax.bitcast_convert_type(v[:, n:].astype(BF16).astype(F32), jnp.uint32)
    return (hi & jnp.uint32(0xFFFF0000)) | (lo >> 16)


def _unpack_bf16_pairs(u):
    hi = lax.bitcast_convert_type(u & jnp.uint32(0xFFFF0000), F32)
    lo = lax.bitcast_convert_type(u << 16, F32)
    return hi, lo


def _layer_norm(v, g, b):
    mu = jnp.mean(v, axis=1, keepdims=True)
    d = v - mu
    var = jnp.mean(d * d, axis=1, keepdims=True)
    return d * lax.rsqrt(var + LN_EPS) * g + b


def _outproj_kernel(yml_ref, yssm_ref, x_ref, wo_ref, g_ref, b_ref, wr_ref, h_ref, hpa_ref, hpb_ref, sc_ref):
    mix = _dot(yml_ref[...], wo_ref[0:ML_WIDTH, :]) + _dot(yssm_ref[...], wo_ref[ML_WIDTH:, :])
    h = _layer_norm(ALPHA * x_ref[...] + mix, g_ref[...], b_ref[...])
    h_ref[...] = h
    hp = _pack_bf16_pairs(h)
    q = hp.shape[1] // 2
    hpa_ref[...] = hp[:, :q]
    hpb_ref[...] = hp[:, q:]
    sc_ref[...] = _sigmoid(_dot(h.astype(BF16), wr_ref[...]))


def _outproj(yml, yssm, x2d, w_out, ln_g, ln_b, w_router, tm=512):
    T, D = x2d.shape
    tm = min(tm, T)
    row = lambda i: (i, 0)
    fixed = lambda i: (0, 0)
    return pl.pallas_call(
        _outproj_kernel,
        grid=(T // tm,),
        in_specs=[
            pl.BlockSpec((tm, ML_WIDTH), row),
            pl.BlockSpec((tm, SSM_WIDTH), row),
            pl.BlockSpec((tm, D), row),
            pl.BlockSpec((ML_WIDTH + SSM_WIDTH, D), fixed),
            pl.BlockSpec((1, D), fixed),
            pl.BlockSpec((1, D), fixed),
            pl.BlockSpec((D, 128), fixed),
        ],
        out_specs=[pl.BlockSpec((tm, D), row), pl.BlockSpec((tm, D // 4), row), pl.BlockSpec((tm, D // 4), row),
                   pl.BlockSpec((tm, 128), row)],
        out_shape=[
            jax.ShapeDtypeStruct((T, D), F32),
            jax.ShapeDtypeStruct((T, D // 4), jnp.uint32),
            jax.ShapeDtypeStruct((T, D // 4), jnp.uint32),
            jax.ShapeDtypeStruct((T, 128), F32),
        ],
        compiler_params=_cparams(("parallel",)),
        name="out_proj_ln1_router",
    )(yml, yssm, x2d, w_out, ln_g, ln_b, w_router)


def _route_kernel(sc_ref, rb_ref, idx_ref, w_ref, rank_ref, cnt_ref, carry_scr, *, Tt):
    E, NG = N_EXPERTS, N_EXPERT_GROUPS
    EG = E // NG

    @pl.when(pl.program_id(0) == 0)
    def _():
        carry_scr[...] = jnp.zeros_like(carry_scr)

    s = sc_ref[...].T[0:E]
    s3 = s.reshape(NG, EG, Tt)
    ch3 = (s + rb_ref[:, 0:1]).reshape(NG, EG, Tt)
    e_in = lax.broadcasted_iota(jnp.int32, (NG, EG, Tt), 1)
    g_in = lax.broadcasted_iota(jnp.int32, (NG, EG, Tt), 0)
    f_i = g_in * EG + e_in
    m1 = jnp.max(ch3, axis=1, keepdims=True)
    i1 = jnp.min(jnp.where(ch3 == m1, e_in, EG), axis=1, keepdims=True)
    m2 = jnp.max(jnp.where(e_in == i1, NEG_INF, ch3), axis=1, keepdims=True)
    gs = m1 + m2
    g_i = lax.broadcasted_iota(jnp.int32, (NG, 1, Tt), 0)
    gsel = jnp.zeros((NG, 1, Tt), jnp.bool_)
    cur = gs
    for _ in range(TOPK_GROUPS):
        mx = jnp.max(cur, axis=0, keepdims=True)
        pick = jnp.min(jnp.where(cur == mx, g_i, NG), axis=0, keepdims=True)
        hit = g_i == pick
        gsel = gsel | hit
        cur = jnp.where(hit, NEG_INF, cur)
    cur = jnp.where(gsel, ch3, NEG_INF)
    selmask = jnp.zeros((NG, EG, Tt), jnp.bool_)
    idxs, ws = [], []
    for _ in range(TOP_K):
        mx = jnp.max(jnp.max(cur, axis=1, keepdims=True), axis=0, keepdims=True)
        cand = jnp.where(cur == mx, f_i, E)
        pick = jnp.min(jnp.min(cand, axis=1, keepdims=True), axis=0, keepdims=True)
        hit = f_i == pick
        wk = jnp.sum(jnp.sum(jnp.where(hit, s3, 0.0), axis=1, keepdims=True), axis=0, keepdims=True)
        idxs.append(pick.reshape(1, Tt))
        ws.append(wk.reshape(1, Tt))
        selmask = selmask | hit
        cur = jnp.where(hit, NEG_INF, cur)
    wsum = ws[0]
    for k in range(1, TOP_K):
        wsum = wsum + ws[k]
    r = lax.broadcasted_iota(jnp.int32, (Tt, Tt), 0)
    c = lax.broadcasted_iota(jnp.int32, (Tt, Tt), 1)
    ustrict = (r < c).astype(BF16)
    self32 = jnp.where(selmask, 1.0, 0.0).reshape(E, Tt)
    excl = _dot(self32.astype(BF16), ustrict) + carry_scr[:, 0:1]
    excl3 = excl.reshape(NG, EG, Tt)
    for k in range(TOP_K):
        hit = f_i == idxs[k].reshape(1, 1, Tt)
        rk = jnp.sum(jnp.sum(jnp.where(hit, excl3, 0.0), axis=1, keepdims=True), axis=0, keepdims=True)
        idx_ref[k:k + 1, :] = idxs[k]
        w_ref[k:k + 1, :] = ws[k] / wsum * ROUTED_SCALE
        rank_ref[k:k + 1, :] = rk.reshape(1, Tt).astype(jnp.int32)
    carry_scr[...] = carry_scr[...] + jnp.sum(self32, axis=1, keepdims=True)
    cnt_ref[...] = carry_scr[...]


def _route(scores, rbias, Tt=512):
    T = scores.shape[0]
    kt = lambda i: (0, i)
    return pl.pallas_call(
        functools.partial(_route_kernel, Tt=Tt),
        grid=(T // Tt,),
        in_specs=[pl.BlockSpec((Tt, 128), lambda i: (i, 0)), pl.BlockSpec((N_EXPERTS, 128), lambda i: (0, 0))],
        out_specs=[
            pl.BlockSpec((TOP_K, Tt), kt),
            pl.BlockSpec((TOP_K, Tt), kt),
            pl.BlockSpec((TOP_K, Tt), kt),
            pl.BlockSpec((N_EXPERTS, 128), lambda i: (0, 0)),
        ],
        out_shape=[
            jax.ShapeDtypeStruct((TOP_K, T), jnp.int32),
            jax.ShapeDtypeStruct((TOP_K, T), F32),
            jax.ShapeDtypeStruct((TOP_K, T), jnp.int32),
            jax.ShapeDtypeStruct((N_EXPERTS, 128), F32),
        ],
        scratch_shapes=[pltpu.VMEM((N_EXPERTS, 128), F32)],
        compiler_params=_cparams(("arbitrary",)),
        name="route_topk",
    )(scores, rbias)


def _dest_kernel(ps_ref, idx_ref, rank_ref, o_ref):
    idx = idx_ref[...]
    start = jnp.zeros(idx.shape, jnp.int32)
    for e in range(N_EXPERTS):
        start = jnp.where(idx == e, ps_ref[e], start)
    o_ref[...] = start + rank_ref[...]


def _dest(pstarts, idx, rank, Tt=2048):
    K, T = idx.shape
    Tt = min(Tt, T)
    blk = pl.BlockSpec((K, Tt), lambda i, ps: (0, i))
    return pl.pallas_call(
        _dest_kernel,
        grid_spec=pltpu.PrefetchScalarGridSpec(num_scalar_prefetch=1, grid=(T // Tt,), in_specs=[blk, blk],
                                               out_specs=blk),
        out_shape=jax.ShapeDtypeStruct((K, T), jnp.int32),
        compiler_params=_cparams(("parallel",)),
        name="dispatch_slots",
    )(pstarts, idx, rank)


def _sc_mesh():
    return plsc.VectorSubcoreMesh(core_axis_name="c", subcore_axis_name="s")


def _sc_scatter_rows(src, idx, n_out, win=SC_WINDOW):
    n = idx.shape[0]
    T, W = src.shape
    nt = T // win

    @functools.partial(pl.kernel, out_type=jax.ShapeDtypeStruct((n_out, W), src.dtype), mesh=_sc_mesh(),
                       scratch_types=[], name="sc_scatter_rows")
    def k(x_hbm, i_hbm, o_hbm):
        def body(x_vmem, i_vmem):
            pltpu.sync_copy(x_vmem, o_hbm.at[i_vmem.at[0]])

        pltpu.emit_pipeline(
            body,
            grid=(n // win,),
            in_specs=[pl.BlockSpec((win, W), lambda i: (i % nt, 0)), pl.BlockSpec((1, win), lambda i: (0, i))],
            out_specs=[],
            core_axis_name=("c", "s"),
            dimension_semantics=(pltpu.PARALLEL,),
        )(x_hbm, i_hbm)

    return k(src, idx.reshape(1, n))


def _sc_gather_rows(table, idx, win=SC_WINDOW):
    n = idx.shape[0]
    W = table.shape[1]

    @functools.partial(pl.kernel, out_type=jax.ShapeDtypeStruct((n, W), table.dtype), mesh=_sc_mesh(),
                       scratch_types=[], name="sc_gather_rows")
    def k(t_hbm, i_hbm, o_hbm):
        def body(i_vmem, o_vmem):
            pltpu.sync_copy(t_hbm.at[i_vmem.at[0]], o_vmem)

        pltpu.emit_pipeline(
            body,
            grid=(n // win,),
            in_specs=[pl.BlockSpec((1, win), lambda i: (0, i))],
            out_specs=[pl.BlockSpec((win, W), lambda i: (i, 0))],
            core_axis_name=("c", "s"),
            dimension_semantics=(pltpu.PARALLEL,),
        )(i_hbm, o_hbm)

    return k(table, idx.reshape(1, n))


def _expert_kernel(be_ref, nu_ref, xa_ref, xb_ref, wg_ref, wu_ref, wd_ref, oa_ref, ob_ref, wg_s, wu_s, wd_s):
    half = D_MODEL // 2
    i = pl.program_id(0)
    active = i < nu_ref[0]

    @pl.when(active & ((i == 0) | (be_ref[i] != be_ref[jnp.maximum(i - 1, 0)])))
    def _():
        wg_s[...] = wg_ref[0].astype(BF16)
        wu_s[...] = wu_ref[0].astype(BF16)
        wd_s[...] = wd_ref[0].astype(BF16)

    @pl.when(active)
    def _():
        hi, lo = _unpack_bf16_pairs(jnp.concatenate([xa_ref[...], xb_ref[...]], axis=1))
        hi, lo = hi.astype(BF16), lo.astype(BF16)
        gt = _dot(hi, wg_s[0:half, :]) + _dot(lo, wg_s[half:, :])
        up = _dot(hi, wu_s[0:half, :]) + _dot(lo, wu_s[half:, :])
        hmid = (gt * _sigmoid(gt) * up).astype(BF16)
        packed = _pack_bf16_pairs(_dot(hmid, wd_s[...]))
        q = packed.shape[1] // 2
        oa_ref[...] = packed[:, :q]
        ob_ref[...] = packed[:, q:]

    @pl.when(jnp.logical_not(active))
    def _():
        oa_ref[...] = jnp.zeros_like(oa_ref)
        ob_ref[...] = jnp.zeros_like(ob_ref)


def _experts(xs_a, xs_b, block_expert, n_used, w_gate, w_up, w_down, blk=MOE_BLOCK):
    P, quarter = xs_a.shape
    nb = P // blk
    D, F = D_MODEL, D_EXPERT
    cur = lambda i, be, nu: jnp.minimum(i, nu[0] - 1)
    return pl.pallas_call(
        _expert_kernel,
        grid_spec=pltpu.PrefetchScalarGridSpec(
            num_scalar_prefetch=2,
            grid=(nb,),
            in_specs=[
                pl.BlockSpec((blk, quarter), lambda i, be, nu: (cur(i, be, nu), 0)),
                pl.BlockSpec((blk, quarter), lambda i, be, nu: (cur(i, be, nu), 0)),
                pl.BlockSpec((1, D, F), lambda i, be, nu: (be[cur(i, be, nu)], 0, 0)),
                pl.BlockSpec((1, D, F), lambda i, be, nu: (be[cur(i, be, nu)], 0, 0)),
                pl.BlockSpec((1, F, D), lambda i, be, nu: (be[cur(i, be, nu)], 0, 0)),
            ],
            out_specs=[pl.BlockSpec((blk, quarter), lambda i, be, nu: (i, 0)),
                       pl.BlockSpec((blk, quarter), lambda i, be, nu: (i, 0))],
            scratch_shapes=[pltpu.VMEM((D, F), BF16), pltpu.VMEM((D, F), BF16), pltpu.VMEM((F, D), BF16)],
        ),
        out_shape=[jax.ShapeDtypeStruct((P, quarter), jnp.uint32), jax.ShapeDtypeStruct((P, quarter), jnp.uint32)],
        compiler_params=_cparams(("arbitrary",)),
        name="expert_mlp",
    )(block_expert, n_used, xs_a, xs_b, w_gate, w_up, w_down)


def _final_kernel(h_ref, ga_ref, gb_ref, wt_ref, wsg_ref, wsu_ref, wsd_ref, g_ref, b_ref, o_ref):
    h = h_ref[...]
    hb = h.astype(BF16)
    gt = _dot(hb, wsg_ref[...])
    up = _dot(hb, wsu_ref[...])
    shared = _dot((gt * _sigmoid(gt) * up).astype(BF16), wsd_ref[...])
    acc_hi = None
    for k in range(TOP_K):
        hi, lo = _unpack_bf16_pairs(jnp.concatenate([ga_ref[k], gb_ref[k]], axis=1))
        wk = wt_ref[:, k:k + 1]
        acc_hi = wk * hi if acc_hi is None else acc_hi + wk * hi
        acc_lo = wk * lo if k == 0 else acc_lo + wk * lo
    routed = jnp.concatenate([acc_hi, acc_lo], axis=1)
    o_ref[...] = _layer_norm(ALPHA * h + (routed + shared), g_ref[...], b_ref[...])


def _final(h, gk_a, gk_b, wt, ws_gate, ws_up, ws_down, ln_g, ln_b, tm=512):
    T, D = h.shape
    tm = min(tm, T)
    F = D_EXPERT
    fixed = lambda i: (0, 0)
    return pl.pallas_call(
        _final_kernel,
        grid=(T // tm,),
        in_specs=[
            pl.BlockSpec((tm, D), lambda i: (i, 0)),
            pl.BlockSpec((TOP_K, tm, D // 4), lambda i: (0, i, 0)),
            pl.BlockSpec((TOP_K, tm, D // 4), lambda i: (0, i, 0)),
            pl.BlockSpec((tm, TOP_K), lambda i: (i, 0)),
            pl.BlockSpec((D, F), fixed),
            pl.BlockSpec((D, F), fixed),
            pl.BlockSpec((F, D), fixed),
            pl.BlockSpec((1, D), fixed),
            pl.BlockSpec((1, D), fixed),
        ],
        out_specs=pl.BlockSpec((tm, D), lambda i: (i, 0)),
        out_shape=jax.ShapeDtypeStruct((T, D), F32),
        compiler_params=_cparams(("parallel",)),
        name="shared_combine_ln2",
    )(h, gk_a, gk_b, wt, ws_gate, ws_up, ws_down, ln_g, ln_b)


def _mixers(x, w_in, ml_i_bias, ml_f_bias, ml_norm_w, conv_w, conv_b, dt_bias, a_log, d_skip, ssm_norm_w):
    B, S, D = x.shape
    T = B * S
    nc = S // CHUNK
    H = ML_HEADS
    k0, v0 = ML_QK, 2 * ML_QK
    g0 = v0 + 2 * ML_WIDTH
    z0 = g0 + 4 * H
    x0 = z0 + SSM_WIDTH
    d0 = x0 + SSM_CONV_DIM
    w_big = jnp.concatenate([w_in[:, x0:d0], w_in[:, v0:g0], w_in[:, z0:x0], w_in[:, :k0]], axis=1).astype(BF16)
    w_small = jnp.concatenate(
        [w_in[:, g0:z0], w_in[:, d0:], jnp.zeros((D, PB_COLS - N_GATE_ROWS), w_in.dtype)], axis=1).astype(BF16)
    w_kt = (w_in[:, k0:v0] * (ML_DQK ** -0.5)).T.astype(BF16)
    pa, pb, kt = _proj(x.reshape(T, D), w_big, w_small, w_kt)
    pa3 = pa.reshape(B, S, PA_COLS)
    kt5 = kt.reshape(B, nc, H, ML_DQK, CHUNK)

    g_rows = pb.reshape(B, S, PB_COLS)[:, :, :N_GATE_ROWS].transpose(0, 2, 1).reshape(B, N_GATE_ROWS, nc, CHUNK)
    bias = jnp.concatenate([ml_i_bias, ml_f_bias, dt_bias]).astype(F32)
    bias_rows = jnp.broadcast_to(bias[:, None, None], (N_GATE_ROWS, 1, CHUNK))
    aneg_rows = jnp.broadcast_to((-jnp.exp(a_log.astype(F32)))[:, None, None], (2 * SSM_HEADS, 1, CHUNK))
    prep = _gate_prep(g_rows, bias_rows, aneg_rows)
    m8 = jnp.stack([prep[:, 0:H], prep[:, 2 * H:3 * H], prep[:, H:2 * H], prep[:, 3 * H:4 * H],
                    prep[:, 4 * H:5 * H], prep[:, 5 * H:6 * H], prep[:, 6 * H:7 * H], prep[:, 7 * H:8 * H]],
                   axis=2)
    mg = m8.transpose(0, 1, 3, 2, 4)
    mc = m8.transpose(0, 1, 3, 4, 2).reshape(B, H, S, 8)
    s0, SH = PREP_SSM0, SSM_HEADS
    sg = prep[:, s0:s0 + 4 * SH].transpose(0, 2, 1, 3)
    sc = jnp.concatenate([prep[:, s0:s0 + 2 * SH], prep[:, s0 + 4 * SH:s0 + 8 * SH]], axis=1)
    sc = sc.transpose(0, 2, 3, 1).reshape(B, S, 6 * SH)
    sc = jnp.concatenate([sc, jnp.zeros((B, S, 128 - 6 * SH), F32)], axis=2)
    decs = prep[:, s0 + 8 * SH:s0 + 10 * SH, :, 0].reshape(B, 2, SH, nc).transpose(0, 3, 1, 2)
    decs = jnp.repeat(decs, SSM_HEADDIM, axis=3)

    y_ml = _mlstm(pa3, kt5, mg, mc, ml_norm_w.astype(F32).reshape(1, ML_WIDTH))

    Lc = 512 if S % 512 == 0 else CHUNK
    nt = S // Lc
    pad = CONV_K // 2
    pa4 = pa3.reshape(B, nt, Lc, PA_COLS)
    top = pa4[:, :, Lc - pad:, PA_XBC0:PA_XBC0 + SSM_CONV_DIM]
    bot = pa4[:, :, :pad, PA_XBC0:PA_XBC0 + SSM_CONV_DIM]
    zrow = jnp.zeros((B, 1, pad, SSM_CONV_DIM), pa.dtype)
    top = jnp.concatenate([zrow, top[:, :-1]], axis=1)
    bot = jnp.concatenate([bot[:, 1:], zrow], axis=1)
    halo = jnp.concatenate([top, bot, jnp.zeros((B, nt, 8 - 2 * pad, SSM_CONV_DIM), pa.dtype)], axis=2)
    conv_w8 = jnp.concatenate([conv_w.astype(F32), jnp.zeros((8 - CONV_K, SSM_CONV_DIM), F32)], axis=0)
    xc = _conv(pa3, halo, conv_w8, conv_b.astype(F32).reshape(1, SSM_CONV_DIM), Lc=Lc)

    hb = _ssd_bwd(xc, sc, decs)
    dsk = jnp.repeat(d_skip.astype(F32), SSM_HEADDIM).reshape(1, SSM_WIDTH)
    y_ssm = _ssd_main(xc, pa3, sg, sc, decs, hb, dsk, ssm_norm_w.astype(F32).reshape(1, SSM_WIDTH))
    return y_ml.reshape(T, ML_WIDTH), y_ssm.reshape(T, SSM_WIDTH)


def _moe_dispatch(idx, rank, counts, T, blk):
    E = N_EXPERTS
    M = T * TOP_K
    nb = (M + E * (blk - 1)) // blk
    pcounts = (counts + blk - 1) // blk * blk
    pends = jnp.cumsum(pcounts)
    pstarts = (pends - pcounts).astype(jnp.int32)
    dest = _dest(pstarts, idx, rank)
    block_start = jnp.arange(nb, dtype=jnp.int32) * blk
    block_expert = jnp.minimum(jnp.sum(pends[None, :] <= block_start[:, None], axis=1), E - 1).astype(jnp.int32)
    n_used = (pends[-1] // blk).astype(jnp.int32).reshape(1)
    return dest, nb * blk, block_expert, n_used


def _layer(x, w_in, ml_i_bias, ml_f_bias, ml_norm_w, conv_w, conv_b, dt_bias, a_log, d_skip, ssm_norm_w,
           w_out, ln1_g, ln1_b, w_router, router_bias, w_exp_gate, w_exp_up, w_exp_down,
           w_sh_gate, w_sh_up, w_sh_down, ln2_g, ln2_b):
    B, S, D = x.shape
    T = B * S
    y_ml, y_ssm = _mixers(x, w_in, ml_i_bias, ml_f_bias, ml_norm_w, conv_w, conv_b, dt_bias, a_log, d_skip,
                          ssm_norm_w)
    w_r = jnp.concatenate([w_router, jnp.zeros((D, 128 - N_EXPERTS), w_router.dtype)], axis=1).astype(BF16)
    h, hp_a, hp_b, scores = _outproj(y_ml, y_ssm, x.reshape(T, D), w_out.astype(BF16),
                                     ln1_g.astype(F32).reshape(1, D), ln1_b.astype(F32).reshape(1, D), w_r)
    rbias = jnp.broadcast_to(router_bias.astype(F32)[:, None], (N_EXPERTS, 128))
    idx, wgt, rank, cnt = _route(scores, rbias)
    counts = cnt[:, 0].astype(jnp.int32)
    dest, n_slots, block_expert, n_used = _moe_dispatch(idx, rank, counts, T, MOE_BLOCK)
    dest_flat = dest.reshape(-1)
    xs_a = _sc_scatter_rows(hp_a, dest_flat, n_slots)
    xs_b = _sc_scatter_rows(hp_b, dest_flat, n_slots)
    os_a, os_b = _experts(xs_a, xs_b, block_expert, n_used, w_exp_gate, w_exp_up, w_exp_down)
    gk_a = _sc_gather_rows(os_a, dest_flat).reshape(TOP_K, T, D // 4)
    gk_b = _sc_gather_rows(os_b, dest_flat).reshape(TOP_K, T, D // 4)
    out = _final(h, gk_a, gk_b, wgt.T, w_sh_gate.astype(BF16), w_sh_up.astype(BF16),
                 w_sh_down.astype(BF16), ln2_g.astype(F32).reshape(1, D), ln2_b.astype(F32).reshape(1, D))
    return out.reshape(B, S, D)


def kernel(x, w_in, ml_i_bias, ml_f_bias, ml_norm_w, conv_w, conv_b, dt_bias, a_log, d_skip, ssm_norm_w, w_out,
           ln1_g, ln1_b, w_router, router_bias, w_exp_gate, w_exp_up, w_exp_down, w_sh_gate, w_sh_up, w_sh_down,
           ln2_g, ln2_b):
    for l in range(w_in.shape[0]):
        x = _layer(x, w_in[l], ml_i_bias[l], ml_f_bias[l], ml_norm_w[l], conv_w[l], conv_b[l], dt_bias[l],
                   a_log[l], d_skip[l], ssm_norm_w[l], w_out[l], ln1_g[l], ln1_b[l], w_router[l],
                   router_bias[l], w_exp_gate[l], w_exp_up[l], w_exp_down[l], w_sh_gate[l], w_sh_up[l],
                   w_sh_down[l], ln2_g[l], ln2_b[l])
    return x
```

```python
import functools

import jax
import jax.numpy as jnp
from jax import lax
from jax.experimental import pallas as pl
from jax.experimental.pallas import tpu as pltpu
from jax.experimental.pallas import tpu_sc as plsc

D_MODEL = 1024
ML_HEADS = 4
ML_DV = 256
ML_DQK = 128
ML_QK = ML_HEADS * ML_DQK
ML_WIDTH = ML_HEADS * ML_DV
SSM_WIDTH = 1024
SSM_HEADDIM = 64
SSM_HEADS = 16
SSM_GROUPS = 4
SSM_STATE = 128
SSM_CONV_DIM = SSM_WIDTH + 2 * SSM_GROUPS * SSM_STATE
CONV_K = 5
CHUNK = 128
N_EXPERTS = 64
TOP_K = 8
N_EXPERT_GROUPS = 8
TOPK_GROUPS = 4
D_EXPERT = 256
ROUTED_SCALE = 2.5
LN_EPS = 1e-5
RMS_EPS = 1e-5
DEPTH = 1
ALPHA = (2 * DEPTH) ** 0.25

PA_XBC0 = 0
PA_V0 = PA_XBC0 + SSM_CONV_DIM
PA_O0 = PA_V0 + ML_WIDTH
PA_Z0 = PA_O0 + ML_WIDTH
PA_Q0 = PA_Z0 + SSM_WIDTH
PA_COLS = PA_Q0 + ML_QK
PB_COLS = 128
N_GATE_ROWS = 2 * ML_HEADS + 2 * ML_HEADS + 2 * SSM_HEADS
PREP_SSM0 = 32
N_PREP_ROWS = PREP_SSM0 + 10 * SSM_HEADS

MOE_BLOCK = 1024
SC_WINDOW = 128
VMEM_LIMIT = 56 * 1024 * 1024

BF16 = jnp.bfloat16
F32 = jnp.float32
NEG_INF = float("-inf")


def _cparams(sem, vmem=VMEM_LIMIT):
    return pltpu.CompilerParams(dimension_semantics=sem, vmem_limit_bytes=vmem)


def _dot(a, b):
    return jnp.dot(a, b, preferred_element_type=F32)


def _dot_nt(a, b):
    return lax.dot_general(a, b, (((1,), (1,)), ((), ())), preferred_element_type=F32)


def _dot_tn(a, b):
    return lax.dot_general(a, b, (((0,), (0,)), ((), ())), preferred_element_type=F32)


def _sigmoid(x):
    return 1.0 / (1.0 + jnp.exp(-x))


def _softplus(x):
    return jnp.maximum(x, 0.0) + jnp.log1p(jnp.exp(-jnp.abs(x)))


def _proj_kernel(x_ref, wb_ref, ws_ref, wkt_ref, oa_ref, ob_ref, kt_ref, xb_scr, *, L):
    @pl.when(pl.program_id(1) == 0)
    def _():
        xb = x_ref[...].astype(BF16)
        xb_scr[...] = xb
        ob_ref[...] = _dot(xb, ws_ref[...])
        kt = _dot_nt(wkt_ref[...], xb).astype(kt_ref.dtype)
        for c in range(kt_ref.shape[0]):
            kt_ref[c] = kt[:, c * L:(c + 1) * L]

    oa_ref[...] = _dot(xb_scr[...], wb_ref[...]).astype(oa_ref.dtype)


def _proj(x2d, w_big, w_small, w_kt, tm=1024, tn=1408):
    T, D = x2d.shape
    N = w_big.shape[1]
    tm = min(tm, T)
    L = CHUNK
    return pl.pallas_call(
        functools.partial(_proj_kernel, L=L),
        grid=(T // tm, N // tn),
        in_specs=[
            pl.BlockSpec((tm, D), lambda i, j: (i, 0)),
            pl.BlockSpec((D, tn), lambda i, j: (0, j)),
            pl.BlockSpec((D, PB_COLS), lambda i, j: (0, 0)),
            pl.BlockSpec((ML_QK, D), lambda i, j: (0, 0)),
        ],
        out_specs=[
            pl.BlockSpec((tm, tn), lambda i, j: (i, j)),
            pl.BlockSpec((tm, PB_COLS), lambda i, j: (i, 0)),
            pl.BlockSpec((tm // L, ML_QK, L), lambda i, j: (i, 0, 0)),
        ],
        out_shape=[jax.ShapeDtypeStruct((T, N), BF16), jax.ShapeDtypeStruct((T, PB_COLS), F32),
                   jax.ShapeDtypeStruct((T // L, ML_QK, L), BF16)],
        scratch_shapes=[pltpu.VMEM((tm, D), BF16)],
        compiler_params=_cparams(("parallel", "arbitrary")),
        name="in_proj",
    )(x2d, w_big, w_small, w_kt)


def _split3_dot(x, u):
    hi = x.astype(BF16)
    r1 = x - hi.astype(F32)
    mid = r1.astype(BF16)
    lo = (r1 - mid.astype(F32)).astype(BF16)
    return _dot(hi, u) + _dot(mid, u) + _dot(lo, u)


def _gate_kernel(g_ref, bias_ref, aneg_ref, o_ref, *, nc, L):
    g = g_ref[0] + bias_ref[...]
    nh = ML_HEADS
    ig = g[0:2 * nh]
    fpre = g[2 * nh:4 * nh]
    lf = -_softplus(-fpre)
    dt = _softplus(g[4 * nh:])
    dA = dt * aneg_ref[...]
    r = lax.broadcasted_iota(jnp.int32, (L, L), 0)
    c = lax.broadcasted_iota(jnp.int32, (L, L), 1)
    u_pre = (r <= c).astype(BF16)
    u_suf = (r >= c).astype(BF16)
    nf = nh + SSM_HEADS
    xf = jnp.concatenate([lf[0:nh], dA[0:SSM_HEADS]], axis=0).reshape(nf * nc, L)
    xb = jnp.concatenate([lf[nh:], dA[SSM_HEADS:]], axis=0).reshape(nf * nc, L)
    cf = _split3_dot(xf, u_pre).reshape(nf, nc, L)
    cb = _split3_dot(xb, u_suf).reshape(nf, nc, L)
    totf = jnp.broadcast_to(cf[:, :, L - 1:L], cf.shape)
    totb = jnp.broadcast_to(cb[:, :, 0:1], cb.shape)
    o_ref[0, 0:nh] = cf[0:nh]
    o_ref[0, nh:2 * nh] = cb[0:nh]
    o_ref[0, 2 * nh:4 * nh] = ig
    o_ref[0, 4 * nh:5 * nh] = totf[0:nh] - cf[0:nh] + ig[0:nh]
    o_ref[0, 5 * nh:6 * nh] = totb[0:nh] - cb[0:nh] + ig[nh:]
    lane = lax.broadcasted_iota(jnp.int32, (nh * nc, L), 1)
    pf = (ig[0:nh] - cf[0:nh]).reshape(nh * nc, L)
    pb = (ig[nh:] - cb[0:nh]).reshape(nh * nc, L)
    sh = 1
    while sh < L:
        pf = jnp.where(lane >= sh, jnp.maximum(pf, pltpu.roll(pf, sh, 1)), pf)
        pb = jnp.where(lane < L - sh, jnp.maximum(pb, pltpu.roll(pb, L - sh, 1)), pb)
        sh *= 2
    o_ref[0, 6 * nh:7 * nh] = cf[0:nh] + pf.reshape(nh, nc, L)
    o_ref[0, 7 * nh:8 * nh] = cb[0:nh] + pb.reshape(nh, nc, L)
    H = SSM_HEADS
    af, ab, dtf, dtb = cf[nh:], cb[nh:], dt[0:H], dt[H:]
    s0 = PREP_SSM0
    o_ref[0, s0:s0 + H] = af
    o_ref[0, s0 + H:s0 + 2 * H] = ab
    o_ref[0, s0 + 2 * H:s0 + 3 * H] = dtf
    o_ref[0, s0 + 3 * H:s0 + 4 * H] = dtb
    o_ref[0, s0 + 4 * H:s0 + 5 * H] = jnp.exp(af)
    o_ref[0, s0 + 5 * H:s0 + 6 * H] = jnp.exp(ab)
    o_ref[0, s0 + 6 * H:s0 + 7 * H] = dtf * jnp.exp(totf[nh:] - af)
    o_ref[0, s0 + 7 * H:s0 + 8 * H] = dtb * jnp.exp(totb[nh:] - ab)
    o_ref[0, s0 + 8 * H:s0 + 9 * H] = jnp.exp(totf[nh:])
    o_ref[0, s0 + 9 * H:s0 + 10 * H] = jnp.exp(totb[nh:])


def _gate_prep(g_rows, bias_rows, aneg_rows):
    B, R, nc, L = g_rows.shape
    return pl.pallas_call(
        functools.partial(_gate_kernel, nc=nc, L=L),
        grid=(B,),
        in_specs=[
            pl.BlockSpec((1, R, nc, L), lambda b: (b, 0, 0, 0)),
            pl.BlockSpec((R, 1, L), lambda b: (0, 0, 0)),
            pl.BlockSpec((2 * SSM_HEADS, 1, L), lambda b: (0, 0, 0)),
        ],
        out_specs=pl.BlockSpec((1, N_PREP_ROWS, nc, L), lambda b: (b, 0, 0, 0)),
        out_shape=jax.ShapeDtypeStruct((B, N_PREP_ROWS, nc, L), F32),
        compiler_params=_cparams(("parallel",)),
        name="gate_prep",
    )(g_rows, bias_rows, aneg_rows)


def _mlstm_kernel(q_ref, kt_ref, v_ref, o_ref, mg_ref, mc_ref, nw_ref, y_ref, hf_scr, hb_scr, c_scr, n_scr, m_scr,
                  *, nc, L):
    row_i = lax.broadcasted_iota(jnp.int32, (L, L), 0)
    col_i = lax.broadcasted_iota(jnp.int32, (L, L), 1)
    ones = jnp.ones((L, L), BF16)

    c_scr[...] = jnp.zeros_like(c_scr)
    n_scr[...] = jnp.zeros_like(n_scr)
    m_scr[...] = jnp.full_like(m_scr, NEG_INF)

    def chunk(c, d):
        sl = pl.ds(pl.multiple_of(c * L, L), L)
        q = q_ref[0, sl, :]
        kt = kt_ref[0, c, 0]
        vo = jnp.concatenate([v_ref[0, sl, :], ones], axis=1)
        g8 = mg_ref[0, 0, c]
        ct = mc_ref[0, 0, sl, :]
        b_r, i_r, a_r = g8[2 * d:2 * d + 1], g8[2 * d + 1:2 * d + 2], g8[4 + d:5 + d]
        b_c = jnp.broadcast_to(ct[:, 2 * d:2 * d + 1], (L, L))
        mi_c = jnp.broadcast_to(ct[:, 6 + d:7 + d], (L, L))
        if d == 0:
            gtot = b_r[:, L - 1:L]
            mask = col_i <= row_i
        else:
            gtot = b_r[:, 0:1]
            mask = col_i >= row_i
        c_ref, n_ref, m_ref = c_scr.at[d], n_scr.at[d], m_scr.at[d]
        m_prev = m_ref[...]
        m_loc = jnp.max(a_r, axis=1, keepdims=True)
        kwt = (kt.astype(F32) * jnp.exp(a_r - m_loc)).astype(BF16)
        uu = _dot(kwt, vo)
        qq = _dot(q, jnp.concatenate([kt, c_ref[...].astype(BF16), n_ref[...].astype(BF16)], axis=1))
        s, q_c, q_n = qq[:, 0:L], qq[:, L:L + ML_DV], qq[:, L + ML_DV:]
        dmat = jnp.where(mask, b_c - b_r + i_r, NEG_INF)
        m_inter = b_c + m_prev
        m_t = jnp.maximum(m_inter, mi_c)
        s_mat = s * jnp.exp(dmat - m_t)
        inter_w = jnp.exp(m_inter - m_t)
        ss = _dot(s_mat.astype(BF16), vo)
        den = ss[:, ML_DV:] + inter_w * q_n
        r = 1.0 / jnp.maximum(jnp.abs(den), jnp.exp(-m_t))
        num = ss[:, 0:ML_DV] + jnp.concatenate([inter_w, inter_w], axis=1) * q_c
        h = num * jnp.concatenate([r, r], axis=1)
        m_new = jnp.maximum(gtot + m_prev, m_loc)
        decay = jnp.exp(gtot + m_prev - m_new)
        sc = jnp.exp(m_loc - m_new)
        c_ref[...] = decay * c_ref[...] + sc * uu[:, 0:ML_DV]
        n_ref[...] = decay * n_ref[...] + sc * uu[:, ML_DV:]
        m_ref[...] = m_new
        return sl, h

    def scan_body(i, carry):
        sl, h = chunk(i, 0)
        hf_scr[sl, :] = h
        sl, h = chunk(nc - 1 - i, 1)
        hb_scr[sl, :] = h
        return carry

    lax.fori_loop(0, nc, scan_body, 0, unroll=2)

    def out_body(c, carry):
        sl = pl.ds(pl.multiple_of(c * L, L), L)
        h = hf_scr[sl, :] + hb_scr[sl, :]
        mu = jnp.mean(h, axis=1, keepdims=True)
        d = h - mu
        var = jnp.mean(d * d, axis=1, keepdims=True)
        y = d * lax.rsqrt(var + LN_EPS) * nw_ref[...]
        y = y * _sigmoid(o_ref[0, sl, :].astype(F32))
        y_ref[0, sl, :] = y.astype(y_ref.dtype)
        return carry

    lax.fori_loop(0, nc, out_body, 0, unroll=2)


def _mlstm(pa3, kt5, mg, mc, norm_w):
    B, S, _ = pa3.shape
    nc = S // CHUNK
    H = ML_HEADS
    q_blk, v_blk, o_blk = PA_Q0 // ML_DQK, PA_V0 // ML_DV, PA_O0 // ML_DV
    return pl.pallas_call(
        functools.partial(_mlstm_kernel, nc=nc, L=CHUNK),
        grid=(B, H),
        in_specs=[
            pl.BlockSpec((1, S, ML_DQK), lambda b, h: (b, 0, q_blk + h)),
            pl.BlockSpec((1, nc, 1, ML_DQK, CHUNK), lambda b, h: (b, 0, h, 0, 0)),
            pl.BlockSpec((1, S, ML_DV), lambda b, h: (b, 0, v_blk + h)),
            pl.BlockSpec((1, S, ML_DV), lambda b, h: (b, 0, o_blk + h)),
            pl.BlockSpec((1, 1, nc, 8, CHUNK), lambda b, h: (b, h, 0, 0, 0)),
            pl.BlockSpec((1, 1, S, 8), lambda b, h: (b, h, 0, 0)),
            pl.BlockSpec((1, ML_DV), lambda b, h: (0, h)),
        ],
        out_specs=pl.BlockSpec((1, S, ML_DV), lambda b, h: (b, 0, h)),
        out_shape=jax.ShapeDtypeStruct((B, S, ML_WIDTH), BF16),
        scratch_shapes=[
            pltpu.VMEM((S, ML_DV), F32),
            pltpu.VMEM((S, ML_DV), F32),
            pltpu.VMEM((2, ML_DQK, ML_DV), F32),
            pltpu.VMEM((2, ML_DQK, CHUNK), F32),
            pltpu.VMEM((2, 1, 1), F32),
        ],
        compiler_params=_cparams(("parallel", "parallel")),
        name="mlstm",
    )(pa3, kt5, pa3, pa3, mg, mc, norm_w)


CONV_EDGE = 16


def _conv_shift_matrix(L):
    pad, E = CONV_K // 2, CONV_EDGE
    t = jnp.arange(L, dtype=jnp.int32)[:, None]
    blocks = []
    u = jnp.arange(L, dtype=jnp.int32)[None, :]
    for j in range(CONV_K):
        blocks.append(u == t + j - pad)
    v = jnp.arange(E, dtype=jnp.int32)[None, :]
    for j in range(pad):
        blocks.append(v - E == t + j - pad)
    for j in range(pad + 1, CONV_K):
        blocks.append(v + L == t + j - pad)
    return jnp.concatenate(blocks, axis=1).astype(BF16)


def _conv_kernel(x_ref, halo_ref, w_ref, b_ref, sh_ref, o_ref, ext_scr, *, Lc, L):
    pad, E = CONV_K // 2, CONV_EDGE
    ext_scr[0:E, :] = halo_ref[0, 0, 0:E, :]
    ext_scr[E:E + Lc, :] = x_ref[0]
    ext_scr[E + Lc:2 * E + Lc, :] = halo_ref[0, 0, E:2 * E, :]
    taps = [w_ref[j:j + 1, :] for j in range(CONV_K)]
    scaled = lambda rows, j: (rows.astype(F32) * taps[j]).astype(BF16)
    for r0 in range(0, Lc, L):
        cen = ext_scr[E + r0:E + r0 + L, :]
        top = ext_scr[r0:r0 + E, :]
        bot = ext_scr[E + r0 + L:2 * E + r0 + L, :]
        pieces = [scaled(cen, j) for j in range(CONV_K)]
        pieces += [scaled(top, j) for j in range(pad)] + [scaled(bot, j) for j in range(pad + 1, CONV_K)]
        acc = _dot(sh_ref[...], jnp.concatenate(pieces, axis=0)) + b_ref[...]
        o_ref[0, r0:r0 + L, :] = (acc * _sigmoid(acc)).astype(o_ref.dtype)


def _conv(pa3, halo, conv_w8, conv_b, Lc=512):
    B, S, _ = pa3.shape
    C = SSM_CONV_DIM
    xbc_blk = PA_XBC0 // C
    L = CHUNK
    sh = _conv_shift_matrix(L)
    return pl.pallas_call(
        functools.partial(_conv_kernel, Lc=Lc, L=L),
        grid=(B, S // Lc),
        in_specs=[
            pl.BlockSpec((1, Lc, C), lambda b, i: (b, i, xbc_blk)),
            pl.BlockSpec((1, 1, 2 * CONV_EDGE, C), lambda b, i: (b, i, 0, 0)),
            pl.BlockSpec((8, C), lambda b, i: (0, 0)),
            pl.BlockSpec((1, C), lambda b, i: (0, 0)),
            pl.BlockSpec(sh.shape, lambda b, i: (0, 0)),
        ],
        out_specs=pl.BlockSpec((1, Lc, C), lambda b, i: (b, i, 0)),
        out_shape=jax.ShapeDtypeStruct((B, S, C), BF16),
        scratch_shapes=[pltpu.VMEM((Lc + 2 * CONV_EDGE, C), BF16)],
        compiler_params=_cparams(("parallel", "parallel")),
        name="ssm_conv",
    )(pa3, halo, conv_w8, conv_b, sh)


def _expansion_matrix(src_cols, width):
    src = jnp.repeat(jnp.asarray(src_cols, jnp.int32), width)
    e = (jnp.arange(128, dtype=jnp.int32)[:, None] == src[None, :]).astype(BF16)
    return jnp.concatenate([e, e], axis=0)


def _split2(t):
    hi = t.astype(BF16)
    return jnp.concatenate([hi, (t - hi.astype(F32)).astype(BF16)], axis=1)


def _ssd_bwd_kernel(x_ref, b_ref, sc_ref, dec_ref, ew_ref, hb_ref, h_scr, *, L):
    G, hg, P, N = SSM_GROUPS, SSM_HEADS // SSM_GROUPS, SSM_HEADDIM, SSM_STATE
    GW = hg * P

    @pl.when(pl.program_id(1) == 0)
    def _():
        h_scr[...] = jnp.zeros_like(h_scr)

    w_all = _dot(_split2(sc_ref[0]), ew_ref[...])
    for g in range(G):
        w = w_all[:, g * GW:(g + 1) * GW]
        xs = (x_ref[0, :, g * GW:(g + 1) * GW].astype(F32) * w).astype(BF16)
        st = _dot_tn(b_ref[0, :, g * N:(g + 1) * N], xs)
        h_old = h_scr[g]
        hb_ref[0, 0, g] = h_old.astype(hb_ref.dtype)
        h_scr[g] = dec_ref[0, 0, 1:2, g * GW:(g + 1) * GW] * h_old + st


def _ssd_bwd(xc, sc, decs):
    B, S, _ = xc.shape
    nc = S // CHUNK
    G, N, W = SSM_GROUPS, SSM_STATE, SSM_WIDTH
    return pl.pallas_call(
        functools.partial(_ssd_bwd_kernel, L=CHUNK),
        grid=(B, nc),
        in_specs=[
            pl.BlockSpec((1, CHUNK, W), lambda b, c: (b, nc - 1 - c, 0)),
            pl.BlockSpec((1, CHUNK, G * N), lambda b, c: (b, nc - 1 - c, W // (G * N))),
            pl.BlockSpec((1, CHUNK, 128), lambda b, c: (b, nc - 1 - c, 0)),
            pl.BlockSpec((1, 1, 2, W), lambda b, c: (b, nc - 1 - c, 0, 0)),
            pl.BlockSpec((256, W), lambda b, c: (0, 0)),
        ],
        out_specs=pl.BlockSpec((1, 1, G, N, W // G), lambda b, c: (b, nc - 1 - c, 0, 0, 0)),
        out_shape=jax.ShapeDtypeStruct((B, nc, G, N, W // G), BF16),
        scratch_shapes=[pltpu.VMEM((G, N, W // G), F32)],
        compiler_params=_cparams(("parallel", "arbitrary")),
        name="ssd_bwd_states",
    )(xc, xc, sc, decs, _expansion_matrix([5 * SSM_HEADS + h for h in range(SSM_HEADS)], SSM_HEADDIM))


def _ssd_main_kernel(x_ref, b_ref, c_ref, z_ref, sg_ref, sc_ref, dec_ref, hb_ref, dsk_ref, nw_ref, ed_ref, ee_ref,
                     y_ref, h_scr, *, L):
    G, hg, P, N = SSM_GROUPS, SSM_HEADS // SSM_GROUPS, SSM_HEADDIM, SSM_STATE
    H = SSM_HEADS
    GW = hg * P

    @pl.when(pl.program_id(1) == 0)
    def _():
        h_scr[...] = jnp.zeros_like(h_scr)

    row_i = lax.broadcasted_iota(jnp.int32, (L, L), 0)
    col_i = lax.broadcasted_iota(jnp.int32, (L, L), 1)
    lower = col_i <= row_i
    upper = col_i >= row_i
    lane_g = lax.broadcasted_iota(jnp.int32, (L, GW), 1)

    sg = sg_ref[0, 0]
    sc2 = _split2(sc_ref[0])
    for g in range(G):
        xg = x_ref[0, :, g * GW:(g + 1) * GW]
        bg = b_ref[0, :, g * N:(g + 1) * N]
        cg = c_ref[0, :, g * N:(g + 1) * N]
        cb = _dot_nt(cg, bg)
        acol = _dot(sc2, ed_ref[g])
        ecol = _dot(sc2, ee_ref[g])
        ms, xms = [], []
        for j in range(hg):
            hd = g * hg + j
            af_r, ab_r = sg[hd:hd + 1], sg[H + hd:H + hd + 1]
            dtf_r, dtb_r = sg[2 * H + hd:2 * H + hd + 1], sg[3 * H + hd:3 * H + hd + 1]
            af_c, ab_c = acol[:, j * L:(j + 1) * L], acol[:, (hg + j) * L:(hg + j + 1) * L]
            fm = jnp.exp(jnp.where(lower, af_c - af_r, NEG_INF)) * dtf_r
            bm = jnp.exp(jnp.where(upper, ab_c - ab_r, NEG_INF)) * dtb_r
            ms.append((cb * (fm + bm)).astype(BF16))
            xms.append(jnp.where((lane_g >= j * P) & (lane_g < (j + 1) * P), xg, jnp.zeros_like(xg)))
        efc, ebc, wfc = ecol[:, 0:GW], ecol[:, GW:2 * GW], ecol[:, 2 * GW:]
        y = _dot(jnp.concatenate(ms, axis=1), jnp.concatenate(xms, axis=0))
        h_old = h_scr[g]
        y = y + efc * _dot(cg, h_old.astype(BF16))
        y = y + ebc * _dot(cg, hb_ref[0, 0, g])
        xgf = xg.astype(F32)
        y = y + dsk_ref[:, g * GW:(g + 1) * GW] * xgf
        xs = (xgf * wfc).astype(BF16)
        h_scr[g] = dec_ref[0, 0, 0:1, g * GW:(g + 1) * GW] * h_old + _dot_tn(bg, xs)
        zg = z_ref[0, :, g * GW:(g + 1) * GW].astype(F32)
        y = y * (zg * _sigmoid(zg))
        y = y * lax.rsqrt(jnp.mean(y * y, axis=1, keepdims=True) + RMS_EPS)
        y_ref[0, :, g * GW:(g + 1) * GW] = (y * nw_ref[:, g * GW:(g + 1) * GW]).astype(y_ref.dtype)


def _ssd_main(xc, pa3, sg, sc, decs, hb, dskip_row, norm_w):
    B, S, _ = xc.shape
    nc = S // CHUNK
    G, N, W = SSM_GROUPS, SSM_STATE, SSM_WIDTH
    H, hg, P = SSM_HEADS, SSM_HEADS // SSM_GROUPS, SSM_HEADDIM
    z_blk = PA_Z0 // W
    e_dec = jnp.stack([_expansion_matrix([q * H + g * hg + j for q in (0, 1) for j in range(hg)], CHUNK)
                       for g in range(G)])
    e_ew = jnp.stack([_expansion_matrix([q * H + g * hg + j for q in (2, 3, 4) for j in range(hg)], P)
                      for g in range(G)])
    return pl.pallas_call(
        functools.partial(_ssd_main_kernel, L=CHUNK),
        grid=(B, nc),
        in_specs=[
            pl.BlockSpec((1, CHUNK, W), lambda b, c: (b, c, 0)),
            pl.BlockSpec((1, CHUNK, G * N), lambda b, c: (b, c, W // (G * N))),
            pl.BlockSpec((1, CHUNK, G * N), lambda b, c: (b, c, W // (G * N) + 1)),
            pl.BlockSpec((1, CHUNK, W), lambda b, c: (b, c, z_blk)),
            pl.BlockSpec((1, 1, 4 * SSM_HEADS, CHUNK), lambda b, c: (b, c, 0, 0)),
            pl.BlockSpec((1, CHUNK, 128), lambda b, c: (b, c, 0)),
            pl.BlockSpec((1, 1, 2, W), lambda b, c: (b, c, 0, 0)),
            pl.BlockSpec((1, 1, G, N, W // G), lambda b, c: (b, c, 0, 0, 0)),
            pl.BlockSpec((1, W), lambda b, c: (0, 0)),
            pl.BlockSpec((1, W), lambda b, c: (0, 0)),
            pl.BlockSpec((G, 256, 2 * hg * CHUNK), lambda b, c: (0, 0, 0)),
            pl.BlockSpec((G, 256, 3 * W // G), lambda b, c: (0, 0, 0)),
        ],
        out_specs=pl.BlockSpec((1, CHUNK, W), lambda b, c: (b, c, 0)),
        out_shape=jax.ShapeDtypeStruct((B, S, W), BF16),
        scratch_shapes=[pltpu.VMEM((G, N, W // G), F32)],
        compiler_params=_cparams(("parallel", "arbitrary")),
        name="ssd_main",
    )(xc, xc, xc, pa3, sg, sc, decs, hb, dskip_row, norm_w, e_dec, e_ew)


def _pack_bf16_pairs(v):
    n = v.shape[1] // 2
    hi = lax.bitcast_convert_type(v[:, :n].astype(BF16).astype(F32), jnp.uint32)
    lo = lax.bitcast_convert_type(v[:, n:].astype(BF16).astype(F32), jnp.uint32)
    return (hi & jnp.uint32(0xFFFF0000)) | (lo >> 16)


def _unpack_bf16_pairs(u):
    hi = lax.bitcast_convert_type(u & jnp.uint32(0xFFFF0000), F32)
    lo = lax.bitcast_convert_type(u << 16, F32)
    return hi, lo


def _layer_norm(v, g, b):
    mu = jnp.mean(v, axis=1, keepdims=True)
    d = v - mu
    var = jnp.mean(d * d, axis=1, keepdims=True)
    return d * lax.rsqrt(var + LN_EPS) * g + b


def _outproj_kernel(yml_ref, yssm_ref, x_ref, wo_ref, g_ref, b_ref, wr_ref, h_ref, hpa_ref, hpb_ref, sc_ref):
    mix = _dot(yml_ref[...], wo_ref[0:ML_WIDTH, :]) + _dot(yssm_ref[...], wo_ref[ML_WIDTH:, :])
    h = _layer_norm(ALPHA * x_ref[...] + mix, g_ref[...], b_ref[...])
    h_ref[...] = h
    hp = _pack_bf16_pairs(h)
    q = hp.shape[1] // 2
    hpa_ref[...] = hp[:, :q]
    hpb_ref[...] = hp[:, q:]
    sc_ref[...] = _sigmoid(_dot(h.astype(BF16), wr_ref[...]))


def _outproj(yml, yssm, x2d, w_out, ln_g, ln_b, w_router, tm=512):
    T, D = x2d.shape
    tm = min(tm, T)
    row = lambda i: (i, 0)
    fixed = lambda i: (0, 0)
    return pl.pallas_call(
        _outproj_kernel,
        grid=(T // tm,),
        in_specs=[
            pl.BlockSpec((tm, ML_WIDTH), row),
            pl.BlockSpec((tm, SSM_WIDTH), row),
            pl.BlockSpec((tm, D), row),
            pl.BlockSpec((ML_WIDTH + SSM_WIDTH, D), fixed),
            pl.BlockSpec((1, D), fixed),
            pl.BlockSpec((1, D), fixed),
            pl.BlockSpec((D, 128), fixed),
        ],
        out_specs=[pl.BlockSpec((tm, D), row), pl.BlockSpec((tm, D // 4), row), pl.BlockSpec((tm, D // 4), row),
                   pl.BlockSpec((tm, 128), row)],
        out_shape=[
            jax.ShapeDtypeStruct((T, D), F32),
            jax.ShapeDtypeStruct((T, D // 4), jnp.uint32),
            jax.ShapeDtypeStruct((T, D // 4), jnp.uint32),
            jax.ShapeDtypeStruct((T, 128), F32),
        ],
        compiler_params=_cparams(("parallel",)),
        name="out_proj_ln1_router",
    )(yml, yssm, x2d, w_out, ln_g, ln_b, w_router)


def _route_kernel(sc_ref, rb_ref, idx_ref, w_ref, rank_ref, cnt_ref, carry_scr, *, Tt):
    E, NG = N_EXPERTS, N_EXPERT_GROUPS
    EG = E // NG

    @pl.when(pl.program_id(0) == 0)
    def _():
        carry_scr[...] = jnp.zeros_like(carry_scr)

    s = sc_ref[...].T[0:E]
    s3 = s.reshape(NG, EG, Tt)
    ch3 = (s + rb_ref[:, 0:1]).reshape(NG, EG, Tt)
    e_in = lax.broadcasted_iota(jnp.int32, (NG, EG, Tt), 1)
    g_in = lax.broadcasted_iota(jnp.int32, (NG, EG, Tt), 0)
    f_i = g_in * EG + e_in
    m1 = jnp.max(ch3, axis=1, keepdims=True)
    i1 = jnp.min(jnp.where(ch3 == m1, e_in, EG), axis=1, keepdims=True)
    m2 = jnp.max(jnp.where(e_in == i1, NEG_INF, ch3), axis=1, keepdims=True)
    gs = m1 + m2
    g_i = lax.broadcasted_iota(jnp.int32, (NG, 1, Tt), 0)
    gsel = jnp.zeros((NG, 1, Tt), jnp.bool_)
    cur = gs
    for _ in range(TOPK_GROUPS):
        mx = jnp.max(cur, axis=0, keepdims=True)
        pick = jnp.min(jnp.where(cur == mx, g_i, NG), axis=0, keepdims=True)
        hit = g_i == pick
        gsel = gsel | hit
        cur = jnp.where(hit, NEG_INF, cur)
    cur = jnp.where(gsel, ch3, NEG_INF)
    selmask = jnp.zeros((NG, EG, Tt), jnp.bool_)
    idxs, ws = [], []
    for _ in range(TOP_K):
        mx = jnp.max(jnp.max(cur, axis=1, keepdims=True), axis=0, keepdims=True)
        cand = jnp.where(cur == mx, f_i, E)
        pick = jnp.min(jnp.min(cand, axis=1, keepdims=True), axis=0, keepdims=True)
        hit = f_i == pick
        wk = jnp.sum(jnp.sum(jnp.where(hit, s3, 0.0), axis=1, keepdims=True), axis=0, keepdims=True)
        idxs.append(pick.reshape(1, Tt))
        ws.append(wk.reshape(1, Tt))
        selmask = selmask | hit
        cur = jnp.where(hit, NEG_INF, cur)
    wsum = ws[0]
    for k in range(1, TOP_K):
        wsum = wsum + ws[k]
    r = lax.broadcasted_iota(jnp.int32, (Tt, Tt), 0)
    c = lax.broadcasted_iota(jnp.int32, (Tt, Tt), 1)
    ustrict = (r < c).astype(BF16)
    self32 = jnp.where(selmask, 1.0, 0.0).reshape(E, Tt)
    excl = _dot(self32.astype(BF16), ustrict) + carry_scr[:, 0:1]
    excl3 = excl.reshape(NG, EG, Tt)
    for k in range(TOP_K):
        hit = f_i == idxs[k].reshape(1, 1, Tt)
        rk = jnp.sum(jnp.sum(jnp.where(hit, excl3, 0.0), axis=1, keepdims=True), axis=0, keepdims=True)
        idx_ref[k:k + 1, :] = idxs[k]
        w_ref[k:k + 1, :] = ws[k] / wsum * ROUTED_SCALE
        rank_ref[k:k + 1, :] = rk.reshape(1, Tt).astype(jnp.int32)
    carry_scr[...] = carry_scr[...] + jnp.sum(self32, axis=1, keepdims=True)
    cnt_ref[...] = carry_scr[...]


def _route(scores, rbias, Tt=512):
    T = scores.shape[0]
    kt = lambda i: (0, i)
    return pl.pallas_call(
        functools.partial(_route_kernel, Tt=Tt),
        grid=(T // Tt,),
        in_specs=[pl.BlockSpec((Tt, 128), lambda i: (i, 0)), pl.BlockSpec((N_EXPERTS, 128), lambda i: (0, 0))],
        out_specs=[
            pl.BlockSpec((TOP_K, Tt), kt),
            pl.BlockSpec((TOP_K, Tt), kt),
            pl.BlockSpec((TOP_K, Tt), kt),
            pl.BlockSpec((N_EXPERTS, 128), lambda i: (0, 0)),
        ],
        out_shape=[
            jax.ShapeDtypeStruct((TOP_K, T), jnp.int32),
            jax.ShapeDtypeStruct((TOP_K, T), F32),
            jax.ShapeDtypeStruct((TOP_K, T), jnp.int32),
            jax.ShapeDtypeStruct((N_EXPERTS, 128), F32),
        ],
        scratch_shapes=[pltpu.VMEM((N_EXPERTS, 128), F32)],
        compiler_params=_cparams(("arbitrary",)),
        name="route_topk",
    )(scores, rbias)


def _dest_kernel(ps_ref, idx_ref, rank_ref, o_ref):
    idx = idx_ref[...]
    start = jnp.zeros(idx.shape, jnp.int32)
    for e in range(N_EXPERTS):
        start = jnp.where(idx == e, ps_ref[e], start)
    o_ref[...] = start + rank_ref[...]


def _dest(pstarts, idx, rank, Tt=2048):
    K, T = idx.shape
    Tt = min(Tt, T)
    blk = pl.BlockSpec((K, Tt), lambda i, ps: (0, i))
    return pl.pallas_call(
        _dest_kernel,
        grid_spec=pltpu.PrefetchScalarGridSpec(num_scalar_prefetch=1, grid=(T // Tt,), in_specs=[blk, blk],
                                               out_specs=blk),
        out_shape=jax.ShapeDtypeStruct((K, T), jnp.int32),
        compiler_params=_cparams(("parallel",)),
        name="dispatch_slots",
    )(pstarts, idx, rank)


def _sc_mesh():
    return plsc.VectorSubcoreMesh(core_axis_name="c", subcore_axis_name="s")


def _sc_scatter_rows(src, idx, n_out, win=SC_WINDOW):
    n = idx.shape[0]
    T, W = src.shape
    nt = T // win

    @functools.partial(pl.kernel, out_type=jax.ShapeDtypeStruct((n_out, W), src.dtype), mesh=_sc_mesh(),
                       scratch_types=[], name="sc_scatter_rows")
    def k(x_hbm, i_hbm, o_hbm):
        def body(x_vmem, i_vmem):
            pltpu.sync_copy(x_vmem, o_hbm.at[i_vmem.at[0]])

        pltpu.emit_pipeline(
            body,
            grid=(n // win,),
            in_specs=[pl.BlockSpec((win, W), lambda i: (i % nt, 0)), pl.BlockSpec((1, win), lambda i: (0, i))],
            out_specs=[],
            core_axis_name=("c", "s"),
            dimension_semantics=(pltpu.PARALLEL,),
        )(x_hbm, i_hbm)

    return k(src, idx.reshape(1, n))


def _sc_gather_rows(table, idx, win=SC_WINDOW):
    n = idx.shape[0]
    W = table.shape[1]

    @functools.partial(pl.kernel, out_type=jax.ShapeDtypeStruct((n, W), table.dtype), mesh=_sc_mesh(),
                       scratch_types=[], name="sc_gather_rows")
    def k(t_hbm, i_hbm, o_hbm):
        def body(i_vmem, o_vmem):
            pltpu.sync_copy(t_hbm.at[i_vmem.at[0]], o_vmem)

        pltpu.emit_pipeline(
            body,
            grid=(n // win,),
            in_specs=[pl.BlockSpec((1, win), lambda i: (0, i))],
            out_specs=[pl.BlockSpec((win, W), lambda i: (i, 0))],
            core_axis_name=("c", "s"),
            dimension_semantics=(pltpu.PARALLEL,),
        )(i_hbm, o_hbm)

    return k(table, idx.reshape(1, n))


def _expert_kernel(be_ref, nu_ref, xa_ref, xb_ref, wg_ref, wu_ref, wd_ref, oa_ref, ob_ref, wg_s, wu_s, wd_s):
    half = D_MODEL // 2
    i = pl.program_id(0)
    active = i < nu_ref[0]

    @pl.when(active & ((i == 0) | (be_ref[i] != be_ref[jnp.maximum(i - 1, 0)])))
    def _():
        wg_s[...] = wg_ref[0].astype(BF16)
        wu_s[...] = wu_ref[0].astype(BF16)
        wd_s[...] = wd_ref[0].astype(BF16)

    @pl.when(active)
    def _():
        hi, lo = _unpack_bf16_pairs(jnp.concatenate([xa_ref[...], xb_ref[...]], axis=1))
        hi, lo = hi.astype(BF16), lo.astype(BF16)
        gt = _dot(hi, wg_s[0:half, :]) + _dot(lo, wg_s[half:, :])
        up = _dot(hi, wu_s[0:half, :]) + _dot(lo, wu_s[half:, :])
        hmid = (gt * _sigmoid(gt) * up).astype(BF16)
        packed = _pack_bf16_pairs(_dot(hmid, wd_s[...]))
        q = packed.shape[1] // 2
        oa_ref[...] = packed[:, :q]
        ob_ref[...] = packed[:, q:]

    @pl.when(jnp.logical_not(active))
    def _():
        oa_ref[...] = jnp.zeros_like(oa_ref)
        ob_ref[...] = jnp.zeros_like(ob_ref)


def _experts(xs_a, xs_b, block_expert, n_used, w_gate, w_up, w_down, blk=MOE_BLOCK):
    P, quarter = xs_a.shape
    nb = P // blk
    D, F = D_MODEL, D_EXPERT
    cur = lambda i, be, nu: jnp.minimum(i, nu[0] - 1)
    return pl.pallas_call(
        _expert_kernel,
        grid_spec=pltpu.PrefetchScalarGridSpec(
            num_scalar_prefetch=2,
            grid=(nb,),
            in_specs=[
                pl.BlockSpec((blk, quarter), lambda i, be, nu: (cur(i, be, nu), 0)),
                pl.BlockSpec((blk, quarter), lambda i, be, nu: (cur(i, be, nu), 0)),
                pl.BlockSpec((1, D, F), lambda i, be, nu: (be[cur(i, be, nu)], 0, 0)),
                pl.BlockSpec((1, D, F), lambda i, be, nu: (be[cur(i, be, nu)], 0, 0)),
                pl.BlockSpec((1, F, D), lambda i, be, nu: (be[cur(i, be, nu)], 0, 0)),
            ],
            out_specs=[pl.BlockSpec((blk, quarter), lambda i, be, nu: (i, 0)),
                       pl.BlockSpec((blk, quarter), lambda i, be, nu: (i, 0))],
            scratch_shapes=[pltpu.VMEM((D, F), BF16), pltpu.VMEM((D, F), BF16), pltpu.VMEM((F, D), BF16)],
        ),
        out_shape=[jax.ShapeDtypeStruct((P, quarter), jnp.uint32), jax.ShapeDtypeStruct((P, quarter), jnp.uint32)],
        compiler_params=_cparams(("arbitrary",)),
        name="expert_mlp",
    )(block_expert, n_used, xs_a, xs_b, w_gate, w_up, w_down)


def _final_kernel(h_ref, ga_ref, gb_ref, wt_ref, wsg_ref, wsu_ref, wsd_ref, g_ref, b_ref, o_ref):
    h = h_ref[...]
    hb = h.astype(BF16)
    gt = _dot(hb, wsg_ref[...])
    up = _dot(hb, wsu_ref[...])
    shared = _dot((gt * _sigmoid(gt) * up).astype(BF16), wsd_ref[...])
    acc_hi = None
    for k in range(TOP_K):
        hi, lo = _unpack_bf16_pairs(jnp.concatenate([ga_ref[k], gb_ref[k]], axis=1))
        wk = wt_ref[:, k:k + 1]
        acc_hi = wk * hi if acc_hi is None else acc_hi + wk * hi
        acc_lo = wk * lo if k == 0 else acc_lo + wk * lo
    routed = jnp.concatenate([acc_hi, acc_lo], axis=1)
    o_ref[...] = _layer_norm(ALPHA * h + (routed + shared), g_ref[...], b_ref[...])


def _final(h, gk_a, gk_b, wt, ws_gate, ws_up, ws_down, ln_g, ln_b, tm=512):
    T, D = h.shape
    tm = min(tm, T)
    F = D_EXPERT
    fixed = lambda i: (0, 0)
    return pl.pallas_call(
        _final_kernel,
        grid=(T // tm,),
        in_specs=[
            pl.BlockSpec((tm, D), lambda i: (i, 0)),
            pl.BlockSpec((TOP_K, tm, D // 4), lambda i: (0, i, 0)),
            pl.BlockSpec((TOP_K, tm, D // 4), lambda i: (0, i, 0)),
            pl.BlockSpec((tm, TOP_K), lambda i: (i, 0)),
            pl.BlockSpec((D, F), fixed),
            pl.BlockSpec((D, F), fixed),
            pl.BlockSpec((F, D), fixed),
            pl.BlockSpec((1, D), fixed),
            pl.BlockSpec((1, D), fixed),
        ],
        out_specs=pl.BlockSpec((tm, D), lambda i: (i, 0)),
        out_shape=jax.ShapeDtypeStruct((T, D), F32),
        compiler_params=_cparams(("parallel",)),
        name="shared_combine_ln2",
    )(h, gk_a, gk_b, wt, ws_gate, ws_up, ws_down, ln_g, ln_b)


def _mixers(x, w_in, ml_i_bias, ml_f_bias, ml_norm_w, conv_w, conv_b, dt_bias, a_log, d_skip, ssm_norm_w):
    B, S, D = x.shape
    T = B * S
    nc = S // CHUNK
    H = ML_HEADS
    k0, v0 = ML_QK, 2 * ML_QK
    g0 = v0 + 2 * ML_WIDTH
    z0 = g0 + 4 * H
    x0 = z0 + SSM_WIDTH
    d0 = x0 + SSM_CONV_DIM
    w_big = jnp.concatenate([w_in[:, x0:d0], w_in[:, v0:g0], w_in[:, z0:x0], w_in[:, :k0]], axis=1).astype(BF16)
    w_small = jnp.concatenate(
        [w_in[:, g0:z0], w_in[:, d0:], jnp.zeros((D, PB_COLS - N_GATE_ROWS), w_in.dtype)], axis=1).astype(BF16)
    w_kt = (w_in[:, k0:v0] * (ML_DQK ** -0.5)).T.astype(BF16)
    pa, pb, kt = _proj(x.reshape(T, D), w_big, w_small, w_kt)
    pa3 = pa.reshape(B, S, PA_COLS)
    kt5 = kt.reshape(B, nc, H, ML_DQK, CHUNK)

    g_rows = pb.reshape(B, S, PB_COLS)[:, :, :N_GATE_ROWS].transpose(0, 2, 1).reshape(B, N_GATE_ROWS, nc, CHUNK)
    bias = jnp.concatenate([ml_i_bias, ml_f_bias, dt_bias]).astype(F32)
    bias_rows = jnp.broadcast_to(bias[:, None, None], (N_GATE_ROWS, 1, CHUNK))
    aneg_rows = jnp.broadcast_to((-jnp.exp(a_log.astype(F32)))[:, None, None], (2 * SSM_HEADS, 1, CHUNK))
    prep = _gate_prep(g_rows, bias_rows, aneg_rows)
    m8 = jnp.stack([prep[:, 0:H], prep[:, 2 * H:3 * H], prep[:, H:2 * H], prep[:, 3 * H:4 * H],
                    prep[:, 4 * H:5 * H], prep[:, 5 * H:6 * H], prep[:, 6 * H:7 * H], prep[:, 7 * H:8 * H]],
                   axis=2)
    mg = m8.transpose(0, 1, 3, 2, 4)
    mc = m8.transpose(0, 1, 3, 4, 2).reshape(B, H, S, 8)
    s0, SH = PREP_SSM0, SSM_HEADS
    sg = prep[:, s0:s0 + 4 * SH].transpose(0, 2, 1, 3)
    sc = jnp.concatenate([prep[:, s0:s0 + 2 * SH], prep[:, s0 + 4 * SH:s0 + 8 * SH]], axis=1)
    sc = sc.transpose(0, 2, 3, 1).reshape(B, S, 6 * SH)
    sc = jnp.concatenate([sc, jnp.zeros((B, S, 128 - 6 * SH), F32)], axis=2)
    decs = prep[:, s0 + 8 * SH:s0 + 10 * SH, :, 0].reshape(B, 2, SH, nc).transpose(0, 3, 1, 2)
    decs = jnp.repeat(decs, SSM_HEADDIM, axis=3)

    y_ml = _mlstm(pa3, kt5, mg, mc, ml_norm_w.astype(F32).reshape(1, ML_WIDTH))

    Lc = 512 if S % 512 == 0 else CHUNK
    nt = S // Lc
    pa4 = pa3.reshape(B, nt, Lc, PA_COLS)
    top = pa4[:, :, Lc - CONV_EDGE:, PA_XBC0:PA_XBC0 + SSM_CONV_DIM]
    bot = pa4[:, :, :CONV_EDGE, PA_XBC0:PA_XBC0 + SSM_CONV_DIM]
    zrow = jnp.zeros((B, 1, CONV_EDGE, SSM_CONV_DIM), pa.dtype)
    top = jnp.concatenate([zrow, top[:, :-1]], axis=1)
    bot = jnp.concatenate([bot[:, 1:], zrow], axis=1)
    halo = jnp.concatenate([top, bot], axis=2)
    conv_w8 = jnp.concatenate([conv_w.astype(F32), jnp.zeros((8 - CONV_K, SSM_CONV_DIM), F32)], axis=0)
    xc = _conv(pa3, halo, conv_w8, conv_b.astype(F32).reshape(1, SSM_CONV_DIM), Lc=Lc)

    hb = _ssd_bwd(xc, sc, decs)
    dsk = jnp.repeat(d_skip.astype(F32), SSM_HEADDIM).reshape(1, SSM_WIDTH)
    y_ssm = _ssd_main(xc, pa3, sg, sc, decs, hb, dsk, ssm_norm_w.astype(F32).reshape(1, SSM_WIDTH))
    return y_ml.reshape(T, ML_WIDTH), y_ssm.reshape(T, SSM_WIDTH)


def _moe_dispatch(idx, rank, counts, T, blk):
    E = N_EXPERTS
    M = T * TOP_K
    nb = (M + E * (blk - 1)) // blk
    pcounts = (counts + blk - 1) // blk * blk
    pends = jnp.cumsum(pcounts)
    pstarts = (pends - pcounts).astype(jnp.int32)
    dest = _dest(pstarts, idx, rank)
    block_start = jnp.arange(nb, dtype=jnp.int32) * blk
    block_expert = jnp.minimum(jnp.sum(pends[None, :] <= block_start[:, None], axis=1), E - 1).astype(jnp.int32)
    n_used = (pends[-1] // blk).astype(jnp.int32).reshape(1)
    return dest, nb * blk, block_expert, n_used


def _layer(x, w_in, ml_i_bias, ml_f_bias, ml_norm_w, conv_w, conv_b, dt_bias, a_log, d_skip, ssm_norm_w,
           w_out, ln1_g, ln1_b, w_router, router_bias, w_exp_gate, w_exp_up, w_exp_down,
           w_sh_gate, w_sh_up, w_sh_down, ln2_g, ln2_b):
    B, S, D = x.shape
    T = B * S
    y_ml, y_ssm = _mixers(x, w_in, ml_i_bias, ml_f_bias, ml_norm_w, conv_w, conv_b, dt_bias, a_log, d_skip,
                          ssm_norm_w)
    w_r = jnp.concatenate([w_router, jnp.zeros((D, 128 - N_EXPERTS), w_router.dtype)], axis=1).astype(BF16)
    h, hp_a, hp_b, scores = _outproj(y_ml, y_ssm, x.reshape(T, D), w_out.astype(BF16),
                                     ln1_g.astype(F32).reshape(1, D), ln1_b.astype(F32).reshape(1, D), w_r)
    rbias = jnp.broadcast_to(router_bias.astype(F32)[:, None], (N_EXPERTS, 128))
    idx, wgt, rank, cnt = _route(scores, rbias)
    counts = cnt[:, 0].astype(jnp.int32)
    dest, n_slots, block_expert, n_used = _moe_dispatch(idx, rank, counts, T, MOE_BLOCK)
    dest_flat = dest.reshape(-1)
    xs_a = _sc_scatter_rows(hp_a, dest_flat, n_slots)
    xs_b = _sc_scatter_rows(hp_b, dest_flat, n_slots)
    os_a, os_b = _experts(xs_a, xs_b, block_expert, n_used, w_exp_gate, w_exp_up, w_exp_down)
    gk_a = _sc_gather_rows(os_a, dest_flat).reshape(TOP_K, T, D // 4)
    gk_b = _sc_gather_rows(os_b, dest_flat).reshape(TOP_K, T, D // 4)
    out = _final(h, gk_a, gk_b, wgt.T, w_sh_gate.astype(BF16), w_sh_up.astype(BF16),
                 w_sh_down.astype(BF16), ln2_g.astype(F32).reshape(1, D), ln2_b.astype(F32).reshape(1, D))
    return out.reshape(B, S, D)


def kernel(x, w_in, ml_i_bias, ml_f_bias, ml_norm_w, conv_w, conv_b, dt_bias, a_log, d_skip, ssm_norm_w, w_out,
           ln1_g, ln1_b, w_router, router_bias, w_exp_gate, w_exp_up, w_exp_down, w_sh_gate, w_sh_up, w_sh_down,
           ln2_g, ln2_b):
    for l in range(w_in.shape[0]):
        x = _layer(x, w_in[l], ml_i_bias[l], ml_f_bias[l], ml_norm_w[l], conv_w[l], conv_b[l], dt_bias[l],
                   a_log[l], d_skip[l], ssm_norm_w[l], w_out[l], ln1_g[l], ln1_b[l], w_router[l],
                   router_bias[l], w_exp_gate[l], w_exp_up[l], w_exp_down[l], w_sh_gate[l], w_sh_up[l],
                   w_sh_down[l], ln2_g[l], ln2_b[l])
    return x
```

```python
import functools

import jax
import jax.numpy as jnp
from jax import lax
from jax.experimental import pallas as pl
from jax.experimental.pallas import tpu as pltpu
from jax.experimental.pallas import tpu_sc as plsc

D_MODEL = 1024
ML_HEADS = 4
ML_DV = 256
ML_DQK = 128
ML_QK = ML_HEADS * ML_DQK
ML_WIDTH = ML_HEADS * ML_DV
SSM_WIDTH = 1024
SSM_HEADDIM = 64
SSM_HEADS = 16
SSM_GROUPS = 4
SSM_STATE = 128
SSM_CONV_DIM = SSM_WIDTH + 2 * SSM_GROUPS * SSM_STATE
CONV_K = 5
CHUNK = 128
N_EXPERTS = 64
TOP_K = 8
N_EXPERT_GROUPS = 8
TOPK_GROUPS = 4
D_EXPERT = 256
ROUTED_SCALE = 2.5
LN_EPS = 1e-5
RMS_EPS = 1e-5
DEPTH = 1
ALPHA = (2 * DEPTH) ** 0.25

PA_XBC0 = 0
PA_V0 = PA_XBC0 + SSM_CONV_DIM
PA_O0 = PA_V0 + ML_WIDTH
PA_Z0 = PA_O0 + ML_WIDTH
PA_Q0 = PA_Z0 + SSM_WIDTH
PA_COLS = PA_Q0 + ML_QK
PB_COLS = 128
N_GATE_ROWS = 2 * ML_HEADS + 2 * ML_HEADS + 2 * SSM_HEADS
PREP_SSM0 = 32
N_PREP_ROWS = PREP_SSM0 + 10 * SSM_HEADS

MOE_BLOCK = 1024
SC_WINDOW = 128
VMEM_LIMIT = 56 * 1024 * 1024

BF16 = jnp.bfloat16
F32 = jnp.float32
NEG_INF = float("-inf")


def _cparams(sem, vmem=VMEM_LIMIT):
    return pltpu.CompilerParams(dimension_semantics=sem, vmem_limit_bytes=vmem)


def _dot(a, b):
    return jnp.dot(a, b, preferred_element_type=F32)


def _dot_nt(a, b):
    return lax.dot_general(a, b, (((1,), (1,)), ((), ())), preferred_element_type=F32)


def _dot_tn(a, b):
    return lax.dot_general(a, b, (((0,), (0,)), ((), ())), preferred_element_type=F32)


def _sigmoid(x):
    return 1.0 / (1.0 + jnp.exp(-x))


def _softplus(x):
    return jnp.maximum(x, 0.0) + jnp.log1p(jnp.exp(-jnp.abs(x)))


def _proj_kernel(x_ref, wb_ref, ws_ref, wkt_ref, oa_ref, ob_ref, kt_ref, xb_scr, *, L):
    @pl.when(pl.program_id(1) == 0)
    def _():
        xb = x_ref[...].astype(BF16)
        xb_scr[...] = xb
        ob_ref[...] = _dot(xb, ws_ref[...])
        kt = _dot_nt(wkt_ref[...], xb).astype(kt_ref.dtype)
        for c in range(kt_ref.shape[0]):
            kt_ref[c] = kt[:, c * L:(c + 1) * L]

    oa_ref[...] = _dot(xb_scr[...], wb_ref[...]).astype(oa_ref.dtype)


def _proj(x2d, w_big, w_small, w_kt, tm=1024, tn=1408):
    T, D = x2d.shape
    N = w_big.shape[1]
    tm = min(tm, T)
    L = CHUNK
    return pl.pallas_call(
        functools.partial(_proj_kernel, L=L),
        grid=(T // tm, N // tn),
        in_specs=[
            pl.BlockSpec((tm, D), lambda i, j: (i, 0)),
            pl.BlockSpec((D, tn), lambda i, j: (0, j)),
            pl.BlockSpec((D, PB_COLS), lambda i, j: (0, 0)),
            pl.BlockSpec((ML_QK, D), lambda i, j: (0, 0)),
        ],
        out_specs=[
            pl.BlockSpec((tm, tn), lambda i, j: (i, j)),
            pl.BlockSpec((tm, PB_COLS), lambda i, j: (i, 0)),
            pl.BlockSpec((tm // L, ML_QK, L), lambda i, j: (i, 0, 0)),
        ],
        out_shape=[jax.ShapeDtypeStruct((T, N), BF16), jax.ShapeDtypeStruct((T, PB_COLS), F32),
                   jax.ShapeDtypeStruct((T // L, ML_QK, L), BF16)],
        scratch_shapes=[pltpu.VMEM((tm, D), BF16)],
        compiler_params=_cparams(("parallel", "arbitrary")),
        name="in_proj",
    )(x2d, w_big, w_small, w_kt)


def _split3_dot(x, u):
    hi = x.astype(BF16)
    r1 = x - hi.astype(F32)
    mid = r1.astype(BF16)
    lo = (r1 - mid.astype(F32)).astype(BF16)
    return _dot(hi, u) + _dot(mid, u) + _dot(lo, u)


def _gate_kernel(g_ref, bias_ref, aneg_ref, o_ref, *, nc, L):
    g = g_ref[0] + bias_ref[...]
    nh = ML_HEADS
    ig = g[0:2 * nh]
    fpre = g[2 * nh:4 * nh]
    lf = -_softplus(-fpre)
    dt = _softplus(g[4 * nh:])
    dA = dt * aneg_ref[...]
    r = lax.broadcasted_iota(jnp.int32, (L, L), 0)
    c = lax.broadcasted_iota(jnp.int32, (L, L), 1)
    u_pre = (r <= c).astype(BF16)
    u_suf = (r >= c).astype(BF16)
    nf = nh + SSM_HEADS
    xf = jnp.concatenate([lf[0:nh], dA[0:SSM_HEADS]], axis=0).reshape(nf * nc, L)
    xb = jnp.concatenate([lf[nh:], dA[SSM_HEADS:]], axis=0).reshape(nf * nc, L)
    cf = _split3_dot(xf, u_pre).reshape(nf, nc, L)
    cb = _split3_dot(xb, u_suf).reshape(nf, nc, L)
    totf = jnp.broadcast_to(cf[:, :, L - 1:L], cf.shape)
    totb = jnp.broadcast_to(cb[:, :, 0:1], cb.shape)
    o_ref[0, 0:nh] = cf[0:nh]
    o_ref[0, nh:2 * nh] = cb[0:nh]
    o_ref[0, 2 * nh:4 * nh] = ig
    o_ref[0, 4 * nh:5 * nh] = totf[0:nh] - cf[0:nh] + ig[0:nh]
    o_ref[0, 5 * nh:6 * nh] = totb[0:nh] - cb[0:nh] + ig[nh:]
    lane = lax.broadcasted_iota(jnp.int32, (nh * nc, L), 1)
    pf = (ig[0:nh] - cf[0:nh]).reshape(nh * nc, L)
    pb = (ig[nh:] - cb[0:nh]).reshape(nh * nc, L)
    sh = 1
    while sh < L:
        pf = jnp.where(lane >= sh, jnp.maximum(pf, pltpu.roll(pf, sh, 1)), pf)
        pb = jnp.where(lane < L - sh, jnp.maximum(pb, pltpu.roll(pb, L - sh, 1)), pb)
        sh *= 2
    o_ref[0, 6 * nh:7 * nh] = cf[0:nh] + pf.reshape(nh, nc, L)
    o_ref[0, 7 * nh:8 * nh] = cb[0:nh] + pb.reshape(nh, nc, L)
    H = SSM_HEADS
    af, ab, dtf, dtb = cf[nh:], cb[nh:], dt[0:H], dt[H:]
    s0 = PREP_SSM0
    o_ref[0, s0:s0 + H] = af
    o_ref[0, s0 + H:s0 + 2 * H] = ab
    o_ref[0, s0 + 2 * H:s0 + 3 * H] = dtf
    o_ref[0, s0 + 3 * H:s0 + 4 * H] = dtb
    o_ref[0, s0 + 4 * H:s0 + 5 * H] = jnp.exp(af)
    o_ref[0, s0 + 5 * H:s0 + 6 * H] = jnp.exp(ab)
    o_ref[0, s0 + 6 * H:s0 + 7 * H] = dtf * jnp.exp(totf[nh:] - af)
    o_ref[0, s0 + 7 * H:s0 + 8 * H] = dtb * jnp.exp(totb[nh:] - ab)
    o_ref[0, s0 + 8 * H:s0 + 9 * H] = jnp.exp(totf[nh:])
    o_ref[0, s0 + 9 * H:s0 + 10 * H] = jnp.exp(totb[nh:])


def _gate_prep(g_rows, bias_rows, aneg_rows):
    B, R, nc, L = g_rows.shape
    return pl.pallas_call(
        functools.partial(_gate_kernel, nc=nc, L=L),
        grid=(B,),
        in_specs=[
            pl.BlockSpec((1, R, nc, L), lambda b: (b, 0, 0, 0)),
            pl.BlockSpec((R, 1, L), lambda b: (0, 0, 0)),
            pl.BlockSpec((2 * SSM_HEADS, 1, L), lambda b: (0, 0, 0)),
        ],
        out_specs=pl.BlockSpec((1, N_PREP_ROWS, nc, L), lambda b: (b, 0, 0, 0)),
        out_shape=jax.ShapeDtypeStruct((B, N_PREP_ROWS, nc, L), F32),
        compiler_params=_cparams(("parallel",)),
        name="gate_prep",
    )(g_rows, bias_rows, aneg_rows)


def _mlstm_kernel(q_ref, kt_ref, v_ref, o_ref, mg_ref, mc_ref, nw_ref, y_ref, hf_scr, hb_scr, c_scr, n_scr, m_scr,
                  *, nc, L):
    row_i = lax.broadcasted_iota(jnp.int32, (L, L), 0)
    col_i = lax.broadcasted_iota(jnp.int32, (L, L), 1)
    ones = jnp.ones((L, L), BF16)

    c_scr[...] = jnp.zeros_like(c_scr)
    n_scr[...] = jnp.zeros_like(n_scr)
    m_scr[...] = jnp.full_like(m_scr, NEG_INF)

    def chunk(c, d):
        sl = pl.ds(pl.multiple_of(c * L, L), L)
        q = q_ref[0, sl, :]
        kt = kt_ref[0, c, 0]
        vo = jnp.concatenate([v_ref[0, sl, :], ones], axis=1)
        g8 = mg_ref[0, 0, c]
        ct = mc_ref[0, 0, sl, :]
        b_r, i_r, a_r = g8[2 * d:2 * d + 1], g8[2 * d + 1:2 * d + 2], g8[4 + d:5 + d]
        b_c = jnp.broadcast_to(ct[:, 2 * d:2 * d + 1], (L, L))
        mi_c = jnp.broadcast_to(ct[:, 6 + d:7 + d], (L, L))
        if d == 0:
            gtot = b_r[:, L - 1:L]
            mask = col_i <= row_i
        else:
            gtot = b_r[:, 0:1]
            mask = col_i >= row_i
        c_ref, n_ref, m_ref = c_scr.at[d], n_scr.at[d], m_scr.at[d]
        m_prev = m_ref[...]
        m_loc = jnp.max(a_r, axis=1, keepdims=True)
        kwt = (kt.astype(F32) * jnp.exp(a_r - m_loc)).astype(BF16)
        uu = _dot(kwt, vo)
        qq = _dot(q, jnp.concatenate([kt, c_ref[...].astype(BF16), n_ref[...].astype(BF16)], axis=1))
        s, q_c, q_n = qq[:, 0:L], qq[:, L:L + ML_DV], qq[:, L + ML_DV:]
        dmat = jnp.where(mask, b_c - b_r + i_r, NEG_INF)
        m_inter = b_c + m_prev
        m_t = jnp.maximum(m_inter, mi_c)
        s_mat = s * jnp.exp(dmat - m_t)
        inter_w = jnp.exp(m_inter - m_t)
        ss = _dot(s_mat.astype(BF16), vo)
        den = ss[:, ML_DV:] + inter_w * q_n
        r = 1.0 / jnp.maximum(jnp.abs(den), jnp.exp(-m_t))
        num = ss[:, 0:ML_DV] + jnp.concatenate([inter_w, inter_w], axis=1) * q_c
        h = num * jnp.concatenate([r, r], axis=1)
        m_new = jnp.maximum(gtot + m_prev, m_loc)
        decay = jnp.exp(gtot + m_prev - m_new)
        sc = jnp.exp(m_loc - m_new)
        c_ref[...] = decay * c_ref[...] + sc * uu[:, 0:ML_DV]
        n_ref[...] = decay * n_ref[...] + sc * uu[:, ML_DV:]
        m_ref[...] = m_new
        return sl, h

    def scan_body(i, carry):
        sl, h = chunk(i, 0)
        hf_scr[sl, :] = h
        sl, h = chunk(nc - 1 - i, 1)
        hb_scr[sl, :] = h
        return carry

    lax.fori_loop(0, nc, scan_body, 0, unroll=4)

    def out_body(c, carry):
        sl = pl.ds(pl.multiple_of(c * L, L), L)
        h = hf_scr[sl, :] + hb_scr[sl, :]
        mu = jnp.mean(h, axis=1, keepdims=True)
        d = h - mu
        var = jnp.mean(d * d, axis=1, keepdims=True)
        y = d * lax.rsqrt(var + LN_EPS) * nw_ref[...]
        y = y * _sigmoid(o_ref[0, sl, :].astype(F32))
        y_ref[0, sl, :] = y.astype(y_ref.dtype)
        return carry

    lax.fori_loop(0, nc, out_body, 0, unroll=2)


def _mlstm(pa3, kt5, mg, mc, norm_w):
    B, S, _ = pa3.shape
    nc = S // CHUNK
    H = ML_HEADS
    q_blk, v_blk, o_blk = PA_Q0 // ML_DQK, PA_V0 // ML_DV, PA_O0 // ML_DV
    return pl.pallas_call(
        functools.partial(_mlstm_kernel, nc=nc, L=CHUNK),
        grid=(B, H),
        in_specs=[
            pl.BlockSpec((1, S, ML_DQK), lambda b, h: (b, 0, q_blk + h)),
            pl.BlockSpec((1, nc, 1, ML_DQK, CHUNK), lambda b, h: (b, 0, h, 0, 0)),
            pl.BlockSpec((1, S, ML_DV), lambda b, h: (b, 0, v_blk + h)),
            pl.BlockSpec((1, S, ML_DV), lambda b, h: (b, 0, o_blk + h)),
            pl.BlockSpec((1, 1, nc, 8, CHUNK), lambda b, h: (b, h, 0, 0, 0)),
            pl.BlockSpec((1, 1, S, 8), lambda b, h: (b, h, 0, 0)),
            pl.BlockSpec((1, ML_DV), lambda b, h: (0, h)),
        ],
        out_specs=pl.BlockSpec((1, S, ML_DV), lambda b, h: (b, 0, h)),
        out_shape=jax.ShapeDtypeStruct((B, S, ML_WIDTH), BF16),
        scratch_shapes=[
            pltpu.VMEM((S, ML_DV), F32),
            pltpu.VMEM((S, ML_DV), F32),
            pltpu.VMEM((2, ML_DQK, ML_DV), F32),
            pltpu.VMEM((2, ML_DQK, CHUNK), F32),
            pltpu.VMEM((2, 1, 1), F32),
        ],
        compiler_params=_cparams(("parallel", "parallel")),
        name="mlstm",
    )(pa3, kt5, pa3, pa3, mg, mc, norm_w)


CONV_EDGE = 16


def _conv_shift_matrix(L):
    pad, E = CONV_K // 2, CONV_EDGE
    t = jnp.arange(L, dtype=jnp.int32)[:, None]
    blocks = []
    u = jnp.arange(L, dtype=jnp.int32)[None, :]
    for j in range(CONV_K):
        blocks.append(u == t + j - pad)
    v = jnp.arange(E, dtype=jnp.int32)[None, :]
    for j in range(pad):
        blocks.append(v - E == t + j - pad)
    for j in range(pad + 1, CONV_K):
        blocks.append(v + L == t + j - pad)
    return jnp.concatenate(blocks, axis=1).astype(BF16)


def _conv_kernel(x_ref, halo_ref, w_ref, b_ref, sh_ref, o_ref, ext_scr, *, Lc, L):
    pad, E = CONV_K // 2, CONV_EDGE
    ext_scr[0:E, :] = halo_ref[0, 0, 0:E, :]
    ext_scr[E:E + Lc, :] = x_ref[0]
    ext_scr[E + Lc:2 * E + Lc, :] = halo_ref[0, 0, E:2 * E, :]
    taps = [w_ref[j:j + 1, :] for j in range(CONV_K)]
    scaled = lambda rows, j: (rows.astype(F32) * taps[j]).astype(BF16)
    for r0 in range(0, Lc, L):
        cen = ext_scr[E + r0:E + r0 + L, :]
        top = ext_scr[r0:r0 + E, :]
        bot = ext_scr[E + r0 + L:2 * E + r0 + L, :]
        pieces = [scaled(cen, j) for j in range(CONV_K)]
        pieces += [scaled(top, j) for j in range(pad)] + [scaled(bot, j) for j in range(pad + 1, CONV_K)]
        acc = _dot(sh_ref[...], jnp.concatenate(pieces, axis=0)) + b_ref[...]
        o_ref[0, r0:r0 + L, :] = (acc * _sigmoid(acc)).astype(o_ref.dtype)


def _conv(pa3, halo, conv_w8, conv_b, Lc=512):
    B, S, _ = pa3.shape
    C = SSM_CONV_DIM
    xbc_blk = PA_XBC0 // C
    L = CHUNK
    sh = _conv_shift_matrix(L)
    return pl.pallas_call(
        functools.partial(_conv_kernel, Lc=Lc, L=L),
        grid=(B, S // Lc),
        in_specs=[
            pl.BlockSpec((1, Lc, C), lambda b, i: (b, i, xbc_blk)),
            pl.BlockSpec((1, 1, 2 * CONV_EDGE, C), lambda b, i: (b, i, 0, 0)),
            pl.BlockSpec((8, C), lambda b, i: (0, 0)),
            pl.BlockSpec((1, C), lambda b, i: (0, 0)),
            pl.BlockSpec(sh.shape, lambda b, i: (0, 0)),
        ],
        out_specs=pl.BlockSpec((1, Lc, C), lambda b, i: (b, i, 0)),
        out_shape=jax.ShapeDtypeStruct((B, S, C), BF16),
        scratch_shapes=[pltpu.VMEM((Lc + 2 * CONV_EDGE, C), BF16)],
        compiler_params=_cparams(("parallel", "parallel")),
        name="ssm_conv",
    )(pa3, halo, conv_w8, conv_b, sh)


def _expansion_matrix(src_cols, width):
    src = jnp.repeat(jnp.asarray(src_cols, jnp.int32), width)
    e = (jnp.arange(128, dtype=jnp.int32)[:, None] == src[None, :]).astype(BF16)
    return jnp.concatenate([e, e], axis=0)


def _split2(t):
    hi = t.astype(BF16)
    return jnp.concatenate([hi, (t - hi.astype(F32)).astype(BF16)], axis=1)


def _ssd_bwd_kernel(x_ref, b_ref, sc_ref, dec_ref, ew_ref, hb_ref, h_scr, *, L):
    G, hg, P, N = SSM_GROUPS, SSM_HEADS // SSM_GROUPS, SSM_HEADDIM, SSM_STATE
    GW = hg * P

    @pl.when(pl.program_id(1) == 0)
    def _():
        h_scr[...] = jnp.zeros_like(h_scr)

    w_all = _dot(_split2(sc_ref[0]), ew_ref[...])
    for s in reversed(range(x_ref.shape[1] // L)):
        rows = slice(s * L, (s + 1) * L)
        for g in range(G):
            w = w_all[rows, g * GW:(g + 1) * GW]
            xs = (x_ref[0, rows, g * GW:(g + 1) * GW].astype(F32) * w).astype(BF16)
            st = _dot_tn(b_ref[0, rows, g * N:(g + 1) * N], xs)
            h_old = h_scr[g]
            hb_ref[0, s, g] = h_old.astype(hb_ref.dtype)
            h_scr[g] = dec_ref[0, s, 1:2, g * GW:(g + 1) * GW] * h_old + st


def _ssd_bwd(xc, sc, decs, cpb):
    B, S, _ = xc.shape
    nc = S // CHUNK
    nblk = nc // cpb
    R = cpb * CHUNK
    G, N, W = SSM_GROUPS, SSM_STATE, SSM_WIDTH
    return pl.pallas_call(
        functools.partial(_ssd_bwd_kernel, L=CHUNK),
        grid=(B, nblk),
        in_specs=[
            pl.BlockSpec((1, R, W), lambda b, c: (b, nblk - 1 - c, 0)),
            pl.BlockSpec((1, R, G * N), lambda b, c: (b, nblk - 1 - c, W // (G * N))),
            pl.BlockSpec((1, R, 128), lambda b, c: (b, nblk - 1 - c, 0)),
            pl.BlockSpec((1, cpb, 2, W), lambda b, c: (b, nblk - 1 - c, 0, 0)),
            pl.BlockSpec((256, W), lambda b, c: (0, 0)),
        ],
        out_specs=pl.BlockSpec((1, cpb, G, N, W // G), lambda b, c: (b, nblk - 1 - c, 0, 0, 0)),
        out_shape=jax.ShapeDtypeStruct((B, nc, G, N, W // G), BF16),
        scratch_shapes=[pltpu.VMEM((G, N, W // G), F32)],
        compiler_params=_cparams(("parallel", "arbitrary")),
        name="ssd_bwd_states",
    )(xc, xc, sc, decs, _expansion_matrix([5 * SSM_HEADS + h for h in range(SSM_HEADS)], SSM_HEADDIM))


def _ssd_main_kernel(x_ref, b_ref, c_ref, z_ref, sg_ref, sc_ref, dec_ref, hb_ref, dsk_ref, nw_ref, ed_ref, ee_ref,
                     y_ref, h_scr, *, L):
    G, hg, P, N = SSM_GROUPS, SSM_HEADS // SSM_GROUPS, SSM_HEADDIM, SSM_STATE
    H = SSM_HEADS
    GW = hg * P

    @pl.when(pl.program_id(1) == 0)
    def _():
        h_scr[...] = jnp.zeros_like(h_scr)

    row_i = lax.broadcasted_iota(jnp.int32, (L, L), 0)
    col_i = lax.broadcasted_iota(jnp.int32, (L, L), 1)
    lower = col_i <= row_i
    upper = col_i >= row_i
    lane_g = lax.broadcasted_iota(jnp.int32, (L, GW), 1)

    for s, g in [(s, g) for s in range(x_ref.shape[1] // L) for g in range(G)]:
        rows = slice(s * L, (s + 1) * L)
        sg = sg_ref[0, s]
        sc2 = _split2(sc_ref[0, rows, :])
        xg = x_ref[0, rows, g * GW:(g + 1) * GW]
        bg = b_ref[0, rows, g * N:(g + 1) * N]
        cg = c_ref[0, rows, g * N:(g + 1) * N]
        cb = _dot_nt(cg, bg)
        acol = _dot(sc2, ed_ref[g])
        ecol = _dot(sc2, ee_ref[g])
        ms, xms = [], []
        for j in range(hg):
            hd = g * hg + j
            af_r, ab_r = sg[hd:hd + 1], sg[H + hd:H + hd + 1]
            dtf_r, dtb_r = sg[2 * H + hd:2 * H + hd + 1], sg[3 * H + hd:3 * H + hd + 1]
            af_c, ab_c = acol[:, j * L:(j + 1) * L], acol[:, (hg + j) * L:(hg + j + 1) * L]
            fm = jnp.exp(jnp.where(lower, af_c - af_r, NEG_INF)) * dtf_r
            bm = jnp.exp(jnp.where(upper, ab_c - ab_r, NEG_INF)) * dtb_r
            ms.append((cb * (fm + bm)).astype(BF16))
            xms.append(jnp.where((lane_g >= j * P) & (lane_g < (j + 1) * P), xg, jnp.zeros_like(xg)))
        efc, ebc, wfc = ecol[:, 0:GW], ecol[:, GW:2 * GW], ecol[:, 2 * GW:]
        y = _dot(jnp.concatenate(ms, axis=1), jnp.concatenate(xms, axis=0))
        h_old = h_scr[g]
        y = y + efc * _dot(cg, h_old.astype(BF16))
        y = y + ebc * _dot(cg, hb_ref[0, s, g])
        xgf = xg.astype(F32)
        y = y + dsk_ref[:, g * GW:(g + 1) * GW] * xgf
        xs = (xgf * wfc).astype(BF16)
        h_scr[g] = dec_ref[0, s, 0:1, g * GW:(g + 1) * GW] * h_old + _dot_tn(bg, xs)
        zg = z_ref[0, rows, g * GW:(g + 1) * GW].astype(F32)
        y = y * (zg * _sigmoid(zg))
        y = y * lax.rsqrt(jnp.mean(y * y, axis=1, keepdims=True) + RMS_EPS)
        y_ref[0, rows, g * GW:(g + 1) * GW] = (y * nw_ref[:, g * GW:(g + 1) * GW]).astype(y_ref.dtype)


def _ssd_main(xc, pa3, sg, sc, decs, hb, dskip_row, norm_w, cpb):
    B, S, _ = xc.shape
    nc = S // CHUNK
    R = cpb * CHUNK
    G, N, W = SSM_GROUPS, SSM_STATE, SSM_WIDTH
    H, hg, P = SSM_HEADS, SSM_HEADS // SSM_GROUPS, SSM_HEADDIM
    z_blk = PA_Z0 // W
    e_dec = jnp.stack([_expansion_matrix([q * H + g * hg + j for q in (0, 1) for j in range(hg)], CHUNK)
                       for g in range(G)])
    e_ew = jnp.stack([_expansion_matrix([q * H + g * hg + j for q in (2, 3, 4) for j in range(hg)], P)
                      for g in range(G)])
    return pl.pallas_call(
        functools.partial(_ssd_main_kernel, L=CHUNK),
        grid=(B, nc // cpb),
        in_specs=[
            pl.BlockSpec((1, R, W), lambda b, c: (b, c, 0)),
            pl.BlockSpec((1, R, G * N), lambda b, c: (b, c, W // (G * N))),
            pl.BlockSpec((1, R, G * N), lambda b, c: (b, c, W // (G * N) + 1)),
            pl.BlockSpec((1, R, W), lambda b, c: (b, c, z_blk)),
            pl.BlockSpec((1, cpb, 4 * SSM_HEADS, CHUNK), lambda b, c: (b, c, 0, 0)),
            pl.BlockSpec((1, R, 128), lambda b, c: (b, c, 0)),
            pl.BlockSpec((1, cpb, 2, W), lambda b, c: (b, c, 0, 0)),
            pl.BlockSpec((1, cpb, G, N, W // G), lambda b, c: (b, c, 0, 0, 0)),
            pl.BlockSpec((1, W), lambda b, c: (0, 0)),
            pl.BlockSpec((1, W), lambda b, c: (0, 0)),
            pl.BlockSpec((G, 256, 2 * hg * CHUNK), lambda b, c: (0, 0, 0)),
            pl.BlockSpec((G, 256, 3 * W // G), lambda b, c: (0, 0, 0)),
        ],
        out_specs=pl.BlockSpec((1, R, W), lambda b, c: (b, c, 0)),
        out_shape=jax.ShapeDtypeStruct((B, S, W), BF16),
        scratch_shapes=[pltpu.VMEM((G, N, W // G), F32)],
        compiler_params=_cparams(("parallel", "arbitrary")),
        name="ssd_main",
    )(xc, xc, xc, pa3, sg, sc, decs, hb, dskip_row, norm_w, e_dec, e_ew)


def _pack_bf16_pairs(v):
    n = v.shape[1] // 2
    hi = lax.bitcast_convert_type(v[:, :n].astype(BF16).astype(F32), jnp.uint32)
    lo = lax.bitcast_convert_type(v[:, n:].astype(BF16).astype(F32), jnp.uint32)
    return (hi & jnp.uint32(0xFFFF0000)) | (lo >> 16)


def _unpack_bf16_pairs(u):
    hi = lax.bitcast_convert_type(u & jnp.uint32(0xFFFF0000), F32)
    lo = lax.bitcast_convert_type(u << 16, F32)
    return hi, lo


def _layer_norm(v, g, b):
    mu = jnp.mean(v, axis=1, keepdims=True)
    d = v - mu
    var = jnp.mean(d * d, axis=1, keepdims=True)
    return d * lax.rsqrt(var + LN_EPS) * g + b


def _outproj_kernel(yml_ref, yssm_ref, x_ref, wo_ref, g_ref, b_ref, wr_ref, h_ref, hpa_ref, hpb_ref, sc_ref):
    mix = _dot(yml_ref[...], wo_ref[0:ML_WIDTH, :]) + _dot(yssm_ref[...], wo_ref[ML_WIDTH:, :])
    h = _layer_norm(ALPHA * x_ref[...] + mix, g_ref[...], b_ref[...])
    h_ref[...] = h
    hp = _pack_bf16_pairs(h)
    q = hp.shape[1] // 2
    hpa_ref[...] = hp[:, :q]
    hpb_ref[...] = hp[:, q:]
    sc_ref[...] = _sigmoid(_dot(h.astype(BF16), wr_ref[...]))


def _outproj(yml, yssm, x2d, w_out, ln_g, ln_b, w_router, tm=512):
    T, D = x2d.shape
    tm = min(tm, T)
    row = lambda i: (i, 0)
    fixed = lambda i: (0, 0)
    return pl.pallas_call(
        _outproj_kernel,
        grid=(T // tm,),
        in_specs=[
            pl.BlockSpec((tm, ML_WIDTH), row),
            pl.BlockSpec((tm, SSM_WIDTH), row),
            pl.BlockSpec((tm, D), row),
            pl.BlockSpec((ML_WIDTH + SSM_WIDTH, D), fixed),
            pl.BlockSpec((1, D), fixed),
            pl.BlockSpec((1, D), fixed),
            pl.BlockSpec((D, 128), fixed),
        ],
        out_specs=[pl.BlockSpec((tm, D), row), pl.BlockSpec((tm, D // 4), row), pl.BlockSpec((tm, D // 4), row),
                   pl.BlockSpec((tm, 128), row)],
        out_shape=[
            jax.ShapeDtypeStruct((T, D), F32),
            jax.ShapeDtypeStruct((T, D // 4), jnp.uint32),
            jax.ShapeDtypeStruct((T, D // 4), jnp.uint32),
            jax.ShapeDtypeStruct((T, 128), F32),
        ],
        compiler_params=_cparams(("parallel",)),
        name="out_proj_ln1_router",
    )(yml, yssm, x2d, w_out, ln_g, ln_b, w_router)


def _route_kernel(sc_ref, rb_ref, idx_ref, w_ref, rank_ref, cnt_ref, carry_scr, *, Tt):
    E, NG = N_EXPERTS, N_EXPERT_GROUPS
    EG = E // NG

    @pl.when(pl.program_id(0) == 0)
    def _():
        carry_scr[...] = jnp.zeros_like(carry_scr)

    s = sc_ref[...].T[0:E]
    s3 = s.reshape(NG, EG, Tt)
    ch3 = (s + rb_ref[:, 0:1]).reshape(NG, EG, Tt)
    e_in = lax.broadcasted_iota(jnp.int32, (NG, EG, Tt), 1)
    g_in = lax.broadcasted_iota(jnp.int32, (NG, EG, Tt), 0)
    f_i = g_in * EG + e_in
    m1 = jnp.max(ch3, axis=1, keepdims=True)
    i1 = jnp.min(jnp.where(ch3 == m1, e_in, EG), axis=1, keepdims=True)
    m2 = jnp.max(jnp.where(e_in == i1, NEG_INF, ch3), axis=1, keepdims=True)
    gs = m1 + m2
    g_i = lax.broadcasted_iota(jnp.int32, (NG, 1, Tt), 0)
    gsel = jnp.zeros((NG, 1, Tt), jnp.bool_)
    cur = gs
    for _ in range(TOPK_GROUPS):
        mx = jnp.max(cur, axis=0, keepdims=True)
        pick = jnp.min(jnp.where(cur == mx, g_i, NG), axis=0, keepdims=True)
        hit = g_i == pick
        gsel = gsel | hit
        cur = jnp.where(hit, NEG_INF, cur)
    cur = jnp.where(gsel, ch3, NEG_INF)
    selmask = jnp.zeros((NG, EG, Tt), jnp.bool_)
    idxs, ws = [], []
    for _ in range(TOP_K):
        mx = jnp.max(jnp.max(cur, axis=1, keepdims=True), axis=0, keepdims=True)
        cand = jnp.where(cur == mx, f_i, E)
        pick = jnp.min(jnp.min(cand, axis=1, keepdims=True), axis=0, keepdims=True)
        hit = f_i == pick
        wk = jnp.sum(jnp.sum(jnp.where(hit, s3, 0.0), axis=1, keepdims=True), axis=0, keepdims=True)
        idxs.append(pick.reshape(1, Tt))
        ws.append(wk.reshape(1, Tt))
        selmask = selmask | hit
        cur = jnp.where(hit, NEG_INF, cur)
    wsum = ws[0]
    for k in range(1, TOP_K):
        wsum = wsum + ws[k]
    r = lax.broadcasted_iota(jnp.int32, (Tt, Tt), 0)
    c = lax.broadcasted_iota(jnp.int32, (Tt, Tt), 1)
    ustrict = (r < c).astype(BF16)
    self32 = jnp.where(selmask, 1.0, 0.0).reshape(E, Tt)
    excl = _dot(self32.astype(BF16), ustrict) + carry_scr[:, 0:1]
    excl3 = excl.reshape(NG, EG, Tt)
    for k in range(TOP_K):
        hit = f_i == idxs[k].reshape(1, 1, Tt)
        rk = jnp.sum(jnp.sum(jnp.where(hit, excl3, 0.0), axis=1, keepdims=True), axis=0, keepdims=True)
        idx_ref[k:k + 1, :] = idxs[k]
        w_ref[k:k + 1, :] = ws[k] / wsum * ROUTED_SCALE
        rank_ref[k:k + 1, :] = rk.reshape(1, Tt).astype(jnp.int32)
    carry_scr[...] = carry_scr[...] + jnp.sum(self32, axis=1, keepdims=True)
    cnt_ref[...] = carry_scr[...]


def _route(scores, rbias, Tt=512):
    T = scores.shape[0]
    kt = lambda i: (0, i)
    return pl.pallas_call(
        functools.partial(_route_kernel, Tt=Tt),
        grid=(T // Tt,),
        in_specs=[pl.BlockSpec((Tt, 128), lambda i: (i, 0)), pl.BlockSpec((N_EXPERTS, 128), lambda i: (0, 0))],
        out_specs=[
            pl.BlockSpec((TOP_K, Tt), kt),
            pl.BlockSpec((TOP_K, Tt), kt),
            pl.BlockSpec((TOP_K, Tt), kt),
            pl.BlockSpec((N_EXPERTS, 128), lambda i: (0, 0)),
        ],
        out_shape=[
            jax.ShapeDtypeStruct((TOP_K, T), jnp.int32),
            jax.ShapeDtypeStruct((TOP_K, T), F32),
            jax.ShapeDtypeStruct((TOP_K, T), jnp.int32),
            jax.ShapeDtypeStruct((N_EXPERTS, 128), F32),
        ],
        scratch_shapes=[pltpu.VMEM((N_EXPERTS, 128), F32)],
        compiler_params=_cparams(("arbitrary",)),
        name="route_topk",
    )(scores, rbias)


def _dest_kernel(ps_ref, idx_ref, rank_ref, o_ref):
    idx = idx_ref[...]
    start = jnp.zeros(idx.shape, jnp.int32)
    for e in range(N_EXPERTS):
        start = jnp.where(idx == e, ps_ref[e], start)
    o_ref[...] = start + rank_ref[...]


def _dest(pstarts, idx, rank, Tt=2048):
    K, T = idx.shape
    Tt = min(Tt, T)
    blk = pl.BlockSpec((K, Tt), lambda i, ps: (0, i))
    return pl.pallas_call(
        _dest_kernel,
        grid_spec=pltpu.PrefetchScalarGridSpec(num_scalar_prefetch=1, grid=(T // Tt,), in_specs=[blk, blk],
                                               out_specs=blk),
        out_shape=jax.ShapeDtypeStruct((K, T), jnp.int32),
        compiler_params=_cparams(("parallel",)),
        name="dispatch_slots",
    )(pstarts, idx, rank)


def _sc_mesh():
    return plsc.VectorSubcoreMesh(core_axis_name="c", subcore_axis_name="s")


def _sc_scatter_rows(src, idx, n_out, win=SC_WINDOW):
    K, T = idx.shape
    W = src.shape[1]

    @functools.partial(pl.kernel, out_type=jax.ShapeDtypeStruct((n_out, W), src.dtype), mesh=_sc_mesh(),
                       scratch_types=[], name="sc_scatter_rows")
    def k(x_hbm, i_hbm, o_hbm):
        def body(x_vmem, i_vmem):
            for j in range(K):
                pltpu.sync_copy(x_vmem, o_hbm.at[i_vmem.at[j]])

        pltpu.emit_pipeline(
            body,
            grid=(T // win,),
            in_specs=[pl.BlockSpec((win, W), lambda i: (i, 0)), pl.BlockSpec((K, win), lambda i: (0, i))],
            out_specs=[],
            core_axis_name=("c", "s"),
            dimension_semantics=(pltpu.PARALLEL,),
        )(x_hbm, i_hbm)

    return k(src, idx)


def _sc_gather_rows(table, idx, win=SC_WINDOW):
    n = idx.shape[0]
    W = table.shape[1]

    @functools.partial(pl.kernel, out_type=jax.ShapeDtypeStruct((n, W), table.dtype), mesh=_sc_mesh(),
                       scratch_types=[], name="sc_gather_rows")
    def k(t_hbm, i_hbm, o_hbm):
        def body(i_vmem, o_vmem):
            pltpu.sync_copy(t_hbm.at[i_vmem.at[0]], o_vmem)

        pltpu.emit_pipeline(
            body,
            grid=(n // win,),
            in_specs=[pl.BlockSpec((1, win), lambda i: (0, i))],
            out_specs=[pl.BlockSpec((win, W), lambda i: (i, 0))],
            core_axis_name=("c", "s"),
            dimension_semantics=(pltpu.PARALLEL,),
        )(i_hbm, o_hbm)

    return k(table, idx.reshape(1, n))


def _expert_kernel(be_ref, nu_ref, xa_ref, xb_ref, wg_ref, wu_ref, wd_ref, oa_ref, ob_ref, wg_s, wu_s, wd_s):
    half = D_MODEL // 2
    i = pl.program_id(0)
    active = i < nu_ref[0]

    @pl.when(active & ((i == 0) | (be_ref[i] != be_ref[jnp.maximum(i - 1, 0)])))
    def _():
        wg_s[...] = wg_ref[0].astype(BF16)
        wu_s[...] = wu_ref[0].astype(BF16)
        wd_s[...] = wd_ref[0].astype(BF16)

    @pl.when(active)
    def _():
        hi, lo = _unpack_bf16_pairs(jnp.concatenate([xa_ref[...], xb_ref[...]], axis=1))
        hi, lo = hi.astype(BF16), lo.astype(BF16)
        gt = _dot(hi, wg_s[0:half, :]) + _dot(lo, wg_s[half:, :])
        up = _dot(hi, wu_s[0:half, :]) + _dot(lo, wu_s[half:, :])
        hmid = (gt * _sigmoid(gt) * up).astype(BF16)
        packed = _pack_bf16_pairs(_dot(hmid, wd_s[...]))
        q = packed.shape[1] // 2
        oa_ref[...] = packed[:, :q]
        ob_ref[...] = packed[:, q:]

    @pl.when(jnp.logical_not(active))
    def _():
        oa_ref[...] = jnp.zeros_like(oa_ref)
        ob_ref[...] = jnp.zeros_like(ob_ref)


def _experts(xs_a, xs_b, block_expert, n_used, w_gate, w_up, w_down, blk=MOE_BLOCK):
    P, quarter = xs_a.shape
    nb = P // blk
    D, F = D_MODEL, D_EXPERT
    cur = lambda i, be, nu: jnp.minimum(i, nu[0] - 1)
    return pl.pallas_call(
        _expert_kernel,
        grid_spec=pltpu.PrefetchScalarGridSpec(
            num_scalar_prefetch=2,
            grid=(nb,),
            in_specs=[
                pl.BlockSpec((blk, quarter), lambda i, be, nu: (cur(i, be, nu), 0)),
                pl.BlockSpec((blk, quarter), lambda i, be, nu: (cur(i, be, nu), 0)),
                pl.BlockSpec((1, D, F), lambda i, be, nu: (be[cur(i, be, nu)], 0, 0)),
                pl.BlockSpec((1, D, F), lambda i, be, nu: (be[cur(i, be, nu)], 0, 0)),
                pl.BlockSpec((1, F, D), lambda i, be, nu: (be[cur(i, be, nu)], 0, 0)),
            ],
            out_specs=[pl.BlockSpec((blk, quarter), lambda i, be, nu: (i, 0)),
                       pl.BlockSpec((blk, quarter), lambda i, be, nu: (i, 0))],
            scratch_shapes=[pltpu.VMEM((D, F), BF16), pltpu.VMEM((D, F), BF16), pltpu.VMEM((F, D), BF16)],
        ),
        out_shape=[jax.ShapeDtypeStruct((P, quarter), jnp.uint32), jax.ShapeDtypeStruct((P, quarter), jnp.uint32)],
        compiler_params=_cparams(("arbitrary",)),
        name="expert_mlp",
    )(block_expert, n_used, xs_a, xs_b, w_gate, w_up, w_down)


def _final_kernel(h_ref, ga_ref, gb_ref, wt_ref, wsg_ref, wsu_ref, wsd_ref, g_ref, b_ref, o_ref):
    h = h_ref[...]
    hb = h.astype(BF16)
    gt = _dot(hb, wsg_ref[...])
    up = _dot(hb, wsu_ref[...])
    shared = _dot((gt * _sigmoid(gt) * up).astype(BF16), wsd_ref[...])
    acc_hi = None
    for k in range(TOP_K):
        hi, lo = _unpack_bf16_pairs(jnp.concatenate([ga_ref[k], gb_ref[k]], axis=1))
        wk = wt_ref[:, k:k + 1]
        acc_hi = wk * hi if acc_hi is None else acc_hi + wk * hi
        acc_lo = wk * lo if k == 0 else acc_lo + wk * lo
    routed = jnp.concatenate([acc_hi, acc_lo], axis=1)
    o_ref[...] = _layer_norm(ALPHA * h + (routed + shared), g_ref[...], b_ref[...])


def _final(h, gk_a, gk_b, wt, ws_gate, ws_up, ws_down, ln_g, ln_b, tm=512):
    T, D = h.shape
    tm = min(tm, T)
    F = D_EXPERT
    fixed = lambda i: (0, 0)
    return pl.pallas_call(
        _final_kernel,
        grid=(T // tm,),
        in_specs=[
            pl.BlockSpec((tm, D), lambda i: (i, 0)),
            pl.BlockSpec((TOP_K, tm, D // 4), lambda i: (0, i, 0)),
            pl.BlockSpec((TOP_K, tm, D // 4), lambda i: (0, i, 0)),
            pl.BlockSpec((tm, TOP_K), lambda i: (i, 0)),
            pl.BlockSpec((D, F), fixed),
            pl.BlockSpec((D, F), fixed),
            pl.BlockSpec((F, D), fixed),
            pl.BlockSpec((1, D), fixed),
            pl.BlockSpec((1, D), fixed),
        ],
        out_specs=pl.BlockSpec((tm, D), lambda i: (i, 0)),
        out_shape=jax.ShapeDtypeStruct((T, D), F32),
        compiler_params=_cparams(("parallel",)),
        name="shared_combine_ln2",
    )(h, gk_a, gk_b, wt, ws_gate, ws_up, ws_down, ln_g, ln_b)


def _mixers(x, w_in, ml_i_bias, ml_f_bias, ml_norm_w, conv_w, conv_b, dt_bias, a_log, d_skip, ssm_norm_w):
    B, S, D = x.shape
    T = B * S
    nc = S // CHUNK
    H = ML_HEADS
    k0, v0 = ML_QK, 2 * ML_QK
    g0 = v0 + 2 * ML_WIDTH
    z0 = g0 + 4 * H
    x0 = z0 + SSM_WIDTH
    d0 = x0 + SSM_CONV_DIM
    w_big = jnp.concatenate([w_in[:, x0:d0], w_in[:, v0:g0], w_in[:, z0:x0], w_in[:, :k0]], axis=1).astype(BF16)
    w_small = jnp.concatenate(
        [w_in[:, g0:z0], w_in[:, d0:], jnp.zeros((D, PB_COLS - N_GATE_ROWS), w_in.dtype)], axis=1).astype(BF16)
    w_kt = (w_in[:, k0:v0] * (ML_DQK ** -0.5)).T.astype(BF16)
    pa, pb, kt = _proj(x.reshape(T, D), w_big, w_small, w_kt)
    pa3 = pa.reshape(B, S, PA_COLS)
    kt5 = kt.reshape(B, nc, H, ML_DQK, CHUNK)

    g_rows = pb.reshape(B, S, PB_COLS)[:, :, :N_GATE_ROWS].transpose(0, 2, 1).reshape(B, N_GATE_ROWS, nc, CHUNK)
    bias = jnp.concatenate([ml_i_bias, ml_f_bias, dt_bias]).astype(F32)
    bias_rows = jnp.broadcast_to(bias[:, None, None], (N_GATE_ROWS, 1, CHUNK))
    aneg_rows = jnp.broadcast_to((-jnp.exp(a_log.astype(F32)))[:, None, None], (2 * SSM_HEADS, 1, CHUNK))
    prep = _gate_prep(g_rows, bias_rows, aneg_rows)
    m8 = jnp.stack([prep[:, 0:H], prep[:, 2 * H:3 * H], prep[:, H:2 * H], prep[:, 3 * H:4 * H],
                    prep[:, 4 * H:5 * H], prep[:, 5 * H:6 * H], prep[:, 6 * H:7 * H], prep[:, 7 * H:8 * H]],
                   axis=2)
    mg = m8.transpose(0, 1, 3, 2, 4)
    mc = m8.transpose(0, 1, 3, 4, 2).reshape(B, H, S, 8)
    s0, SH = PREP_SSM0, SSM_HEADS
    sg = prep[:, s0:s0 + 4 * SH].transpose(0, 2, 1, 3)
    sc = jnp.concatenate([prep[:, s0:s0 + 2 * SH], prep[:, s0 + 4 * SH:s0 + 8 * SH]], axis=1)
    sc = sc.transpose(0, 2, 3, 1).reshape(B, S, 6 * SH)
    sc = jnp.concatenate([sc, jnp.zeros((B, S, 128 - 6 * SH), F32)], axis=2)
    decs = prep[:, s0 + 8 * SH:s0 + 10 * SH, :, 0].reshape(B, 2, SH, nc).transpose(0, 3, 1, 2)
    decs = jnp.repeat(decs, SSM_HEADDIM, axis=3)

    y_ml = _mlstm(pa3, kt5, mg, mc, ml_norm_w.astype(F32).reshape(1, ML_WIDTH))

    Lc = 512 if S % 512 == 0 else CHUNK
    nt = S // Lc
    pa4 = pa3.reshape(B, nt, Lc, PA_COLS)
    top = pa4[:, :, Lc - CONV_EDGE:, PA_XBC0:PA_XBC0 + SSM_CONV_DIM]
    bot = pa4[:, :, :CONV_EDGE, PA_XBC0:PA_XBC0 + SSM_CONV_DIM]
    zrow = jnp.zeros((B, 1, CONV_EDGE, SSM_CONV_DIM), pa.dtype)
    top = jnp.concatenate([zrow, top[:, :-1]], axis=1)
    bot = jnp.concatenate([bot[:, 1:], zrow], axis=1)
    halo = jnp.concatenate([top, bot], axis=2)
    conv_w8 = jnp.concatenate([conv_w.astype(F32), jnp.zeros((8 - CONV_K, SSM_CONV_DIM), F32)], axis=0)
    xc = _conv(pa3, halo, conv_w8, conv_b.astype(F32).reshape(1, SSM_CONV_DIM), Lc=Lc)

    cpb = 2 if nc % 2 == 0 else 1
    hb = _ssd_bwd(xc, sc, decs, cpb)
    dsk = jnp.repeat(d_skip.astype(F32), SSM_HEADDIM).reshape(1, SSM_WIDTH)
    y_ssm = _ssd_main(xc, pa3, sg, sc, decs, hb, dsk, ssm_norm_w.astype(F32).reshape(1, SSM_WIDTH), cpb)
    return y_ml.reshape(T, ML_WIDTH), y_ssm.reshape(T, SSM_WIDTH)


def _moe_dispatch(idx, rank, counts, T, blk):
    E = N_EXPERTS
    M = T * TOP_K
    nb = (M + E * (blk - 1)) // blk
    pcounts = (counts + blk - 1) // blk * blk
    pends = jnp.cumsum(pcounts)
    pstarts = (pends - pcounts).astype(jnp.int32)
    dest = _dest(pstarts, idx, rank)
    block_start = jnp.arange(nb, dtype=jnp.int32) * blk
    block_expert = jnp.minimum(jnp.sum(pends[None, :] <= block_start[:, None], axis=1), E - 1).astype(jnp.int32)
    n_used = (pends[-1] // blk).astype(jnp.int32).reshape(1)
    return dest, nb * blk, block_expert, n_used


def _layer(x, w_in, ml_i_bias, ml_f_bias, ml_norm_w, conv_w, conv_b, dt_bias, a_log, d_skip, ssm_norm_w,
           w_out, ln1_g, ln1_b, w_router, router_bias, w_exp_gate, w_exp_up, w_exp_down,
           w_sh_gate, w_sh_up, w_sh_down, ln2_g, ln2_b):
    B, S, D = x.shape
    T = B * S
    y_ml, y_ssm = _mixers(x, w_in, ml_i_bias, ml_f_bias, ml_norm_w, conv_w, conv_b, dt_bias, a_log, d_skip,
                          ssm_norm_w)
    w_r = jnp.concatenate([w_router, jnp.zeros((D, 128 - N_EXPERTS), w_router.dtype)], axis=1).astype(BF16)
    h, hp_a, hp_b, scores = _outproj(y_ml, y_ssm, x.reshape(T, D), w_out.astype(BF16),
                                     ln1_g.astype(F32).reshape(1, D), ln1_b.astype(F32).reshape(1, D), w_r)
    rbias = jnp.broadcast_to(router_bias.astype(F32)[:, None], (N_EXPERTS, 128))
    idx, wgt, rank, cnt = _route(scores, rbias)
    counts = cnt[:, 0].astype(jnp.int32)
    dest, n_slots, block_expert, n_used = _moe_dispatch(idx, rank, counts, T, MOE_BLOCK)
    dest_flat = dest.reshape(-1)
    xs_a = _sc_scatter_rows(hp_a, dest, n_slots)
    xs_b = _sc_scatter_rows(hp_b, dest, n_slots)
    os_a, os_b = _experts(xs_a, xs_b, block_expert, n_used, w_exp_gate, w_exp_up, w_exp_down)
    gk_a = _sc_gather_rows(os_a, dest_flat).reshape(TOP_K, T, D // 4)
    gk_b = _sc_gather_rows(os_b, dest_flat).reshape(TOP_K, T, D // 4)
    out = _final(h, gk_a, gk_b, wgt.T, w_sh_gate.astype(BF16), w_sh_up.astype(BF16),
                 w_sh_down.astype(BF16), ln2_g.astype(F32).reshape(1, D), ln2_b.astype(F32).reshape(1, D))
    return out.reshape(B, S, D)


def kernel(x, w_in, ml_i_bias, ml_f_bias, ml_norm_w, conv_w, conv_b, dt_bias, a_log, d_skip, ssm_norm_w, w_out,
           ln1_g, ln1_b, w_router, router_bias, w_exp_gate, w_exp_up, w_exp_down, w_sh_gate, w_sh_up, w_sh_down,
           ln2_g, ln2_b):
    for l in range(w_in.shape[0]):
        x = _layer(x, w_in[l], ml_i_bias[l], ml_f_bias[l], ml_norm_w[l], conv_w[l], conv_b[l], dt_bias[l],
                   a_log[l], d_skip[l], ssm_norm_w[l], w_out[l], ln1_g[l], ln1_b[l], w_router[l],
                   router_bias[l], w_exp_gate[l], w_exp_up[l], w_exp_down[l], w_sh_gate[l], w_sh_up[l],
                   w_sh_down[l], ln2_g[l], ln2_b[l])
    return x
```

```python
import functools

import jax
import jax.numpy as jnp
from jax import lax
from jax.experimental import pallas as pl
from jax.experimental.pallas import tpu as pltpu
from jax.experimental.pallas import tpu_sc as plsc

D_MODEL = 1024
ML_HEADS = 4
ML_DV = 256
ML_DQK = 128
ML_QK = ML_HEADS * ML_DQK
ML_WIDTH = ML_HEADS * ML_DV
SSM_WIDTH = 1024
SSM_HEADDIM = 64
SSM_HEADS = 16
SSM_GROUPS = 4
SSM_STATE = 128
SSM_CONV_DIM = SSM_WIDTH + 2 * SSM_GROUPS * SSM_STATE
CONV_K = 5
CHUNK = 128
N_EXPERTS = 64
TOP_K = 8
N_EXPERT_GROUPS = 8
TOPK_GROUPS = 4
D_EXPERT = 256
ROUTED_SCALE = 2.5
LN_EPS = 1e-5
RMS_EPS = 1e-5
DEPTH = 1
ALPHA = (2 * DEPTH) ** 0.25

PA_XBC0 = 0
PA_V0 = PA_XBC0 + SSM_CONV_DIM
PA_O0 = PA_V0 + ML_WIDTH
PA_Z0 = PA_O0 + ML_WIDTH
PA_Q0 = PA_Z0 + SSM_WIDTH
PA_COLS = PA_Q0 + ML_QK
PB_COLS = 128
N_GATE_ROWS = 2 * ML_HEADS + 2 * ML_HEADS + 2 * SSM_HEADS
PREP_SSM0 = 32
N_PREP_ROWS = PREP_SSM0 + 10 * SSM_HEADS

MOE_BLOCK = 1024
SC_WINDOW = 128
VMEM_LIMIT = 56 * 1024 * 1024

BF16 = jnp.bfloat16
F32 = jnp.float32
NEG_INF = float("-inf")


def _cparams(sem, vmem=VMEM_LIMIT):
    return pltpu.CompilerParams(dimension_semantics=sem, vmem_limit_bytes=vmem)


def _dot(a, b):
    return jnp.dot(a, b, preferred_element_type=F32)


def _dot_nt(a, b):
    return lax.dot_general(a, b, (((1,), (1,)), ((), ())), preferred_element_type=F32)


def _dot_tn(a, b):
    return lax.dot_general(a, b, (((0,), (0,)), ((), ())), preferred_element_type=F32)


def _sigmoid(x):
    return 1.0 / (1.0 + jnp.exp(-x))


def _softplus(x):
    return jnp.maximum(x, 0.0) + jnp.log1p(jnp.exp(-jnp.abs(x)))


def _proj_kernel(x_ref, wb_ref, ws_ref, wkt_ref, oa_ref, ob_ref, kt_ref, xb_scr, *, L):
    @pl.when(pl.program_id(1) == 0)
    def _():
        xb = x_ref[...].astype(BF16)
        xb_scr[...] = xb
        ob_ref[...] = _dot(xb, ws_ref[...])
        kt = _dot_nt(wkt_ref[...], xb).astype(kt_ref.dtype)
        for c in range(kt_ref.shape[0]):
            kt_ref[c] = kt[:, c * L:(c + 1) * L]

    oa_ref[...] = _dot(xb_scr[...], wb_ref[...]).astype(oa_ref.dtype)


def _proj(x2d, w_big, w_small, w_kt, tm=1024, tn=1408):
    T, D = x2d.shape
    N = w_big.shape[1]
    tm = min(tm, T)
    L = CHUNK
    return pl.pallas_call(
        functools.partial(_proj_kernel, L=L),
        grid=(T // tm, N // tn),
        in_specs=[
            pl.BlockSpec((tm, D), lambda i, j: (i, 0)),
            pl.BlockSpec((D, tn), lambda i, j: (0, j)),
            pl.BlockSpec((D, PB_COLS), lambda i, j: (0, 0)),
            pl.BlockSpec((ML_QK, D), lambda i, j: (0, 0)),
        ],
        out_specs=[
            pl.BlockSpec((tm, tn), lambda i, j: (i, j)),
            pl.BlockSpec((tm, PB_COLS), lambda i, j: (i, 0)),
            pl.BlockSpec((tm // L, ML_QK, L), lambda i, j: (i, 0, 0)),
        ],
        out_shape=[jax.ShapeDtypeStruct((T, N), BF16), jax.ShapeDtypeStruct((T, PB_COLS), F32),
                   jax.ShapeDtypeStruct((T // L, ML_QK, L), BF16)],
        scratch_shapes=[pltpu.VMEM((tm, D), BF16)],
        compiler_params=_cparams(("parallel", "arbitrary")),
        name="in_proj",
    )(x2d, w_big, w_small, w_kt)


def _split3_dot(x, u):
    hi = x.astype(BF16)
    r1 = x - hi.astype(F32)
    mid = r1.astype(BF16)
    lo = (r1 - mid.astype(F32)).astype(BF16)
    return _dot(hi, u) + _dot(mid, u) + _dot(lo, u)


def _gate_kernel(g_ref, bias_ref, aneg_ref, o_ref, *, nc, L):
    g = g_ref[0] + bias_ref[...]
    nh = ML_HEADS
    ig = g[0:2 * nh]
    fpre = g[2 * nh:4 * nh]
    lf = -_softplus(-fpre)
    dt = _softplus(g[4 * nh:])
    dA = dt * aneg_ref[...]
    r = lax.broadcasted_iota(jnp.int32, (L, L), 0)
    c = lax.broadcasted_iota(jnp.int32, (L, L), 1)
    u_pre = (r <= c).astype(BF16)
    u_suf = (r >= c).astype(BF16)
    nf = nh + SSM_HEADS
    xf = jnp.concatenate([lf[0:nh], dA[0:SSM_HEADS]], axis=0).reshape(nf * nc, L)
    xb = jnp.concatenate([lf[nh:], dA[SSM_HEADS:]], axis=0).reshape(nf * nc, L)
    cf = _split3_dot(xf, u_pre).reshape(nf, nc, L)
    cb = _split3_dot(xb, u_suf).reshape(nf, nc, L)
    totf = jnp.broadcast_to(cf[:, :, L - 1:L], cf.shape)
    totb = jnp.broadcast_to(cb[:, :, 0:1], cb.shape)
    o_ref[0, 0:nh] = cf[0:nh]
    o_ref[0, nh:2 * nh] = cb[0:nh]
    o_ref[0, 2 * nh:4 * nh] = ig
    o_ref[0, 4 * nh:5 * nh] = totf[0:nh] - cf[0:nh] + ig[0:nh]
    o_ref[0, 5 * nh:6 * nh] = totb[0:nh] - cb[0:nh] + ig[nh:]
    lane = lax.broadcasted_iota(jnp.int32, (nh * nc, L), 1)
    pf = (ig[0:nh] - cf[0:nh]).reshape(nh * nc, L)
    pb = (ig[nh:] - cb[0:nh]).reshape(nh * nc, L)
    sh = 1
    while sh < L:
        pf = jnp.where(lane >= sh, jnp.maximum(pf, pltpu.roll(pf, sh, 1)), pf)
        pb = jnp.where(lane < L - sh, jnp.maximum(pb, pltpu.roll(pb, L - sh, 1)), pb)
        sh *= 2
    o_ref[0, 6 * nh:7 * nh] = cf[0:nh] + pf.reshape(nh, nc, L)
    o_ref[0, 7 * nh:8 * nh] = cb[0:nh] + pb.reshape(nh, nc, L)
    H = SSM_HEADS
    af, ab, dtf, dtb = cf[nh:], cb[nh:], dt[0:H], dt[H:]
    s0 = PREP_SSM0
    o_ref[0, s0:s0 + H] = af
    o_ref[0, s0 + H:s0 + 2 * H] = ab
    o_ref[0, s0 + 2 * H:s0 + 3 * H] = dtf
    o_ref[0, s0 + 3 * H:s0 + 4 * H] = dtb
    o_ref[0, s0 + 4 * H:s0 + 5 * H] = jnp.exp(af)
    o_ref[0, s0 + 5 * H:s0 + 6 * H] = jnp.exp(ab)
    o_ref[0, s0 + 6 * H:s0 + 7 * H] = dtf * jnp.exp(totf[nh:] - af)
    o_ref[0, s0 + 7 * H:s0 + 8 * H] = dtb * jnp.exp(totb[nh:] - ab)
    o_ref[0, s0 + 8 * H:s0 + 9 * H] = jnp.exp(totf[nh:])
    o_ref[0, s0 + 9 * H:s0 + 10 * H] = jnp.exp(totb[nh:])


def _gate_prep(g_rows, bias_rows, aneg_rows):
    B, R, nc, L = g_rows.shape
    return pl.pallas_call(
        functools.partial(_gate_kernel, nc=nc, L=L),
        grid=(B,),
        in_specs=[
            pl.BlockSpec((1, R, nc, L), lambda b: (b, 0, 0, 0)),
            pl.BlockSpec((R, 1, L), lambda b: (0, 0, 0)),
            pl.BlockSpec((2 * SSM_HEADS, 1, L), lambda b: (0, 0, 0)),
        ],
        out_specs=pl.BlockSpec((1, N_PREP_ROWS, nc, L), lambda b: (b, 0, 0, 0)),
        out_shape=jax.ShapeDtypeStruct((B, N_PREP_ROWS, nc, L), F32),
        compiler_params=_cparams(("parallel",)),
        name="gate_prep",
    )(g_rows, bias_rows, aneg_rows)


def _mlstm_kernel(q_ref, kt_ref, v_ref, o_ref, mg_ref, mc_ref, nw_ref, y_ref, hf_scr, hb_scr, c_scr, n_scr, m_scr,
                  *, nc, L):
    row_i = lax.broadcasted_iota(jnp.int32, (L, L), 0)
    col_i = lax.broadcasted_iota(jnp.int32, (L, L), 1)
    ones = jnp.ones((L, L), BF16)

    c_scr[...] = jnp.zeros_like(c_scr)
    n_scr[...] = jnp.zeros_like(n_scr)
    m_scr[...] = jnp.full_like(m_scr, NEG_INF)

    def chunk(c, d):
        c_ref, n_ref, m_ref = c_scr.at[d], n_scr.at[d], m_scr.at[d]
        c_prev, n_prev, m_prev = c_ref[...], n_ref[...], m_ref[...]
        sl = pl.ds(pl.multiple_of(c * L, L), L)
        q = q_ref[0, sl, :]
        kt = kt_ref[0, c, 0]
        vo = jnp.concatenate([v_ref[0, sl, :], ones], axis=1)
        g8 = mg_ref[0, 0, c]
        ct = mc_ref[0, 0, sl, :]
        b_r, i_r, a_r = g8[2 * d:2 * d + 1], g8[2 * d + 1:2 * d + 2], g8[4 + d:5 + d]
        b_c = jnp.broadcast_to(ct[:, 2 * d:2 * d + 1], (L, L))
        mi_c = jnp.broadcast_to(ct[:, 6 + d:7 + d], (L, L))
        if d == 0:
            gtot = b_r[:, L - 1:L]
            mask = col_i <= row_i
        else:
            gtot = b_r[:, 0:1]
            mask = col_i >= row_i
        m_loc = jnp.max(a_r, axis=1, keepdims=True)
        kwt = (kt.astype(F32) * jnp.exp(a_r - m_loc)).astype(BF16)
        uu = _dot(kwt, vo)
        qq = _dot(q, jnp.concatenate([kt, c_prev.astype(BF16), n_prev.astype(BF16)], axis=1))
        s, q_c, q_n = qq[:, 0:L], qq[:, L:L + ML_DV], qq[:, L + ML_DV:]
        dmat = jnp.where(mask, b_c - b_r + i_r, NEG_INF)
        m_inter = b_c + m_prev
        m_t = jnp.maximum(m_inter, mi_c)
        s_mat = s * jnp.exp(dmat - m_t)
        inter_w = jnp.exp(m_inter - m_t)
        ss = _dot(s_mat.astype(BF16), vo)
        den = ss[:, ML_DV:] + inter_w * q_n
        r = 1.0 / jnp.maximum(jnp.abs(den), jnp.exp(-m_t))
        num = ss[:, 0:ML_DV] + jnp.concatenate([inter_w, inter_w], axis=1) * q_c
        h = num * jnp.concatenate([r, r], axis=1)
        m_new = jnp.maximum(gtot + m_prev, m_loc)
        decay = jnp.exp(gtot + m_prev - m_new)
        sc = jnp.exp(m_loc - m_new)
        c_ref[...] = decay * c_prev + sc * uu[:, 0:ML_DV]
        n_ref[...] = decay * n_prev + sc * uu[:, ML_DV:]
        m_ref[...] = m_new
        return sl, h

    def finish(sl, h):
        mu = jnp.mean(h, axis=1, keepdims=True)
        d = h - mu
        var = jnp.mean(d * d, axis=1, keepdims=True)
        y = d * lax.rsqrt(var + LN_EPS) * nw_ref[...]
        y = y * _sigmoid(o_ref[0, sl, :].astype(F32))
        y_ref[0, sl, :] = y.astype(y_ref.dtype)

    def first_half(i, carry):
        sl, h = chunk(i, 0)
        hf_scr[sl, :] = h
        sl, h = chunk(nc - 1 - i, 1)
        hb_scr[sl, :] = h
        return carry

    def second_half(i, carry):
        sl, h = chunk(i, 0)
        finish(sl, h + hb_scr[sl, :])
        sl, h = chunk(nc - 1 - i, 1)
        finish(sl, hf_scr[sl, :] + h)
        return carry

    half = nc // 2
    lax.fori_loop(0, half, first_half, 0, unroll=4)
    lax.fori_loop(half, nc, second_half, 0, unroll=4)


def _mlstm(pa3, kt5, mg, mc, norm_w):
    B, S, _ = pa3.shape
    nc = S // CHUNK
    H = ML_HEADS
    q_blk, v_blk, o_blk = PA_Q0 // ML_DQK, PA_V0 // ML_DV, PA_O0 // ML_DV
    return pl.pallas_call(
        functools.partial(_mlstm_kernel, nc=nc, L=CHUNK),
        grid=(B, H),
        in_specs=[
            pl.BlockSpec((1, S, ML_DQK), lambda b, h: (b, 0, q_blk + h)),
            pl.BlockSpec((1, nc, 1, ML_DQK, CHUNK), lambda b, h: (b, 0, h, 0, 0)),
            pl.BlockSpec((1, S, ML_DV), lambda b, h: (b, 0, v_blk + h)),
            pl.BlockSpec((1, S, ML_DV), lambda b, h: (b, 0, o_blk + h)),
            pl.BlockSpec((1, 1, nc, 8, CHUNK), lambda b, h: (b, h, 0, 0, 0)),
            pl.BlockSpec((1, 1, S, 8), lambda b, h: (b, h, 0, 0)),
            pl.BlockSpec((1, ML_DV), lambda b, h: (0, h)),
        ],
        out_specs=pl.BlockSpec((1, S, ML_DV), lambda b, h: (b, 0, h)),
        out_shape=jax.ShapeDtypeStruct((B, S, ML_WIDTH), BF16),
        scratch_shapes=[
            pltpu.VMEM((S, ML_DV), F32),
            pltpu.VMEM((S, ML_DV), F32),
            pltpu.VMEM((2, ML_DQK, ML_DV), F32),
            pltpu.VMEM((2, ML_DQK, CHUNK), F32),
            pltpu.VMEM((2, 1, 1), F32),
        ],
        compiler_params=_cparams(("parallel", "parallel")),
        name="mlstm",
    )(pa3, kt5, pa3, pa3, mg, mc, norm_w)


CONV_EDGE = 16


def _conv_shift_matrix(L):
    pad, E = CONV_K // 2, CONV_EDGE
    t = jnp.arange(L, dtype=jnp.int32)[:, None]
    blocks = []
    u = jnp.arange(L, dtype=jnp.int32)[None, :]
    for j in range(CONV_K):
        blocks.append(u == t + j - pad)
    v = jnp.arange(E, dtype=jnp.int32)[None, :]
    for j in range(pad):
        blocks.append(v - E == t + j - pad)
    for j in range(pad + 1, CONV_K):
        blocks.append(v + L == t + j - pad)
    return jnp.concatenate(blocks, axis=1).astype(BF16)


def _conv_kernel(x_ref, halo_ref, w_ref, b_ref, sh_ref, o_ref, ext_scr, *, Lc, L):
    pad, E = CONV_K // 2, CONV_EDGE
    ext_scr[0:E, :] = halo_ref[0, 0, 0:E, :]
    ext_scr[E:E + Lc, :] = x_ref[0]
    ext_scr[E + Lc:2 * E + Lc, :] = halo_ref[0, 0, E:2 * E, :]
    taps = [w_ref[j:j + 1, :].astype(BF16) for j in range(CONV_K)]
    scaled = lambda rows, j: rows * taps[j]
    for r0 in range(0, Lc, L):
        cen = ext_scr[E + r0:E + r0 + L, :]
        top = ext_scr[r0:r0 + E, :]
        bot = ext_scr[E + r0 + L:2 * E + r0 + L, :]
        pieces = [scaled(cen, j) for j in range(CONV_K)]
        pieces += [scaled(top, j) for j in range(pad)] + [scaled(bot, j) for j in range(pad + 1, CONV_K)]
        acc = _dot(sh_ref[...], jnp.concatenate(pieces, axis=0)) + b_ref[...]
        o_ref[0, r0:r0 + L, :] = (acc * _sigmoid(acc)).astype(o_ref.dtype)


def _conv(pa3, halo, conv_w8, conv_b, Lc=512):
    B, S, _ = pa3.shape
    C = SSM_CONV_DIM
    xbc_blk = PA_XBC0 // C
    L = CHUNK
    sh = _conv_shift_matrix(L)
    return pl.pallas_call(
        functools.partial(_conv_kernel, Lc=Lc, L=L),
        grid=(B, S // Lc),
        in_specs=[
            pl.BlockSpec((1, Lc, C), lambda b, i: (b, i, xbc_blk)),
            pl.BlockSpec((1, 1, 2 * CONV_EDGE, C), lambda b, i: (b, i, 0, 0)),
            pl.BlockSpec((8, C), lambda b, i: (0, 0)),
            pl.BlockSpec((1, C), lambda b, i: (0, 0)),
            pl.BlockSpec(sh.shape, lambda b, i: (0, 0)),
        ],
        out_specs=pl.BlockSpec((1, Lc, C), lambda b, i: (b, i, 0)),
        out_shape=jax.ShapeDtypeStruct((B, S, C), BF16),
        scratch_shapes=[pltpu.VMEM((Lc + 2 * CONV_EDGE, C), BF16)],
        compiler_params=_cparams(("parallel", "parallel")),
        name="ssm_conv",
    )(pa3, halo, conv_w8, conv_b, sh)


def _expansion_matrix(src_cols, width):
    src = jnp.repeat(jnp.asarray(src_cols, jnp.int32), width)
    e = (jnp.arange(128, dtype=jnp.int32)[:, None] == src[None, :]).astype(BF16)
    return jnp.concatenate([e, e], axis=0)


def _split2(t):
    hi = t.astype(BF16)
    return jnp.concatenate([hi, (t - hi.astype(F32)).astype(BF16)], axis=1)


def _ssd_bwd_kernel(x_ref, b_ref, tb_ref, dec_ref, ew_ref, hb_ref, h_scr, *, L):
    G, hg, P, N = SSM_GROUPS, SSM_HEADS // SSM_GROUPS, SSM_HEADDIM, SSM_STATE
    GW = hg * P

    @pl.when(pl.program_id(1) == 0)
    def _():
        h_scr[...] = jnp.zeros_like(h_scr)

    for s in reversed(range(x_ref.shape[1] // L)):
        rows = slice(s * L, (s + 1) * L)
        w_all = _dot(_split2(tb_ref[0, s].T), ew_ref[...])
        for g in range(G):
            w = w_all[:, g * GW:(g + 1) * GW]
            xs = (x_ref[0, rows, g * GW:(g + 1) * GW].astype(F32) * w).astype(BF16)
            st = _dot_tn(b_ref[0, rows, g * N:(g + 1) * N], xs)
            h_old = h_scr[g]
            hb_ref[0, s, g] = h_old.astype(hb_ref.dtype)
            h_scr[g] = dec_ref[0, s, 1:2, g * GW:(g + 1) * GW] * h_old + st


def _ssd_bwd(xc, tb, decs, cpb):
    B, S, _ = xc.shape
    nc = S // CHUNK
    nblk = nc // cpb
    R = cpb * CHUNK
    G, N, W = SSM_GROUPS, SSM_STATE, SSM_WIDTH
    return pl.pallas_call(
        functools.partial(_ssd_bwd_kernel, L=CHUNK),
        grid=(B, nblk),
        in_specs=[
            pl.BlockSpec((1, R, W), lambda b, c: (b, nblk - 1 - c, 0)),
            pl.BlockSpec((1, R, G * N), lambda b, c: (b, nblk - 1 - c, W // (G * N))),
            pl.BlockSpec((1, cpb, 128, CHUNK), lambda b, c: (b, nblk - 1 - c, 0, 0)),
            pl.BlockSpec((1, cpb, 2, W), lambda b, c: (b, nblk - 1 - c, 0, 0)),
            pl.BlockSpec((256, W), lambda b, c: (0, 0)),
        ],
        out_specs=pl.BlockSpec((1, cpb, G, N, W // G), lambda b, c: (b, nblk - 1 - c, 0, 0, 0)),
        out_shape=jax.ShapeDtypeStruct((B, nc, G, N, W // G), BF16),
        scratch_shapes=[pltpu.VMEM((G, N, W // G), F32)],
        compiler_params=_cparams(("parallel", "arbitrary")),
        name="ssd_bwd_states",
    )(xc, xc, tb, decs, _expansion_matrix([5 * SSM_HEADS + h for h in range(SSM_HEADS)], SSM_HEADDIM))


def _ssd_main_kernel(x_ref, b_ref, c_ref, z_ref, tb_ref, dec_ref, hb_ref, dsk_ref, nw_ref, ed_ref, ee_ref,
                     y_ref, h_scr, *, L):
    G, hg, P, N = SSM_GROUPS, SSM_HEADS // SSM_GROUPS, SSM_HEADDIM, SSM_STATE
    H = SSM_HEADS
    GW = hg * P

    @pl.when(pl.program_id(1) == 0)
    def _():
        h_scr[...] = jnp.zeros_like(h_scr)

    row_i = lax.broadcasted_iota(jnp.int32, (L, L), 0)
    col_i = lax.broadcasted_iota(jnp.int32, (L, L), 1)
    lower = col_i <= row_i
    upper = col_i >= row_i
    lane_g = lax.broadcasted_iota(jnp.int32, (L, GW), 1)

    for s, g in [(s, g) for s in range(x_ref.shape[1] // L) for g in range(G)]:
        rows = slice(s * L, (s + 1) * L)
        tb = tb_ref[0, s]
        sc2 = _split2(tb.T)
        xg = x_ref[0, rows, g * GW:(g + 1) * GW]
        bg = b_ref[0, rows, g * N:(g + 1) * N]
        cg = c_ref[0, rows, g * N:(g + 1) * N]
        cb = _dot_nt(cg, bg)
        acol = _dot(sc2, ed_ref[g])
        ecol = _dot(sc2, ee_ref[g])
        ms, xms = [], []
        for j in range(hg):
            hd = g * hg + j
            af_r, ab_r = tb[hd:hd + 1], tb[H + hd:H + hd + 1]
            dtf_r, dtb_r = tb[6 * H + hd:6 * H + hd + 1], tb[7 * H + hd:7 * H + hd + 1]
            af_c, ab_c = acol[:, j * L:(j + 1) * L], acol[:, (hg + j) * L:(hg + j + 1) * L]
            fm = jnp.exp(jnp.where(lower, af_c - af_r, NEG_INF)) * dtf_r
            bm = jnp.exp(jnp.where(upper, ab_c - ab_r, NEG_INF)) * dtb_r
            ms.append((cb * (fm + bm)).astype(BF16))
            xms.append(jnp.where((lane_g >= j * P) & (lane_g < (j + 1) * P), xg, jnp.zeros_like(xg)))
        efc, ebc, wfc = ecol[:, 0:GW], ecol[:, GW:2 * GW], ecol[:, 2 * GW:]
        y = _dot(jnp.concatenate(ms, axis=1), jnp.concatenate(xms, axis=0))
        h_old = h_scr[g]
        y = y + efc * _dot(cg, h_old.astype(BF16))
        y = y + ebc * _dot(cg, hb_ref[0, s, g])
        xgf = xg.astype(F32)
        y = y + dsk_ref[:, g * GW:(g + 1) * GW] * xgf
        xs = (xgf * wfc).astype(BF16)
        h_scr[g] = dec_ref[0, s, 0:1, g * GW:(g + 1) * GW] * h_old + _dot_tn(bg, xs)
        zg = z_ref[0, rows, g * GW:(g + 1) * GW].astype(F32)
        y = y * (zg * _sigmoid(zg))
        y = y * lax.rsqrt(jnp.mean(y * y, axis=1, keepdims=True) + RMS_EPS)
        y_ref[0, rows, g * GW:(g + 1) * GW] = (y * nw_ref[:, g * GW:(g + 1) * GW]).astype(y_ref.dtype)


def _ssd_main(xc, pa3, tb, decs, hb, dskip_row, norm_w, cpb):
    B, S, _ = xc.shape
    nc = S // CHUNK
    R = cpb * CHUNK
    G, N, W = SSM_GROUPS, SSM_STATE, SSM_WIDTH
    H, hg, P = SSM_HEADS, SSM_HEADS // SSM_GROUPS, SSM_HEADDIM
    z_blk = PA_Z0 // W
    e_dec = jnp.stack([_expansion_matrix([q * H + g * hg + j for q in (0, 1) for j in range(hg)], CHUNK)
                       for g in range(G)])
    e_ew = jnp.stack([_expansion_matrix([q * H + g * hg + j for q in (2, 3, 4) for j in range(hg)], P)
                      for g in range(G)])
    return pl.pallas_call(
        functools.partial(_ssd_main_kernel, L=CHUNK),
        grid=(B, nc // cpb),
        in_specs=[
            pl.BlockSpec((1, R, W), lambda b, c: (b, c, 0)),
            pl.BlockSpec((1, R, G * N), lambda b, c: (b, c, W // (G * N))),
            pl.BlockSpec((1, R, G * N), lambda b, c: (b, c, W // (G * N) + 1)),
            pl.BlockSpec((1, R, W), lambda b, c: (b, c, z_blk)),
            pl.BlockSpec((1, cpb, 128, CHUNK), lambda b, c: (b, c, 0, 0)),
            pl.BlockSpec((1, cpb, 2, W), lambda b, c: (b, c, 0, 0)),
            pl.BlockSpec((1, cpb, G, N, W // G), lambda b, c: (b, c, 0, 0, 0)),
            pl.BlockSpec((1, W), lambda b, c: (0, 0)),
            pl.BlockSpec((1, W), lambda b, c: (0, 0)),
            pl.BlockSpec((G, 256, 2 * hg * CHUNK), lambda b, c: (0, 0, 0)),
            pl.BlockSpec((G, 256, 3 * W // G), lambda b, c: (0, 0, 0)),
        ],
        out_specs=pl.BlockSpec((1, R, W), lambda b, c: (b, c, 0)),
        out_shape=jax.ShapeDtypeStruct((B, S, W), BF16),
        scratch_shapes=[pltpu.VMEM((G, N, W // G), F32)],
        compiler_params=_cparams(("parallel", "arbitrary")),
        name="ssd_main",
    )(xc, xc, xc, pa3, tb, decs, hb, dskip_row, norm_w, e_dec, e_ew)


def _pack_bf16_pairs(v):
    n = v.shape[1] // 2
    hi = lax.bitcast_convert_type(v[:, :n].astype(BF16).astype(F32), jnp.uint32)
    lo = lax.bitcast_convert_type(v[:, n:].astype(BF16).astype(F32), jnp.uint32)
    return (hi & jnp.uint32(0xFFFF0000)) | (lo >> 16)


def _unpack_bf16_pairs(u):
    hi = lax.bitcast_convert_type(u & jnp.uint32(0xFFFF0000), F32)
    lo = lax.bitcast_convert_type(u << 16, F32)
    return hi, lo


def _layer_norm(v, g, b):
    mu = jnp.mean(v, axis=1, keepdims=True)
    d = v - mu
    var = jnp.mean(d * d, axis=1, keepdims=True)
    return d * lax.rsqrt(var + LN_EPS) * g + b


def _outproj_kernel(yml_ref, yssm_ref, x_ref, wo_ref, g_ref, b_ref, wr_ref, h_ref, hpa_ref, hpb_ref, sc_ref):
    mix = _dot(yml_ref[...], wo_ref[0:ML_WIDTH, :]) + _dot(yssm_ref[...], wo_ref[ML_WIDTH:, :])
    h = _layer_norm(ALPHA * x_ref[...] + mix, g_ref[...], b_ref[...])
    h_ref[...] = h
    hp = _pack_bf16_pairs(h)
    q = hp.shape[1] // 2
    hpa_ref[...] = hp[:, :q]
    hpb_ref[...] = hp[:, q:]
    sc_ref[...] = _sigmoid(_dot(h.astype(BF16), wr_ref[...]))


def _outproj(yml, yssm, x2d, w_out, ln_g, ln_b, w_router, tm=512):
    T, D = x2d.shape
    tm = min(tm, T)
    row = lambda i: (i, 0)
    fixed = lambda i: (0, 0)
    return pl.pallas_call(
        _outproj_kernel,
        grid=(T // tm,),
        in_specs=[
            pl.BlockSpec((tm, ML_WIDTH), row),
            pl.BlockSpec((tm, SSM_WIDTH), row),
            pl.BlockSpec((tm, D), row),
            pl.BlockSpec((ML_WIDTH + SSM_WIDTH, D), fixed),
            pl.BlockSpec((1, D), fixed),
            pl.BlockSpec((1, D), fixed),
            pl.BlockSpec((D, 128), fixed),
        ],
        out_specs=[pl.BlockSpec((tm, D), row), pl.BlockSpec((tm, D // 4), row), pl.BlockSpec((tm, D // 4), row),
                   pl.BlockSpec((tm, 128), row)],
        out_shape=[
            jax.ShapeDtypeStruct((T, D), F32),
            jax.ShapeDtypeStruct((T, D // 4), jnp.uint32),
            jax.ShapeDtypeStruct((T, D // 4), jnp.uint32),
            jax.ShapeDtypeStruct((T, 128), F32),
        ],
        compiler_params=_cparams(("parallel",)),
        name="out_proj_ln1_router",
    )(yml, yssm, x2d, w_out, ln_g, ln_b, w_router)


def _route_kernel(sc_ref, rb_ref, idx_ref, w_ref, rank_ref, cnt_ref, carry_scr, *, Tt):
    E, NG = N_EXPERTS, N_EXPERT_GROUPS
    EG = E // NG

    @pl.when(pl.program_id(0) == 0)
    def _():
        carry_scr[...] = jnp.zeros_like(carry_scr)

    s = sc_ref[...].T[0:E]
    s3 = s.reshape(NG, EG, Tt)
    ch3 = (s + rb_ref[:, 0:1]).reshape(NG, EG, Tt)
    e_in = lax.broadcasted_iota(jnp.int32, (NG, EG, Tt), 1)
    g_in = lax.broadcasted_iota(jnp.int32, (NG, EG, Tt), 0)
    f_i = g_in * EG + e_in
    m1 = jnp.max(ch3, axis=1, keepdims=True)
    i1 = jnp.min(jnp.where(ch3 == m1, e_in, EG), axis=1, keepdims=True)
    m2 = jnp.max(jnp.where(e_in == i1, NEG_INF, ch3), axis=1, keepdims=True)
    gs = m1 + m2
    g_i = lax.broadcasted_iota(jnp.int32, (NG, 1, Tt), 0)
    gsel = jnp.zeros((NG, 1, Tt), jnp.bool_)
    cur = gs
    for _ in range(TOPK_GROUPS):
        mx = jnp.max(cur, axis=0, keepdims=True)
        pick = jnp.min(jnp.where(cur == mx, g_i, NG), axis=0, keepdims=True)
        hit = g_i == pick
        gsel = gsel | hit
        cur = jnp.where(hit, NEG_INF, cur)
    cur = jnp.where(gsel, ch3, NEG_INF)
    selmask = jnp.zeros((NG, EG, Tt), jnp.bool_)
    idxs, ws = [], []
    for _ in range(TOP_K):
        mx = jnp.max(jnp.max(cur, axis=1, keepdims=True), axis=0, keepdims=True)
        cand = jnp.where(cur == mx, f_i, E)
        pick = jnp.min(jnp.min(cand, axis=1, keepdims=True), axis=0, keepdims=True)
        hit = f_i == pick
        wk = jnp.sum(jnp.sum(jnp.where(hit, s3, 0.0), axis=1, keepdims=True), axis=0, keepdims=True)
        idxs.append(pick.reshape(1, Tt))
        ws.append(wk.reshape(1, Tt))
        selmask = selmask | hit
        cur = jnp.where(hit, NEG_INF, cur)
    wsum = ws[0]
    for k in range(1, TOP_K):
        wsum = wsum + ws[k]
    r = lax.broadcasted_iota(jnp.int32, (Tt, Tt), 0)
    c = lax.broadcasted_iota(jnp.int32, (Tt, Tt), 1)
    ustrict = (r < c).astype(BF16)
    self32 = jnp.where(selmask, 1.0, 0.0).reshape(E, Tt)
    excl = _dot(self32.astype(BF16), ustrict) + carry_scr[:, 0:1]
    excl3 = excl.reshape(NG, EG, Tt)
    for k in range(TOP_K):
        hit = f_i == idxs[k].reshape(1, 1, Tt)
        rk = jnp.sum(jnp.sum(jnp.where(hit, excl3, 0.0), axis=1, keepdims=True), axis=0, keepdims=True)
        idx_ref[k:k + 1, :] = idxs[k]
        w_ref[k:k + 1, :] = ws[k] / wsum * ROUTED_SCALE
        rank_ref[k:k + 1, :] = rk.reshape(1, Tt).astype(jnp.int32)
    carry_scr[...] = carry_scr[...] + jnp.sum(self32, axis=1, keepdims=True)
    cnt_ref[...] = carry_scr[...]


def _route(scores, rbias, Tt=512):
    T = scores.shape[0]
    kt = lambda i: (0, i)
    return pl.pallas_call(
        functools.partial(_route_kernel, Tt=Tt),
        grid=(T // Tt,),
        in_specs=[pl.BlockSpec((Tt, 128), lambda i: (i, 0)), pl.BlockSpec((N_EXPERTS, 128), lambda i: (0, 0))],
        out_specs=[
            pl.BlockSpec((TOP_K, Tt), kt),
            pl.BlockSpec((TOP_K, Tt), kt),
            pl.BlockSpec((TOP_K, Tt), kt),
            pl.BlockSpec((N_EXPERTS, 128), lambda i: (0, 0)),
        ],
        out_shape=[
            jax.ShapeDtypeStruct((TOP_K, T), jnp.int32),
            jax.ShapeDtypeStruct((TOP_K, T), F32),
            jax.ShapeDtypeStruct((TOP_K, T), jnp.int32),
            jax.ShapeDtypeStruct((N_EXPERTS, 128), F32),
        ],
        scratch_shapes=[pltpu.VMEM((N_EXPERTS, 128), F32)],
        compiler_params=_cparams(("arbitrary",)),
        name="route_topk",
    )(scores, rbias)


def _dest_kernel(ps_ref, idx_ref, rank_ref, o_ref):
    idx = idx_ref[...]
    start = jnp.zeros(idx.shape, jnp.int32)
    for e in range(N_EXPERTS):
        start = jnp.where(idx == e, ps_ref[e], start)
    o_ref[...] = start + rank_ref[...]


def _dest(pstarts, idx, rank, Tt=2048):
    K, T = idx.shape
    Tt = min(Tt, T)
    blk = pl.BlockSpec((K, Tt), lambda i, ps: (0, i))
    return pl.pallas_call(
        _dest_kernel,
        grid_spec=pltpu.PrefetchScalarGridSpec(num_scalar_prefetch=1, grid=(T // Tt,), in_specs=[blk, blk],
                                               out_specs=blk),
        out_shape=jax.ShapeDtypeStruct((K, T), jnp.int32),
        compiler_params=_cparams(("parallel",)),
        name="dispatch_slots",
    )(pstarts, idx, rank)


def _sc_mesh():
    return plsc.VectorSubcoreMesh(core_axis_name="c", subcore_axis_name="s")


def _sc_scatter_rows(src, idx, n_out, win=SC_WINDOW):
    K, T = idx.shape
    W = src.shape[1]

    @functools.partial(pl.kernel, out_type=jax.ShapeDtypeStruct((n_out, W), src.dtype), mesh=_sc_mesh(),
                       scratch_types=[], name="sc_scatter_rows")
    def k(x_hbm, i_hbm, o_hbm):
        def body(x_vmem, i_vmem):
            for j in range(K):
                pltpu.sync_copy(x_vmem, o_hbm.at[i_vmem.at[j]])

        pltpu.emit_pipeline(
            body,
            grid=(T // win,),
            in_specs=[pl.BlockSpec((win, W), lambda i: (i, 0)), pl.BlockSpec((K, win), lambda i: (0, i))],
            out_specs=[],
            core_axis_name=("c", "s"),
            dimension_semantics=(pltpu.PARALLEL,),
        )(x_hbm, i_hbm)

    return k(src, idx)


def _sc_gather_rows(table, idx, win=SC_WINDOW):
    n = idx.shape[0]
    W = table.shape[1]

    @functools.partial(pl.kernel, out_type=jax.ShapeDtypeStruct((n, W), table.dtype), mesh=_sc_mesh(),
                       scratch_types=[], name="sc_gather_rows")
    def k(t_hbm, i_hbm, o_hbm):
        def body(i_vmem, o_vmem):
            pltpu.sync_copy(t_hbm.at[i_vmem.at[0]], o_vmem)

        pltpu.emit_pipeline(
            body,
            grid=(n // win,),
            in_specs=[pl.BlockSpec((1, win), lambda i: (0, i))],
            out_specs=[pl.BlockSpec((win, W), lambda i: (i, 0))],
            core_axis_name=("c", "s"),
            dimension_semantics=(pltpu.PARALLEL,),
        )(i_hbm, o_hbm)

    return k(table, idx.reshape(1, n))


def _expert_kernel(be_ref, nu_ref, xa_ref, xb_ref, wg_ref, wu_ref, wd_ref, oa_ref, ob_ref, wg_s, wu_s, wd_s):
    half = D_MODEL // 2
    i = pl.program_id(0)
    active = i < nu_ref[0]

    @pl.when(active & ((i == 0) | (be_ref[i] != be_ref[jnp.maximum(i - 1, 0)])))
    def _():
        wg_s[...] = wg_ref[0].astype(BF16)
        wu_s[...] = wu_ref[0].astype(BF16)
        wd_s[...] = wd_ref[0].astype(BF16)

    @pl.when(active)
    def _():
        hi, lo = _unpack_bf16_pairs(jnp.concatenate([xa_ref[...], xb_ref[...]], axis=1))
        hi, lo = hi.astype(BF16), lo.astype(BF16)
        gt = _dot(hi, wg_s[0:half, :]) + _dot(lo, wg_s[half:, :])
        up = _dot(hi, wu_s[0:half, :]) + _dot(lo, wu_s[half:, :])
        hmid = (gt * _sigmoid(gt) * up).astype(BF16)
        packed = _pack_bf16_pairs(_dot(hmid, wd_s[...]))
        q = packed.shape[1] // 2
        oa_ref[...] = packed[:, :q]
        ob_ref[...] = packed[:, q:]

    @pl.when(jnp.logical_not(active))
    def _():
        oa_ref[...] = jnp.zeros_like(oa_ref)
        ob_ref[...] = jnp.zeros_like(ob_ref)


def _experts(xs_a, xs_b, block_expert, n_used, w_gate, w_up, w_down, blk=MOE_BLOCK):
    P, quarter = xs_a.shape
    nb = P // blk
    D, F = D_MODEL, D_EXPERT
    cur = lambda i, be, nu: jnp.minimum(i, nu[0] - 1)
    return pl.pallas_call(
        _expert_kernel,
        grid_spec=pltpu.PrefetchScalarGridSpec(
            num_scalar_prefetch=2,
            grid=(nb,),
            in_specs=[
                pl.BlockSpec((blk, quarter), lambda i, be, nu: (cur(i, be, nu), 0)),
                pl.BlockSpec((blk, quarter), lambda i, be, nu: (cur(i, be, nu), 0)),
                pl.BlockSpec((1, D, F), lambda i, be, nu: (be[cur(i, be, nu)], 0, 0)),
                pl.BlockSpec((1, D, F), lambda i, be, nu: (be[cur(i, be, nu)], 0, 0)),
                pl.BlockSpec((1, F, D), lambda i, be, nu: (be[cur(i, be, nu)], 0, 0)),
            ],
            out_specs=[pl.BlockSpec((blk, quarter), lambda i, be, nu: (i, 0)),
                       pl.BlockSpec((blk, quarter), lambda i, be, nu: (i, 0))],
            scratch_shapes=[pltpu.VMEM((D, F), BF16), pltpu.VMEM((D, F), BF16), pltpu.VMEM((F, D), BF16)],
        ),
        out_shape=[jax.ShapeDtypeStruct((P, quarter), jnp.uint32), jax.ShapeDtypeStruct((P, quarter), jnp.uint32)],
        compiler_params=_cparams(("arbitrary",)),
        name="expert_mlp",
    )(block_expert, n_used, xs_a, xs_b, w_gate, w_up, w_down)


def _final_kernel(h_ref, ga_ref, gb_ref, wt_ref, wsg_ref, wsu_ref, wsd_ref, g_ref, b_ref, o_ref):
    h = h_ref[...]
    hb = h.astype(BF16)
    gt = _dot(hb, wsg_ref[...])
    up = _dot(hb, wsu_ref[...])
    shared = _dot((gt * _sigmoid(gt) * up).astype(BF16), wsd_ref[...])
    acc_hi = None
    for k in range(TOP_K):
        hi, lo = _unpack_bf16_pairs(jnp.concatenate([ga_ref[k], gb_ref[k]], axis=1))
        wk = wt_ref[:, k:k + 1]
        acc_hi = wk * hi if acc_hi is None else acc_hi + wk * hi
        acc_lo = wk * lo if k == 0 else acc_lo + wk * lo
    routed = jnp.concatenate([acc_hi, acc_lo], axis=1)
    o_ref[...] = _layer_norm(ALPHA * h + (routed + shared), g_ref[...], b_ref[...])


def _final(h, gk_a, gk_b, wt, ws_gate, ws_up, ws_down, ln_g, ln_b, tm=512):
    T, D = h.shape
    tm = min(tm, T)
    F = D_EXPERT
    fixed = lambda i: (0, 0)
    return pl.pallas_call(
        _final_kernel,
        grid=(T // tm,),
        in_specs=[
            pl.BlockSpec((tm, D), lambda i: (i, 0)),
            pl.BlockSpec((TOP_K, tm, D // 4), lambda i: (0, i, 0)),
            pl.BlockSpec((TOP_K, tm, D // 4), lambda i: (0, i, 0)),
            pl.BlockSpec((tm, TOP_K), lambda i: (i, 0)),
            pl.BlockSpec((D, F), fixed),
            pl.BlockSpec((D, F), fixed),
            pl.BlockSpec((F, D), fixed),
            pl.BlockSpec((1, D), fixed),
            pl.BlockSpec((1, D), fixed),
        ],
        out_specs=pl.BlockSpec((tm, D), lambda i: (i, 0)),
        out_shape=jax.ShapeDtypeStruct((T, D), F32),
        compiler_params=_cparams(("parallel",)),
        name="shared_combine_ln2",
    )(h, gk_a, gk_b, wt, ws_gate, ws_up, ws_down, ln_g, ln_b)


def _mixers(x, w_in, ml_i_bias, ml_f_bias, ml_norm_w, conv_w, conv_b, dt_bias, a_log, d_skip, ssm_norm_w):
    B, S, D = x.shape
    T = B * S
    nc = S // CHUNK
    H = ML_HEADS
    k0, v0 = ML_QK, 2 * ML_QK
    g0 = v0 + 2 * ML_WIDTH
    z0 = g0 + 4 * H
    x0 = z0 + SSM_WIDTH
    d0 = x0 + SSM_CONV_DIM
    w_big = jnp.concatenate([w_in[:, x0:d0], w_in[:, v0:g0], w_in[:, z0:x0], w_in[:, :k0]], axis=1).astype(BF16)
    w_small = jnp.concatenate(
        [w_in[:, g0:z0], w_in[:, d0:], jnp.zeros((D, PB_COLS - N_GATE_ROWS), w_in.dtype)], axis=1).astype(BF16)
    w_kt = (w_in[:, k0:v0] * (ML_DQK ** -0.5)).T.astype(BF16)
    pa, pb, kt = _proj(x.reshape(T, D), w_big, w_small, w_kt)
    pa3 = pa.reshape(B, S, PA_COLS)
    kt5 = kt.reshape(B, nc, H, ML_DQK, CHUNK)

    g_rows = pb.reshape(B, S, PB_COLS)[:, :, :N_GATE_ROWS].transpose(0, 2, 1).reshape(B, N_GATE_ROWS, nc, CHUNK)
    bias = jnp.concatenate([ml_i_bias, ml_f_bias, dt_bias]).astype(F32)
    bias_rows = jnp.broadcast_to(bias[:, None, None], (N_GATE_ROWS, 1, CHUNK))
    aneg_rows = jnp.broadcast_to((-jnp.exp(a_log.astype(F32)))[:, None, None], (2 * SSM_HEADS, 1, CHUNK))
    prep = _gate_prep(g_rows, bias_rows, aneg_rows)
    m8 = jnp.stack([prep[:, 0:H], prep[:, 2 * H:3 * H], prep[:, H:2 * H], prep[:, 3 * H:4 * H],
                    prep[:, 4 * H:5 * H], prep[:, 5 * H:6 * H], prep[:, 6 * H:7 * H], prep[:, 7 * H:8 * H]],
                   axis=2)
    mg = m8.transpose(0, 1, 3, 2, 4)
    mc = m8.transpose(0, 1, 3, 4, 2).reshape(B, H, S, 8)
    s0, SH = PREP_SSM0, SSM_HEADS
    tb = jnp.concatenate([prep[:, s0:s0 + 2 * SH], prep[:, s0 + 4 * SH:s0 + 8 * SH],
                          prep[:, s0 + 2 * SH:s0 + 4 * SH]], axis=1).transpose(0, 2, 1, 3)
    decs = prep[:, s0 + 8 * SH:s0 + 10 * SH, :, 0].reshape(B, 2, SH, nc).transpose(0, 3, 1, 2)
    decs = jnp.repeat(decs, SSM_HEADDIM, axis=3)

    y_ml = _mlstm(pa3, kt5, mg, mc, ml_norm_w.astype(F32).reshape(1, ML_WIDTH))

    Lc = 512 if S % 512 == 0 else CHUNK
    nt = S // Lc
    pa4 = pa3.reshape(B, nt, Lc, PA_COLS)
    top = pa4[:, :, Lc - CONV_EDGE:, PA_XBC0:PA_XBC0 + SSM_CONV_DIM]
    bot = pa4[:, :, :CONV_EDGE, PA_XBC0:PA_XBC0 + SSM_CONV_DIM]
    zrow = jnp.zeros((B, 1, CONV_EDGE, SSM_CONV_DIM), pa.dtype)
    top = jnp.concatenate([zrow, top[:, :-1]], axis=1)
    bot = jnp.concatenate([bot[:, 1:], zrow], axis=1)
    halo = jnp.concatenate([top, bot], axis=2)
    conv_w8 = jnp.concatenate([conv_w.astype(F32), jnp.zeros((8 - CONV_K, SSM_CONV_DIM), F32)], axis=0)
    xc = _conv(pa3, halo, conv_w8, conv_b.astype(F32).reshape(1, SSM_CONV_DIM), Lc=Lc)

    cpb = 2 if nc % 2 == 0 else 1
    hb = _ssd_bwd(xc, tb, decs, cpb)
    dsk = jnp.repeat(d_skip.astype(F32), SSM_HEADDIM).reshape(1, SSM_WIDTH)
    y_ssm = _ssd_main(xc, pa3, tb, decs, hb, dsk, ssm_norm_w.astype(F32).reshape(1, SSM_WIDTH), cpb)
    return y_ml.reshape(T, ML_WIDTH), y_ssm.reshape(T, SSM_WIDTH)


def _moe_dispatch(idx, rank, counts, T, blk):
    E = N_EXPERTS
    M = T * TOP_K
    nb = (M + E * (blk - 1)) // blk
    pcounts = (counts + blk - 1) // blk * blk
    pends = jnp.cumsum(pcounts)
    pstarts = (pends - pcounts).astype(jnp.int32)
    dest = _dest(pstarts, idx, rank)
    block_start = jnp.arange(nb, dtype=jnp.int32) * blk
    block_expert = jnp.minimum(jnp.sum(pends[None, :] <= block_start[:, None], axis=1), E - 1).astype(jnp.int32)
    n_used = (pends[-1] // blk).astype(jnp.int32).reshape(1)
    return dest, nb * blk, block_expert, n_used


def _layer(x, w_in, ml_i_bias, ml_f_bias, ml_norm_w, conv_w, conv_b, dt_bias, a_log, d_skip, ssm_norm_w,
           w_out, ln1_g, ln1_b, w_router, router_bias, w_exp_gate, w_exp_up, w_exp_down,
           w_sh_gate, w_sh_up, w_sh_down, ln2_g, ln2_b):
    B, S, D = x.shape
    T = B * S
    y_ml, y_ssm = _mixers(x, w_in, ml_i_bias, ml_f_bias, ml_norm_w, conv_w, conv_b, dt_bias, a_log, d_skip,
                          ssm_norm_w)
    w_r = jnp.concatenate([w_router, jnp.zeros((D, 128 - N_EXPERTS), w_router.dtype)], axis=1).astype(BF16)
    h, hp_a, hp_b, scores = _outproj(y_ml, y_ssm, x.reshape(T, D), w_out.astype(BF16),
                                     ln1_g.astype(F32).reshape(1, D), ln1_b.astype(F32).reshape(1, D), w_r)
    rbias = jnp.broadcast_to(router_bias.astype(F32)[:, None], (N_EXPERTS, 128))
    idx, wgt, rank, cnt = _route(scores, rbias)
    counts = cnt[:, 0].astype(jnp.int32)
    dest, n_slots, block_expert, n_used = _moe_dispatch(idx, rank, counts, T, MOE_BLOCK)
    dest_flat = dest.reshape(-1)
    xs_a = _sc_scatter_rows(hp_a, dest, n_slots)
    xs_b = _sc_scatter_rows(hp_b, dest, n_slots)
    os_a, os_b = _experts(xs_a, xs_b, block_expert, n_used, w_exp_gate, w_exp_up, w_exp_down)
    gk_a = _sc_gather_rows(os_a, dest_flat).reshape(TOP_K, T, D // 4)
    gk_b = _sc_gather_rows(os_b, dest_flat).reshape(TOP_K, T, D // 4)
    out = _final(h, gk_a, gk_b, wgt.T, w_sh_gate.astype(BF16), w_sh_up.astype(BF16),
                 w_sh_down.astype(BF16), ln2_g.astype(F32).reshape(1, D), ln2_b.astype(F32).reshape(1, D))
    return out.reshape(B, S, D)


def kernel(x, w_in, ml_i_bias, ml_f_bias, ml_norm_w, conv_w, conv_b, dt_bias, a_log, d_skip, ssm_norm_w, w_out,
           ln1_g, ln1_b, w_router, router_bias, w_exp_gate, w_exp_up, w_exp_down, w_sh_gate, w_sh_up, w_sh_down,
           ln2_g, ln2_b):
    for l in range(w_in.shape[0]):
        x = _layer(x, w_in[l], ml_i_bias[l], ml_f_bias[l], ml_norm_w[l], conv_w[l], conv_b[l], dt_bias[l],
                   a_log[l], d_skip[l], ssm_norm_w[l], w_out[l], ln1_g[l], ln1_b[l], w_router[l],
                   router_bias[l], w_exp_gate[l], w_exp_up[l], w_exp_down[l], w_sh_gate[l], w_sh_up[l],
                   w_sh_down[l], ln2_g[l], ln2_b[l])
    return x
```

```python
import functools

import jax
import jax.numpy as jnp
from jax import lax
from jax.experimental import pallas as pl
from jax.experimental.pallas import tpu as pltpu
from jax.experimental.pallas import tpu_sc as plsc

D_MODEL = 1024
ML_HEADS = 4
ML_DV = 256
ML_DQK = 128
ML_QK = ML_HEADS * ML_DQK
ML_WIDTH = ML_HEADS * ML_DV
SSM_WIDTH = 1024
SSM_HEADDIM = 64
SSM_HEADS = 16
SSM_GROUPS = 4
SSM_STATE = 128
SSM_CONV_DIM = SSM_WIDTH + 2 * SSM_GROUPS * SSM_STATE
CONV_K = 5
CHUNK = 128
N_EXPERTS = 64
TOP_K = 8
N_EXPERT_GROUPS = 8
TOPK_GROUPS = 4
D_EXPERT = 256
ROUTED_SCALE = 2.5
LN_EPS = 1e-5
RMS_EPS = 1e-5
DEPTH = 1
ALPHA = (2 * DEPTH) ** 0.25

PA_XBC0 = 0
PA_V0 = PA_XBC0 + SSM_CONV_DIM
PA_O0 = PA_V0 + ML_WIDTH
PA_Z0 = PA_O0 + ML_WIDTH
PA_Q0 = PA_Z0 + SSM_WIDTH
PA_COLS = PA_Q0 + ML_QK
PB_COLS = 128
N_GATE_ROWS = 2 * ML_HEADS + 2 * ML_HEADS + 2 * SSM_HEADS
PREP_SSM0 = 32
N_PREP_ROWS = PREP_SSM0 + 10 * SSM_HEADS

MOE_BLOCK = 1024
SC_WINDOW = 128
VMEM_LIMIT = 56 * 1024 * 1024

BF16 = jnp.bfloat16
F32 = jnp.float32
NEG_INF = float("-inf")


def _cparams(sem, vmem=VMEM_LIMIT):
    return pltpu.CompilerParams(dimension_semantics=sem, vmem_limit_bytes=vmem)


def _dot(a, b):
    return jnp.dot(a, b, preferred_element_type=F32)


def _dot_nt(a, b):
    return lax.dot_general(a, b, (((1,), (1,)), ((), ())), preferred_element_type=F32)


def _dot_tn(a, b):
    return lax.dot_general(a, b, (((0,), (0,)), ((), ())), preferred_element_type=F32)


def _sigmoid(x):
    return 1.0 / (1.0 + jnp.exp(-x))


def _softplus(x):
    return jnp.maximum(x, 0.0) + jnp.log1p(jnp.exp(-jnp.abs(x)))


def _proj_kernel(x_ref, wb_ref, ws_ref, wkt_ref, oa_ref, ob_ref, kt_ref, xb_scr, *, L):
    @pl.when(pl.program_id(1) == 0)
    def _():
        xb = x_ref[...].astype(BF16)
        xb_scr[...] = xb
        ob_ref[...] = _dot(xb, ws_ref[...])
        kt = _dot_nt(wkt_ref[...], xb).astype(kt_ref.dtype)
        for c in range(kt_ref.shape[0]):
            kt_ref[c] = kt[:, c * L:(c + 1) * L]

    oa_ref[...] = _dot(xb_scr[...], wb_ref[...]).astype(oa_ref.dtype)


def _proj(x2d, w_big, w_small, w_kt, tm=1024, tn=1408):
    T, D = x2d.shape
    N = w_big.shape[1]
    tm = min(tm, T)
    L = CHUNK
    return pl.pallas_call(
        functools.partial(_proj_kernel, L=L),
        grid=(T // tm, N // tn),
        in_specs=[
            pl.BlockSpec((tm, D), lambda i, j: (i, 0)),
            pl.BlockSpec((D, tn), lambda i, j: (0, j)),
            pl.BlockSpec((D, PB_COLS), lambda i, j: (0, 0)),
            pl.BlockSpec((ML_QK, D), lambda i, j: (0, 0)),
        ],
        out_specs=[
            pl.BlockSpec((tm, tn), lambda i, j: (i, j)),
            pl.BlockSpec((tm, PB_COLS), lambda i, j: (i, 0)),
            pl.BlockSpec((tm // L, ML_QK, L), lambda i, j: (i, 0, 0)),
        ],
        out_shape=[jax.ShapeDtypeStruct((T, N), BF16), jax.ShapeDtypeStruct((T, PB_COLS), F32),
                   jax.ShapeDtypeStruct((T // L, ML_QK, L), BF16)],
        scratch_shapes=[pltpu.VMEM((tm, D), BF16)],
        compiler_params=_cparams(("parallel", "arbitrary")),
        name="in_proj",
    )(x2d, w_big, w_small, w_kt)


def _split3_dot(x, u):
    hi = x.astype(BF16)
    r1 = x - hi.astype(F32)
    mid = r1.astype(BF16)
    lo = (r1 - mid.astype(F32)).astype(BF16)
    return _dot(hi, u) + _dot(mid, u) + _dot(lo, u)


def _gate_kernel(g_ref, bias_ref, aneg_ref, o_ref, *, nc, L):
    g = g_ref[0] + bias_ref[...]
    nh = ML_HEADS
    ig = g[0:2 * nh]
    fpre = g[2 * nh:4 * nh]
    lf = -_softplus(-fpre)
    dt = _softplus(g[4 * nh:])
    dA = dt * aneg_ref[...]
    r = lax.broadcasted_iota(jnp.int32, (L, L), 0)
    c = lax.broadcasted_iota(jnp.int32, (L, L), 1)
    u_pre = (r <= c).astype(BF16)
    u_suf = (r >= c).astype(BF16)
    nf = nh + SSM_HEADS
    xf = jnp.concatenate([lf[0:nh], dA[0:SSM_HEADS]], axis=0).reshape(nf * nc, L)
    xb = jnp.concatenate([lf[nh:], dA[SSM_HEADS:]], axis=0).reshape(nf * nc, L)
    cf = _split3_dot(xf, u_pre).reshape(nf, nc, L)
    cb = _split3_dot(xb, u_suf).reshape(nf, nc, L)
    totf = jnp.broadcast_to(cf[:, :, L - 1:L], cf.shape)
    totb = jnp.broadcast_to(cb[:, :, 0:1], cb.shape)
    o_ref[0, 0:nh] = cf[0:nh]
    o_ref[0, nh:2 * nh] = cb[0:nh]
    o_ref[0, 2 * nh:4 * nh] = ig
    o_ref[0, 4 * nh:5 * nh] = totf[0:nh] - cf[0:nh] + ig[0:nh]
    o_ref[0, 5 * nh:6 * nh] = totb[0:nh] - cb[0:nh] + ig[nh:]
    lane = lax.broadcasted_iota(jnp.int32, (nh * nc, L), 1)
    pf = (ig[0:nh] - cf[0:nh]).reshape(nh * nc, L)
    pb = (ig[nh:] - cb[0:nh]).reshape(nh * nc, L)
    sh = 1
    while sh < L:
        pf = jnp.where(lane >= sh, jnp.maximum(pf, pltpu.roll(pf, sh, 1)), pf)
        pb = jnp.where(lane < L - sh, jnp.maximum(pb, pltpu.roll(pb, L - sh, 1)), pb)
        sh *= 2
    o_ref[0, 6 * nh:7 * nh] = cf[0:nh] + pf.reshape(nh, nc, L)
    o_ref[0, 7 * nh:8 * nh] = cb[0:nh] + pb.reshape(nh, nc, L)
    H = SSM_HEADS
    af, ab, dtf, dtb = cf[nh:], cb[nh:], dt[0:H], dt[H:]
    s0 = PREP_SSM0
    o_ref[0, s0:s0 + H] = af
    o_ref[0, s0 + H:s0 + 2 * H] = ab
    o_ref[0, s0 + 2 * H:s0 + 3 * H] = af - jnp.log(dtf)
    o_ref[0, s0 + 3 * H:s0 + 4 * H] = ab - jnp.log(dtb)
    o_ref[0, s0 + 4 * H:s0 + 5 * H] = jnp.exp(af)
    o_ref[0, s0 + 5 * H:s0 + 6 * H] = jnp.exp(ab)
    o_ref[0, s0 + 6 * H:s0 + 7 * H] = dtf * jnp.exp(totf[nh:] - af)
    o_ref[0, s0 + 7 * H:s0 + 8 * H] = dtb * jnp.exp(totb[nh:] - ab)
    o_ref[0, s0 + 8 * H:s0 + 9 * H] = jnp.exp(totf[nh:])
    o_ref[0, s0 + 9 * H:s0 + 10 * H] = jnp.exp(totb[nh:])


def _gate_prep(g_rows, bias_rows, aneg_rows):
    B, R, nc, L = g_rows.shape
    return pl.pallas_call(
        functools.partial(_gate_kernel, nc=nc, L=L),
        grid=(B,),
        in_specs=[
            pl.BlockSpec((1, R, nc, L), lambda b: (b, 0, 0, 0)),
            pl.BlockSpec((R, 1, L), lambda b: (0, 0, 0)),
            pl.BlockSpec((2 * SSM_HEADS, 1, L), lambda b: (0, 0, 0)),
        ],
        out_specs=pl.BlockSpec((1, N_PREP_ROWS, nc, L), lambda b: (b, 0, 0, 0)),
        out_shape=jax.ShapeDtypeStruct((B, N_PREP_ROWS, nc, L), F32),
        compiler_params=_cparams(("parallel",)),
        name="gate_prep",
    )(g_rows, bias_rows, aneg_rows)


def _mlstm_kernel(q_ref, kt_ref, v_ref, o_ref, mg_ref, mc_ref, nw_ref, y_ref, park_scr, c_scr, n_scr, m_scr,
                  *, nc, L, nh):
    row_i = lax.broadcasted_iota(jnp.int32, (L, L), 0)
    col_i = lax.broadcasted_iota(jnp.int32, (L, L), 1)
    ones = jnp.ones((L, L), BF16)

    c_scr[...] = jnp.zeros_like(c_scr)
    n_scr[...] = jnp.zeros_like(n_scr)
    m_scr[...] = jnp.full_like(m_scr, NEG_INF)

    def chunk(c, d, hh):
        c_ref, n_ref, m_ref = c_scr.at[2 * hh + d], n_scr.at[2 * hh + d], m_scr.at[2 * hh + d]
        c_prev, n_prev, m_prev = c_ref[...], n_ref[...], m_ref[...]
        sl = pl.ds(pl.multiple_of(c * L, L), L)
        q = q_ref[0, sl, hh * ML_DQK:(hh + 1) * ML_DQK]
        kt = kt_ref[0, c, hh]
        vo = jnp.concatenate([v_ref[0, sl, hh * ML_DV:(hh + 1) * ML_DV], ones], axis=1)
        g8 = mg_ref[0, hh, c]
        ct = mc_ref[0, hh, sl, :]
        b_r, i_r, a_r = g8[2 * d:2 * d + 1], g8[2 * d + 1:2 * d + 2], g8[4 + d:5 + d]
        b_c = jnp.broadcast_to(ct[:, 2 * d:2 * d + 1], (L, L))
        mi_c = jnp.broadcast_to(ct[:, 6 + d:7 + d], (L, L))
        if d == 0:
            gtot = b_r[:, L - 1:L]
            mask = col_i <= row_i
        else:
            gtot = b_r[:, 0:1]
            mask = col_i >= row_i
        m_loc = jnp.max(a_r, axis=1, keepdims=True)
        kwt = (kt.astype(F32) * jnp.exp(a_r - m_loc)).astype(BF16)
        uu = _dot(kwt, vo)
        qq = _dot(q, jnp.concatenate([kt, c_prev.astype(BF16), n_prev.astype(BF16)], axis=1))
        s, q_c, q_n = qq[:, 0:L], qq[:, L:L + ML_DV], qq[:, L + ML_DV:]
        dmat = jnp.where(mask, b_c - b_r + i_r, NEG_INF)
        m_inter = b_c + m_prev
        m_t = jnp.maximum(m_inter, mi_c)
        s_mat = s * jnp.exp(dmat - m_t)
        inter_w = jnp.exp(m_inter - m_t)
        ss = _dot(s_mat.astype(BF16), vo)
        den = ss[:, ML_DV:] + inter_w * q_n
        r = 1.0 / jnp.maximum(jnp.abs(den), jnp.exp(-m_t))
        num = ss[:, 0:ML_DV] + jnp.concatenate([inter_w, inter_w], axis=1) * q_c
        h = num * jnp.concatenate([r, r], axis=1)
        m_new = jnp.maximum(gtot + m_prev, m_loc)
        decay = jnp.exp(gtot + m_prev - m_new)
        sc = jnp.exp(m_loc - m_new)
        c_ref[...] = decay * c_prev + sc * uu[:, 0:ML_DV]
        n_ref[...] = decay * n_prev + sc * uu[:, ML_DV:]
        m_ref[...] = m_new
        return sl, h

    def finish(sl, hh, h):
        cols = slice(hh * ML_DV, (hh + 1) * ML_DV)
        mu = jnp.mean(h, axis=1, keepdims=True)
        d = h - mu
        var = jnp.mean(d * d, axis=1, keepdims=True)
        y = d * lax.rsqrt(var + LN_EPS) * nw_ref[:, cols]
        y = y * _sigmoid(o_ref[0, sl, cols].astype(F32))
        y_ref[0, sl, cols] = y.astype(y_ref.dtype)

    def first_half(i, carry):
        for hh in range(nh):
            for d, c in ((0, i), (1, nc - 1 - i)):
                sl, h = chunk(c, d, hh)
                park_scr[hh, sl, :] = h
        return carry

    def second_half(i, carry):
        for hh in range(nh):
            for d, c in ((0, i), (1, nc - 1 - i)):
                sl, h = chunk(c, d, hh)
                finish(sl, hh, h + park_scr[hh, sl, :])
        return carry

    half = nc // 2
    lax.fori_loop(0, half, first_half, 0, unroll=4 // nh)
    lax.fori_loop(half, nc, second_half, 0, unroll=4 // nh)


def _mlstm(pa3, kt5, mg, mc, norm_w, nh=1):
    B, S, _ = pa3.shape
    nc = S // CHUNK
    H = ML_HEADS
    q_blk, v_blk, o_blk = PA_Q0 // (nh * ML_DQK), PA_V0 // (nh * ML_DV), PA_O0 // (nh * ML_DV)
    return pl.pallas_call(
        functools.partial(_mlstm_kernel, nc=nc, L=CHUNK, nh=nh),
        grid=(B, H // nh),
        in_specs=[
            pl.BlockSpec((1, S, nh * ML_DQK), lambda b, h: (b, 0, q_blk + h)),
            pl.BlockSpec((1, nc, nh, ML_DQK, CHUNK), lambda b, h: (b, 0, h, 0, 0)),
            pl.BlockSpec((1, S, nh * ML_DV), lambda b, h: (b, 0, v_blk + h)),
            pl.BlockSpec((1, S, nh * ML_DV), lambda b, h: (b, 0, o_blk + h)),
            pl.BlockSpec((1, nh, nc, 8, CHUNK), lambda b, h: (b, h, 0, 0, 0)),
            pl.BlockSpec((1, nh, S, 8), lambda b, h: (b, h, 0, 0)),
            pl.BlockSpec((1, nh * ML_DV), lambda b, h: (0, h)),
        ],
        out_specs=pl.BlockSpec((1, S, nh * ML_DV), lambda b, h: (b, 0, h)),
        out_shape=jax.ShapeDtypeStruct((B, S, ML_WIDTH), BF16),
        scratch_shapes=[
            pltpu.VMEM((nh, S, ML_DV), F32),
            pltpu.VMEM((2 * nh, ML_DQK, ML_DV), F32),
            pltpu.VMEM((2 * nh, ML_DQK, CHUNK), F32),
            pltpu.VMEM((2 * nh, 1, 1), F32),
        ],
        compiler_params=_cparams(("parallel", "parallel")),
        name="mlstm",
    )(pa3, kt5, pa3, pa3, mg, mc, norm_w)


CONV_EDGE = 16


def _conv_shift_matrix(L):
    pad, E = CONV_K // 2, CONV_EDGE
    t = jnp.arange(L, dtype=jnp.int32)[:, None]
    blocks = []
    u = jnp.arange(L, dtype=jnp.int32)[None, :]
    for j in range(CONV_K):
        blocks.append(u == t + j - pad)
    v = jnp.arange(E, dtype=jnp.int32)[None, :]
    for j in range(pad):
        blocks.append(v - E == t + j - pad)
    for j in range(pad + 1, CONV_K):
        blocks.append(v + L == t + j - pad)
    return jnp.concatenate(blocks, axis=1).astype(BF16)


def _conv_kernel(x_ref, halo_ref, w_ref, b_ref, sh_ref, o_ref, ext_scr, *, Lc, L):
    pad, E = CONV_K // 2, CONV_EDGE
    ext_scr[0:E, :] = halo_ref[0, 0, 0:E, :]
    ext_scr[E:E + Lc, :] = x_ref[0]
    ext_scr[E + Lc:2 * E + Lc, :] = halo_ref[0, 0, E:2 * E, :]
    taps = [w_ref[j:j + 1, :].astype(BF16) for j in range(CONV_K)]
    scaled = lambda rows, j: rows * taps[j]
    for r0 in range(0, Lc, L):
        cen = ext_scr[E + r0:E + r0 + L, :]
        top = ext_scr[r0:r0 + E, :]
        bot = ext_scr[E + r0 + L:2 * E + r0 + L, :]
        pieces = [scaled(cen, j) for j in range(CONV_K)]
        pieces += [scaled(top, j) for j in range(pad)] + [scaled(bot, j) for j in range(pad + 1, CONV_K)]
        acc = _dot(sh_ref[...], jnp.concatenate(pieces, axis=0)) + b_ref[...]
        o_ref[0, r0:r0 + L, :] = (acc * _sigmoid(acc)).astype(o_ref.dtype)


def _conv(pa3, halo, conv_w8, conv_b, Lc=512):
    B, S, _ = pa3.shape
    C = SSM_CONV_DIM
    xbc_blk = PA_XBC0 // C
    L = CHUNK
    sh = _conv_shift_matrix(L)
    return pl.pallas_call(
        functools.partial(_conv_kernel, Lc=Lc, L=L),
        grid=(B, S // Lc),
        in_specs=[
            pl.BlockSpec((1, Lc, C), lambda b, i: (b, i, xbc_blk)),
            pl.BlockSpec((1, 1, 2 * CONV_EDGE, C), lambda b, i: (b, i, 0, 0)),
            pl.BlockSpec((8, C), lambda b, i: (0, 0)),
            pl.BlockSpec((1, C), lambda b, i: (0, 0)),
            pl.BlockSpec(sh.shape, lambda b, i: (0, 0)),
        ],
        out_specs=pl.BlockSpec((1, Lc, C), lambda b, i: (b, i, 0)),
        out_shape=jax.ShapeDtypeStruct((B, S, C), BF16),
        scratch_shapes=[pltpu.VMEM((Lc + 2 * CONV_EDGE, C), BF16)],
        compiler_params=_cparams(("parallel", "parallel")),
        name="ssm_conv",
    )(pa3, halo, conv_w8, conv_b, sh)


def _expansion_matrix(src_cols, width):
    src = jnp.repeat(jnp.asarray(src_cols, jnp.int32), width)
    e = (jnp.arange(128, dtype=jnp.int32)[:, None] == src[None, :]).astype(BF16)
    return jnp.concatenate([e, e], axis=0)


def _split2(t):
    hi = t.astype(BF16)
    return jnp.concatenate([hi, (t - hi.astype(F32)).astype(BF16)], axis=1)


def _ssd_bwd_kernel(x_ref, b_ref, tb_ref, dec_ref, ew_ref, hb_ref, h_scr, *, L):
    G, hg, P, N = SSM_GROUPS, SSM_HEADS // SSM_GROUPS, SSM_HEADDIM, SSM_STATE
    GW = hg * P

    @pl.when(pl.program_id(1) == 0)
    def _():
        h_scr[...] = jnp.zeros_like(h_scr)

    for s in reversed(range(x_ref.shape[1] // L)):
        rows = slice(s * L, (s + 1) * L)
        w_all = _dot(_split2(tb_ref[0, s].T), ew_ref[...])
        for g in range(G):
            w = w_all[:, g * GW:(g + 1) * GW]
            xs = (x_ref[0, rows, g * GW:(g + 1) * GW].astype(F32) * w).astype(BF16)
            st = _dot_tn(b_ref[0, rows, g * N:(g + 1) * N], xs)
            h_old = h_scr[g]
            hb_ref[0, s, g] = h_old.astype(hb_ref.dtype)
            h_scr[g] = dec_ref[0, s, 1:2, g * GW:(g + 1) * GW] * h_old + st


def _ssd_bwd(xc, tb, decs, cpb):
    B, S, _ = xc.shape
    nc = S // CHUNK
    nblk = nc // cpb
    R = cpb * CHUNK
    G, N, W = SSM_GROUPS, SSM_STATE, SSM_WIDTH
    return pl.pallas_call(
        functools.partial(_ssd_bwd_kernel, L=CHUNK),
        grid=(B, nblk),
        in_specs=[
            pl.BlockSpec((1, R, W), lambda b, c: (b, nblk - 1 - c, 0)),
            pl.BlockSpec((1, R, G * N), lambda b, c: (b, nblk - 1 - c, W // (G * N))),
            pl.BlockSpec((1, cpb, 128, CHUNK), lambda b, c: (b, nblk - 1 - c, 0, 0)),
            pl.BlockSpec((1, cpb, 2, W), lambda b, c: (b, nblk - 1 - c, 0, 0)),
            pl.BlockSpec((256, W), lambda b, c: (0, 0)),
        ],
        out_specs=pl.BlockSpec((1, cpb, G, N, W // G), lambda b, c: (b, nblk - 1 - c, 0, 0, 0)),
        out_shape=jax.ShapeDtypeStruct((B, nc, G, N, W // G), BF16),
        scratch_shapes=[pltpu.VMEM((G, N, W // G), F32)],
        compiler_params=_cparams(("parallel", "arbitrary")),
        name="ssd_bwd_states",
    )(xc, xc, tb, decs, _expansion_matrix([5 * SSM_HEADS + h for h in range(SSM_HEADS)], SSM_HEADDIM))


def _ssd_main_kernel(x_ref, b_ref, c_ref, z_ref, tb_ref, dec_ref, hb_ref, dsk_ref, nw_ref, ed_ref, ee_ref,
                     y_ref, h_scr, *, L):
    G, hg, P, N = SSM_GROUPS, SSM_HEADS // SSM_GROUPS, SSM_HEADDIM, SSM_STATE
    H = SSM_HEADS
    GW = hg * P

    @pl.when(pl.program_id(1) == 0)
    def _():
        h_scr[...] = jnp.zeros_like(h_scr)

    row_i = lax.broadcasted_iota(jnp.int32, (L, L), 0)
    col_i = lax.broadcasted_iota(jnp.int32, (L, L), 1)
    lower = col_i <= row_i
    upper = col_i >= row_i
    lane_g = lax.broadcasted_iota(jnp.int32, (L, GW), 1)

    for s, g in [(s, g) for s in range(x_ref.shape[1] // L) for g in range(G)]:
        rows = slice(s * L, (s + 1) * L)
        tb = tb_ref[0, s]
        sc2 = _split2(tb.T)
        xg = x_ref[0, rows, g * GW:(g + 1) * GW]
        bg = b_ref[0, rows, g * N:(g + 1) * N]
        cg = c_ref[0, rows, g * N:(g + 1) * N]
        cb = _dot_nt(cg, bg)
        acol = _dot(sc2, ed_ref[g])
        ecol = _dot(sc2, ee_ref[g])
        ms, xms = [], []
        for j in range(hg):
            hd = g * hg + j
            rf_r, rb_r = tb[6 * H + hd:6 * H + hd + 1], tb[7 * H + hd:7 * H + hd + 1]
            af_c, ab_c = acol[:, j * L:(j + 1) * L], acol[:, (hg + j) * L:(hg + j + 1) * L]
            fm = jnp.exp(jnp.where(lower, af_c - rf_r, NEG_INF))
            bm = jnp.exp(jnp.where(upper, ab_c - rb_r, NEG_INF))
            ms.append((cb * (fm + bm)).astype(BF16))
            xms.append(jnp.where((lane_g >= j * P) & (lane_g < (j + 1) * P), xg, jnp.zeros_like(xg)))
        efc, ebc, wfc = ecol[:, 0:GW], ecol[:, GW:2 * GW], ecol[:, 2 * GW:]
        y = _dot(jnp.concatenate(ms, axis=1), jnp.concatenate(xms, axis=0))
        h_old = h_scr[g]
        y = y + efc * _dot(cg, h_old.astype(BF16))
        y = y + ebc * _dot(cg, hb_ref[0, s, g])
        xgf = xg.astype(F32)
        y = y + dsk_ref[:, g * GW:(g + 1) * GW] * xgf
        xs = (xgf * wfc).astype(BF16)
        h_scr[g] = dec_ref[0, s, 0:1, g * GW:(g + 1) * GW] * h_old + _dot_tn(bg, xs)
        zg = z_ref[0, rows, g * GW:(g + 1) * GW].astype(F32)
        y = y * (zg * _sigmoid(zg))
        y = y * lax.rsqrt(jnp.mean(y * y, axis=1, keepdims=True) + RMS_EPS)
        y_ref[0, rows, g * GW:(g + 1) * GW] = (y * nw_ref[:, g * GW:(g + 1) * GW]).astype(y_ref.dtype)


def _ssd_main(xc, pa3, tb, decs, hb, dskip_row, norm_w, cpb):
    B, S, _ = xc.shape
    nc = S // CHUNK
    R = cpb * CHUNK
    G, N, W = SSM_GROUPS, SSM_STATE, SSM_WIDTH
    H, hg, P = SSM_HEADS, SSM_HEADS // SSM_GROUPS, SSM_HEADDIM
    z_blk = PA_Z0 // W
    e_dec = jnp.stack([_expansion_matrix([q * H + g * hg + j for q in (0, 1) for j in range(hg)], CHUNK)
                       for g in range(G)])
    e_ew = jnp.stack([_expansion_matrix([q * H + g * hg + j for q in (2, 3, 4) for j in range(hg)], P)
                      for g in range(G)])
    return pl.pallas_call(
        functools.partial(_ssd_main_kernel, L=CHUNK),
        grid=(B, nc // cpb),
        in_specs=[
            pl.BlockSpec((1, R, W), lambda b, c: (b, c, 0)),
            pl.BlockSpec((1, R, G * N), lambda b, c: (b, c, W // (G * N))),
            pl.BlockSpec((1, R, G * N), lambda b, c: (b, c, W // (G * N) + 1)),
            pl.BlockSpec((1, R, W), lambda b, c: (b, c, z_blk)),
            pl.BlockSpec((1, cpb, 128, CHUNK), lambda b, c: (b, c, 0, 0)),
            pl.BlockSpec((1, cpb, 2, W), lambda b, c: (b, c, 0, 0)),
            pl.BlockSpec((1, cpb, G, N, W // G), lambda b, c: (b, c, 0, 0, 0)),
            pl.BlockSpec((1, W), lambda b, c: (0, 0)),
            pl.BlockSpec((1, W), lambda b, c: (0, 0)),
            pl.BlockSpec((G, 256, 2 * hg * CHUNK), lambda b, c: (0, 0, 0)),
            pl.BlockSpec((G, 256, 3 * W // G), lambda b, c: (0, 0, 0)),
        ],
        out_specs=pl.BlockSpec((1, R, W), lambda b, c: (b, c, 0)),
        out_shape=jax.ShapeDtypeStruct((B, S, W), BF16),
        scratch_shapes=[pltpu.VMEM((G, N, W // G), F32)],
        compiler_params=_cparams(("parallel", "arbitrary")),
        name="ssd_main",
    )(xc, xc, xc, pa3, tb, decs, hb, dskip_row, norm_w, e_dec, e_ew)


def _pack_bf16_pairs(v):
    n = v.shape[1] // 2
    hi = lax.bitcast_convert_type(v[:, :n].astype(BF16).astype(F32), jnp.uint32)
    lo = lax.bitcast_convert_type(v[:, n:].astype(BF16).astype(F32), jnp.uint32)
    return (hi & jnp.uint32(0xFFFF0000)) | (lo >> 16)


def _unpack_bf16_pairs(u):
    hi = lax.bitcast_convert_type(u & jnp.uint32(0xFFFF0000), F32)
    lo = lax.bitcast_convert_type(u << 16, F32)
    return hi, lo


def _layer_norm(v, g, b):
    mu = jnp.mean(v, axis=1, keepdims=True)
    d = v - mu
    var = jnp.mean(d * d, axis=1, keepdims=True)
    return d * lax.rsqrt(var + LN_EPS) * g + b


def _outproj_kernel(yml_ref, yssm_ref, x_ref, wo_ref, g_ref, b_ref, wr_ref, h_ref, hpa_ref, hpb_ref, sc_ref):
    mix = _dot(yml_ref[...], wo_ref[0:ML_WIDTH, :]) + _dot(yssm_ref[...], wo_ref[ML_WIDTH:, :])
    h = _layer_norm(ALPHA * x_ref[...] + mix, g_ref[...], b_ref[...])
    h_ref[...] = h
    hp = _pack_bf16_pairs(h)
    q = hp.shape[1] // 2
    hpa_ref[...] = hp[:, :q]
    hpb_ref[...] = hp[:, q:]
    sc_ref[...] = _sigmoid(_dot(h.astype(BF16), wr_ref[...]))


def _outproj(yml, yssm, x2d, w_out, ln_g, ln_b, w_router, tm=1024):
    T, D = x2d.shape
    tm = min(tm, T)
    row = lambda i: (i, 0)
    fixed = lambda i: (0, 0)
    return pl.pallas_call(
        _outproj_kernel,
        grid=(T // tm,),
        in_specs=[
            pl.BlockSpec((tm, ML_WIDTH), row),
            pl.BlockSpec((tm, SSM_WIDTH), row),
            pl.BlockSpec((tm, D), row),
            pl.BlockSpec((ML_WIDTH + SSM_WIDTH, D), fixed),
            pl.BlockSpec((1, D), fixed),
            pl.BlockSpec((1, D), fixed),
            pl.BlockSpec((D, 128), fixed),
        ],
        out_specs=[pl.BlockSpec((tm, D), row), pl.BlockSpec((tm, D // 4), row), pl.BlockSpec((tm, D // 4), row),
                   pl.BlockSpec((tm, 128), row)],
        out_shape=[
            jax.ShapeDtypeStruct((T, D), F32),
            jax.ShapeDtypeStruct((T, D // 4), jnp.uint32),
            jax.ShapeDtypeStruct((T, D // 4), jnp.uint32),
            jax.ShapeDtypeStruct((T, 128), F32),
        ],
        compiler_params=_cparams(("parallel",)),
        name="out_proj_ln1_router",
    )(yml, yssm, x2d, w_out, ln_g, ln_b, w_router)


def _route_kernel(sc_ref, rb_ref, idx_ref, w_ref, rank_ref, cnt_ref, carry_scr, *, Tt):
    E, NG = N_EXPERTS, N_EXPERT_GROUPS
    EG = E // NG

    @pl.when(pl.program_id(0) == 0)
    def _():
        carry_scr[...] = jnp.zeros_like(carry_scr)

    s = sc_ref[...].T[0:E]
    s3 = s.reshape(NG, EG, Tt)
    ch3 = (s + rb_ref[:, 0:1]).reshape(NG, EG, Tt)
    e_in = lax.broadcasted_iota(jnp.int32, (NG, EG, Tt), 1)
    g_in = lax.broadcasted_iota(jnp.int32, (NG, EG, Tt), 0)
    f_i = g_in * EG + e_in
    m1 = jnp.max(ch3, axis=1, keepdims=True)
    i1 = jnp.min(jnp.where(ch3 == m1, e_in, EG), axis=1, keepdims=True)
    m2 = jnp.max(jnp.where(e_in == i1, NEG_INF, ch3), axis=1, keepdims=True)
    gs = m1 + m2
    g_i = lax.broadcasted_iota(jnp.int32, (NG, 1, Tt), 0)
    gsel = jnp.zeros((NG, 1, Tt), jnp.bool_)
    cur = gs
    for _ in range(TOPK_GROUPS):
        mx = jnp.max(cur, axis=0, keepdims=True)
        pick = jnp.min(jnp.where(cur == mx, g_i, NG), axis=0, keepdims=True)
        hit = g_i == pick
        gsel = gsel | hit
        cur = jnp.where(hit, NEG_INF, cur)
    cur = jnp.where(gsel, ch3, NEG_INF)
    selmask = jnp.zeros((NG, EG, Tt), jnp.bool_)
    idxs, ws = [], []
    for _ in range(TOP_K):
        mx = jnp.max(jnp.max(cur, axis=1, keepdims=True), axis=0, keepdims=True)
        cand = jnp.where(cur == mx, f_i, E)
        pick = jnp.min(jnp.min(cand, axis=1, keepdims=True), axis=0, keepdims=True)
        hit = f_i == pick
        wk = jnp.sum(jnp.sum(jnp.where(hit, s3, 0.0), axis=1, keepdims=True), axis=0, keepdims=True)
        idxs.append(pick.reshape(1, Tt))
        ws.append(wk.reshape(1, Tt))
        selmask = selmask | hit
        cur = jnp.where(hit, NEG_INF, cur)
    wsum = ws[0]
    for k in range(1, TOP_K):
        wsum = wsum + ws[k]
    r = lax.broadcasted_iota(jnp.int32, (Tt, Tt), 0)
    c = lax.broadcasted_iota(jnp.int32, (Tt, Tt), 1)
    ustrict = (r < c).astype(BF16)
    self32 = jnp.where(selmask, 1.0, 0.0).reshape(E, Tt)
    excl = _dot(self32.astype(BF16), ustrict) + carry_scr[:, 0:1]
    excl3 = excl.reshape(NG, EG, Tt)
    for k in range(TOP_K):
        hit = f_i == idxs[k].reshape(1, 1, Tt)
        rk = jnp.sum(jnp.sum(jnp.where(hit, excl3, 0.0), axis=1, keepdims=True), axis=0, keepdims=True)
        idx_ref[k:k + 1, :] = idxs[k]
        w_ref[k:k + 1, :] = ws[k] / wsum * ROUTED_SCALE
        rank_ref[k:k + 1, :] = rk.reshape(1, Tt).astype(jnp.int32)
    carry_scr[...] = carry_scr[...] + jnp.sum(self32, axis=1, keepdims=True)
    cnt_ref[...] = carry_scr[...]


def _route(scores, rbias, Tt=512):
    T = scores.shape[0]
    kt = lambda i: (0, i)
    return pl.pallas_call(
        functools.partial(_route_kernel, Tt=Tt),
        grid=(T // Tt,),
        in_specs=[pl.BlockSpec((Tt, 128), lambda i: (i, 0)), pl.BlockSpec((N_EXPERTS, 128), lambda i: (0, 0))],
        out_specs=[
            pl.BlockSpec((TOP_K, Tt), kt),
            pl.BlockSpec((TOP_K, Tt), kt),
            pl.BlockSpec((TOP_K, Tt), kt),
            pl.BlockSpec((N_EXPERTS, 128), lambda i: (0, 0)),
        ],
        out_shape=[
            jax.ShapeDtypeStruct((TOP_K, T), jnp.int32),
            jax.ShapeDtypeStruct((TOP_K, T), F32),
            jax.ShapeDtypeStruct((TOP_K, T), jnp.int32),
            jax.ShapeDtypeStruct((N_EXPERTS, 128), F32),
        ],
        scratch_shapes=[pltpu.VMEM((N_EXPERTS, 128), F32)],
        compiler_params=_cparams(("arbitrary",)),
        name="route_topk",
    )(scores, rbias)


def _dest_kernel(ps_ref, idx_ref, rank_ref, o_ref):
    idx = idx_ref[...]
    start = jnp.zeros(idx.shape, jnp.int32)
    for e in range(N_EXPERTS):
        start = jnp.where(idx == e, ps_ref[e], start)
    o_ref[...] = start + rank_ref[...]


def _dest(pstarts, idx, rank, Tt=2048):
    K, T = idx.shape
    Tt = min(Tt, T)
    blk = pl.BlockSpec((K, Tt), lambda i, ps: (0, i))
    return pl.pallas_call(
        _dest_kernel,
        grid_spec=pltpu.PrefetchScalarGridSpec(num_scalar_prefetch=1, grid=(T // Tt,), in_specs=[blk, blk],
                                               out_specs=blk),
        out_shape=jax.ShapeDtypeStruct((K, T), jnp.int32),
        compiler_params=_cparams(("parallel",)),
        name="dispatch_slots",
    )(pstarts, idx, rank)


def _sc_mesh():
    return plsc.VectorSubcoreMesh(core_axis_name="c", subcore_axis_name="s")


def _sc_scatter_rows(src, idx, n_out, win=SC_WINDOW):
    K, T = idx.shape
    W = src.shape[1]

    @functools.partial(pl.kernel, out_type=jax.ShapeDtypeStruct((n_out, W), src.dtype), mesh=_sc_mesh(),
                       scratch_types=[], name="sc_scatter_rows")
    def k(x_hbm, i_hbm, o_hbm):
        def body(x_vmem, i_vmem):
            for j in range(K):
                pltpu.sync_copy(x_vmem, o_hbm.at[i_vmem.at[j]])

        pltpu.emit_pipeline(
            body,
            grid=(T // win,),
            in_specs=[pl.BlockSpec((win, W), lambda i: (i, 0)), pl.BlockSpec((K, win), lambda i: (0, i))],
            out_specs=[],
            core_axis_name=("c", "s"),
            dimension_semantics=(pltpu.PARALLEL,),
        )(x_hbm, i_hbm)

    return k(src, idx)


def _sc_gather_rows(table, idx, win=SC_WINDOW):
    n = idx.shape[0]
    W = table.shape[1]

    @functools.partial(pl.kernel, out_type=jax.ShapeDtypeStruct((n, W), table.dtype), mesh=_sc_mesh(),
                       scratch_types=[], name="sc_gather_rows")
    def k(t_hbm, i_hbm, o_hbm):
        def body(i_vmem, o_vmem):
            pltpu.sync_copy(t_hbm.at[i_vmem.at[0]], o_vmem)

        pltpu.emit_pipeline(
            body,
            grid=(n // win,),
            in_specs=[pl.BlockSpec((1, win), lambda i: (0, i))],
            out_specs=[pl.BlockSpec((win, W), lambda i: (i, 0))],
            core_axis_name=("c", "s"),
            dimension_semantics=(pltpu.PARALLEL,),
        )(i_hbm, o_hbm)

    return k(table, idx.reshape(1, n))


def _expert_kernel(be_ref, nu_ref, xa_ref, xb_ref, wg_ref, wu_ref, wd_ref, oa_ref, ob_ref, wg_s, wu_s, wd_s):
    half = D_MODEL // 2
    i = pl.program_id(0)
    active = i < nu_ref[0]

    @pl.when(active & ((i == 0) | (be_ref[i] != be_ref[jnp.maximum(i - 1, 0)])))
    def _():
        wg_s[...] = wg_ref[0].astype(BF16)
        wu_s[...] = wu_ref[0].astype(BF16)
        wd_s[...] = wd_ref[0].astype(BF16)

    @pl.when(active)
    def _():
        hi, lo = _unpack_bf16_pairs(jnp.concatenate([xa_ref[...], xb_ref[...]], axis=1))
        x = jnp.concatenate([hi.astype(BF16), lo.astype(BF16)], axis=1)
        gt = _dot(x, wg_s[...])
        up = _dot(x, wu_s[...])
        hmid = (gt * _sigmoid(gt) * up).astype(BF16)
        packed = _pack_bf16_pairs(_dot(hmid, wd_s[...]))
        q = packed.shape[1] // 2
        oa_ref[...] = packed[:, :q]
        ob_ref[...] = packed[:, q:]

    @pl.when(jnp.logical_not(active))
    def _():
        oa_ref[...] = jnp.zeros_like(oa_ref)
        ob_ref[...] = jnp.zeros_like(ob_ref)


def _experts(xs_a, xs_b, block_expert, n_used, w_gate, w_up, w_down, blk=MOE_BLOCK):
    P, quarter = xs_a.shape
    nb = P // blk
    D, F = D_MODEL, D_EXPERT
    cur = lambda i, be, nu: jnp.minimum(i, nu[0] - 1)
    return pl.pallas_call(
        _expert_kernel,
        grid_spec=pltpu.PrefetchScalarGridSpec(
            num_scalar_prefetch=2,
            grid=(nb,),
            in_specs=[
                pl.BlockSpec((blk, quarter), lambda i, be, nu: (cur(i, be, nu), 0)),
                pl.BlockSpec((blk, quarter), lambda i, be, nu: (cur(i, be, nu), 0)),
                pl.BlockSpec((1, D, F), lambda i, be, nu: (be[cur(i, be, nu)], 0, 0)),
                pl.BlockSpec((1, D, F), lambda i, be, nu: (be[cur(i, be, nu)], 0, 0)),
                pl.BlockSpec((1, F, D), lambda i, be, nu: (be[cur(i, be, nu)], 0, 0)),
            ],
            out_specs=[pl.BlockSpec((blk, quarter), lambda i, be, nu: (i, 0)),
                       pl.BlockSpec((blk, quarter), lambda i, be, nu: (i, 0))],
            scratch_shapes=[pltpu.VMEM((D, F), BF16), pltpu.VMEM((D, F), BF16), pltpu.VMEM((F, D), BF16)],
        ),
        out_shape=[jax.ShapeDtypeStruct((P, quarter), jnp.uint32), jax.ShapeDtypeStruct((P, quarter), jnp.uint32)],
        compiler_params=_cparams(("arbitrary",)),
        name="expert_mlp",
    )(block_expert, n_used, xs_a, xs_b, w_gate, w_up, w_down)


def _final_kernel(h_ref, ga_ref, gb_ref, wt_ref, wsg_ref, wsu_ref, wsd_ref, g_ref, b_ref, o_ref):
    h = h_ref[...]
    hb = h.astype(BF16)
    gt = _dot(hb, wsg_ref[...])
    up = _dot(hb, wsu_ref[...])
    shared = _dot((gt * _sigmoid(gt) * up).astype(BF16), wsd_ref[...])
    acc_hi = None
    for k in range(TOP_K):
        hi, lo = _unpack_bf16_pairs(jnp.concatenate([ga_ref[k], gb_ref[k]], axis=1))
        wk = wt_ref[:, k:k + 1]
        acc_hi = wk * hi if acc_hi is None else acc_hi + wk * hi
        acc_lo = wk * lo if k == 0 else acc_lo + wk * lo
    routed = jnp.concatenate([acc_hi, acc_lo], axis=1)
    o_ref[...] = _layer_norm(ALPHA * h + (routed + shared), g_ref[...], b_ref[...])


def _final(h, gk_a, gk_b, wt, ws_gate, ws_up, ws_down, ln_g, ln_b, tm=512):
    T, D = h.shape
    tm = min(tm, T)
    F = D_EXPERT
    fixed = lambda i: (0, 0)
    return pl.pallas_call(
        _final_kernel,
        grid=(T // tm,),
        in_specs=[
            pl.BlockSpec((tm, D), lambda i: (i, 0)),
            pl.BlockSpec((TOP_K, tm, D // 4), lambda i: (0, i, 0)),
            pl.BlockSpec((TOP_K, tm, D // 4), lambda i: (0, i, 0)),
            pl.BlockSpec((tm, TOP_K), lambda i: (i, 0)),
            pl.BlockSpec((D, F), fixed),
            pl.BlockSpec((D, F), fixed),
            pl.BlockSpec((F, D), fixed),
            pl.BlockSpec((1, D), fixed),
            pl.BlockSpec((1, D), fixed),
        ],
        out_specs=pl.BlockSpec((tm, D), lambda i: (i, 0)),
        out_shape=jax.ShapeDtypeStruct((T, D), F32),
        compiler_params=_cparams(("parallel",)),
        name="shared_combine_ln2",
    )(h, gk_a, gk_b, wt, ws_gate, ws_up, ws_down, ln_g, ln_b)


def _mixers(x, w_in, ml_i_bias, ml_f_bias, ml_norm_w, conv_w, conv_b, dt_bias, a_log, d_skip, ssm_norm_w):
    B, S, D = x.shape
    T = B * S
    nc = S // CHUNK
    H = ML_HEADS
    k0, v0 = ML_QK, 2 * ML_QK
    g0 = v0 + 2 * ML_WIDTH
    z0 = g0 + 4 * H
    x0 = z0 + SSM_WIDTH
    d0 = x0 + SSM_CONV_DIM
    w_big = jnp.concatenate([w_in[:, x0:d0], w_in[:, v0:g0], w_in[:, z0:x0], w_in[:, :k0]], axis=1).astype(BF16)
    w_small = jnp.concatenate(
        [w_in[:, g0:z0], w_in[:, d0:], jnp.zeros((D, PB_COLS - N_GATE_ROWS), w_in.dtype)], axis=1).astype(BF16)
    w_kt = (w_in[:, k0:v0] * (ML_DQK ** -0.5)).T.astype(BF16)
    pa, pb, kt = _proj(x.reshape(T, D), w_big, w_small, w_kt)
    pa3 = pa.reshape(B, S, PA_COLS)
    kt5 = kt.reshape(B, nc, H, ML_DQK, CHUNK)

    g_rows = pb.reshape(B, S, PB_COLS)[:, :, :N_GATE_ROWS].transpose(0, 2, 1).reshape(B, N_GATE_ROWS, nc, CHUNK)
    bias = jnp.concatenate([ml_i_bias, ml_f_bias, dt_bias]).astype(F32)
    bias_rows = jnp.broadcast_to(bias[:, None, None], (N_GATE_ROWS, 1, CHUNK))
    aneg_rows = jnp.broadcast_to((-jnp.exp(a_log.astype(F32)))[:, None, None], (2 * SSM_HEADS, 1, CHUNK))
    prep = _gate_prep(g_rows, bias_rows, aneg_rows)
    m8 = jnp.stack([prep[:, 0:H], prep[:, 2 * H:3 * H], prep[:, H:2 * H], prep[:, 3 * H:4 * H],
                    prep[:, 4 * H:5 * H], prep[:, 5 * H:6 * H], prep[:, 6 * H:7 * H], prep[:, 7 * H:8 * H]],
                   axis=2)
    mg = m8.transpose(0, 1, 3, 2, 4)
    mc = m8.transpose(0, 1, 3, 4, 2).reshape(B, H, S, 8)
    s0, SH = PREP_SSM0, SSM_HEADS
    tb = jnp.concatenate([prep[:, s0:s0 + 2 * SH], prep[:, s0 + 4 * SH:s0 + 8 * SH],
                          prep[:, s0 + 2 * SH:s0 + 4 * SH]], axis=1).transpose(0, 2, 1, 3)
    decs = prep[:, s0 + 8 * SH:s0 + 10 * SH, :, 0].reshape(B, 2, SH, nc).transpose(0, 3, 1, 2)
    decs = jnp.repeat(decs, SSM_HEADDIM, axis=3)

    y_ml = _mlstm(pa3, kt5, mg, mc, ml_norm_w.astype(F32).reshape(1, ML_WIDTH))

    Lc = next(t for t in (1024, 512, CHUNK) if S % t == 0)
    nt = S // Lc
    pa4 = pa3.reshape(B, nt, Lc, PA_COLS)
    top = pa4[:, :, Lc - CONV_EDGE:, PA_XBC0:PA_XBC0 + SSM_CONV_DIM]
    bot = pa4[:, :, :CONV_EDGE, PA_XBC0:PA_XBC0 + SSM_CONV_DIM]
    zrow = jnp.zeros((B, 1, CONV_EDGE, SSM_CONV_DIM), pa.dtype)
    top = jnp.concatenate([zrow, top[:, :-1]], axis=1)
    bot = jnp.concatenate([bot[:, 1:], zrow], axis=1)
    halo = jnp.concatenate([top, bot], axis=2)
    conv_w8 = jnp.concatenate([conv_w.astype(F32), jnp.zeros((8 - CONV_K, SSM_CONV_DIM), F32)], axis=0)
    xc = _conv(pa3, halo, conv_w8, conv_b.astype(F32).reshape(1, SSM_CONV_DIM), Lc=Lc)

    cpb = 2 if nc % 2 == 0 else 1
    hb = _ssd_bwd(xc, tb, decs, cpb)
    dsk = jnp.repeat(d_skip.astype(F32), SSM_HEADDIM).reshape(1, SSM_WIDTH)
    y_ssm = _ssd_main(xc, pa3, tb, decs, hb, dsk, ssm_norm_w.astype(F32).reshape(1, SSM_WIDTH), cpb)
    return y_ml.reshape(T, ML_WIDTH), y_ssm.reshape(T, SSM_WIDTH)


def _moe_dispatch(idx, rank, counts, T, blk):
    E = N_EXPERTS
    M = T * TOP_K
    nb = (M + E * (blk - 1)) // blk
    pcounts = (counts + blk - 1) // blk * blk
    pends = jnp.cumsum(pcounts)
    pstarts = (pends - pcounts).astype(jnp.int32)
    dest = _dest(pstarts, idx, rank)
    block_start = jnp.arange(nb, dtype=jnp.int32) * blk
    block_expert = jnp.minimum(jnp.sum(pends[None, :] <= block_start[:, None], axis=1), E - 1).astype(jnp.int32)
    n_used = (pends[-1] // blk).astype(jnp.int32).reshape(1)
    return dest, nb * blk, block_expert, n_used


def _layer(x, w_in, ml_i_bias, ml_f_bias, ml_norm_w, conv_w, conv_b, dt_bias, a_log, d_skip, ssm_norm_w,
           w_out, ln1_g, ln1_b, w_router, router_bias, w_exp_gate, w_exp_up, w_exp_down,
           w_sh_gate, w_sh_up, w_sh_down, ln2_g, ln2_b):
    B, S, D = x.shape
    T = B * S
    y_ml, y_ssm = _mixers(x, w_in, ml_i_bias, ml_f_bias, ml_norm_w, conv_w, conv_b, dt_bias, a_log, d_skip,
                          ssm_norm_w)
    w_r = jnp.concatenate([w_router, jnp.zeros((D, 128 - N_EXPERTS), w_router.dtype)], axis=1).astype(BF16)
    h, hp_a, hp_b, scores = _outproj(y_ml, y_ssm, x.reshape(T, D), w_out.astype(BF16),
                                     ln1_g.astype(F32).reshape(1, D), ln1_b.astype(F32).reshape(1, D), w_r)
    rbias = jnp.broadcast_to(router_bias.astype(F32)[:, None], (N_EXPERTS, 128))
    idx, wgt, rank, cnt = _route(scores, rbias)
    counts = cnt[:, 0].astype(jnp.int32)
    dest, n_slots, block_expert, n_used = _moe_dispatch(idx, rank, counts, T, MOE_BLOCK)
    dest_flat = dest.reshape(-1)
    xs_a = _sc_scatter_rows(hp_a, dest, n_slots)
    xs_b = _sc_scatter_rows(hp_b, dest, n_slots)
    os_a, os_b = _experts(xs_a, xs_b, block_expert, n_used, w_exp_gate, w_exp_up, w_exp_down)
    gk_a = _sc_gather_rows(os_a, dest_flat).reshape(TOP_K, T, D // 4)
    gk_b = _sc_gather_rows(os_b, dest_flat).reshape(TOP_K, T, D // 4)
    out = _final(h, gk_a, gk_b, wgt.T, w_sh_gate.astype(BF16), w_sh_up.astype(BF16),
                 w_sh_down.astype(BF16), ln2_g.astype(F32).reshape(1, D), ln2_b.astype(F32).reshape(1, D))
    return out.reshape(B, S, D)


def kernel(x, w_in, ml_i_bias, ml_f_bias, ml_norm_w, conv_w, conv_b, dt_bias, a_log, d_skip, ssm_norm_w, w_out,
           ln1_g, ln1_b, w_router, router_bias, w_exp_gate, w_exp_up, w_exp_down, w_sh_gate, w_sh_up, w_sh_down,
           ln2_g, ln2_b):
    for l in range(w_in.shape[0]):
        x = _layer(x, w_in[l], ml_i_bias[l], ml_f_bias[l], ml_norm_w[l], conv_w[l], conv_b[l], dt_bias[l],
                   a_log[l], d_skip[l], ssm_norm_w[l], w_out[l], ln1_g[l], ln1_b[l], w_router[l],
                   router_bias[l], w_exp_gate[l], w_exp_up[l], w_exp_down[l], w_sh_gate[l], w_sh_up[l],
                   w_sh_down[l], ln2_g[l], ln2_b[l])
    return x
```

```python
import functools

import jax
import jax.numpy as jnp
from jax import lax
from jax.experimental import pallas as pl
from jax.experimental.pallas import tpu as pltpu
from jax.experimental.pallas import tpu_sc as plsc

D_MODEL = 1024
ML_HEADS = 4
ML_DV = 256
ML_DQK = 128
ML_QK = ML_HEADS * ML_DQK
ML_WIDTH = ML_HEADS * ML_DV
SSM_WIDTH = 1024
SSM_HEADDIM = 64
SSM_HEADS = 16
SSM_GROUPS = 4
SSM_STATE = 128
SSM_CONV_DIM = SSM_WIDTH + 2 * SSM_GROUPS * SSM_STATE
CONV_K = 5
CHUNK = 128
N_EXPERTS = 64
TOP_K = 8
N_EXPERT_GROUPS = 8
TOPK_GROUPS = 4
D_EXPERT = 256
ROUTED_SCALE = 2.5
LN_EPS = 1e-5
RMS_EPS = 1e-5
DEPTH = 1
ALPHA = (2 * DEPTH) ** 0.25

PA_XBC0 = 0
PA_V0 = PA_XBC0 + SSM_CONV_DIM
PA_O0 = PA_V0 + ML_WIDTH
PA_Z0 = PA_O0 + ML_WIDTH
PA_Q0 = PA_Z0 + SSM_WIDTH
PA_COLS = PA_Q0 + ML_QK
PB_COLS = 128
N_GATE_ROWS = 2 * ML_HEADS + 2 * ML_HEADS + 2 * SSM_HEADS
PREP_SSM0 = 32
N_PREP_ROWS = PREP_SSM0 + 10 * SSM_HEADS

MOE_BLOCK = 1024
SC_WINDOW = 128
VMEM_LIMIT = 56 * 1024 * 1024

BF16 = jnp.bfloat16
F32 = jnp.float32
NEG_INF = float("-inf")


def _cparams(sem, vmem=VMEM_LIMIT):
    return pltpu.CompilerParams(dimension_semantics=sem, vmem_limit_bytes=vmem)


def _dot(a, b):
    return jnp.dot(a, b, preferred_element_type=F32)


def _dot_nt(a, b):
    return lax.dot_general(a, b, (((1,), (1,)), ((), ())), preferred_element_type=F32)


def _dot_tn(a, b):
    return lax.dot_general(a, b, (((0,), (0,)), ((), ())), preferred_element_type=F32)


def _sigmoid(x):
    return 1.0 / (1.0 + jnp.exp(-x))


def _softplus(x):
    return jnp.maximum(x, 0.0) + jnp.log1p(jnp.exp(-jnp.abs(x)))


def _proj_kernel(x_ref, wb_ref, ws_ref, wkt_ref, oa_ref, ob_ref, kt_ref, xb_scr, *, L):
    @pl.when(pl.program_id(1) == 0)
    def _():
        xb = x_ref[...].astype(BF16)
        xb_scr[...] = xb
        ob_ref[...] = _dot(xb, ws_ref[...])
        kt = _dot_nt(wkt_ref[...], xb).astype(kt_ref.dtype)
        for c in range(kt_ref.shape[0]):
            kt_ref[c] = kt[:, c * L:(c + 1) * L]

    oa_ref[...] = _dot(xb_scr[...], wb_ref[...]).astype(oa_ref.dtype)


def _proj(x2d, w_big, w_small, w_kt, tm=1024, tn=2816):
    T, D = x2d.shape
    N = w_big.shape[1]
    tm = min(tm, T)
    L = CHUNK
    return pl.pallas_call(
        functools.partial(_proj_kernel, L=L),
        grid=(T // tm, N // tn),
        in_specs=[
            pl.BlockSpec((tm, D), lambda i, j: (i, 0)),
            pl.BlockSpec((D, tn), lambda i, j: (0, j)),
            pl.BlockSpec((D, PB_COLS), lambda i, j: (0, 0)),
            pl.BlockSpec((ML_QK, D), lambda i, j: (0, 0)),
        ],
        out_specs=[
            pl.BlockSpec((tm, tn), lambda i, j: (i, j)),
            pl.BlockSpec((tm, PB_COLS), lambda i, j: (i, 0)),
            pl.BlockSpec((tm // L, ML_QK, L), lambda i, j: (i, 0, 0)),
        ],
        out_shape=[jax.ShapeDtypeStruct((T, N), BF16), jax.ShapeDtypeStruct((T, PB_COLS), F32),
                   jax.ShapeDtypeStruct((T // L, ML_QK, L), BF16)],
        scratch_shapes=[pltpu.VMEM((tm, D), BF16)],
        compiler_params=_cparams(("parallel", "arbitrary")),
        name="in_proj",
    )(x2d, w_big, w_small, w_kt)


def _split3_dot(x, u):
    hi = x.astype(BF16)
    r1 = x - hi.astype(F32)
    mid = r1.astype(BF16)
    lo = (r1 - mid.astype(F32)).astype(BF16)
    return _dot(hi, u) + _dot(mid, u) + _dot(lo, u)


def _gate_kernel(g_ref, bias_ref, aneg_ref, o_ref, *, nc, L):
    g = g_ref[0] + bias_ref[...]
    nh = ML_HEADS
    ig = g[0:2 * nh]
    fpre = g[2 * nh:4 * nh]
    lf = -_softplus(-fpre)
    dt = _softplus(g[4 * nh:])
    dA = dt * aneg_ref[...]
    r = lax.broadcasted_iota(jnp.int32, (L, L), 0)
    c = lax.broadcasted_iota(jnp.int32, (L, L), 1)
    u_pre = (r <= c).astype(BF16)
    u_suf = (r >= c).astype(BF16)
    nf = nh + SSM_HEADS
    xf = jnp.concatenate([lf[0:nh], dA[0:SSM_HEADS]], axis=0).reshape(nf * nc, L)
    xb = jnp.concatenate([lf[nh:], dA[SSM_HEADS:]], axis=0).reshape(nf * nc, L)
    cf = _split3_dot(xf, u_pre).reshape(nf, nc, L)
    cb = _split3_dot(xb, u_suf).reshape(nf, nc, L)
    totf = jnp.broadcast_to(cf[:, :, L - 1:L], cf.shape)
    totb = jnp.broadcast_to(cb[:, :, 0:1], cb.shape)
    o_ref[0, 0:nh] = cf[0:nh]
    o_ref[0, nh:2 * nh] = cb[0:nh]
    o_ref[0, 2 * nh:4 * nh] = ig
    o_ref[0, 4 * nh:5 * nh] = totf[0:nh] - cf[0:nh] + ig[0:nh]
    o_ref[0, 5 * nh:6 * nh] = totb[0:nh] - cb[0:nh] + ig[nh:]
    lane = lax.broadcasted_iota(jnp.int32, (nh * nc, L), 1)
    pf = (ig[0:nh] - cf[0:nh]).reshape(nh * nc, L)
    pb = (ig[nh:] - cb[0:nh]).reshape(nh * nc, L)
    sh = 1
    while sh < L:
        pf = jnp.where(lane >= sh, jnp.maximum(pf, pltpu.roll(pf, sh, 1)), pf)
        pb = jnp.where(lane < L - sh, jnp.maximum(pb, pltpu.roll(pb, L - sh, 1)), pb)
        sh *= 2
    o_ref[0, 6 * nh:7 * nh] = cf[0:nh] + pf.reshape(nh, nc, L)
    o_ref[0, 7 * nh:8 * nh] = cb[0:nh] + pb.reshape(nh, nc, L)
    H = SSM_HEADS
    af, ab, dtf, dtb = cf[nh:], cb[nh:], dt[0:H], dt[H:]
    s0 = PREP_SSM0
    o_ref[0, s0:s0 + H] = af
    o_ref[0, s0 + H:s0 + 2 * H] = ab
    o_ref[0, s0 + 2 * H:s0 + 3 * H] = af - jnp.log(dtf)
    o_ref[0, s0 + 3 * H:s0 + 4 * H] = ab - jnp.log(dtb)
    o_ref[0, s0 + 4 * H:s0 + 5 * H] = jnp.exp(af)
    o_ref[0, s0 + 5 * H:s0 + 6 * H] = jnp.exp(ab)
    o_ref[0, s0 + 6 * H:s0 + 7 * H] = dtf * jnp.exp(totf[nh:] - af)
    o_ref[0, s0 + 7 * H:s0 + 8 * H] = dtb * jnp.exp(totb[nh:] - ab)
    o_ref[0, s0 + 8 * H:s0 + 9 * H] = jnp.exp(totf[nh:])
    o_ref[0, s0 + 9 * H:s0 + 10 * H] = jnp.exp(totb[nh:])


def _gate_prep(g_rows, bias_rows, aneg_rows):
    B, R, nc, L = g_rows.shape
    return pl.pallas_call(
        functools.partial(_gate_kernel, nc=nc, L=L),
        grid=(B,),
        in_specs=[
            pl.BlockSpec((1, R, nc, L), lambda b: (b, 0, 0, 0)),
            pl.BlockSpec((R, 1, L), lambda b: (0, 0, 0)),
            pl.BlockSpec((2 * SSM_HEADS, 1, L), lambda b: (0, 0, 0)),
        ],
        out_specs=pl.BlockSpec((1, N_PREP_ROWS, nc, L), lambda b: (b, 0, 0, 0)),
        out_shape=jax.ShapeDtypeStruct((B, N_PREP_ROWS, nc, L), F32),
        compiler_params=_cparams(("parallel",)),
        name="gate_prep",
    )(g_rows, bias_rows, aneg_rows)


def _mlstm_kernel(q_ref, kt_ref, v_ref, o_ref, mg_ref, mc_ref, nw_ref, y_ref, park_scr, c_scr, n_scr, m_scr,
                  *, nc, L, nh):
    row_i = lax.broadcasted_iota(jnp.int32, (L, L), 0)
    col_i = lax.broadcasted_iota(jnp.int32, (L, L), 1)
    ones = jnp.ones((L, L), BF16)

    c_scr[...] = jnp.zeros_like(c_scr)
    n_scr[...] = jnp.zeros_like(n_scr)
    m_scr[...] = jnp.full_like(m_scr, NEG_INF)

    def chunk(c, d, hh):
        c_ref, n_ref, m_ref = c_scr.at[2 * hh + d], n_scr.at[2 * hh + d], m_scr.at[2 * hh + d]
        c_prev, n_prev, m_prev = c_ref[...], n_ref[...], m_ref[...]
        sl = pl.ds(pl.multiple_of(c * L, L), L)
        q = q_ref[0, sl, hh * ML_DQK:(hh + 1) * ML_DQK]
        kt = kt_ref[0, c, hh]
        vo = jnp.concatenate([v_ref[0, sl, hh * ML_DV:(hh + 1) * ML_DV], ones], axis=1)
        g8 = mg_ref[0, hh, c]
        ct = mc_ref[0, hh, sl, :]
        b_r, i_r, a_r = g8[2 * d:2 * d + 1], g8[2 * d + 1:2 * d + 2], g8[4 + d:5 + d]
        b_c = jnp.broadcast_to(ct[:, 2 * d:2 * d + 1], (L, L))
        mi_c = jnp.broadcast_to(ct[:, 6 + d:7 + d], (L, L))
        if d == 0:
            gtot = b_r[:, L - 1:L]
            mask = col_i <= row_i
        else:
            gtot = b_r[:, 0:1]
            mask = col_i >= row_i
        m_loc = jnp.max(a_r, axis=1, keepdims=True)
        kwt = (kt.astype(F32) * jnp.exp(a_r - m_loc)).astype(BF16)
        uu = _dot(kwt, vo)
        qq = _dot(q, jnp.concatenate([kt, c_prev.astype(BF16), n_prev.astype(BF16)], axis=1))
        s, q_c, q_n = qq[:, 0:L], qq[:, L:L + ML_DV], qq[:, L + ML_DV:]
        dmat = jnp.where(mask, b_c - b_r + i_r, NEG_INF)
        m_inter = b_c + m_prev
        m_t = jnp.maximum(m_inter, mi_c)
        s_mat = s * jnp.exp(dmat - m_t)
        inter_w = jnp.exp(m_inter - m_t)
        ss = _dot(s_mat.astype(BF16), vo)
        den = ss[:, ML_DV:] + inter_w * q_n
        r = 1.0 / jnp.maximum(jnp.abs(den), jnp.exp(-m_t))
        num = ss[:, 0:ML_DV] + jnp.concatenate([inter_w, inter_w], axis=1) * q_c
        h = num * jnp.concatenate([r, r], axis=1)
        m_new = jnp.maximum(gtot + m_prev, m_loc)
        decay = jnp.exp(gtot + m_prev - m_new)
        sc = jnp.exp(m_loc - m_new)
        c_ref[...] = decay * c_prev + sc * uu[:, 0:ML_DV]
        n_ref[...] = decay * n_prev + sc * uu[:, ML_DV:]
        m_ref[...] = m_new
        return sl, h

    def finish(sl, hh, h):
        cols = slice(hh * ML_DV, (hh + 1) * ML_DV)
        mu = jnp.mean(h, axis=1, keepdims=True)
        d = h - mu
        var = jnp.mean(d * d, axis=1, keepdims=True)
        y = d * lax.rsqrt(var + LN_EPS) * nw_ref[:, cols]
        y = y * _sigmoid(o_ref[0, sl, cols].astype(F32))
        y_ref[0, sl, cols] = y.astype(y_ref.dtype)

    def first_half(i, carry):
        for hh in range(nh):
            for d, c in ((0, i), (1, nc - 1 - i)):
                sl, h = chunk(c, d, hh)
                park_scr[hh, sl, :] = h
        return carry

    def second_half(i, carry):
        for hh in range(nh):
            for d, c in ((0, i), (1, nc - 1 - i)):
                sl, h = chunk(c, d, hh)
                finish(sl, hh, h + park_scr[hh, sl, :])
        return carry

    half = nc // 2
    lax.fori_loop(0, half, first_half, 0, unroll=4 // nh)
    lax.fori_loop(half, nc, second_half, 0, unroll=4 // nh)


def _mlstm(pa3, kt5, mg, mc, norm_w, nh=1):
    B, S, _ = pa3.shape
    nc = S // CHUNK
    H = ML_HEADS
    q_blk, v_blk, o_blk = PA_Q0 // (nh * ML_DQK), PA_V0 // (nh * ML_DV), PA_O0 // (nh * ML_DV)
    return pl.pallas_call(
        functools.partial(_mlstm_kernel, nc=nc, L=CHUNK, nh=nh),
        grid=(B, H // nh),
        in_specs=[
            pl.BlockSpec((1, S, nh * ML_DQK), lambda b, h: (b, 0, q_blk + h)),
            pl.BlockSpec((1, nc, nh, ML_DQK, CHUNK), lambda b, h: (b, 0, h, 0, 0)),
            pl.BlockSpec((1, S, nh * ML_DV), lambda b, h: (b, 0, v_blk + h)),
            pl.BlockSpec((1, S, nh * ML_DV), lambda b, h: (b, 0, o_blk + h)),
            pl.BlockSpec((1, nh, nc, 8, CHUNK), lambda b, h: (b, h, 0, 0, 0)),
            pl.BlockSpec((1, nh, S, 8), lambda b, h: (b, h, 0, 0)),
            pl.BlockSpec((1, nh * ML_DV), lambda b, h: (0, h)),
        ],
        out_specs=pl.BlockSpec((1, S, nh * ML_DV), lambda b, h: (b, 0, h)),
        out_shape=jax.ShapeDtypeStruct((B, S, ML_WIDTH), BF16),
        scratch_shapes=[
            pltpu.VMEM((nh, S, ML_DV), F32),
            pltpu.VMEM((2 * nh, ML_DQK, ML_DV), F32),
            pltpu.VMEM((2 * nh, ML_DQK, CHUNK), F32),
            pltpu.VMEM((2 * nh, 1, 1), F32),
        ],
        compiler_params=_cparams(("parallel", "parallel")),
        name="mlstm",
    )(pa3, kt5, pa3, pa3, mg, mc, norm_w)


CONV_EDGE = 16


def _conv_shift_matrix(L):
    pad, E = CONV_K // 2, CONV_EDGE
    t = jnp.arange(L, dtype=jnp.int32)[:, None]
    blocks = []
    u = jnp.arange(L, dtype=jnp.int32)[None, :]
    for j in range(CONV_K):
        blocks.append(u == t + j - pad)
    v = jnp.arange(E, dtype=jnp.int32)[None, :]
    for j in range(pad):
        blocks.append(v - E == t + j - pad)
    for j in range(pad + 1, CONV_K):
        blocks.append(v + L == t + j - pad)
    return jnp.concatenate(blocks, axis=1).astype(BF16)


def _conv_kernel(x_ref, top_ref, bot_ref, w_ref, b_ref, sh_ref, o_ref, ext_scr, *, Lc, L):
    pad, E = CONV_K // 2, CONV_EDGE
    i = pl.program_id(1)
    first, last = i == 0, i == pl.num_programs(1) - 1
    ext_scr[0:E, :] = jnp.where(first, jnp.zeros_like(top_ref[0]), top_ref[0])
    ext_scr[E:E + Lc, :] = x_ref[0]
    ext_scr[E + Lc:2 * E + Lc, :] = jnp.where(last, jnp.zeros_like(bot_ref[0]), bot_ref[0])
    taps = [w_ref[j:j + 1, :].astype(BF16) for j in range(CONV_K)]
    scaled = lambda rows, j: rows * taps[j]
    for r0 in range(0, Lc, L):
        cen = ext_scr[E + r0:E + r0 + L, :]
        top = ext_scr[r0:r0 + E, :]
        bot = ext_scr[E + r0 + L:2 * E + r0 + L, :]
        pieces = [scaled(cen, j) for j in range(CONV_K)]
        pieces += [scaled(top, j) for j in range(pad)] + [scaled(bot, j) for j in range(pad + 1, CONV_K)]
        acc = _dot(sh_ref[...], jnp.concatenate(pieces, axis=0)) + b_ref[...]
        o_ref[0, r0:r0 + L, :] = (acc * _sigmoid(acc)).astype(o_ref.dtype)


def _conv(pa3, conv_w8, conv_b, Lc):
    B, S, _ = pa3.shape
    C = SSM_CONV_DIM
    xbc_blk = PA_XBC0 // C
    L = CHUNK
    sh = _conv_shift_matrix(L)
    epb = Lc // CONV_EDGE
    n_edge = S // CONV_EDGE
    return pl.pallas_call(
        functools.partial(_conv_kernel, Lc=Lc, L=L),
        grid=(B, S // Lc),
        in_specs=[
            pl.BlockSpec((1, Lc, C), lambda b, i: (b, i, xbc_blk)),
            pl.BlockSpec((1, CONV_EDGE, C), lambda b, i: (b, jnp.maximum(i * epb - 1, 0), xbc_blk)),
            pl.BlockSpec((1, CONV_EDGE, C), lambda b, i: (b, jnp.minimum((i + 1) * epb, n_edge - 1), xbc_blk)),
            pl.BlockSpec((8, C), lambda b, i: (0, 0)),
            pl.BlockSpec((1, C), lambda b, i: (0, 0)),
            pl.BlockSpec(sh.shape, lambda b, i: (0, 0)),
        ],
        out_specs=pl.BlockSpec((1, Lc, C), lambda b, i: (b, i, 0)),
        out_shape=jax.ShapeDtypeStruct((B, S, C), BF16),
        scratch_shapes=[pltpu.VMEM((Lc + 2 * CONV_EDGE, C), BF16)],
        compiler_params=_cparams(("parallel", "parallel")),
        name="ssm_conv",
    )(pa3, pa3, pa3, conv_w8, conv_b, sh)


def _expansion_matrix(src_cols, width):
    src = jnp.repeat(jnp.asarray(src_cols, jnp.int32), width)
    e = (jnp.arange(128, dtype=jnp.int32)[:, None] == src[None, :]).astype(BF16)
    return jnp.concatenate([e, e], axis=0)


def _split2(t):
    hi = t.astype(BF16)
    return jnp.concatenate([hi, (t - hi.astype(F32)).astype(BF16)], axis=1)


def _ssd_bwd_kernel(x_ref, b_ref, tb_ref, dec_ref, ew_ref, hb_ref, h_scr, *, L):
    G, hg, P, N = SSM_GROUPS, SSM_HEADS // SSM_GROUPS, SSM_HEADDIM, SSM_STATE
    GW = hg * P

    @pl.when(pl.program_id(1) == 0)
    def _():
        h_scr[...] = jnp.zeros_like(h_scr)

    for s in reversed(range(x_ref.shape[1] // L)):
        rows = slice(s * L, (s + 1) * L)
        w_all = _dot(_split2(tb_ref[0, s].T), ew_ref[...])
        for g in range(G):
            w = w_all[:, g * GW:(g + 1) * GW]
            xs = (x_ref[0, rows, g * GW:(g + 1) * GW].astype(F32) * w).astype(BF16)
            st = _dot_tn(b_ref[0, rows, g * N:(g + 1) * N], xs)
            h_old = h_scr[g]
            hb_ref[0, s, g] = h_old.astype(hb_ref.dtype)
            h_scr[g] = dec_ref[0, s, 1:2, g * GW:(g + 1) * GW] * h_old + st


def _ssd_bwd(xc, tb, decs, cpb):
    B, S, _ = xc.shape
    nc = S // CHUNK
    nblk = nc // cpb
    R = cpb * CHUNK
    G, N, W = SSM_GROUPS, SSM_STATE, SSM_WIDTH
    return pl.pallas_call(
        functools.partial(_ssd_bwd_kernel, L=CHUNK),
        grid=(B, nblk),
        in_specs=[
            pl.BlockSpec((1, R, W), lambda b, c: (b, nblk - 1 - c, 0)),
            pl.BlockSpec((1, R, G * N), lambda b, c: (b, nblk - 1 - c, W // (G * N))),
            pl.BlockSpec((1, cpb, 128, CHUNK), lambda b, c: (b, nblk - 1 - c, 0, 0)),
            pl.BlockSpec((1, cpb, 2, W), lambda b, c: (b, nblk - 1 - c, 0, 0)),
            pl.BlockSpec((256, W), lambda b, c: (0, 0)),
        ],
        out_specs=pl.BlockSpec((1, cpb, G, N, W // G), lambda b, c: (b, nblk - 1 - c, 0, 0, 0)),
        out_shape=jax.ShapeDtypeStruct((B, nc, G, N, W // G), BF16),
        scratch_shapes=[pltpu.VMEM((G, N, W // G), F32)],
        compiler_params=_cparams(("parallel", "arbitrary")),
        name="ssd_bwd_states",
    )(xc, xc, tb, decs, _expansion_matrix([5 * SSM_HEADS + h for h in range(SSM_HEADS)], SSM_HEADDIM))


def _ssd_main_kernel(x_ref, b_ref, c_ref, z_ref, tb_ref, dec_ref, hb_ref, dsk_ref, nw_ref, ed_ref, ee_ref,
                     y_ref, h_scr, *, L):
    G, hg, P, N = SSM_GROUPS, SSM_HEADS // SSM_GROUPS, SSM_HEADDIM, SSM_STATE
    H = SSM_HEADS
    GW = hg * P

    @pl.when(pl.program_id(1) == 0)
    def _():
        h_scr[...] = jnp.zeros_like(h_scr)

    row_i = lax.broadcasted_iota(jnp.int32, (L, L), 0)
    col_i = lax.broadcasted_iota(jnp.int32, (L, L), 1)
    lower = col_i <= row_i
    upper = col_i >= row_i
    lane_g = lax.broadcasted_iota(jnp.int32, (L, GW), 1)

    for s, g in [(s, g) for s in range(x_ref.shape[1] // L) for g in range(G)]:
        rows = slice(s * L, (s + 1) * L)
        tb = tb_ref[0, s]
        sc2 = _split2(tb.T)
        xg = x_ref[0, rows, g * GW:(g + 1) * GW]
        bg = b_ref[0, rows, g * N:(g + 1) * N]
        cg = c_ref[0, rows, g * N:(g + 1) * N]
        cb = _dot_nt(cg, bg)
        acol = _dot(sc2, ed_ref[g])
        ecol = _dot(sc2, ee_ref[g])
        ms, xms = [], []
        for j in range(hg):
            hd = g * hg + j
            rf_r, rb_r = tb[6 * H + hd:6 * H + hd + 1], tb[7 * H + hd:7 * H + hd + 1]
            af_c, ab_c = acol[:, j * L:(j + 1) * L], acol[:, (hg + j) * L:(hg + j + 1) * L]
            fm = jnp.exp(jnp.where(lower, af_c - rf_r, NEG_INF))
            bm = jnp.exp(jnp.where(upper, ab_c - rb_r, NEG_INF))
            ms.append((cb * (fm + bm)).astype(BF16))
            xms.append(jnp.where((lane_g >= j * P) & (lane_g < (j + 1) * P), xg, jnp.zeros_like(xg)))
        efc, ebc, wfc = ecol[:, 0:GW], ecol[:, GW:2 * GW], ecol[:, 2 * GW:]
        y = _dot(jnp.concatenate(ms, axis=1), jnp.concatenate(xms, axis=0))
        h_old = h_scr[g]
        y = y + efc * _dot(cg, h_old.astype(BF16))
        y = y + ebc * _dot(cg, hb_ref[0, s, g])
        xgf = xg.astype(F32)
        y = y + dsk_ref[:, g * GW:(g + 1) * GW] * xgf
        xs = (xgf * wfc).astype(BF16)
        h_scr[g] = dec_ref[0, s, 0:1, g * GW:(g + 1) * GW] * h_old + _dot_tn(bg, xs)
        zg = z_ref[0, rows, g * GW:(g + 1) * GW].astype(F32)
        y = y * (zg * _sigmoid(zg))
        y = y * lax.rsqrt(jnp.mean(y * y, axis=1, keepdims=True) + RMS_EPS)
        y_ref[0, rows, g * GW:(g + 1) * GW] = (y * nw_ref[:, g * GW:(g + 1) * GW]).astype(y_ref.dtype)


def _ssd_main(xc, pa3, tb, decs, hb, dskip_row, norm_w, cpb):
    B, S, _ = xc.shape
    nc = S // CHUNK
    R = cpb * CHUNK
    G, N, W = SSM_GROUPS, SSM_STATE, SSM_WIDTH
    H, hg, P = SSM_HEADS, SSM_HEADS // SSM_GROUPS, SSM_HEADDIM
    z_blk = PA_Z0 // W
    e_dec = jnp.stack([_expansion_matrix([q * H + g * hg + j for q in (0, 1) for j in range(hg)], CHUNK)
                       for g in range(G)])
    e_ew = jnp.stack([_expansion_matrix([q * H + g * hg + j for q in (2, 3, 4) for j in range(hg)], P)
                      for g in range(G)])
    return pl.pallas_call(
        functools.partial(_ssd_main_kernel, L=CHUNK),
        grid=(B, nc // cpb),
        in_specs=[
            pl.BlockSpec((1, R, W), lambda b, c: (b, c, 0)),
            pl.BlockSpec((1, R, G * N), lambda b, c: (b, c, W // (G * N))),
            pl.BlockSpec((1, R, G * N), lambda b, c: (b, c, W // (G * N) + 1)),
            pl.BlockSpec((1, R, W), lambda b, c: (b, c, z_blk)),
            pl.BlockSpec((1, cpb, 128, CHUNK), lambda b, c: (b, c, 0, 0)),
            pl.BlockSpec((1, cpb, 2, W), lambda b, c: (b, c, 0, 0)),
            pl.BlockSpec((1, cpb, G, N, W // G), lambda b, c: (b, c, 0, 0, 0)),
            pl.BlockSpec((1, W), lambda b, c: (0, 0)),
            pl.BlockSpec((1, W), lambda b, c: (0, 0)),
            pl.BlockSpec((G, 256, 2 * hg * CHUNK), lambda b, c: (0, 0, 0)),
            pl.BlockSpec((G, 256, 3 * W // G), lambda b, c: (0, 0, 0)),
        ],
        out_specs=pl.BlockSpec((1, R, W), lambda b, c: (b, c, 0)),
        out_shape=jax.ShapeDtypeStruct((B, S, W), BF16),
        scratch_shapes=[pltpu.VMEM((G, N, W // G), F32)],
        compiler_params=_cparams(("parallel", "arbitrary")),
        name="ssd_main",
    )(xc, xc, xc, pa3, tb, decs, hb, dskip_row, norm_w, e_dec, e_ew)


def _pack_bf16_pairs(v):
    n = v.shape[1] // 2
    hi = lax.bitcast_convert_type(v[:, :n].astype(BF16).astype(F32), jnp.uint32)
    lo = lax.bitcast_convert_type(v[:, n:].astype(BF16).astype(F32), jnp.uint32)
    return (hi & jnp.uint32(0xFFFF0000)) | (lo >> 16)


def _unpack_bf16_pairs(u):
    hi = lax.bitcast_convert_type(u & jnp.uint32(0xFFFF0000), F32)
    lo = lax.bitcast_convert_type(u << 16, F32)
    return hi, lo


def _layer_norm(v, g, b):
    mu = jnp.mean(v, axis=1, keepdims=True)
    d = v - mu
    var = jnp.mean(d * d, axis=1, keepdims=True)
    return d * lax.rsqrt(var + LN_EPS) * g + b


def _outproj_kernel(yml_ref, yssm_ref, x_ref, wo_ref, g_ref, b_ref, wr_ref, h_ref, hpa_ref, hpb_ref, sc_ref):
    mix = _dot(yml_ref[...], wo_ref[0:ML_WIDTH, :]) + _dot(yssm_ref[...], wo_ref[ML_WIDTH:, :])
    h = _layer_norm(ALPHA * x_ref[...] + mix, g_ref[...], b_ref[...])
    h_ref[...] = h
    hp = _pack_bf16_pairs(h)
    q = hp.shape[1] // 2
    hpa_ref[...] = hp[:, :q]
    hpb_ref[...] = hp[:, q:]
    sc_ref[...] = _sigmoid(_dot(h.astype(BF16), wr_ref[...]))


def _outproj(yml, yssm, x2d, w_out, ln_g, ln_b, w_router, tm=1024):
    T, D = x2d.shape
    tm = min(tm, T)
    row = lambda i: (i, 0)
    fixed = lambda i: (0, 0)
    return pl.pallas_call(
        _outproj_kernel,
        grid=(T // tm,),
        in_specs=[
            pl.BlockSpec((tm, ML_WIDTH), row),
            pl.BlockSpec((tm, SSM_WIDTH), row),
            pl.BlockSpec((tm, D), row),
            pl.BlockSpec((ML_WIDTH + SSM_WIDTH, D), fixed),
            pl.BlockSpec((1, D), fixed),
            pl.BlockSpec((1, D), fixed),
            pl.BlockSpec((D, 128), fixed),
        ],
        out_specs=[pl.BlockSpec((tm, D), row), pl.BlockSpec((tm, D // 4), row), pl.BlockSpec((tm, D // 4), row),
                   pl.BlockSpec((tm, 128), row)],
        out_shape=[
            jax.ShapeDtypeStruct((T, D), F32),
            jax.ShapeDtypeStruct((T, D // 4), jnp.uint32),
            jax.ShapeDtypeStruct((T, D // 4), jnp.uint32),
            jax.ShapeDtypeStruct((T, 128), F32),
        ],
        compiler_params=_cparams(("parallel",)),
        name="out_proj_ln1_router",
    )(yml, yssm, x2d, w_out, ln_g, ln_b, w_router)


def _route_kernel(sc_ref, rb_ref, idx_ref, w_ref, rank_ref, cnt_ref, carry_scr, *, Tt):
    E, NG = N_EXPERTS, N_EXPERT_GROUPS
    EG = E // NG

    @pl.when(pl.program_id(0) == 0)
    def _():
        carry_scr[...] = jnp.zeros_like(carry_scr)

    s = sc_ref[...].T[0:E]
    s3 = s.reshape(NG, EG, Tt)
    ch3 = (s + rb_ref[:, 0:1]).reshape(NG, EG, Tt)
    e_in = lax.broadcasted_iota(jnp.int32, (NG, EG, Tt), 1)
    g_in = lax.broadcasted_iota(jnp.int32, (NG, EG, Tt), 0)
    f_i = g_in * EG + e_in
    m1 = jnp.max(ch3, axis=1, keepdims=True)
    i1 = jnp.min(jnp.where(ch3 == m1, e_in, EG), axis=1, keepdims=True)
    m2 = jnp.max(jnp.where(e_in == i1, NEG_INF, ch3), axis=1, keepdims=True)
    gs = m1 + m2
    g_i = lax.broadcasted_iota(jnp.int32, (NG, 1, Tt), 0)
    gsel = jnp.zeros((NG, 1, Tt), jnp.bool_)
    cur = gs
    for _ in range(TOPK_GROUPS):
        mx = jnp.max(cur, axis=0, keepdims=True)
        pick = jnp.min(jnp.where(cur == mx, g_i, NG), axis=0, keepdims=True)
        hit = g_i == pick
        gsel = gsel | hit
        cur = jnp.where(hit, NEG_INF, cur)
    cur = jnp.where(gsel, ch3, NEG_INF)
    selmask = jnp.zeros((NG, EG, Tt), jnp.bool_)
    idxs, ws = [], []
    for _ in range(TOP_K):
        mx = jnp.max(jnp.max(cur, axis=1, keepdims=True), axis=0, keepdims=True)
        cand = jnp.where(cur == mx, f_i, E)
        pick = jnp.min(jnp.min(cand, axis=1, keepdims=True), axis=0, keepdims=True)
        hit = f_i == pick
        wk = jnp.sum(jnp.sum(jnp.where(hit, s3, 0.0), axis=1, keepdims=True), axis=0, keepdims=True)
        idxs.append(pick.reshape(1, Tt))
        ws.append(wk.reshape(1, Tt))
        selmask = selmask | hit
        cur = jnp.where(hit, NEG_INF, cur)
    wsum = ws[0]
    for k in range(1, TOP_K):
        wsum = wsum + ws[k]
    r = lax.broadcasted_iota(jnp.int32, (Tt, Tt), 0)
    c = lax.broadcasted_iota(jnp.int32, (Tt, Tt), 1)
    ustrict = (r < c).astype(BF16)
    self32 = jnp.where(selmask, 1.0, 0.0).reshape(E, Tt)
    excl = _dot(self32.astype(BF16), ustrict) + carry_scr[:, 0:1]
    excl3 = excl.reshape(NG, EG, Tt)
    for k in range(TOP_K):
        hit = f_i == idxs[k].reshape(1, 1, Tt)
        rk = jnp.sum(jnp.sum(jnp.where(hit, excl3, 0.0), axis=1, keepdims=True), axis=0, keepdims=True)
        idx_ref[k:k + 1, :] = idxs[k]
        w_ref[k:k + 1, :] = ws[k] / wsum * ROUTED_SCALE
        rank_ref[k:k + 1, :] = rk.reshape(1, Tt).astype(jnp.int32)
    carry_scr[...] = carry_scr[...] + jnp.sum(self32, axis=1, keepdims=True)
    cnt_ref[...] = carry_scr[...]


def _route(scores, rbias, Tt=512):
    T = scores.shape[0]
    kt = lambda i: (0, i)
    return pl.pallas_call(
        functools.partial(_route_kernel, Tt=Tt),
        grid=(T // Tt,),
        in_specs=[pl.BlockSpec((Tt, 128), lambda i: (i, 0)), pl.BlockSpec((N_EXPERTS, 128), lambda i: (0, 0))],
        out_specs=[
            pl.BlockSpec((TOP_K, Tt), kt),
            pl.BlockSpec((TOP_K, Tt), kt),
            pl.BlockSpec((TOP_K, Tt), kt),
            pl.BlockSpec((N_EXPERTS, 128), lambda i: (0, 0)),
        ],
        out_shape=[
            jax.ShapeDtypeStruct((TOP_K, T), jnp.int32),
            jax.ShapeDtypeStruct((TOP_K, T), F32),
            jax.ShapeDtypeStruct((TOP_K, T), jnp.int32),
            jax.ShapeDtypeStruct((N_EXPERTS, 128), F32),
        ],
        scratch_shapes=[pltpu.VMEM((N_EXPERTS, 128), F32)],
        compiler_params=_cparams(("arbitrary",)),
        name="route_topk",
    )(scores, rbias)


def _dest_kernel(ps_ref, idx_ref, rank_ref, o_ref):
    idx = idx_ref[...]
    start = jnp.zeros(idx.shape, jnp.int32)
    for e in range(N_EXPERTS):
        start = jnp.where(idx == e, ps_ref[e], start)
    o_ref[...] = start + rank_ref[...]


def _dest(pstarts, idx, rank, Tt=2048):
    K, T = idx.shape
    Tt = min(Tt, T)
    blk = pl.BlockSpec((K, Tt), lambda i, ps: (0, i))
    return pl.pallas_call(
        _dest_kernel,
        grid_spec=pltpu.PrefetchScalarGridSpec(num_scalar_prefetch=1, grid=(T // Tt,), in_specs=[blk, blk],
                                               out_specs=blk),
        out_shape=jax.ShapeDtypeStruct((K, T), jnp.int32),
        compiler_params=_cparams(("parallel",)),
        name="dispatch_slots",
    )(pstarts, idx, rank)


def _sc_mesh():
    return plsc.VectorSubcoreMesh(core_axis_name="c", subcore_axis_name="s")


def _sc_scatter_rows(src, idx, n_out, win=SC_WINDOW):
    K, T = idx.shape
    W = src.shape[1]

    @functools.partial(pl.kernel, out_type=jax.ShapeDtypeStruct((n_out, W), src.dtype), mesh=_sc_mesh(),
                       scratch_types=[], name="sc_scatter_rows")
    def k(x_hbm, i_hbm, o_hbm):
        def body(x_vmem, i_vmem):
            for j in range(K):
                pltpu.sync_copy(x_vmem, o_hbm.at[i_vmem.at[j]])

        pltpu.emit_pipeline(
            body,
            grid=(T // win,),
            in_specs=[pl.BlockSpec((win, W), lambda i: (i, 0)), pl.BlockSpec((K, win), lambda i: (0, i))],
            out_specs=[],
            core_axis_name=("c", "s"),
            dimension_semantics=(pltpu.PARALLEL,),
        )(x_hbm, i_hbm)

    return k(src, idx)


def _sc_gather_rows(table, idx, win=SC_WINDOW):
    n = idx.shape[0]
    W = table.shape[1]

    @functools.partial(pl.kernel, out_type=jax.ShapeDtypeStruct((n, W), table.dtype), mesh=_sc_mesh(),
                       scratch_types=[], name="sc_gather_rows")
    def k(t_hbm, i_hbm, o_hbm):
        def body(i_vmem, o_vmem):
            pltpu.sync_copy(t_hbm.at[i_vmem.at[0]], o_vmem)

        pltpu.emit_pipeline(
            body,
            grid=(n // win,),
            in_specs=[pl.BlockSpec((1, win), lambda i: (0, i))],
            out_specs=[pl.BlockSpec((win, W), lambda i: (i, 0))],
            core_axis_name=("c", "s"),
            dimension_semantics=(pltpu.PARALLEL,),
        )(i_hbm, o_hbm)

    return k(table, idx.reshape(1, n))


def _expert_kernel(be_ref, nu_ref, xa_ref, xb_ref, wg_ref, wu_ref, wd_ref, oa_ref, ob_ref, wg_s, wu_s, wd_s):
    half = D_MODEL // 2
    i = pl.program_id(0)
    active = i < nu_ref[0]

    @pl.when(active & ((i == 0) | (be_ref[i] != be_ref[jnp.maximum(i - 1, 0)])))
    def _():
        wg_s[...] = wg_ref[0].astype(BF16)
        wu_s[...] = wu_ref[0].astype(BF16)
        wd_s[...] = wd_ref[0].astype(BF16)

    @pl.when(active)
    def _():
        hi, lo = _unpack_bf16_pairs(jnp.concatenate([xa_ref[...], xb_ref[...]], axis=1))
        x = jnp.concatenate([hi.astype(BF16), lo.astype(BF16)], axis=1)
        gt = _dot(x, wg_s[...])
        up = _dot(x, wu_s[...])
        hmid = (gt * _sigmoid(gt) * up).astype(BF16)
        packed = _pack_bf16_pairs(_dot(hmid, wd_s[...]))
        q = packed.shape[1] // 2
        oa_ref[...] = packed[:, :q]
        ob_ref[...] = packed[:, q:]

    @pl.when(jnp.logical_not(active))
    def _():
        oa_ref[...] = jnp.zeros_like(oa_ref)
        ob_ref[...] = jnp.zeros_like(ob_ref)


def _experts(xs_a, xs_b, block_expert, n_used, w_gate, w_up, w_down, blk=MOE_BLOCK):
    P, quarter = xs_a.shape
    nb = P // blk
    D, F = D_MODEL, D_EXPERT
    cur = lambda i, be, nu: jnp.minimum(i, nu[0] - 1)
    return pl.pallas_call(
        _expert_kernel,
        grid_spec=pltpu.PrefetchScalarGridSpec(
            num_scalar_prefetch=2,
            grid=(nb,),
            in_specs=[
                pl.BlockSpec((blk, quarter), lambda i, be, nu: (cur(i, be, nu), 0)),
                pl.BlockSpec((blk, quarter), lambda i, be, nu: (cur(i, be, nu), 0)),
                pl.BlockSpec((1, D, F), lambda i, be, nu: (be[cur(i, be, nu)], 0, 0)),
                pl.BlockSpec((1, D, F), lambda i, be, nu: (be[cur(i, be, nu)], 0, 0)),
                pl.BlockSpec((1, F, D), lambda i, be, nu: (be[cur(i, be, nu)], 0, 0)),
            ],
            out_specs=[pl.BlockSpec((blk, quarter), lambda i, be, nu: (i, 0)),
                       pl.BlockSpec((blk, quarter), lambda i, be, nu: (i, 0))],
            scratch_shapes=[pltpu.VMEM((D, F), BF16), pltpu.VMEM((D, F), BF16), pltpu.VMEM((F, D), BF16)],
        ),
        out_shape=[jax.ShapeDtypeStruct((P, quarter), jnp.uint32), jax.ShapeDtypeStruct((P, quarter), jnp.uint32)],
        compiler_params=_cparams(("arbitrary",)),
        name="expert_mlp",
    )(block_expert, n_used, xs_a, xs_b, w_gate, w_up, w_down)


def _final_kernel(h_ref, ga_ref, gb_ref, wt_ref, wsg_ref, wsu_ref, wsd_ref, g_ref, b_ref, o_ref):
    h = h_ref[...]
    hb = h.astype(BF16)
    gt = _dot(hb, wsg_ref[...])
    up = _dot(hb, wsu_ref[...])
    shared = _dot((gt * _sigmoid(gt) * up).astype(BF16), wsd_ref[...])
    acc_hi = None
    for k in range(TOP_K):
        hi, lo = _unpack_bf16_pairs(jnp.concatenate([ga_ref[k], gb_ref[k]], axis=1))
        wk = wt_ref[:, k:k + 1]
        acc_hi = wk * hi if acc_hi is None else acc_hi + wk * hi
        acc_lo = wk * lo if k == 0 else acc_lo + wk * lo
    routed = jnp.concatenate([acc_hi, acc_lo], axis=1)
    o_ref[...] = _layer_norm(ALPHA * h + (routed + shared), g_ref[...], b_ref[...])


def _final(h, gk_a, gk_b, wt, ws_gate, ws_up, ws_down, ln_g, ln_b, tm=512):
    T, D = h.shape
    tm = min(tm, T)
    F = D_EXPERT
    fixed = lambda i: (0, 0)
    return pl.pallas_call(
        _final_kernel,
        grid=(T // tm,),
        in_specs=[
            pl.BlockSpec((tm, D), lambda i: (i, 0)),
            pl.BlockSpec((TOP_K, tm, D // 4), lambda i: (0, i, 0)),
            pl.BlockSpec((TOP_K, tm, D // 4), lambda i: (0, i, 0)),
            pl.BlockSpec((tm, TOP_K), lambda i: (i, 0)),
            pl.BlockSpec((D, F), fixed),
            pl.BlockSpec((D, F), fixed),
            pl.BlockSpec((F, D), fixed),
            pl.BlockSpec((1, D), fixed),
            pl.BlockSpec((1, D), fixed),
        ],
        out_specs=pl.BlockSpec((tm, D), lambda i: (i, 0)),
        out_shape=jax.ShapeDtypeStruct((T, D), F32),
        compiler_params=_cparams(("parallel",)),
        name="shared_combine_ln2",
    )(h, gk_a, gk_b, wt, ws_gate, ws_up, ws_down, ln_g, ln_b)


def _mixers(x, w_in, ml_i_bias, ml_f_bias, ml_norm_w, conv_w, conv_b, dt_bias, a_log, d_skip, ssm_norm_w):
    B, S, D = x.shape
    T = B * S
    nc = S // CHUNK
    H = ML_HEADS
    k0, v0 = ML_QK, 2 * ML_QK
    g0 = v0 + 2 * ML_WIDTH
    z0 = g0 + 4 * H
    x0 = z0 + SSM_WIDTH
    d0 = x0 + SSM_CONV_DIM
    w_big = jnp.concatenate([w_in[:, x0:d0], w_in[:, v0:g0], w_in[:, z0:x0], w_in[:, :k0]], axis=1).astype(BF16)
    w_small = jnp.concatenate(
        [w_in[:, g0:z0], w_in[:, d0:], jnp.zeros((D, PB_COLS - N_GATE_ROWS), w_in.dtype)], axis=1).astype(BF16)
    w_kt = (w_in[:, k0:v0] * (ML_DQK ** -0.5)).T.astype(BF16)
    pa, pb, kt = _proj(x.reshape(T, D), w_big, w_small, w_kt)
    pa3 = pa.reshape(B, S, PA_COLS)
    kt5 = kt.reshape(B, nc, H, ML_DQK, CHUNK)

    g_rows = pb.reshape(B, S, PB_COLS)[:, :, :N_GATE_ROWS].transpose(0, 2, 1).reshape(B, N_GATE_ROWS, nc, CHUNK)
    bias = jnp.concatenate([ml_i_bias, ml_f_bias, dt_bias]).astype(F32)
    bias_rows = jnp.broadcast_to(bias[:, None, None], (N_GATE_ROWS, 1, CHUNK))
    aneg_rows = jnp.broadcast_to((-jnp.exp(a_log.astype(F32)))[:, None, None], (2 * SSM_HEADS, 1, CHUNK))
    prep = _gate_prep(g_rows, bias_rows, aneg_rows)
    m8 = jnp.stack([prep[:, 0:H], prep[:, 2 * H:3 * H], prep[:, H:2 * H], prep[:, 3 * H:4 * H],
                    prep[:, 4 * H:5 * H], prep[:, 5 * H:6 * H], prep[:, 6 * H:7 * H], prep[:, 7 * H:8 * H]],
                   axis=2)
    mg = m8.transpose(0, 1, 3, 2, 4)
    mc = m8.transpose(0, 1, 3, 4, 2).reshape(B, H, S, 8)
    s0, SH = PREP_SSM0, SSM_HEADS
    tb = jnp.concatenate([prep[:, s0:s0 + 2 * SH], prep[:, s0 + 4 * SH:s0 + 8 * SH],
                          prep[:, s0 + 2 * SH:s0 + 4 * SH]], axis=1).transpose(0, 2, 1, 3)
    decs = prep[:, s0 + 8 * SH:s0 + 10 * SH, :, 0].reshape(B, 2, SH, nc).transpose(0, 3, 1, 2)
    decs = jnp.repeat(decs, SSM_HEADDIM, axis=3)

    y_ml = _mlstm(pa3, kt5, mg, mc, ml_norm_w.astype(F32).reshape(1, ML_WIDTH))

    Lc = next(t for t in (1024, 512, CHUNK) if S % t == 0)
    conv_w8 = jnp.concatenate([conv_w.astype(F32), jnp.zeros((8 - CONV_K, SSM_CONV_DIM), F32)], axis=0)
    xc = _conv(pa3, conv_w8, conv_b.astype(F32).reshape(1, SSM_CONV_DIM), Lc)

    cpb = 2 if nc % 2 == 0 else 1
    hb = _ssd_bwd(xc, tb, decs, cpb)
    dsk = jnp.repeat(d_skip.astype(F32), SSM_HEADDIM).reshape(1, SSM_WIDTH)
    y_ssm = _ssd_main(xc, pa3, tb, decs, hb, dsk, ssm_norm_w.astype(F32).reshape(1, SSM_WIDTH), cpb)
    return y_ml.reshape(T, ML_WIDTH), y_ssm.reshape(T, SSM_WIDTH)


def _moe_dispatch(idx, rank, counts, T, blk):
    E = N_EXPERTS
    M = T * TOP_K
    nb = (M + E * (blk - 1)) // blk
    pcounts = (counts + blk - 1) // blk * blk
    pends = jnp.cumsum(pcounts)
    pstarts = (pends - pcounts).astype(jnp.int32)
    dest = _dest(pstarts, idx, rank)
    block_start = jnp.arange(nb, dtype=jnp.int32) * blk
    block_expert = jnp.minimum(jnp.sum(pends[None, :] <= block_start[:, None], axis=1), E - 1).astype(jnp.int32)
    n_used = (pends[-1] // blk).astype(jnp.int32).reshape(1)
    return dest, nb * blk, block_expert, n_used


def _layer(x, w_in, ml_i_bias, ml_f_bias, ml_norm_w, conv_w, conv_b, dt_bias, a_log, d_skip, ssm_norm_w,
           w_out, ln1_g, ln1_b, w_router, router_bias, w_exp_gate, w_exp_up, w_exp_down,
           w_sh_gate, w_sh_up, w_sh_down, ln2_g, ln2_b):
    B, S, D = x.shape
    T = B * S
    y_ml, y_ssm = _mixers(x, w_in, ml_i_bias, ml_f_bias, ml_norm_w, conv_w, conv_b, dt_bias, a_log, d_skip,
                          ssm_norm_w)
    w_r = jnp.concatenate([w_router, jnp.zeros((D, 128 - N_EXPERTS), w_router.dtype)], axis=1).astype(BF16)
    h, hp_a, hp_b, scores = _outproj(y_ml, y_ssm, x.reshape(T, D), w_out.astype(BF16),
                                     ln1_g.astype(F32).reshape(1, D), ln1_b.astype(F32).reshape(1, D), w_r)
    rbias = jnp.broadcast_to(router_bias.astype(F32)[:, None], (N_EXPERTS, 128))
    idx, wgt, rank, cnt = _route(scores, rbias)
    counts = cnt[:, 0].astype(jnp.int32)
    dest, n_slots, block_expert, n_used = _moe_dispatch(idx, rank, counts, T, MOE_BLOCK)
    dest_flat = dest.reshape(-1)
    xs_a = _sc_scatter_rows(hp_a, dest, n_slots)
    xs_b = _sc_scatter_rows(hp_b, dest, n_slots)
    os_a, os_b = _experts(xs_a, xs_b, block_expert, n_used, w_exp_gate, w_exp_up, w_exp_down)
    gk_a = _sc_gather_rows(os_a, dest_flat).reshape(TOP_K, T, D // 4)
    gk_b = _sc_gather_rows(os_b, dest_flat).reshape(TOP_K, T, D // 4)
    out = _final(h, gk_a, gk_b, wgt.T, w_sh_gate.astype(BF16), w_sh_up.astype(BF16),
                 w_sh_down.astype(BF16), ln2_g.astype(F32).reshape(1, D), ln2_b.astype(F32).reshape(1, D))
    return out.reshape(B, S, D)


def kernel(x, w_in, ml_i_bias, ml_f_bias, ml_norm_w, conv_w, conv_b, dt_bias, a_log, d_skip, ssm_norm_w, w_out,
           ln1_g, ln1_b, w_router, router_bias, w_exp_gate, w_exp_up, w_exp_down, w_sh_gate, w_sh_up, w_sh_down,
           ln2_g, ln2_b):
    for l in range(w_in.shape[0]):
        x = _layer(x, w_in[l], ml_i_bias[l], ml_f_bias[l], ml_norm_w[l], conv_w[l], conv_b[l], dt_bias[l],
                   a_log[l], d_skip[l], ssm_norm_w[l], w_out[l], ln1_g[l], ln1_b[l], w_router[l],
                   router_bias[l], w_exp_gate[l], w_exp_up[l], w_exp_down[l], w_sh_gate[l], w_sh_up[l],
                   w_sh_down[l], ln2_g[l], ln2_b[l])
    return x
```

```python
import functools

import jax
import jax.numpy as jnp
from jax import lax
from jax.experimental import pallas as pl
from jax.experimental.pallas import tpu as pltpu
from jax.experimental.pallas import tpu_sc as plsc

D_MODEL = 1024
ML_HEADS = 4
ML_DV = 256
ML_DQK = 128
ML_QK = ML_HEADS * ML_DQK
ML_WIDTH = ML_HEADS * ML_DV
SSM_WIDTH = 1024
SSM_HEADDIM = 64
SSM_HEADS = 16
SSM_GROUPS = 4
SSM_STATE = 128
SSM_CONV_DIM = SSM_WIDTH + 2 * SSM_GROUPS * SSM_STATE
CONV_K = 5
CHUNK = 128
N_EXPERTS = 64
TOP_K = 8
N_EXPERT_GROUPS = 8
TOPK_GROUPS = 4
D_EXPERT = 256
ROUTED_SCALE = 2.5
LN_EPS = 1e-5
RMS_EPS = 1e-5
DEPTH = 1
ALPHA = (2 * DEPTH) ** 0.25

PA_XBC0 = 0
PA_V0 = PA_XBC0 + SSM_CONV_DIM
PA_O0 = PA_V0 + ML_WIDTH
PA_Z0 = PA_O0 + ML_WIDTH
PA_Q0 = PA_Z0 + SSM_WIDTH
PA_COLS = PA_Q0 + ML_QK
PB_COLS = 128
N_GATE_ROWS = 2 * ML_HEADS + 2 * ML_HEADS + 2 * SSM_HEADS
PREP_SSM0 = 32
N_PREP_ROWS = PREP_SSM0 + 10 * SSM_HEADS

MOE_BLOCK = 1024
SC_WINDOW = 128
VMEM_LIMIT = 56 * 1024 * 1024

BF16 = jnp.bfloat16
F32 = jnp.float32
NEG_INF = float("-inf")


def _cparams(sem, vmem=VMEM_LIMIT):
    return pltpu.CompilerParams(dimension_semantics=sem, vmem_limit_bytes=vmem)


def _dot(a, b):
    return jnp.dot(a, b, preferred_element_type=F32)


def _dot_nt(a, b):
    return lax.dot_general(a, b, (((1,), (1,)), ((), ())), preferred_element_type=F32)


def _dot_tn(a, b):
    return lax.dot_general(a, b, (((0,), (0,)), ((), ())), preferred_element_type=F32)


def _sigmoid(x):
    return 1.0 / (1.0 + jnp.exp(-x))


def _softplus(x):
    return jnp.maximum(x, 0.0) + jnp.log1p(jnp.exp(-jnp.abs(x)))


def _proj_kernel(x_ref, wb_ref, ws_ref, wkt_ref, oa_ref, ob_ref, kt_ref, xb_scr, *, L):
    @pl.when(pl.program_id(1) == 0)
    def _():
        xb = x_ref[...].astype(BF16)
        xb_scr[...] = xb
        ob_ref[...] = _dot(xb, ws_ref[...])
        kt = _dot_nt(wkt_ref[...], xb).astype(kt_ref.dtype)
        for c in range(kt_ref.shape[0]):
            kt_ref[c] = kt[:, c * L:(c + 1) * L]

    oa_ref[...] = _dot(xb_scr[...], wb_ref[...]).astype(oa_ref.dtype)


def _proj(x2d, w_big, w_small, w_kt, tm=1024, tn=2816):
    T, D = x2d.shape
    N = w_big.shape[1]
    tm = min(tm, T)
    L = CHUNK
    return pl.pallas_call(
        functools.partial(_proj_kernel, L=L),
        grid=(T // tm, N // tn),
        in_specs=[
            pl.BlockSpec((tm, D), lambda i, j: (i, 0)),
            pl.BlockSpec((D, tn), lambda i, j: (0, j)),
            pl.BlockSpec((D, PB_COLS), lambda i, j: (0, 0)),
            pl.BlockSpec((ML_QK, D), lambda i, j: (0, 0)),
        ],
        out_specs=[
            pl.BlockSpec((tm, tn), lambda i, j: (i, j)),
            pl.BlockSpec((tm, PB_COLS), lambda i, j: (i, 0)),
            pl.BlockSpec((tm // L, ML_QK, L), lambda i, j: (i, 0, 0)),
        ],
        out_shape=[jax.ShapeDtypeStruct((T, N), BF16), jax.ShapeDtypeStruct((T, PB_COLS), F32),
                   jax.ShapeDtypeStruct((T // L, ML_QK, L), BF16)],
        scratch_shapes=[pltpu.VMEM((tm, D), BF16)],
        compiler_params=_cparams(("parallel", "arbitrary")),
        name="in_proj",
    )(x2d, w_big, w_small, w_kt)


def _split3_dot(x, u):
    hi = x.astype(BF16)
    r1 = x - hi.astype(F32)
    mid = r1.astype(BF16)
    lo = (r1 - mid.astype(F32)).astype(BF16)
    return _dot(hi, u) + _dot(mid, u) + _dot(lo, u)


def _gate_kernel(g_ref, bias_ref, aneg_ref, o_ref, *, nc, L):
    g = g_ref[0] + bias_ref[...]
    nh = ML_HEADS
    ig = g[0:2 * nh]
    fpre = g[2 * nh:4 * nh]
    lf = -_softplus(-fpre)
    dt = _softplus(g[4 * nh:])
    dA = dt * aneg_ref[...]
    r = lax.broadcasted_iota(jnp.int32, (L, L), 0)
    c = lax.broadcasted_iota(jnp.int32, (L, L), 1)
    u_pre = (r <= c).astype(BF16)
    u_suf = (r >= c).astype(BF16)
    nf = nh + SSM_HEADS
    xf = jnp.concatenate([lf[0:nh], dA[0:SSM_HEADS]], axis=0).reshape(nf * nc, L)
    xb = jnp.concatenate([lf[nh:], dA[SSM_HEADS:]], axis=0).reshape(nf * nc, L)
    cf = _split3_dot(xf, u_pre).reshape(nf, nc, L)
    cb = _split3_dot(xb, u_suf).reshape(nf, nc, L)
    totf = jnp.broadcast_to(cf[:, :, L - 1:L], cf.shape)
    totb = jnp.broadcast_to(cb[:, :, 0:1], cb.shape)
    o_ref[0, 0:nh] = cf[0:nh]
    o_ref[0, nh:2 * nh] = cb[0:nh]
    o_ref[0, 2 * nh:4 * nh] = ig
    o_ref[0, 4 * nh:5 * nh] = totf[0:nh] - cf[0:nh] + ig[0:nh]
    o_ref[0, 5 * nh:6 * nh] = totb[0:nh] - cb[0:nh] + ig[nh:]
    lane = lax.broadcasted_iota(jnp.int32, (nh * nc, L), 1)
    pf = (ig[0:nh] - cf[0:nh]).reshape(nh * nc, L)
    pb = (ig[nh:] - cb[0:nh]).reshape(nh * nc, L)
    sh = 1
    while sh < L:
        pf = jnp.where(lane >= sh, jnp.maximum(pf, pltpu.roll(pf, sh, 1)), pf)
        pb = jnp.where(lane < L - sh, jnp.maximum(pb, pltpu.roll(pb, L - sh, 1)), pb)
        sh *= 2
    o_ref[0, 6 * nh:7 * nh] = cf[0:nh] + pf.reshape(nh, nc, L)
    o_ref[0, 7 * nh:8 * nh] = cb[0:nh] + pb.reshape(nh, nc, L)
    H = SSM_HEADS
    af, ab, dtf, dtb = cf[nh:], cb[nh:], dt[0:H], dt[H:]
    s0 = PREP_SSM0
    o_ref[0, s0:s0 + H] = af
    o_ref[0, s0 + H:s0 + 2 * H] = ab
    o_ref[0, s0 + 2 * H:s0 + 3 * H] = af - jnp.log(dtf)
    o_ref[0, s0 + 3 * H:s0 + 4 * H] = ab - jnp.log(dtb)
    o_ref[0, s0 + 4 * H:s0 + 5 * H] = jnp.exp(af)
    o_ref[0, s0 + 5 * H:s0 + 6 * H] = jnp.exp(ab)
    o_ref[0, s0 + 6 * H:s0 + 7 * H] = dtf * jnp.exp(totf[nh:] - af)
    o_ref[0, s0 + 7 * H:s0 + 8 * H] = dtb * jnp.exp(totb[nh:] - ab)
    o_ref[0, s0 + 8 * H:s0 + 9 * H] = jnp.exp(totf[nh:])
    o_ref[0, s0 + 9 * H:s0 + 10 * H] = jnp.exp(totb[nh:])


def _gate_prep(g_rows, bias_rows, aneg_rows):
    B, R, nc, L = g_rows.shape
    return pl.pallas_call(
        functools.partial(_gate_kernel, nc=nc, L=L),
        grid=(B,),
        in_specs=[
            pl.BlockSpec((1, R, nc, L), lambda b: (b, 0, 0, 0)),
            pl.BlockSpec((R, 1, L), lambda b: (0, 0, 0)),
            pl.BlockSpec((2 * SSM_HEADS, 1, L), lambda b: (0, 0, 0)),
        ],
        out_specs=pl.BlockSpec((1, N_PREP_ROWS, nc, L), lambda b: (b, 0, 0, 0)),
        out_shape=jax.ShapeDtypeStruct((B, N_PREP_ROWS, nc, L), F32),
        compiler_params=_cparams(("parallel",)),
        name="gate_prep",
    )(g_rows, bias_rows, aneg_rows)


def _mlstm_kernel(q_ref, kt_ref, v_ref, o_ref, mg_ref, mc_ref, nw_ref, y_ref, park_scr, c_scr, n_scr, m_scr,
                  *, nc, L, nh):
    row_i = lax.broadcasted_iota(jnp.int32, (L, L), 0)
    col_i = lax.broadcasted_iota(jnp.int32, (L, L), 1)
    ones = jnp.ones((L, L), BF16)

    c_scr[...] = jnp.zeros_like(c_scr)
    n_scr[...] = jnp.zeros_like(n_scr)
    m_scr[...] = jnp.full_like(m_scr, NEG_INF)

    def chunk(c, d, hh):
        c_ref, n_ref, m_ref = c_scr.at[2 * hh + d], n_scr.at[2 * hh + d], m_scr.at[2 * hh + d]
        c_prev, n_prev, m_prev = c_ref[...], n_ref[...], m_ref[...]
        sl = pl.ds(pl.multiple_of(c * L, L), L)
        q = q_ref[0, sl, hh * ML_DQK:(hh + 1) * ML_DQK]
        kt = kt_ref[0, c, hh]
        vo = jnp.concatenate([v_ref[0, sl, hh * ML_DV:(hh + 1) * ML_DV], ones], axis=1)
        g8 = mg_ref[0, hh, c]
        ct = mc_ref[0, hh, sl, :]
        b_r, i_r, a_r = g8[2 * d:2 * d + 1], g8[2 * d + 1:2 * d + 2], g8[4 + d:5 + d]
        b_c = jnp.broadcast_to(ct[:, 2 * d:2 * d + 1], (L, L))
        mi_c = jnp.broadcast_to(ct[:, 6 + d:7 + d], (L, L))
        if d == 0:
            gtot = b_r[:, L - 1:L]
            mask = col_i <= row_i
        else:
            gtot = b_r[:, 0:1]
            mask = col_i >= row_i
        m_loc = jnp.max(a_r, axis=1, keepdims=True)
        kwt = (kt.astype(F32) * jnp.exp(a_r - m_loc)).astype(BF16)
        uu = _dot(kwt, vo)
        qq = _dot(q, jnp.concatenate([kt, c_prev.astype(BF16), n_prev.astype(BF16)], axis=1))
        s, q_c, q_n = qq[:, 0:L], qq[:, L:L + ML_DV], qq[:, L + ML_DV:]
        dmat = jnp.where(mask, b_c - b_r + i_r, NEG_INF)
        m_inter = b_c + m_prev
        m_t = jnp.maximum(m_inter, mi_c)
        s_mat = s * jnp.exp(dmat - m_t)
        inter_w = jnp.exp(m_inter - m_t)
        ss = _dot(s_mat.astype(BF16), vo)
        den = ss[:, ML_DV:] + inter_w * q_n
        r = 1.0 / jnp.maximum(jnp.abs(den), jnp.exp(-m_t))
        num = ss[:, 0:ML_DV] + jnp.concatenate([inter_w, inter_w], axis=1) * q_c
        h = num * jnp.concatenate([r, r], axis=1)
        m_new = jnp.maximum(gtot + m_prev, m_loc)
        decay = jnp.exp(gtot + m_prev - m_new)
        sc = jnp.exp(m_loc - m_new)
        c_ref[...] = decay * c_prev + sc * uu[:, 0:ML_DV]
        n_ref[...] = decay * n_prev + sc * uu[:, ML_DV:]
        m_ref[...] = m_new
        return sl, h

    def finish(sl, hh, h):
        cols = slice(hh * ML_DV, (hh + 1) * ML_DV)
        mu = jnp.mean(h, axis=1, keepdims=True)
        d = h - mu
        var = jnp.mean(d * d, axis=1, keepdims=True)
        y = d * lax.rsqrt(var + LN_EPS) * nw_ref[:, cols]
        y = y * _sigmoid(o_ref[0, sl, cols].astype(F32))
        y_ref[0, sl, cols] = y.astype(y_ref.dtype)

    def first_half(i, carry):
        for hh in range(nh):
            for d, c in ((0, i), (1, nc - 1 - i)):
                sl, h = chunk(c, d, hh)
                park_scr[hh, sl, :] = h
        return carry

    def second_half(i, carry):
        for hh in range(nh):
            for d, c in ((0, i), (1, nc - 1 - i)):
                sl, h = chunk(c, d, hh)
                finish(sl, hh, h + park_scr[hh, sl, :])
        return carry

    half = nc // 2
    lax.fori_loop(0, half, first_half, 0, unroll=4 // nh)
    lax.fori_loop(half, nc, second_half, 0, unroll=4 // nh)


def _mlstm(pa3, kt5, mg, mc, norm_w, nh=1):
    B, S, _ = pa3.shape
    nc = S // CHUNK
    H = ML_HEADS
    q_blk, v_blk, o_blk = PA_Q0 // (nh * ML_DQK), PA_V0 // (nh * ML_DV), PA_O0 // (nh * ML_DV)
    return pl.pallas_call(
        functools.partial(_mlstm_kernel, nc=nc, L=CHUNK, nh=nh),
        grid=(B, H // nh),
        in_specs=[
            pl.BlockSpec((1, S, nh * ML_DQK), lambda b, h: (b, 0, q_blk + h)),
            pl.BlockSpec((1, nc, nh, ML_DQK, CHUNK), lambda b, h: (b, 0, h, 0, 0)),
            pl.BlockSpec((1, S, nh * ML_DV), lambda b, h: (b, 0, v_blk + h)),
            pl.BlockSpec((1, S, nh * ML_DV), lambda b, h: (b, 0, o_blk + h)),
            pl.BlockSpec((1, nh, nc, 8, CHUNK), lambda b, h: (b, h, 0, 0, 0)),
            pl.BlockSpec((1, nh, S, 8), lambda b, h: (b, h, 0, 0)),
            pl.BlockSpec((1, nh * ML_DV), lambda b, h: (0, h)),
        ],
        out_specs=pl.BlockSpec((1, S, nh * ML_DV), lambda b, h: (b, 0, h)),
        out_shape=jax.ShapeDtypeStruct((B, S, ML_WIDTH), BF16),
        scratch_shapes=[
            pltpu.VMEM((nh, S, ML_DV), F32),
            pltpu.VMEM((2 * nh, ML_DQK, ML_DV), F32),
            pltpu.VMEM((2 * nh, ML_DQK, CHUNK), F32),
            pltpu.VMEM((2 * nh, 1, 1), F32),
        ],
        compiler_params=_cparams(("parallel", "parallel")),
        name="mlstm",
    )(pa3, kt5, pa3, pa3, mg, mc, norm_w)


CONV_EDGE = 16


def _conv_shift_matrix(L):
    pad, E = CONV_K // 2, CONV_EDGE
    t = jnp.arange(L, dtype=jnp.int32)[:, None]
    blocks = []
    u = jnp.arange(L, dtype=jnp.int32)[None, :]
    for j in range(CONV_K):
        blocks.append(u == t + j - pad)
    v = jnp.arange(E, dtype=jnp.int32)[None, :]
    for j in range(pad):
        blocks.append(v - E == t + j - pad)
    for j in range(pad + 1, CONV_K):
        blocks.append(v + L == t + j - pad)
    return jnp.concatenate(blocks, axis=1).astype(BF16)


def _conv_kernel(x_ref, top_ref, bot_ref, w_ref, b_ref, sh_ref, o_ref, ext_scr, *, Lc, L):
    pad, E = CONV_K // 2, CONV_EDGE
    i = pl.program_id(1)
    first, last = i == 0, i == pl.num_programs(1) - 1
    ext_scr[0:E, :] = jnp.where(first, jnp.zeros_like(top_ref[0]), top_ref[0])
    ext_scr[E:E + Lc, :] = x_ref[0]
    ext_scr[E + Lc:2 * E + Lc, :] = jnp.where(last, jnp.zeros_like(bot_ref[0]), bot_ref[0])
    taps = [w_ref[j:j + 1, :].astype(BF16) for j in range(CONV_K)]
    scaled = lambda rows, j: rows * taps[j]
    for r0 in range(0, Lc, L):
        cen = ext_scr[E + r0:E + r0 + L, :]
        top = ext_scr[r0:r0 + E, :]
        bot = ext_scr[E + r0 + L:2 * E + r0 + L, :]
        pieces = [scaled(cen, j) for j in range(CONV_K)]
        pieces += [scaled(top, j) for j in range(pad)] + [scaled(bot, j) for j in range(pad + 1, CONV_K)]
        acc = _dot(sh_ref[...], jnp.concatenate(pieces, axis=0)) + b_ref[...]
        o_ref[0, r0:r0 + L, :] = (acc * _sigmoid(acc)).astype(o_ref.dtype)


def _conv(pa3, conv_w8, conv_b, Lc):
    B, S, _ = pa3.shape
    C = SSM_CONV_DIM
    xbc_blk = PA_XBC0 // C
    L = CHUNK
    sh = _conv_shift_matrix(L)
    epb = Lc // CONV_EDGE
    n_edge = S // CONV_EDGE
    return pl.pallas_call(
        functools.partial(_conv_kernel, Lc=Lc, L=L),
        grid=(B, S // Lc),
        in_specs=[
            pl.BlockSpec((1, Lc, C), lambda b, i: (b, i, xbc_blk)),
            pl.BlockSpec((1, CONV_EDGE, C), lambda b, i: (b, jnp.maximum(i * epb - 1, 0), xbc_blk)),
            pl.BlockSpec((1, CONV_EDGE, C), lambda b, i: (b, jnp.minimum((i + 1) * epb, n_edge - 1), xbc_blk)),
            pl.BlockSpec((8, C), lambda b, i: (0, 0)),
            pl.BlockSpec((1, C), lambda b, i: (0, 0)),
            pl.BlockSpec(sh.shape, lambda b, i: (0, 0)),
        ],
        out_specs=pl.BlockSpec((1, Lc, C), lambda b, i: (b, i, 0)),
        out_shape=jax.ShapeDtypeStruct((B, S, C), BF16),
        scratch_shapes=[pltpu.VMEM((Lc + 2 * CONV_EDGE, C), BF16)],
        compiler_params=_cparams(("parallel", "parallel")),
        name="ssm_conv",
    )(pa3, pa3, pa3, conv_w8, conv_b, sh)


def _expansion_matrix(src_cols, width):
    src = jnp.repeat(jnp.asarray(src_cols, jnp.int32), width)
    e = (jnp.arange(128, dtype=jnp.int32)[:, None] == src[None, :]).astype(BF16)
    return jnp.concatenate([e, e], axis=0)


def _split2(t):
    hi = t.astype(BF16)
    return jnp.concatenate([hi, (t - hi.astype(F32)).astype(BF16)], axis=1)


def _ssd_bwd_kernel(x_ref, b_ref, tb_ref, dec_ref, ew_ref, hb_ref, h_scr, *, L):
    G, hg, P, N = SSM_GROUPS, SSM_HEADS // SSM_GROUPS, SSM_HEADDIM, SSM_STATE
    GW = hg * P

    @pl.when(pl.program_id(1) == 0)
    def _():
        h_scr[...] = jnp.zeros_like(h_scr)

    for s in reversed(range(x_ref.shape[1] // L)):
        rows = slice(s * L, (s + 1) * L)
        w_all = _dot(_split2(tb_ref[0, s].T), ew_ref[...])
        for g in range(G):
            w = w_all[:, g * GW:(g + 1) * GW]
            xs = (x_ref[0, rows, g * GW:(g + 1) * GW].astype(F32) * w).astype(BF16)
            st = _dot_tn(b_ref[0, rows, g * N:(g + 1) * N], xs)
            h_old = h_scr[g]
            hb_ref[0, s, g] = h_old.astype(hb_ref.dtype)
            h_scr[g] = dec_ref[0, s, 1:2, g * GW:(g + 1) * GW] * h_old + st


def _ssd_bwd(xc, tb, decs, cpb):
    B, S, _ = xc.shape
    nc = S // CHUNK
    nblk = nc // cpb
    R = cpb * CHUNK
    G, N, W = SSM_GROUPS, SSM_STATE, SSM_WIDTH
    return pl.pallas_call(
        functools.partial(_ssd_bwd_kernel, L=CHUNK),
        grid=(B, nblk),
        in_specs=[
            pl.BlockSpec((1, R, W), lambda b, c: (b, nblk - 1 - c, 0)),
            pl.BlockSpec((1, R, G * N), lambda b, c: (b, nblk - 1 - c, W // (G * N))),
            pl.BlockSpec((1, cpb, 128, CHUNK), lambda b, c: (b, nblk - 1 - c, 0, 0)),
            pl.BlockSpec((1, cpb, 2, W), lambda b, c: (b, nblk - 1 - c, 0, 0)),
            pl.BlockSpec((256, W), lambda b, c: (0, 0)),
        ],
        out_specs=pl.BlockSpec((1, cpb, G, N, W // G), lambda b, c: (b, nblk - 1 - c, 0, 0, 0)),
        out_shape=jax.ShapeDtypeStruct((B, nc, G, N, W // G), BF16),
        scratch_shapes=[pltpu.VMEM((G, N, W // G), F32)],
        compiler_params=_cparams(("parallel", "arbitrary")),
        name="ssd_bwd_states",
    )(xc, xc, tb, decs, _expansion_matrix([5 * SSM_HEADS + h for h in range(SSM_HEADS)], SSM_HEADDIM))


def _ssd_main_kernel(x_ref, b_ref, c_ref, z_ref, tb_ref, dec_ref, hb_ref, dsk_ref, nw_ref, ed_ref, ee_ref,
                     y_ref, h_scr, *, L):
    G, hg, P, N = SSM_GROUPS, SSM_HEADS // SSM_GROUPS, SSM_HEADDIM, SSM_STATE
    H = SSM_HEADS
    GW = hg * P

    @pl.when(pl.program_id(1) == 0)
    def _():
        h_scr[...] = jnp.zeros_like(h_scr)

    row_i = lax.broadcasted_iota(jnp.int32, (L, L), 0)
    col_i = lax.broadcasted_iota(jnp.int32, (L, L), 1)
    lower = col_i <= row_i
    upper = col_i >= row_i
    lane_g = lax.broadcasted_iota(jnp.int32, (L, GW), 1)

    for s, g in [(s, g) for s in range(x_ref.shape[1] // L) for g in range(G)]:
        rows = slice(s * L, (s + 1) * L)
        tb = tb_ref[0, s]
        sc2 = _split2(tb.T)
        xg = x_ref[0, rows, g * GW:(g + 1) * GW]
        bg = b_ref[0, rows, g * N:(g + 1) * N]
        cg = c_ref[0, rows, g * N:(g + 1) * N]
        cb = _dot_nt(cg, bg)
        acol = _dot(sc2, ed_ref[g])
        ecol = _dot(sc2, ee_ref[g])
        ms, xms = [], []
        for j in range(hg):
            hd = g * hg + j
            rf_r, rb_r = tb[6 * H + hd:6 * H + hd + 1], tb[7 * H + hd:7 * H + hd + 1]
            af_c, ab_c = acol[:, j * L:(j + 1) * L], acol[:, (hg + j) * L:(hg + j + 1) * L]
            fm = jnp.exp(jnp.where(lower, af_c - rf_r, NEG_INF))
            bm = jnp.exp(jnp.where(upper, ab_c - rb_r, NEG_INF))
            ms.append((cb * (fm + bm)).astype(BF16))
            xms.append(jnp.where((lane_g >= j * P) & (lane_g < (j + 1) * P), xg, jnp.zeros_like(xg)))
        efc, ebc, wfc = ecol[:, 0:GW], ecol[:, GW:2 * GW], ecol[:, 2 * GW:]
        y = _dot(jnp.concatenate(ms, axis=1), jnp.concatenate(xms, axis=0))
        h_old = h_scr[g]
        y = y + efc * _dot(cg, h_old.astype(BF16))
        y = y + ebc * _dot(cg, hb_ref[0, s, g])
        xgf = xg.astype(F32)
        y = y + dsk_ref[:, g * GW:(g + 1) * GW] * xgf
        xs = (xgf * wfc).astype(BF16)
        h_scr[g] = dec_ref[0, s, 0:1, g * GW:(g + 1) * GW] * h_old + _dot_tn(bg, xs)
        zg = z_ref[0, rows, g * GW:(g + 1) * GW].astype(F32)
        y = y * (zg * _sigmoid(zg))
        y = y * lax.rsqrt(jnp.mean(y * y, axis=1, keepdims=True) + RMS_EPS)
        y_ref[0, rows, g * GW:(g + 1) * GW] = (y * nw_ref[:, g * GW:(g + 1) * GW]).astype(y_ref.dtype)


def _ssd_main(xc, pa3, tb, decs, hb, dskip_row, norm_w, cpb):
    B, S, _ = xc.shape
    nc = S // CHUNK
    R = cpb * CHUNK
    G, N, W = SSM_GROUPS, SSM_STATE, SSM_WIDTH
    H, hg, P = SSM_HEADS, SSM_HEADS // SSM_GROUPS, SSM_HEADDIM
    z_blk = PA_Z0 // W
    e_dec = jnp.stack([_expansion_matrix([q * H + g * hg + j for q in (0, 1) for j in range(hg)], CHUNK)
                       for g in range(G)])
    e_ew = jnp.stack([_expansion_matrix([q * H + g * hg + j for q in (2, 3, 4) for j in range(hg)], P)
                      for g in range(G)])
    return pl.pallas_call(
        functools.partial(_ssd_main_kernel, L=CHUNK),
        grid=(B, nc // cpb),
        in_specs=[
            pl.BlockSpec((1, R, W), lambda b, c: (b, c, 0)),
            pl.BlockSpec((1, R, G * N), lambda b, c: (b, c, W // (G * N))),
            pl.BlockSpec((1, R, G * N), lambda b, c: (b, c, W // (G * N) + 1)),
            pl.BlockSpec((1, R, W), lambda b, c: (b, c, z_blk)),
            pl.BlockSpec((1, cpb, 128, CHUNK), lambda b, c: (b, c, 0, 0)),
            pl.BlockSpec((1, cpb, 2, W), lambda b, c: (b, c, 0, 0)),
            pl.BlockSpec((1, cpb, G, N, W // G), lambda b, c: (b, c, 0, 0, 0)),
            pl.BlockSpec((1, W), lambda b, c: (0, 0)),
            pl.BlockSpec((1, W), lambda b, c: (0, 0)),
            pl.BlockSpec((G, 256, 2 * hg * CHUNK), lambda b, c: (0, 0, 0)),
            pl.BlockSpec((G, 256, 3 * W // G), lambda b, c: (0, 0, 0)),
        ],
        out_specs=pl.BlockSpec((1, R, W), lambda b, c: (b, c, 0)),
        out_shape=jax.ShapeDtypeStruct((B, S, W), BF16),
        scratch_shapes=[pltpu.VMEM((G, N, W // G), F32)],
        compiler_params=_cparams(("parallel", "arbitrary")),
        name="ssd_main",
    )(xc, xc, xc, pa3, tb, decs, hb, dskip_row, norm_w, e_dec, e_ew)


def _pack_bf16_pairs(v):
    n = v.shape[1] // 2
    hi = lax.bitcast_convert_type(v[:, :n].astype(BF16).astype(F32), jnp.uint32)
    lo = lax.bitcast_convert_type(v[:, n:].astype(BF16).astype(F32), jnp.uint32)
    return (hi & jnp.uint32(0xFFFF0000)) | (lo >> 16)


def _unpack_bf16_pairs(u):
    hi = lax.bitcast_convert_type(u & jnp.uint32(0xFFFF0000), F32)
    lo = lax.bitcast_convert_type(u << 16, F32)
    return hi, lo


def _layer_norm(v, g, b):
    mu = jnp.mean(v, axis=1, keepdims=True)
    d = v - mu
    var = jnp.mean(d * d, axis=1, keepdims=True)
    return d * lax.rsqrt(var + LN_EPS) * g + b


def _outproj_kernel(yml_ref, yssm_ref, x_ref, wo_ref, g_ref, b_ref, wr_ref, h_ref, hpa_ref, hpb_ref, sc_ref):
    mix = _dot(yml_ref[...], wo_ref[0:ML_WIDTH, :]) + _dot(yssm_ref[...], wo_ref[ML_WIDTH:, :])
    h = _layer_norm(ALPHA * x_ref[...] + mix, g_ref[...], b_ref[...])
    h_ref[...] = h
    hp = _pack_bf16_pairs(h)
    q = hp.shape[1] // 2
    hpa_ref[...] = hp[:, :q]
    hpb_ref[...] = hp[:, q:]
    sc_ref[...] = _sigmoid(_dot(h.astype(BF16), wr_ref[...]))


def _outproj(yml, yssm, x2d, w_out, ln_g, ln_b, w_router, tm=1024):
    T, D = x2d.shape
    tm = min(tm, T)
    row = lambda i: (i, 0)
    fixed = lambda i: (0, 0)
    return pl.pallas_call(
        _outproj_kernel,
        grid=(T // tm,),
        in_specs=[
            pl.BlockSpec((tm, ML_WIDTH), row),
            pl.BlockSpec((tm, SSM_WIDTH), row),
            pl.BlockSpec((tm, D), row),
            pl.BlockSpec((ML_WIDTH + SSM_WIDTH, D), fixed),
            pl.BlockSpec((1, D), fixed),
            pl.BlockSpec((1, D), fixed),
            pl.BlockSpec((D, 128), fixed),
        ],
        out_specs=[pl.BlockSpec((tm, D), row), pl.BlockSpec((tm, D // 4), row), pl.BlockSpec((tm, D // 4), row),
                   pl.BlockSpec((tm, 128), row)],
        out_shape=[
            jax.ShapeDtypeStruct((T, D), F32),
            jax.ShapeDtypeStruct((T, D // 4), jnp.uint32),
            jax.ShapeDtypeStruct((T, D // 4), jnp.uint32),
            jax.ShapeDtypeStruct((T, 128), F32),
        ],
        compiler_params=_cparams(("parallel",)),
        name="out_proj_ln1_router",
    )(yml, yssm, x2d, w_out, ln_g, ln_b, w_router)


def _route_kernel(sc_ref, rb_ref, idx_ref, w_ref, rank_ref, cnt_ref, carry_scr, *, Tt):
    E, NG = N_EXPERTS, N_EXPERT_GROUPS
    EG = E // NG

    @pl.when(pl.program_id(0) == 0)
    def _():
        carry_scr[...] = jnp.zeros_like(carry_scr)

    s = sc_ref[...].T[0:E]
    s3 = s.reshape(NG, EG, Tt)
    ch3 = (s + rb_ref[:, 0:1]).reshape(NG, EG, Tt)
    e_in = lax.broadcasted_iota(jnp.int32, (NG, EG, Tt), 1)
    g_in = lax.broadcasted_iota(jnp.int32, (NG, EG, Tt), 0)
    f_i = g_in * EG + e_in
    m1 = jnp.max(ch3, axis=1, keepdims=True)
    i1 = jnp.min(jnp.where(ch3 == m1, e_in, EG), axis=1, keepdims=True)
    m2 = jnp.max(jnp.where(e_in == i1, NEG_INF, ch3), axis=1, keepdims=True)
    gs = m1 + m2
    g_i = lax.broadcasted_iota(jnp.int32, (NG, 1, Tt), 0)
    gsel = jnp.zeros((NG, 1, Tt), jnp.bool_)
    cur = gs
    for _ in range(TOPK_GROUPS):
        mx = jnp.max(cur, axis=0, keepdims=True)
        pick = jnp.min(jnp.where(cur == mx, g_i, NG), axis=0, keepdims=True)
        hit = g_i == pick
        gsel = gsel | hit
        cur = jnp.where(hit, NEG_INF, cur)
    cur = jnp.where(gsel, ch3, NEG_INF)
    selmask = jnp.zeros((NG, EG, Tt), jnp.bool_)
    idxs, ws = [], []
    for _ in range(TOP_K):
        mx = jnp.max(jnp.max(cur, axis=1, keepdims=True), axis=0, keepdims=True)
        cand = jnp.where(cur == mx, f_i, E)
        pick = jnp.min(jnp.min(cand, axis=1, keepdims=True), axis=0, keepdims=True)
        hit = f_i == pick
        wk = jnp.sum(jnp.sum(jnp.where(hit, s3, 0.0), axis=1, keepdims=True), axis=0, keepdims=True)
        idxs.append(pick.reshape(1, Tt))
        ws.append(wk.reshape(1, Tt))
        selmask = selmask | hit
        cur = jnp.where(hit, NEG_INF, cur)
    wsum = ws[0]
    for k in range(1, TOP_K):
        wsum = wsum + ws[k]
    r = lax.broadcasted_iota(jnp.int32, (Tt, Tt), 0)
    c = lax.broadcasted_iota(jnp.int32, (Tt, Tt), 1)
    ustrict = (r < c).astype(BF16)
    self32 = jnp.where(selmask, 1.0, 0.0).reshape(E, Tt)
    excl = _dot(self32.astype(BF16), ustrict) + carry_scr[:, 0:1]
    excl3 = excl.reshape(NG, EG, Tt)
    for k in range(TOP_K):
        hit = f_i == idxs[k].reshape(1, 1, Tt)
        rk = jnp.sum(jnp.sum(jnp.where(hit, excl3, 0.0), axis=1, keepdims=True), axis=0, keepdims=True)
        idx_ref[k:k + 1, :] = idxs[k]
        w_ref[k:k + 1, :] = ws[k] / wsum * ROUTED_SCALE
        rank_ref[k:k + 1, :] = rk.reshape(1, Tt).astype(jnp.int32)
    carry_scr[...] = carry_scr[...] + jnp.sum(self32, axis=1, keepdims=True)
    cnt_ref[...] = carry_scr[...]


def _route(scores, rbias, Tt=512):
    T = scores.shape[0]
    kt = lambda i: (0, i)
    return pl.pallas_call(
        functools.partial(_route_kernel, Tt=Tt),
        grid=(T // Tt,),
        in_specs=[pl.BlockSpec((Tt, 128), lambda i: (i, 0)), pl.BlockSpec((N_EXPERTS, 128), lambda i: (0, 0))],
        out_specs=[
            pl.BlockSpec((TOP_K, Tt), kt),
            pl.BlockSpec((TOP_K, Tt), kt),
            pl.BlockSpec((TOP_K, Tt), kt),
            pl.BlockSpec((N_EXPERTS, 128), lambda i: (0, 0)),
        ],
        out_shape=[
            jax.ShapeDtypeStruct((TOP_K, T), jnp.int32),
            jax.ShapeDtypeStruct((TOP_K, T), F32),
            jax.ShapeDtypeStruct((TOP_K, T), jnp.int32),
            jax.ShapeDtypeStruct((N_EXPERTS, 128), F32),
        ],
        scratch_shapes=[pltpu.VMEM((N_EXPERTS, 128), F32)],
        compiler_params=_cparams(("arbitrary",)),
        name="route_topk",
    )(scores, rbias)


def _dest_kernel(ps_ref, idx_ref, rank_ref, o_ref):
    idx = idx_ref[...]
    start = jnp.zeros(idx.shape, jnp.int32)
    for e in range(N_EXPERTS):
        start = jnp.where(idx == e, ps_ref[e], start)
    o_ref[...] = start + rank_ref[...]


def _dest(pstarts, idx, rank, Tt=2048):
    K, T = idx.shape
    Tt = min(Tt, T)
    blk = pl.BlockSpec((K, Tt), lambda i, ps: (0, i))
    return pl.pallas_call(
        _dest_kernel,
        grid_spec=pltpu.PrefetchScalarGridSpec(num_scalar_prefetch=1, grid=(T // Tt,), in_specs=[blk, blk],
                                               out_specs=blk),
        out_shape=jax.ShapeDtypeStruct((K, T), jnp.int32),
        compiler_params=_cparams(("parallel",)),
        name="dispatch_slots",
    )(pstarts, idx, rank)


def _sc_mesh():
    return plsc.VectorSubcoreMesh(core_axis_name="c", subcore_axis_name="s")


def _sc_scatter_rows(src, idx, n_out, win=SC_WINDOW):
    K, T = idx.shape
    W = src.shape[1]

    @functools.partial(pl.kernel, out_type=jax.ShapeDtypeStruct((n_out, W), src.dtype), mesh=_sc_mesh(),
                       scratch_types=[], name="sc_scatter_rows")
    def k(x_hbm, i_hbm, o_hbm):
        def body(x_vmem, i_vmem):
            for j in range(K):
                pltpu.sync_copy(x_vmem, o_hbm.at[i_vmem.at[j]])

        pltpu.emit_pipeline(
            body,
            grid=(T // win,),
            in_specs=[pl.BlockSpec((win, W), lambda i: (i, 0)), pl.BlockSpec((K, win), lambda i: (0, i))],
            out_specs=[],
            core_axis_name=("c", "s"),
            dimension_semantics=(pltpu.PARALLEL,),
        )(x_hbm, i_hbm)

    return k(src, idx)


def _sc_gather_rows(table, idx, win=SC_WINDOW):
    n = idx.shape[0]
    W = table.shape[1]

    @functools.partial(pl.kernel, out_type=jax.ShapeDtypeStruct((n, W), table.dtype), mesh=_sc_mesh(),
                       scratch_types=[], name="sc_gather_rows")
    def k(t_hbm, i_hbm, o_hbm):
        def body(i_vmem, o_vmem):
            pltpu.sync_copy(t_hbm.at[i_vmem.at[0]], o_vmem)

        pltpu.emit_pipeline(
            body,
            grid=(n // win,),
            in_specs=[pl.BlockSpec((1, win), lambda i: (0, i))],
            out_specs=[pl.BlockSpec((win, W), lambda i: (i, 0))],
            core_axis_name=("c", "s"),
            dimension_semantics=(pltpu.PARALLEL,),
        )(i_hbm, o_hbm)

    return k(table, idx.reshape(1, n))


def _expert_kernel(be_ref, nu_ref, xa_ref, xb_ref, wg_ref, wu_ref, wd_ref, oa_ref, ob_ref, wg_s, wu_s, wd_s):
    half = D_MODEL // 2
    i = pl.program_id(0)
    active = i < nu_ref[0]

    @pl.when(active & ((i == 0) | (be_ref[i] != be_ref[jnp.maximum(i - 1, 0)])))
    def _():
        wg_s[...] = wg_ref[0].astype(BF16)
        wu_s[...] = wu_ref[0].astype(BF16)
        wd_s[...] = wd_ref[0].astype(BF16)

    @pl.when(active)
    def _():
        hi, lo = _unpack_bf16_pairs(jnp.concatenate([xa_ref[...], xb_ref[...]], axis=1))
        x = jnp.concatenate([hi.astype(BF16), lo.astype(BF16)], axis=1)
        gt = _dot(x, wg_s[...])
        up = _dot(x, wu_s[...])
        hmid = (gt * _sigmoid(gt) * up).astype(BF16)
        packed = _pack_bf16_pairs(_dot(hmid, wd_s[...]))
        q = packed.shape[1] // 2
        oa_ref[...] = packed[:, :q]
        ob_ref[...] = packed[:, q:]

    @pl.when(jnp.logical_not(active))
    def _():
        oa_ref[...] = jnp.zeros_like(oa_ref)
        ob_ref[...] = jnp.zeros_like(ob_ref)


def _experts(xs_a, xs_b, block_expert, n_used, w_gate, w_up, w_down, blk=MOE_BLOCK):
    P, quarter = xs_a.shape
    nb = P // blk
    D, F = D_MODEL, D_EXPERT
    cur = lambda i, be, nu: jnp.minimum(i, nu[0] - 1)
    return pl.pallas_call(
        _expert_kernel,
        grid_spec=pltpu.PrefetchScalarGridSpec(
            num_scalar_prefetch=2,
            grid=(nb,),
            in_specs=[
                pl.BlockSpec((blk, quarter), lambda i, be, nu: (cur(i, be, nu), 0)),
                pl.BlockSpec((blk, quarter), lambda i, be, nu: (cur(i, be, nu), 0)),
                pl.BlockSpec((1, D, F), lambda i, be, nu: (be[cur(i, be, nu)], 0, 0)),
                pl.BlockSpec((1, D, F), lambda i, be, nu: (be[cur(i, be, nu)], 0, 0)),
                pl.BlockSpec((1, F, D), lambda i, be, nu: (be[cur(i, be, nu)], 0, 0)),
            ],
            out_specs=[pl.BlockSpec((blk, quarter), lambda i, be, nu: (i, 0)),
                       pl.BlockSpec((blk, quarter), lambda i, be, nu: (i, 0))],
            scratch_shapes=[pltpu.VMEM((D, F), BF16), pltpu.VMEM((D, F), BF16), pltpu.VMEM((F, D), BF16)],
        ),
        out_shape=[jax.ShapeDtypeStruct((P, quarter), jnp.uint32), jax.ShapeDtypeStruct((P, quarter), jnp.uint32)],
        compiler_params=_cparams(("arbitrary",)),
        name="expert_mlp",
    )(block_expert, n_used, xs_a, xs_b, w_gate, w_up, w_down)


def _final_kernel(h_ref, ga_ref, gb_ref, wt_ref, wsg_ref, wsu_ref, wsd_ref, g_ref, b_ref, o_ref):
    h = h_ref[...]
    hb = h.astype(BF16)
    gt = _dot(hb, wsg_ref[...])
    up = _dot(hb, wsu_ref[...])
    shared = _dot((gt * _sigmoid(gt) * up).astype(BF16), wsd_ref[...])
    acc_hi = None
    for k in range(TOP_K):
        hi, lo = _unpack_bf16_pairs(jnp.concatenate([ga_ref[k], gb_ref[k]], axis=1))
        wk = wt_ref[:, k:k + 1]
        acc_hi = wk * hi if acc_hi is None else acc_hi + wk * hi
        acc_lo = wk * lo if k == 0 else acc_lo + wk * lo
    routed = jnp.concatenate([acc_hi, acc_lo], axis=1)
    o_ref[...] = _layer_norm(ALPHA * h + (routed + shared), g_ref[...], b_ref[...])


def _final(h, gk_a, gk_b, wt, ws_gate, ws_up, ws_down, ln_g, ln_b, tm=512):
    T, D = h.shape
    tm = min(tm, T)
    F = D_EXPERT
    fixed = lambda i: (0, 0)
    return pl.pallas_call(
        _final_kernel,
        grid=(T // tm,),
        in_specs=[
            pl.BlockSpec((tm, D), lambda i: (i, 0)),
            pl.BlockSpec((TOP_K, tm, D // 4), lambda i: (0, i, 0)),
            pl.BlockSpec((TOP_K, tm, D // 4), lambda i: (0, i, 0)),
            pl.BlockSpec((tm, TOP_K), lambda i: (i, 0)),
            pl.BlockSpec((D, F), fixed),
            pl.BlockSpec((D, F), fixed),
            pl.BlockSpec((F, D), fixed),
            pl.BlockSpec((1, D), fixed),
            pl.BlockSpec((1, D), fixed),
        ],
        out_specs=pl.BlockSpec((tm, D), lambda i: (i, 0)),
        out_shape=jax.ShapeDtypeStruct((T, D), F32),
        compiler_params=_cparams(("parallel",)),
        name="shared_combine_ln2",
    )(h, gk_a, gk_b, wt, ws_gate, ws_up, ws_down, ln_g, ln_b)


def _mixers(x, w_in, ml_i_bias, ml_f_bias, ml_norm_w, conv_w, conv_b, dt_bias, a_log, d_skip, ssm_norm_w):
    B, S, D = x.shape
    T = B * S
    nc = S // CHUNK
    H = ML_HEADS
    k0, v0 = ML_QK, 2 * ML_QK
    g0 = v0 + 2 * ML_WIDTH
    z0 = g0 + 4 * H
    x0 = z0 + SSM_WIDTH
    d0 = x0 + SSM_CONV_DIM
    w_big = jnp.concatenate([w_in[:, x0:d0], w_in[:, v0:g0], w_in[:, z0:x0], w_in[:, :k0]], axis=1).astype(BF16)
    w_small = jnp.concatenate(
        [w_in[:, g0:z0], w_in[:, d0:], jnp.zeros((D, PB_COLS - N_GATE_ROWS), w_in.dtype)], axis=1).astype(BF16)
    w_kt = (w_in[:, k0:v0] * (ML_DQK ** -0.5)).T.astype(BF16)
    pa, pb, kt = _proj(x.reshape(T, D), w_big, w_small, w_kt)
    pa3 = pa.reshape(B, S, PA_COLS)
    kt5 = kt.reshape(B, nc, H, ML_DQK, CHUNK)

    g_rows = pb.reshape(B, S, PB_COLS)[:, :, :N_GATE_ROWS].transpose(0, 2, 1).reshape(B, N_GATE_ROWS, nc, CHUNK)
    bias = jnp.concatenate([ml_i_bias, ml_f_bias, dt_bias]).astype(F32)
    bias_rows = jnp.broadcast_to(bias[:, None, None], (N_GATE_ROWS, 1, CHUNK))
    aneg_rows = jnp.broadcast_to((-jnp.exp(a_log.astype(F32)))[:, None, None], (2 * SSM_HEADS, 1, CHUNK))
    prep = _gate_prep(g_rows, bias_rows, aneg_rows)
    m8 = jnp.stack([prep[:, 0:H], prep[:, 2 * H:3 * H], prep[:, H:2 * H], prep[:, 3 * H:4 * H],
                    prep[:, 4 * H:5 * H], prep[:, 5 * H:6 * H], prep[:, 6 * H:7 * H], prep[:, 7 * H:8 * H]],
                   axis=2)
    mg = m8.transpose(0, 1, 3, 2, 4)
    mc = m8.transpose(0, 1, 3, 4, 2).reshape(B, H, S, 8)
    s0, SH = PREP_SSM0, SSM_HEADS
    tb = jnp.concatenate([prep[:, s0:s0 + 2 * SH], prep[:, s0 + 4 * SH:s0 + 8 * SH],
                          prep[:, s0 + 2 * SH:s0 + 4 * SH]], axis=1).transpose(0, 2, 1, 3)
    decs = prep[:, s0 + 8 * SH:s0 + 10 * SH, :, 0].reshape(B, 2, SH, nc).transpose(0, 3, 1, 2)
    decs = jnp.repeat(decs, SSM_HEADDIM, axis=3)

    y_ml = _mlstm(pa3, kt5, mg, mc, ml_norm_w.astype(F32).reshape(1, ML_WIDTH))

    Lc = next(t for t in (1024, 512, CHUNK) if S % t == 0)
    conv_w8 = jnp.concatenate([conv_w.astype(F32), jnp.zeros((8 - CONV_K, SSM_CONV_DIM), F32)], axis=0)
    xc = _conv(pa3, conv_w8, conv_b.astype(F32).reshape(1, SSM_CONV_DIM), Lc)

    cpb = next(c for c in (4, 2, 1) if nc % c == 0)
    hb = _ssd_bwd(xc, tb, decs, cpb)
    dsk = jnp.repeat(d_skip.astype(F32), SSM_HEADDIM).reshape(1, SSM_WIDTH)
    y_ssm = _ssd_main(xc, pa3, tb, decs, hb, dsk, ssm_norm_w.astype(F32).reshape(1, SSM_WIDTH), cpb)
    return y_ml.reshape(T, ML_WIDTH), y_ssm.reshape(T, SSM_WIDTH)


def _moe_dispatch(idx, rank, counts, T, blk):
    E = N_EXPERTS
    M = T * TOP_K
    nb = (M + E * (blk - 1)) // blk
    pcounts = (counts + blk - 1) // blk * blk
    pends = jnp.cumsum(pcounts)
    pstarts = (pends - pcounts).astype(jnp.int32)
    dest = _dest(pstarts, idx, rank)
    block_start = jnp.arange(nb, dtype=jnp.int32) * blk
    block_expert = jnp.minimum(jnp.sum(pends[None, :] <= block_start[:, None], axis=1), E - 1).astype(jnp.int32)
    n_used = (pends[-1] // blk).astype(jnp.int32).reshape(1)
    return dest, nb * blk, block_expert, n_used


def _layer(x, w_in, ml_i_bias, ml_f_bias, ml_norm_w, conv_w, conv_b, dt_bias, a_log, d_skip, ssm_norm_w,
           w_out, ln1_g, ln1_b, w_router, router_bias, w_exp_gate, w_exp_up, w_exp_down,
           w_sh_gate, w_sh_up, w_sh_down, ln2_g, ln2_b):
    B, S, D = x.shape
    T = B * S
    y_ml, y_ssm = _mixers(x, w_in, ml_i_bias, ml_f_bias, ml_norm_w, conv_w, conv_b, dt_bias, a_log, d_skip,
                          ssm_norm_w)
    w_r = jnp.concatenate([w_router, jnp.zeros((D, 128 - N_EXPERTS), w_router.dtype)], axis=1).astype(BF16)
    h, hp_a, hp_b, scores = _outproj(y_ml, y_ssm, x.reshape(T, D), w_out.astype(BF16),
                                     ln1_g.astype(F32).reshape(1, D), ln1_b.astype(F32).reshape(1, D), w_r)
    rbias = jnp.broadcast_to(router_bias.astype(F32)[:, None], (N_EXPERTS, 128))
    idx, wgt, rank, cnt = _route(scores, rbias)
    counts = cnt[:, 0].astype(jnp.int32)
    dest, n_slots, block_expert, n_used = _moe_dispatch(idx, rank, counts, T, MOE_BLOCK)
    dest_flat = dest.reshape(-1)
    xs_a = _sc_scatter_rows(hp_a, dest, n_slots)
    xs_b = _sc_scatter_rows(hp_b, dest, n_slots)
    os_a, os_b = _experts(xs_a, xs_b, block_expert, n_used, w_exp_gate, w_exp_up, w_exp_down)
    gk_a = _sc_gather_rows(os_a, dest_flat).reshape(TOP_K, T, D // 4)
    gk_b = _sc_gather_rows(os_b, dest_flat).reshape(TOP_K, T, D // 4)
    out = _final(h, gk_a, gk_b, wgt.T, w_sh_gate.astype(BF16), w_sh_up.astype(BF16),
                 w_sh_down.astype(BF16), ln2_g.astype(F32).reshape(1, D), ln2_b.astype(F32).reshape(1, D))
    return out.reshape(B, S, D)


def kernel(x, w_in, ml_i_bias, ml_f_bias, ml_norm_w, conv_w, conv_b, dt_bias, a_log, d_skip, ssm_norm_w, w_out,
           ln1_g, ln1_b, w_router, router_bias, w_exp_gate, w_exp_up, w_exp_down, w_sh_gate, w_sh_up, w_sh_down,
           ln2_g, ln2_b):
    for l in range(w_in.shape[0]):
        x = _layer(x, w_in[l], ml_i_bias[l], ml_f_bias[l], ml_norm_w[l], conv_w[l], conv_b[l], dt_bias[l],
                   a_log[l], d_skip[l], ssm_norm_w[l], w_out[l], ln1_g[l], ln1_b[l], w_router[l],
                   router_bias[l], w_exp_gate[l], w_exp_up[l], w_exp_down[l], w_sh_gate[l], w_sh_up[l],
                   w_sh_down[l], ln2_g[l], ln2_b[l])
    return x
```

```python
import functools

import jax
import jax.numpy as jnp
from jax import lax
from jax.experimental import pallas as pl
from jax.experimental.pallas import tpu as pltpu
from jax.experimental.pallas import tpu_sc as plsc

D_MODEL = 1024
ML_HEADS = 4
ML_DV = 256
ML_DQK = 128
ML_QK = ML_HEADS * ML_DQK
ML_WIDTH = ML_HEADS * ML_DV
SSM_WIDTH = 1024
SSM_HEADDIM = 64
SSM_HEADS = 16
SSM_GROUPS = 4
SSM_STATE = 128
SSM_CONV_DIM = SSM_WIDTH + 2 * SSM_GROUPS * SSM_STATE
CONV_K = 5
CHUNK = 128
N_EXPERTS = 64
TOP_K = 8
N_EXPERT_GROUPS = 8
TOPK_GROUPS = 4
D_EXPERT = 256
ROUTED_SCALE = 2.5
LN_EPS = 1e-5
RMS_EPS = 1e-5
DEPTH = 1
ALPHA = (2 * DEPTH) ** 0.25

PA_XBC0 = 0
PA_V0 = PA_XBC0 + SSM_CONV_DIM
PA_O0 = PA_V0 + ML_WIDTH
PA_Z0 = PA_O0 + ML_WIDTH
PA_Q0 = PA_Z0 + SSM_WIDTH
PA_COLS = PA_Q0 + ML_QK
PB_COLS = 128
N_GATE_ROWS = 2 * ML_HEADS + 2 * ML_HEADS + 2 * SSM_HEADS
PREP_SSM0 = 32
N_PREP_ROWS = PREP_SSM0 + 10 * SSM_HEADS

MOE_BLOCK = 1024
SC_WINDOW = 128
V7X_VMEM_BYTES = 64 * 1024 * 1024
VMEM_LIMIT = V7X_VMEM_BYTES * 7 // 8

BF16 = jnp.bfloat16
F32 = jnp.float32
NEG_INF = float("-inf")


def _cparams(sem, vmem=VMEM_LIMIT):
    return pltpu.CompilerParams(dimension_semantics=sem, vmem_limit_bytes=vmem)


def _dot(a, b):
    return jnp.dot(a, b, preferred_element_type=F32)


def _dot_nt(a, b):
    return lax.dot_general(a, b, (((1,), (1,)), ((), ())), preferred_element_type=F32)


def _dot_tn(a, b):
    return lax.dot_general(a, b, (((0,), (0,)), ((), ())), preferred_element_type=F32)


def _sigmoid(x):
    return 1.0 / (1.0 + jnp.exp(-x))


def _softplus(x):
    return jnp.maximum(x, 0.0) + jnp.log1p(jnp.exp(-jnp.abs(x)))


def _proj_kernel(x_ref, wb_ref, ws_ref, wkt_ref, oa_ref, ob_ref, kt_ref, xb_scr, *, L):
    @pl.when(pl.program_id(1) == 0)
    def _():
        xb = x_ref[...].astype(BF16)
        xb_scr[...] = xb
        ob_ref[...] = _dot(xb, ws_ref[...])
        kt = _dot_nt(wkt_ref[...], xb).astype(kt_ref.dtype)
        for c in range(kt_ref.shape[0]):
            kt_ref[c] = kt[:, c * L:(c + 1) * L]

    oa_ref[...] = _dot(xb_scr[...], wb_ref[...]).astype(oa_ref.dtype)


def _proj(x2d, w_big, w_small, w_kt, tm=1024, tn=2816):
    T, D = x2d.shape
    N = w_big.shape[1]
    tm = min(tm, T)
    L = CHUNK
    return pl.pallas_call(
        functools.partial(_proj_kernel, L=L),
        grid=(T // tm, N // tn),
        in_specs=[
            pl.BlockSpec((tm, D), lambda i, j: (i, 0)),
            pl.BlockSpec((D, tn), lambda i, j: (0, j)),
            pl.BlockSpec((D, PB_COLS), lambda i, j: (0, 0)),
            pl.BlockSpec((ML_QK, D), lambda i, j: (0, 0)),
        ],
        out_specs=[
            pl.BlockSpec((tm, tn), lambda i, j: (i, j)),
            pl.BlockSpec((tm, PB_COLS), lambda i, j: (i, 0)),
            pl.BlockSpec((tm // L, ML_QK, L), lambda i, j: (i, 0, 0)),
        ],
        out_shape=[jax.ShapeDtypeStruct((T, N), BF16), jax.ShapeDtypeStruct((T, PB_COLS), F32),
                   jax.ShapeDtypeStruct((T // L, ML_QK, L), BF16)],
        scratch_shapes=[pltpu.VMEM((tm, D), BF16)],
        compiler_params=_cparams(("parallel", "arbitrary")),
        name="in_proj",
    )(x2d, w_big, w_small, w_kt)


def _split3_dot(x, u):
    hi = x.astype(BF16)
    r1 = x - hi.astype(F32)
    mid = r1.astype(BF16)
    lo = (r1 - mid.astype(F32)).astype(BF16)
    return _dot(hi, u) + _dot(mid, u) + _dot(lo, u)


def _gate_kernel(g_ref, bias_ref, aneg_ref, o_ref, *, nc, L):
    g = g_ref[0] + bias_ref[...]
    nh = ML_HEADS
    ig = g[0:2 * nh]
    fpre = g[2 * nh:4 * nh]
    lf = -_softplus(-fpre)
    dt = _softplus(g[4 * nh:])
    dA = dt * aneg_ref[...]
    r = lax.broadcasted_iota(jnp.int32, (L, L), 0)
    c = lax.broadcasted_iota(jnp.int32, (L, L), 1)
    u_pre = (r <= c).astype(BF16)
    u_suf = (r >= c).astype(BF16)
    nf = nh + SSM_HEADS
    xf = jnp.concatenate([lf[0:nh], dA[0:SSM_HEADS]], axis=0).reshape(nf * nc, L)
    xb = jnp.concatenate([lf[nh:], dA[SSM_HEADS:]], axis=0).reshape(nf * nc, L)
    cf = _split3_dot(xf, u_pre).reshape(nf, nc, L)
    cb = _split3_dot(xb, u_suf).reshape(nf, nc, L)
    totf = jnp.broadcast_to(cf[:, :, L - 1:L], cf.shape)
    totb = jnp.broadcast_to(cb[:, :, 0:1], cb.shape)
    o_ref[0, 0:nh] = cf[0:nh]
    o_ref[0, nh:2 * nh] = cb[0:nh]
    o_ref[0, 2 * nh:4 * nh] = ig
    o_ref[0, 4 * nh:5 * nh] = totf[0:nh] - cf[0:nh] + ig[0:nh]
    o_ref[0, 5 * nh:6 * nh] = totb[0:nh] - cb[0:nh] + ig[nh:]
    lane = lax.broadcasted_iota(jnp.int32, (nh * nc, L), 1)
    pf = (ig[0:nh] - cf[0:nh]).reshape(nh * nc, L)
    pb = (ig[nh:] - cb[0:nh]).reshape(nh * nc, L)
    sh = 1
    while sh < L:
        pf = jnp.where(lane >= sh, jnp.maximum(pf, pltpu.roll(pf, sh, 1)), pf)
        pb = jnp.where(lane < L - sh, jnp.maximum(pb, pltpu.roll(pb, L - sh, 1)), pb)
        sh *= 2
    o_ref[0, 6 * nh:7 * nh] = cf[0:nh] + pf.reshape(nh, nc, L)
    o_ref[0, 7 * nh:8 * nh] = cb[0:nh] + pb.reshape(nh, nc, L)
    H = SSM_HEADS
    af, ab, dtf, dtb = cf[nh:], cb[nh:], dt[0:H], dt[H:]
    s0 = PREP_SSM0
    o_ref[0, s0:s0 + H] = af
    o_ref[0, s0 + H:s0 + 2 * H] = ab
    o_ref[0, s0 + 2 * H:s0 + 3 * H] = af - jnp.log(dtf)
    o_ref[0, s0 + 3 * H:s0 + 4 * H] = ab - jnp.log(dtb)
    o_ref[0, s0 + 4 * H:s0 + 5 * H] = jnp.exp(af)
    o_ref[0, s0 + 5 * H:s0 + 6 * H] = jnp.exp(ab)
    o_ref[0, s0 + 6 * H:s0 + 7 * H] = dtf * jnp.exp(totf[nh:] - af)
    o_ref[0, s0 + 7 * H:s0 + 8 * H] = dtb * jnp.exp(totb[nh:] - ab)
    o_ref[0, s0 + 8 * H:s0 + 9 * H] = jnp.exp(totf[nh:])
    o_ref[0, s0 + 9 * H:s0 + 10 * H] = jnp.exp(totb[nh:])


def _gate_prep(g_rows, bias_rows, aneg_rows):
    B, R, nc, L = g_rows.shape
    return pl.pallas_call(
        functools.partial(_gate_kernel, nc=nc, L=L),
        grid=(B,),
        in_specs=[
            pl.BlockSpec((1, R, nc, L), lambda b: (b, 0, 0, 0)),
            pl.BlockSpec((R, 1, L), lambda b: (0, 0, 0)),
            pl.BlockSpec((2 * SSM_HEADS, 1, L), lambda b: (0, 0, 0)),
        ],
        out_specs=pl.BlockSpec((1, N_PREP_ROWS, nc, L), lambda b: (b, 0, 0, 0)),
        out_shape=jax.ShapeDtypeStruct((B, N_PREP_ROWS, nc, L), F32),
        compiler_params=_cparams(("parallel",)),
        name="gate_prep",
    )(g_rows, bias_rows, aneg_rows)


def _mlstm_kernel(q_ref, kt_ref, v_ref, o_ref, mg_ref, mc_ref, nw_ref, y_ref, park_scr, c_scr, n_scr, m_scr,
                  *, nc, L, nh):
    row_i = lax.broadcasted_iota(jnp.int32, (L, L), 0)
    col_i = lax.broadcasted_iota(jnp.int32, (L, L), 1)
    ones = jnp.ones((L, L), BF16)

    c_scr[...] = jnp.zeros_like(c_scr)
    n_scr[...] = jnp.zeros_like(n_scr)
    m_scr[...] = jnp.full_like(m_scr, NEG_INF)

    def chunk(c, d, hh):
        c_ref, n_ref, m_ref = c_scr.at[2 * hh + d], n_scr.at[2 * hh + d], m_scr.at[2 * hh + d]
        c_prev, n_prev, m_prev = c_ref[...], n_ref[...], m_ref[...]
        sl = pl.ds(pl.multiple_of(c * L, L), L)
        q = q_ref[0, sl, hh * ML_DQK:(hh + 1) * ML_DQK]
        kt = kt_ref[0, c, hh]
        vo = jnp.concatenate([v_ref[0, sl, hh * ML_DV:(hh + 1) * ML_DV], ones], axis=1)
        g8 = mg_ref[0, hh, c]
        ct = mc_ref[0, hh, sl, :]
        b_r, i_r, a_r = g8[2 * d:2 * d + 1], g8[2 * d + 1:2 * d + 2], g8[4 + d:5 + d]
        b_c = jnp.broadcast_to(ct[:, 2 * d:2 * d + 1], (L, L))
        mi_c = jnp.broadcast_to(ct[:, 6 + d:7 + d], (L, L))
        if d == 0:
            gtot = b_r[:, L - 1:L]
            mask = col_i <= row_i
        else:
            gtot = b_r[:, 0:1]
            mask = col_i >= row_i
        m_loc = jnp.max(a_r, axis=1, keepdims=True)
        kwt = (kt.astype(F32) * jnp.exp(a_r - m_loc)).astype(BF16)
        uu = _dot(kwt, vo)
        qq = _dot(q, jnp.concatenate([kt, c_prev.astype(BF16), n_prev.astype(BF16)], axis=1))
        s, q_c, q_n = qq[:, 0:L], qq[:, L:L + ML_DV], qq[:, L + ML_DV:]
        dmat = jnp.where(mask, b_c - b_r + i_r, NEG_INF)
        m_inter = b_c + m_prev
        m_t = jnp.maximum(m_inter, mi_c)
        s_mat = s * jnp.exp(dmat - m_t)
        inter_w = jnp.exp(m_inter - m_t)
        ss = _dot(s_mat.astype(BF16), vo)
        den = ss[:, ML_DV:] + inter_w * q_n
        r = 1.0 / jnp.maximum(jnp.abs(den), jnp.exp(-m_t))
        num = ss[:, 0:ML_DV] + jnp.concatenate([inter_w, inter_w], axis=1) * q_c
        h = num * jnp.concatenate([r, r], axis=1)
        m_new = jnp.maximum(gtot + m_prev, m_loc)
        decay = jnp.exp(gtot + m_prev - m_new)
        sc = jnp.exp(m_loc - m_new)
        c_ref[...] = decay * c_prev + sc * uu[:, 0:ML_DV]
        n_ref[...] = decay * n_prev + sc * uu[:, ML_DV:]
        m_ref[...] = m_new
        return sl, h

    def finish(sl, hh, h):
        cols = slice(hh * ML_DV, (hh + 1) * ML_DV)
        mu = jnp.mean(h, axis=1, keepdims=True)
        d = h - mu
        var = jnp.mean(d * d, axis=1, keepdims=True)
        y = d * lax.rsqrt(var + LN_EPS) * nw_ref[:, cols]
        y = y * _sigmoid(o_ref[0, sl, cols].astype(F32))
        y_ref[0, sl, cols] = y.astype(y_ref.dtype)

    def first_half(i, carry):
        for hh in range(nh):
            for d, c in ((0, i), (1, nc - 1 - i)):
                sl, h = chunk(c, d, hh)
                park_scr[hh, sl, :] = h
        return carry

    def second_half(i, carry):
        for hh in range(nh):
            for d, c in ((0, i), (1, nc - 1 - i)):
                sl, h = chunk(c, d, hh)
                finish(sl, hh, h + park_scr[hh, sl, :])
        return carry

    half = nc // 2
    lax.fori_loop(0, half, first_half, 0, unroll=8 // nh)
    lax.fori_loop(half, nc, second_half, 0, unroll=8 // nh)


def _mlstm(pa3, kt5, mg, mc, norm_w, nh=1):
    B, S, _ = pa3.shape
    nc = S // CHUNK
    H = ML_HEADS
    q_blk, v_blk, o_blk = PA_Q0 // (nh * ML_DQK), PA_V0 // (nh * ML_DV), PA_O0 // (nh * ML_DV)
    return pl.pallas_call(
        functools.partial(_mlstm_kernel, nc=nc, L=CHUNK, nh=nh),
        grid=(B, H // nh),
        in_specs=[
            pl.BlockSpec((1, S, nh * ML_DQK), lambda b, h: (b, 0, q_blk + h)),
            pl.BlockSpec((1, nc, nh, ML_DQK, CHUNK), lambda b, h: (b, 0, h, 0, 0)),
            pl.BlockSpec((1, S, nh * ML_DV), lambda b, h: (b, 0, v_blk + h)),
            pl.BlockSpec((1, S, nh * ML_DV), lambda b, h: (b, 0, o_blk + h)),
            pl.BlockSpec((1, nh, nc, 8, CHUNK), lambda b, h: (b, h, 0, 0, 0)),
            pl.BlockSpec((1, nh, S, 8), lambda b, h: (b, h, 0, 0)),
            pl.BlockSpec((1, nh * ML_DV), lambda b, h: (0, h)),
        ],
        out_specs=pl.BlockSpec((1, S, nh * ML_DV), lambda b, h: (b, 0, h)),
        out_shape=jax.ShapeDtypeStruct((B, S, ML_WIDTH), BF16),
        scratch_shapes=[
            pltpu.VMEM((nh, S, ML_DV), F32),
            pltpu.VMEM((2 * nh, ML_DQK, ML_DV), F32),
            pltpu.VMEM((2 * nh, ML_DQK, CHUNK), F32),
            pltpu.VMEM((2 * nh, 1, 1), F32),
        ],
        compiler_params=_cparams(("parallel", "parallel")),
        name="mlstm",
    )(pa3, kt5, pa3, pa3, mg, mc, norm_w)


CONV_EDGE = 16


def _conv_shift_matrix(L):
    pad, E = CONV_K // 2, CONV_EDGE
    t = jnp.arange(L, dtype=jnp.int32)[:, None]
    blocks = []
    u = jnp.arange(L, dtype=jnp.int32)[None, :]
    for j in range(CONV_K):
        blocks.append(u == t + j - pad)
    v = jnp.arange(E, dtype=jnp.int32)[None, :]
    for j in range(pad):
        blocks.append(v - E == t + j - pad)
    for j in range(pad + 1, CONV_K):
        blocks.append(v + L == t + j - pad)
    return jnp.concatenate(blocks, axis=1).astype(BF16)


def _conv_kernel(x_ref, top_ref, bot_ref, w_ref, b_ref, sh_ref, o_ref, ext_scr, *, Lc, L):
    pad, E = CONV_K // 2, CONV_EDGE
    i = pl.program_id(1)
    first, last = i == 0, i == pl.num_programs(1) - 1
    ext_scr[0:E, :] = jnp.where(first, jnp.zeros_like(top_ref[0]), top_ref[0])
    ext_scr[E:E + Lc, :] = x_ref[0]
    ext_scr[E + Lc:2 * E + Lc, :] = jnp.where(last, jnp.zeros_like(bot_ref[0]), bot_ref[0])
    taps = [w_ref[j:j + 1, :].astype(BF16) for j in range(CONV_K)]
    scaled = lambda rows, j: rows * taps[j]
    for r0 in range(0, Lc, L):
        cen = ext_scr[E + r0:E + r0 + L, :]
        top = ext_scr[r0:r0 + E, :]
        bot = ext_scr[E + r0 + L:2 * E + r0 + L, :]
        pieces = [scaled(cen, j) for j in range(CONV_K)]
        pieces += [scaled(top, j) for j in range(pad)] + [scaled(bot, j) for j in range(pad + 1, CONV_K)]
        acc = _dot(sh_ref[...], jnp.concatenate(pieces, axis=0)) + b_ref[...]
        o_ref[0, r0:r0 + L, :] = (acc * _sigmoid(acc)).astype(o_ref.dtype)


def _conv(pa3, conv_w8, conv_b, Lc):
    B, S, _ = pa3.shape
    C = SSM_CONV_DIM
    xbc_blk = PA_XBC0 // C
    L = CHUNK
    sh = _conv_shift_matrix(L)
    epb = Lc // CONV_EDGE
    n_edge = S // CONV_EDGE
    return pl.pallas_call(
        functools.partial(_conv_kernel, Lc=Lc, L=L),
        grid=(B, S // Lc),
        in_specs=[
            pl.BlockSpec((1, Lc, C), lambda b, i: (b, i, xbc_blk)),
            pl.BlockSpec((1, CONV_EDGE, C), lambda b, i: (b, jnp.maximum(i * epb - 1, 0), xbc_blk)),
            pl.BlockSpec((1, CONV_EDGE, C), lambda b, i: (b, jnp.minimum((i + 1) * epb, n_edge - 1), xbc_blk)),
            pl.BlockSpec((8, C), lambda b, i: (0, 0)),
            pl.BlockSpec((1, C), lambda b, i: (0, 0)),
            pl.BlockSpec(sh.shape, lambda b, i: (0, 0)),
        ],
        out_specs=pl.BlockSpec((1, Lc, C), lambda b, i: (b, i, 0)),
        out_shape=jax.ShapeDtypeStruct((B, S, C), BF16),
        scratch_shapes=[pltpu.VMEM((Lc + 2 * CONV_EDGE, C), BF16)],
        compiler_params=_cparams(("parallel", "parallel")),
        name="ssm_conv",
    )(pa3, pa3, pa3, conv_w8, conv_b, sh)


def _expansion_matrix(src_cols, width):
    src = jnp.repeat(jnp.asarray(src_cols, jnp.int32), width)
    e = (jnp.arange(128, dtype=jnp.int32)[:, None] == src[None, :]).astype(BF16)
    return jnp.concatenate([e, e], axis=0)


def _split2(t):
    hi = t.astype(BF16)
    return jnp.concatenate([hi, (t - hi.astype(F32)).astype(BF16)], axis=1)


def _ssd_bwd_kernel(x_ref, b_ref, tb_ref, dec_ref, ew_ref, hb_ref, h_scr, *, L):
    G, hg, P, N = SSM_GROUPS, SSM_HEADS // SSM_GROUPS, SSM_HEADDIM, SSM_STATE
    GW = hg * P

    @pl.when(pl.program_id(1) == 0)
    def _():
        h_scr[...] = jnp.zeros_like(h_scr)

    for s in reversed(range(x_ref.shape[1] // L)):
        rows = slice(s * L, (s + 1) * L)
        w_all = _dot(_split2(tb_ref[0, s].T), ew_ref[...])
        for g in range(G):
            w = w_all[:, g * GW:(g + 1) * GW]
            xs = (x_ref[0, rows, g * GW:(g + 1) * GW].astype(F32) * w).astype(BF16)
            st = _dot_tn(b_ref[0, rows, g * N:(g + 1) * N], xs)
            h_old = h_scr[g]
            hb_ref[0, s, g] = h_old.astype(hb_ref.dtype)
            h_scr[g] = dec_ref[0, s, 1:2, g * GW:(g + 1) * GW] * h_old + st


def _ssd_bwd(xc, tb, decs, cpb):
    B, S, _ = xc.shape
    nc = S // CHUNK
    nblk = nc // cpb
    R = cpb * CHUNK
    G, N, W = SSM_GROUPS, SSM_STATE, SSM_WIDTH
    return pl.pallas_call(
        functools.partial(_ssd_bwd_kernel, L=CHUNK),
        grid=(B, nblk),
        in_specs=[
            pl.BlockSpec((1, R, W), lambda b, c: (b, nblk - 1 - c, 0)),
            pl.BlockSpec((1, R, G * N), lambda b, c: (b, nblk - 1 - c, W // (G * N))),
            pl.BlockSpec((1, cpb, 128, CHUNK), lambda b, c: (b, nblk - 1 - c, 0, 0)),
            pl.BlockSpec((1, cpb, 2, W), lambda b, c: (b, nblk - 1 - c, 0, 0)),
            pl.BlockSpec((256, W), lambda b, c: (0, 0)),
        ],
        out_specs=pl.BlockSpec((1, cpb, G, N, W // G), lambda b, c: (b, nblk - 1 - c, 0, 0, 0)),
        out_shape=jax.ShapeDtypeStruct((B, nc, G, N, W // G), BF16),
        scratch_shapes=[pltpu.VMEM((G, N, W // G), F32)],
        compiler_params=_cparams(("parallel", "arbitrary")),
        name="ssd_bwd_states",
    )(xc, xc, tb, decs, _expansion_matrix([5 * SSM_HEADS + h for h in range(SSM_HEADS)], SSM_HEADDIM))


def _ssd_main_kernel(x_ref, b_ref, c_ref, z_ref, tb_ref, dec_ref, hb_ref, dsk_ref, nw_ref, ed_ref, ee_ref,
                     y_ref, h_scr, *, L):
    G, hg, P, N = SSM_GROUPS, SSM_HEADS // SSM_GROUPS, SSM_HEADDIM, SSM_STATE
    H = SSM_HEADS
    GW = hg * P

    @pl.when(pl.program_id(1) == 0)
    def _():
        h_scr[...] = jnp.zeros_like(h_scr)

    row_i = lax.broadcasted_iota(jnp.int32, (L, L), 0)
    col_i = lax.broadcasted_iota(jnp.int32, (L, L), 1)
    lower = col_i <= row_i
    upper = col_i >= row_i
    lane_g = lax.broadcasted_iota(jnp.int32, (L, GW), 1)

    for s, g in [(s, g) for s in range(x_ref.shape[1] // L) for g in range(G)]:
        rows = slice(s * L, (s + 1) * L)
        tb = tb_ref[0, s]
        sc2 = _split2(tb.T)
        xg = x_ref[0, rows, g * GW:(g + 1) * GW]
        bg = b_ref[0, rows, g * N:(g + 1) * N]
        cg = c_ref[0, rows, g * N:(g + 1) * N]
        cb = _dot_nt(cg, bg)
        acol = _dot(sc2, ed_ref[g])
        ecol = _dot(sc2, ee_ref[g])
        ms, xms = [], []
        for j in range(hg):
            hd = g * hg + j
            rf_r, rb_r = tb[6 * H + hd:6 * H + hd + 1], tb[7 * H + hd:7 * H + hd + 1]
            af_c, ab_c = acol[:, j * L:(j + 1) * L], acol[:, (hg + j) * L:(hg + j + 1) * L]
            fm = jnp.exp(jnp.where(lower, af_c - rf_r, NEG_INF))
            bm = jnp.exp(jnp.where(upper, ab_c - rb_r, NEG_INF))
            ms.append((cb * (fm + bm)).astype(BF16))
            xms.append(jnp.where((lane_g >= j * P) & (lane_g < (j + 1) * P), xg, jnp.zeros_like(xg)))
        efc, ebc, wfc = ecol[:, 0:GW], ecol[:, GW:2 * GW], ecol[:, 2 * GW:]
        y = _dot(jnp.concatenate(ms, axis=1), jnp.concatenate(xms, axis=0))
        h_old = h_scr[g]
        y = y + efc * _dot(cg, h_old.astype(BF16))
        y = y + ebc * _dot(cg, hb_ref[0, s, g])
        xgf = xg.astype(F32)
        y = y + dsk_ref[:, g * GW:(g + 1) * GW] * xgf
        xs = (xgf * wfc).astype(BF16)
        h_scr[g] = dec_ref[0, s, 0:1, g * GW:(g + 1) * GW] * h_old + _dot_tn(bg, xs)
        zg = z_ref[0, rows, g * GW:(g + 1) * GW].astype(F32)
        y = y * (zg * _sigmoid(zg))
        y = y * lax.rsqrt(jnp.mean(y * y, axis=1, keepdims=True) + RMS_EPS)
        y_ref[0, rows, g * GW:(g + 1) * GW] = (y * nw_ref[:, g * GW:(g + 1) * GW]).astype(y_ref.dtype)


def _ssd_main(xc, pa3, tb, decs, hb, dskip_row, norm_w, cpb):
    B, S, _ = xc.shape
    nc = S // CHUNK
    R = cpb * CHUNK
    G, N, W = SSM_GROUPS, SSM_STATE, SSM_WIDTH
    H, hg, P = SSM_HEADS, SSM_HEADS // SSM_GROUPS, SSM_HEADDIM
    z_blk = PA_Z0 // W
    e_dec = jnp.stack([_expansion_matrix([q * H + g * hg + j for q in (0, 1) for j in range(hg)], CHUNK)
                       for g in range(G)])
    e_ew = jnp.stack([_expansion_matrix([q * H + g * hg + j for q in (2, 3, 4) for j in range(hg)], P)
                      for g in range(G)])
    return pl.pallas_call(
        functools.partial(_ssd_main_kernel, L=CHUNK),
        grid=(B, nc // cpb),
        in_specs=[
            pl.BlockSpec((1, R, W), lambda b, c: (b, c, 0)),
            pl.BlockSpec((1, R, G * N), lambda b, c: (b, c, W // (G * N))),
            pl.BlockSpec((1, R, G * N), lambda b, c: (b, c, W // (G * N) + 1)),
            pl.BlockSpec((1, R, W), lambda b, c: (b, c, z_blk)),
            pl.BlockSpec((1, cpb, 128, CHUNK), lambda b, c: (b, c, 0, 0)),
            pl.BlockSpec((1, cpb, 2, W), lambda b, c: (b, c, 0, 0)),
            pl.BlockSpec((1, cpb, G, N, W // G), lambda b, c: (b, c, 0, 0, 0)),
            pl.BlockSpec((1, W), lambda b, c: (0, 0)),
            pl.BlockSpec((1, W), lambda b, c: (0, 0)),
            pl.BlockSpec((G, 256, 2 * hg * CHUNK), lambda b, c: (0, 0, 0)),
            pl.BlockSpec((G, 256, 3 * W // G), lambda b, c: (0, 0, 0)),
        ],
        out_specs=pl.BlockSpec((1, R, W), lambda b, c: (b, c, 0)),
        out_shape=jax.ShapeDtypeStruct((B, S, W), BF16),
        scratch_shapes=[pltpu.VMEM((G, N, W // G), F32)],
        compiler_params=_cparams(("parallel", "arbitrary")),
        name="ssd_main",
    )(xc, xc, xc, pa3, tb, decs, hb, dskip_row, norm_w, e_dec, e_ew)


def _pack_bf16_pairs(v):
    n = v.shape[1] // 2
    hi = lax.bitcast_convert_type(v[:, :n].astype(BF16).astype(F32), jnp.uint32)
    lo = lax.bitcast_convert_type(v[:, n:].astype(BF16).astype(F32), jnp.uint32)
    return (hi & jnp.uint32(0xFFFF0000)) | (lo >> 16)


def _unpack_bf16_pairs(u):
    hi = lax.bitcast_convert_type(u & jnp.uint32(0xFFFF0000), F32)
    lo = lax.bitcast_convert_type(u << 16, F32)
    return hi, lo


def _layer_norm(v, g, b):
    mu = jnp.mean(v, axis=1, keepdims=True)
    d = v - mu
    var = jnp.mean(d * d, axis=1, keepdims=True)
    return d * lax.rsqrt(var + LN_EPS) * g + b


def _outproj_kernel(yml_ref, yssm_ref, x_ref, wo_ref, g_ref, b_ref, wr_ref, h_ref, hpa_ref, hpb_ref, sc_ref):
    mix = _dot(yml_ref[...], wo_ref[0:ML_WIDTH, :]) + _dot(yssm_ref[...], wo_ref[ML_WIDTH:, :])
    h = _layer_norm(ALPHA * x_ref[...] + mix, g_ref[...], b_ref[...])
    h_ref[...] = h
    hp = _pack_bf16_pairs(h)
    q = hp.shape[1] // 2
    hpa_ref[...] = hp[:, :q]
    hpb_ref[...] = hp[:, q:]
    sc_ref[...] = _sigmoid(_dot(h.astype(BF16), wr_ref[...]))


def _outproj(yml, yssm, x2d, w_out, ln_g, ln_b, w_router, tm=1024):
    T, D = x2d.shape
    tm = min(tm, T)
    row = lambda i: (i, 0)
    fixed = lambda i: (0, 0)
    return pl.pallas_call(
        _outproj_kernel,
        grid=(T // tm,),
        in_specs=[
            pl.BlockSpec((tm, ML_WIDTH), row),
            pl.BlockSpec((tm, SSM_WIDTH), row),
            pl.BlockSpec((tm, D), row),
            pl.BlockSpec((ML_WIDTH + SSM_WIDTH, D), fixed),
            pl.BlockSpec((1, D), fixed),
            pl.BlockSpec((1, D), fixed),
            pl.BlockSpec((D, 128), fixed),
        ],
        out_specs=[pl.BlockSpec((tm, D), row), pl.BlockSpec((tm, D // 4), row), pl.BlockSpec((tm, D // 4), row),
                   pl.BlockSpec((tm, 128), row)],
        out_shape=[
            jax.ShapeDtypeStruct((T, D), F32),
            jax.ShapeDtypeStruct((T, D // 4), jnp.uint32),
            jax.ShapeDtypeStruct((T, D // 4), jnp.uint32),
            jax.ShapeDtypeStruct((T, 128), F32),
        ],
        compiler_params=_cparams(("parallel",)),
        name="out_proj_ln1_router",
    )(yml, yssm, x2d, w_out, ln_g, ln_b, w_router)


def _route_kernel(sc_ref, rb_ref, idx_ref, w_ref, rank_ref, cnt_ref, carry_scr, *, Tt):
    E, NG = N_EXPERTS, N_EXPERT_GROUPS
    EG = E // NG

    @pl.when(pl.program_id(0) == 0)
    def _():
        carry_scr[...] = jnp.zeros_like(carry_scr)

    s = sc_ref[...].T[0:E]
    s3 = s.reshape(NG, EG, Tt)
    ch3 = (s + rb_ref[:, 0:1]).reshape(NG, EG, Tt)
    e_in = lax.broadcasted_iota(jnp.int32, (NG, EG, Tt), 1)
    g_in = lax.broadcasted_iota(jnp.int32, (NG, EG, Tt), 0)
    f_i = g_in * EG + e_in
    m1 = jnp.max(ch3, axis=1, keepdims=True)
    i1 = jnp.min(jnp.where(ch3 == m1, e_in, EG), axis=1, keepdims=True)
    m2 = jnp.max(jnp.where(e_in == i1, NEG_INF, ch3), axis=1, keepdims=True)
    gs = m1 + m2
    g_i = lax.broadcasted_iota(jnp.int32, (NG, 1, Tt), 0)
    gsel = jnp.zeros((NG, 1, Tt), jnp.bool_)
    cur = gs
    for _ in range(TOPK_GROUPS):
        mx = jnp.max(cur, axis=0, keepdims=True)
        pick = jnp.min(jnp.where(cur == mx, g_i, NG), axis=0, keepdims=True)
        hit = g_i == pick
        gsel = gsel | hit
        cur = jnp.where(hit, NEG_INF, cur)
    cur = jnp.where(gsel, ch3, NEG_INF)
    selmask = jnp.zeros((NG, EG, Tt), jnp.bool_)
    idxs, ws = [], []
    for _ in range(TOP_K):
        mx = jnp.max(jnp.max(cur, axis=1, keepdims=True), axis=0, keepdims=True)
        cand = jnp.where(cur == mx, f_i, E)
        pick = jnp.min(jnp.min(cand, axis=1, keepdims=True), axis=0, keepdims=True)
        hit = f_i == pick
        wk = jnp.sum(jnp.sum(jnp.where(hit, s3, 0.0), axis=1, keepdims=True), axis=0, keepdims=True)
        idxs.append(pick.reshape(1, Tt))
        ws.append(wk.reshape(1, Tt))
        selmask = selmask | hit
        cur = jnp.where(hit, NEG_INF, cur)
    wsum = ws[0]
    for k in range(1, TOP_K):
        wsum = wsum + ws[k]
    r = lax.broadcasted_iota(jnp.int32, (Tt, Tt), 0)
    c = lax.broadcasted_iota(jnp.int32, (Tt, Tt), 1)
    ustrict = (r < c).astype(BF16)
    self32 = jnp.where(selmask, 1.0, 0.0).reshape(E, Tt)
    excl = _dot(self32.astype(BF16), ustrict) + carry_scr[:, 0:1]
    excl3 = excl.reshape(NG, EG, Tt)
    for k in range(TOP_K):
        hit = f_i == idxs[k].reshape(1, 1, Tt)
        rk = jnp.sum(jnp.sum(jnp.where(hit, excl3, 0.0), axis=1, keepdims=True), axis=0, keepdims=True)
        idx_ref[k:k + 1, :] = idxs[k]
        w_ref[k:k + 1, :] = ws[k] / wsum * ROUTED_SCALE
        rank_ref[k:k + 1, :] = rk.reshape(1, Tt).astype(jnp.int32)
    carry_scr[...] = carry_scr[...] + jnp.sum(self32, axis=1, keepdims=True)
    cnt_ref[...] = carry_scr[...]


def _route(scores, rbias, Tt=512):
    T = scores.shape[0]
    kt = lambda i: (0, i)
    return pl.pallas_call(
        functools.partial(_route_kernel, Tt=Tt),
        grid=(T // Tt,),
        in_specs=[pl.BlockSpec((Tt, 128), lambda i: (i, 0)), pl.BlockSpec((N_EXPERTS, 128), lambda i: (0, 0))],
        out_specs=[
            pl.BlockSpec((TOP_K, Tt), kt),
            pl.BlockSpec((TOP_K, Tt), kt),
            pl.BlockSpec((TOP_K, Tt), kt),
            pl.BlockSpec((N_EXPERTS, 128), lambda i: (0, 0)),
        ],
        out_shape=[
            jax.ShapeDtypeStruct((TOP_K, T), jnp.int32),
            jax.ShapeDtypeStruct((TOP_K, T), F32),
            jax.ShapeDtypeStruct((TOP_K, T), jnp.int32),
            jax.ShapeDtypeStruct((N_EXPERTS, 128), F32),
        ],
        scratch_shapes=[pltpu.VMEM((N_EXPERTS, 128), F32)],
        compiler_params=_cparams(("arbitrary",)),
        name="route_topk",
    )(scores, rbias)


def _dest_kernel(ps_ref, idx_ref, rank_ref, o_ref):
    idx = idx_ref[...]
    start = jnp.zeros(idx.shape, jnp.int32)
    for e in range(N_EXPERTS):
        start = jnp.where(idx == e, ps_ref[e], start)
    o_ref[...] = start + rank_ref[...]


def _dest(pstarts, idx, rank, Tt=2048):
    K, T = idx.shape
    Tt = min(Tt, T)
    blk = pl.BlockSpec((K, Tt), lambda i, ps: (0, i))
    return pl.pallas_call(
        _dest_kernel,
        grid_spec=pltpu.PrefetchScalarGridSpec(num_scalar_prefetch=1, grid=(T // Tt,), in_specs=[blk, blk],
                                               out_specs=blk),
        out_shape=jax.ShapeDtypeStruct((K, T), jnp.int32),
        compiler_params=_cparams(("parallel",)),
        name="dispatch_slots",
    )(pstarts, idx, rank)


def _sc_mesh():
    return plsc.VectorSubcoreMesh(core_axis_name="c", subcore_axis_name="s")


def _sc_scatter_rows(src, idx, n_out, win=SC_WINDOW):
    K, T = idx.shape
    W = src.shape[1]

    @functools.partial(pl.kernel, out_type=jax.ShapeDtypeStruct((n_out, W), src.dtype), mesh=_sc_mesh(),
                       scratch_types=[], name="sc_scatter_rows")
    def k(x_hbm, i_hbm, o_hbm):
        def body(x_vmem, i_vmem):
            for j in range(K):
                pltpu.sync_copy(x_vmem, o_hbm.at[i_vmem.at[j]])

        pltpu.emit_pipeline(
            body,
            grid=(T // win,),
            in_specs=[pl.BlockSpec((win, W), lambda i: (i, 0)), pl.BlockSpec((K, win), lambda i: (0, i))],
            out_specs=[],
            core_axis_name=("c", "s"),
            dimension_semantics=(pltpu.PARALLEL,),
        )(x_hbm, i_hbm)

    return k(src, idx)


def _sc_gather_rows(table, idx, win=SC_WINDOW):
    n = idx.shape[0]
    W = table.shape[1]

    @functools.partial(pl.kernel, out_type=jax.ShapeDtypeStruct((n, W), table.dtype), mesh=_sc_mesh(),
                       scratch_types=[], name="sc_gather_rows")
    def k(t_hbm, i_hbm, o_hbm):
        def body(i_vmem, o_vmem):
            pltpu.sync_copy(t_hbm.at[i_vmem.at[0]], o_vmem)

        pltpu.emit_pipeline(
            body,
            grid=(n // win,),
            in_specs=[pl.BlockSpec((1, win), lambda i: (0, i))],
            out_specs=[pl.BlockSpec((win, W), lambda i: (i, 0))],
            core_axis_name=("c", "s"),
            dimension_semantics=(pltpu.PARALLEL,),
        )(i_hbm, o_hbm)

    return k(table, idx.reshape(1, n))


def _expert_kernel(be_ref, nu_ref, xa_ref, xb_ref, wg_ref, wu_ref, wd_ref, oa_ref, ob_ref, wg_s, wu_s, wd_s):
    i = pl.program_id(0)
    active = i < nu_ref[0]

    @pl.when(active & ((i == 0) | (be_ref[i] != be_ref[jnp.maximum(i - 1, 0)])))
    def _():
        wg_s[...] = wg_ref[0].astype(BF16)
        wu_s[...] = wu_ref[0].astype(BF16)
        wd_s[...] = wd_ref[0].astype(BF16)

    @pl.when(active)
    def _():
        hi, lo = _unpack_bf16_pairs(jnp.concatenate([xa_ref[...], xb_ref[...]], axis=1))
        x = jnp.concatenate([hi.astype(BF16), lo.astype(BF16)], axis=1)
        gt = _dot(x, wg_s[...])
        up = _dot(x, wu_s[...])
        hmid = (gt * _sigmoid(gt) * up).astype(BF16)
        packed = _pack_bf16_pairs(_dot(hmid, wd_s[...]))
        q = packed.shape[1] // 2
        oa_ref[...] = packed[:, :q]
        ob_ref[...] = packed[:, q:]

    @pl.when(jnp.logical_not(active))
    def _():
        oa_ref[...] = jnp.zeros_like(oa_ref)
        ob_ref[...] = jnp.zeros_like(ob_ref)


def _experts(xs_a, xs_b, block_expert, n_used, w_gate, w_up, w_down, blk=MOE_BLOCK):
    P, quarter = xs_a.shape
    nb = P // blk
    D, F = D_MODEL, D_EXPERT
    cur = lambda i, be, nu: jnp.minimum(i, nu[0] - 1)
    return pl.pallas_call(
        _expert_kernel,
        grid_spec=pltpu.PrefetchScalarGridSpec(
            num_scalar_prefetch=2,
            grid=(nb,),
            in_specs=[
                pl.BlockSpec((blk, quarter), lambda i, be, nu: (cur(i, be, nu), 0)),
                pl.BlockSpec((blk, quarter), lambda i, be, nu: (cur(i, be, nu), 0)),
                pl.BlockSpec((1, D, F), lambda i, be, nu: (be[cur(i, be, nu)], 0, 0)),
                pl.BlockSpec((1, D, F), lambda i, be, nu: (be[cur(i, be, nu)], 0, 0)),
                pl.BlockSpec((1, F, D), lambda i, be, nu: (be[cur(i, be, nu)], 0, 0)),
            ],
            out_specs=[pl.BlockSpec((blk, quarter), lambda i, be, nu: (i, 0)),
                       pl.BlockSpec((blk, quarter), lambda i, be, nu: (i, 0))],
            scratch_shapes=[pltpu.VMEM((D, F), BF16), pltpu.VMEM((D, F), BF16), pltpu.VMEM((F, D), BF16)],
        ),
        out_shape=[jax.ShapeDtypeStruct((P, quarter), jnp.uint32), jax.ShapeDtypeStruct((P, quarter), jnp.uint32)],
        compiler_params=_cparams(("arbitrary",)),
        name="expert_mlp",
    )(block_expert, n_used, xs_a, xs_b, w_gate, w_up, w_down)


def _final_kernel(h_ref, ga_ref, gb_ref, wt_ref, wsg_ref, wsu_ref, wsd_ref, g_ref, b_ref, o_ref):
    h = h_ref[...]
    hb = h.astype(BF16)
    gt = _dot(hb, wsg_ref[...])
    up = _dot(hb, wsu_ref[...])
    shared = _dot((gt * _sigmoid(gt) * up).astype(BF16), wsd_ref[...])
    acc_hi = None
    for k in range(TOP_K):
        hi, lo = _unpack_bf16_pairs(jnp.concatenate([ga_ref[k], gb_ref[k]], axis=1))
        wk = wt_ref[:, k:k + 1]
        acc_hi = wk * hi if acc_hi is None else acc_hi + wk * hi
        acc_lo = wk * lo if k == 0 else acc_lo + wk * lo
    routed = jnp.concatenate([acc_hi, acc_lo], axis=1)
    o_ref[...] = _layer_norm(ALPHA * h + (routed + shared), g_ref[...], b_ref[...])


def _final(h, gk_a, gk_b, wt, ws_gate, ws_up, ws_down, ln_g, ln_b, tm=512):
    T, D = h.shape
    tm = min(tm, T)
    F = D_EXPERT
    fixed = lambda i: (0, 0)
    return pl.pallas_call(
        _final_kernel,
        grid=(T // tm,),
        in_specs=[
            pl.BlockSpec((tm, D), lambda i: (i, 0)),
            pl.BlockSpec((TOP_K, tm, D // 4), lambda i: (0, i, 0)),
            pl.BlockSpec((TOP_K, tm, D // 4), lambda i: (0, i, 0)),
            pl.BlockSpec((tm, TOP_K), lambda i: (i, 0)),
            pl.BlockSpec((D, F), fixed),
            pl.BlockSpec((D, F), fixed),
            pl.BlockSpec((F, D), fixed),
            pl.BlockSpec((1, D), fixed),
            pl.BlockSpec((1, D), fixed),
        ],
        out_specs=pl.BlockSpec((tm, D), lambda i: (i, 0)),
        out_shape=jax.ShapeDtypeStruct((T, D), F32),
        compiler_params=_cparams(("parallel",)),
        name="shared_combine_ln2",
    )(h, gk_a, gk_b, wt, ws_gate, ws_up, ws_down, ln_g, ln_b)


def _mixers(x, w_in, ml_i_bias, ml_f_bias, ml_norm_w, conv_w, conv_b, dt_bias, a_log, d_skip, ssm_norm_w):
    B, S, D = x.shape
    T = B * S
    nc = S // CHUNK
    H = ML_HEADS
    k0, v0 = ML_QK, 2 * ML_QK
    g0 = v0 + 2 * ML_WIDTH
    z0 = g0 + 4 * H
    x0 = z0 + SSM_WIDTH
    d0 = x0 + SSM_CONV_DIM
    w_big = jnp.concatenate([w_in[:, x0:d0], w_in[:, v0:g0], w_in[:, z0:x0], w_in[:, :k0]], axis=1).astype(BF16)
    w_small = jnp.concatenate(
        [w_in[:, g0:z0], w_in[:, d0:], jnp.zeros((D, PB_COLS - N_GATE_ROWS), w_in.dtype)], axis=1).astype(BF16)
    w_kt = (w_in[:, k0:v0] * (ML_DQK ** -0.5)).T.astype(BF16)
    pa, pb, kt = _proj(x.reshape(T, D), w_big, w_small, w_kt)
    pa3 = pa.reshape(B, S, PA_COLS)
    kt5 = kt.reshape(B, nc, H, ML_DQK, CHUNK)

    g_rows = pb.reshape(B, S, PB_COLS)[:, :, :N_GATE_ROWS].transpose(0, 2, 1).reshape(B, N_GATE_ROWS, nc, CHUNK)
    bias = jnp.concatenate([ml_i_bias, ml_f_bias, dt_bias]).astype(F32)
    bias_rows = jnp.broadcast_to(bias[:, None, None], (N_GATE_ROWS, 1, CHUNK))
    aneg_rows = jnp.broadcast_to((-jnp.exp(a_log.astype(F32)))[:, None, None], (2 * SSM_HEADS, 1, CHUNK))
    prep = _gate_prep(g_rows, bias_rows, aneg_rows)
    m8 = jnp.stack([prep[:, 0:H], prep[:, 2 * H:3 * H], prep[:, H:2 * H], prep[:, 3 * H:4 * H],
                    prep[:, 4 * H:5 * H], prep[:, 5 * H:6 * H], prep[:, 6 * H:7 * H], prep[:, 7 * H:8 * H]],
                   axis=2)
    mg = m8.transpose(0, 1, 3, 2, 4)
    mc = m8.transpose(0, 1, 3, 4, 2).reshape(B, H, S, 8)
    s0, SH = PREP_SSM0, SSM_HEADS
    tb = jnp.concatenate([prep[:, s0:s0 + 2 * SH], prep[:, s0 + 4 * SH:s0 + 8 * SH],
                          prep[:, s0 + 2 * SH:s0 + 4 * SH]], axis=1).transpose(0, 2, 1, 3)
    decs = prep[:, s0 + 8 * SH:s0 + 10 * SH, :, 0].reshape(B, 2, SH, nc).transpose(0, 3, 1, 2)
    decs = jnp.repeat(decs, SSM_HEADDIM, axis=3)

    y_ml = _mlstm(pa3, kt5, mg, mc, ml_norm_w.astype(F32).reshape(1, ML_WIDTH))

    Lc = next(t for t in (1024, 512, CHUNK) if S % t == 0)
    conv_w8 = jnp.concatenate([conv_w.astype(F32), jnp.zeros((8 - CONV_K, SSM_CONV_DIM), F32)], axis=0)
    xc = _conv(pa3, conv_w8, conv_b.astype(F32).reshape(1, SSM_CONV_DIM), Lc)

    cpb = next(c for c in (4, 2, 1) if nc % c == 0)
    hb = _ssd_bwd(xc, tb, decs, cpb)
    dsk = jnp.repeat(d_skip.astype(F32), SSM_HEADDIM).reshape(1, SSM_WIDTH)
    y_ssm = _ssd_main(xc, pa3, tb, decs, hb, dsk, ssm_norm_w.astype(F32).reshape(1, SSM_WIDTH), cpb)
    return y_ml.reshape(T, ML_WIDTH), y_ssm.reshape(T, SSM_WIDTH)


def _moe_dispatch(idx, rank, counts, T, blk):
    E = N_EXPERTS
    M = T * TOP_K
    nb = (M + E * (blk - 1)) // blk
    pcounts = (counts + blk - 1) // blk * blk
    pends = jnp.cumsum(pcounts)
    pstarts = (pends - pcounts).astype(jnp.int32)
    dest = _dest(pstarts, idx, rank)
    block_start = jnp.arange(nb, dtype=jnp.int32) * blk
    block_expert = jnp.minimum(jnp.sum(pends[None, :] <= block_start[:, None], axis=1), E - 1).astype(jnp.int32)
    n_used = (pends[-1] // blk).astype(jnp.int32).reshape(1)
    return dest, nb * blk, block_expert, n_used


def _layer(x, w_in, ml_i_bias, ml_f_bias, ml_norm_w, conv_w, conv_b, dt_bias, a_log, d_skip, ssm_norm_w,
           w_out, ln1_g, ln1_b, w_router, router_bias, w_exp_gate, w_exp_up, w_exp_down,
           w_sh_gate, w_sh_up, w_sh_down, ln2_g, ln2_b):
    B, S, D = x.shape
    T = B * S
    y_ml, y_ssm = _mixers(x, w_in, ml_i_bias, ml_f_bias, ml_norm_w, conv_w, conv_b, dt_bias, a_log, d_skip,
                          ssm_norm_w)
    w_r = jnp.concatenate([w_router, jnp.zeros((D, 128 - N_EXPERTS), w_router.dtype)], axis=1).astype(BF16)
    h, hp_a, hp_b, scores = _outproj(y_ml, y_ssm, x.reshape(T, D), w_out.astype(BF16),
                                     ln1_g.astype(F32).reshape(1, D), ln1_b.astype(F32).reshape(1, D), w_r)
    rbias = jnp.broadcast_to(router_bias.astype(F32)[:, None], (N_EXPERTS, 128))
    idx, wgt, rank, cnt = _route(scores, rbias)
    counts = cnt[:, 0].astype(jnp.int32)
    dest, n_slots, block_expert, n_used = _moe_dispatch(idx, rank, counts, T, MOE_BLOCK)
    dest_flat = dest.reshape(-1)
    xs_a = _sc_scatter_rows(hp_a, dest, n_slots)
    xs_b = _sc_scatter_rows(hp_b, dest, n_slots)
    os_a, os_b = _experts(xs_a, xs_b, block_expert, n_used, w_exp_gate, w_exp_up, w_exp_down)
    gk_a = _sc_gather_rows(os_a, dest_flat).reshape(TOP_K, T, D // 4)
    gk_b = _sc_gather_rows(os_b, dest_flat).reshape(TOP_K, T, D // 4)
    out = _final(h, gk_a, gk_b, wgt.T, w_sh_gate.astype(BF16), w_sh_up.astype(BF16),
                 w_sh_down.astype(BF16), ln2_g.astype(F32).reshape(1, D), ln2_b.astype(F32).reshape(1, D))
    return out.reshape(B, S, D)


def kernel(x, w_in, ml_i_bias, ml_f_bias, ml_norm_w, conv_w, conv_b, dt_bias, a_log, d_skip, ssm_norm_w, w_out,
           ln1_g, ln1_b, w_router, router_bias, w_exp_gate, w_exp_up, w_exp_down, w_sh_gate, w_sh_up, w_sh_down,
           ln2_g, ln2_b):
    for l in range(w_in.shape[0]):
        x = _layer(x, w_in[l], ml_i_bias[l], ml_f_bias[l], ml_norm_w[l], conv_w[l], conv_b[l], dt_bias[l],
                   a_log[l], d_skip[l], ssm_norm_w[l], w_out[l], ln1_g[l], ln1_b[l], w_router[l],
                   router_bias[l], w_exp_gate[l], w_exp_up[l], w_exp_down[l], w_sh_gate[l], w_sh_up[l],
                   w_sh_down[l], ln2_g[l], ln2_b[l])
    return x
```

```python
import functools

import jax
import jax.numpy as jnp
from jax import lax
from jax.experimental import pallas as pl
from jax.experimental.pallas import tpu as pltpu
from jax.experimental.pallas import tpu_sc as plsc

D_MODEL = 1024
ML_HEADS = 4
ML_DV = 256
ML_DQK = 128
ML_QK = ML_HEADS * ML_DQK
ML_WIDTH = ML_HEADS * ML_DV
SSM_WIDTH = 1024
SSM_HEADDIM = 64
SSM_HEADS = 16
SSM_GROUPS = 4
SSM_STATE = 128
SSM_CONV_DIM = SSM_WIDTH + 2 * SSM_GROUPS * SSM_STATE
CONV_K = 5
CHUNK = 128
N_EXPERTS = 64
TOP_K = 8
N_EXPERT_GROUPS = 8
TOPK_GROUPS = 4
D_EXPERT = 256
ROUTED_SCALE = 2.5
LN_EPS = 1e-5
RMS_EPS = 1e-5
DEPTH = 1
ALPHA = (2 * DEPTH) ** 0.25

PA_XBC0 = 0
PA_V0 = PA_XBC0 + SSM_CONV_DIM
PA_O0 = PA_V0 + ML_WIDTH
PA_Z0 = PA_O0 + ML_WIDTH
PA_Q0 = PA_Z0 + SSM_WIDTH
PA_COLS = PA_Q0 + ML_QK
PB_COLS = 128
N_GATE_ROWS = 2 * ML_HEADS + 2 * ML_HEADS + 2 * SSM_HEADS
PREP_SSM0 = 32
N_PREP_ROWS = PREP_SSM0 + 10 * SSM_HEADS

MOE_BLOCK = 1024
SC_WINDOW = 128
V7X_VMEM_BYTES = 64 * 1024 * 1024
VMEM_LIMIT = V7X_VMEM_BYTES * 7 // 8

BF16 = jnp.bfloat16
F32 = jnp.float32
NEG_INF = float("-inf")


def _cparams(sem, vmem=VMEM_LIMIT):
    return pltpu.CompilerParams(dimension_semantics=sem, vmem_limit_bytes=vmem)


def _dot(a, b):
    return jnp.dot(a, b, preferred_element_type=F32)


def _dot_nt(a, b):
    return lax.dot_general(a, b, (((1,), (1,)), ((), ())), preferred_element_type=F32)


def _dot_tn(a, b):
    return lax.dot_general(a, b, (((0,), (0,)), ((), ())), preferred_element_type=F32)


def _sigmoid(x):
    return 1.0 / (1.0 + jnp.exp(-x))


def _softplus(x):
    return jnp.maximum(x, 0.0) + jnp.log1p(jnp.exp(-jnp.abs(x)))


def _proj_kernel(x_ref, wb_ref, ws_ref, wkt_ref, oa_ref, ob_ref, kt_ref, xb_scr, *, L):
    @pl.when(pl.program_id(1) == 0)
    def _():
        xb = x_ref[...].astype(BF16)
        xb_scr[...] = xb
        ob_ref[...] = _dot(xb, ws_ref[...])
        kt = _dot_nt(wkt_ref[...], xb).astype(kt_ref.dtype)
        for c in range(kt_ref.shape[0]):
            kt_ref[c] = kt[:, c * L:(c + 1) * L]

    oa_ref[...] = _dot(xb_scr[...], wb_ref[...]).astype(oa_ref.dtype)


def _proj(x2d, w_big, w_small, w_kt, tm=1024, tn=2816):
    T, D = x2d.shape
    N = w_big.shape[1]
    tm = min(tm, T)
    L = CHUNK
    return pl.pallas_call(
        functools.partial(_proj_kernel, L=L),
        grid=(T // tm, N // tn),
        in_specs=[
            pl.BlockSpec((tm, D), lambda i, j: (i, 0)),
            pl.BlockSpec((D, tn), lambda i, j: (0, j)),
            pl.BlockSpec((D, PB_COLS), lambda i, j: (0, 0)),
            pl.BlockSpec((ML_QK, D), lambda i, j: (0, 0)),
        ],
        out_specs=[
            pl.BlockSpec((tm, tn), lambda i, j: (i, j)),
            pl.BlockSpec((tm, PB_COLS), lambda i, j: (i, 0)),
            pl.BlockSpec((tm // L, ML_QK, L), lambda i, j: (i, 0, 0)),
        ],
        out_shape=[jax.ShapeDtypeStruct((T, N), BF16), jax.ShapeDtypeStruct((T, PB_COLS), F32),
                   jax.ShapeDtypeStruct((T // L, ML_QK, L), BF16)],
        scratch_shapes=[pltpu.VMEM((tm, D), BF16)],
        compiler_params=_cparams(("parallel", "arbitrary")),
        name="in_proj",
    )(x2d, w_big, w_small, w_kt)


def _split3_dot(x, u):
    hi = x.astype(BF16)
    r1 = x - hi.astype(F32)
    mid = r1.astype(BF16)
    lo = (r1 - mid.astype(F32)).astype(BF16)
    return _dot(hi, u) + _dot(mid, u) + _dot(lo, u)


def _gate_kernel(g_ref, bias_ref, aneg_ref, o_ref, *, nc, L):
    g = g_ref[0] + bias_ref[...]
    nh = ML_HEADS
    ig = g[0:2 * nh]
    fpre = g[2 * nh:4 * nh]
    lf = -_softplus(-fpre)
    dt = _softplus(g[4 * nh:])
    dA = dt * aneg_ref[...]
    r = lax.broadcasted_iota(jnp.int32, (L, L), 0)
    c = lax.broadcasted_iota(jnp.int32, (L, L), 1)
    u_pre = (r <= c).astype(BF16)
    u_suf = (r >= c).astype(BF16)
    nf = nh + SSM_HEADS
    xf = jnp.concatenate([lf[0:nh], dA[0:SSM_HEADS]], axis=0).reshape(nf * nc, L)
    xb = jnp.concatenate([lf[nh:], dA[SSM_HEADS:]], axis=0).reshape(nf * nc, L)
    cf = _split3_dot(xf, u_pre).reshape(nf, nc, L)
    cb = _split3_dot(xb, u_suf).reshape(nf, nc, L)
    totf = jnp.broadcast_to(cf[:, :, L - 1:L], cf.shape)
    totb = jnp.broadcast_to(cb[:, :, 0:1], cb.shape)
    o_ref[0, 0:nh] = cf[0:nh]
    o_ref[0, nh:2 * nh] = cb[0:nh]
    o_ref[0, 2 * nh:4 * nh] = ig
    o_ref[0, 4 * nh:5 * nh] = totf[0:nh] - cf[0:nh] + ig[0:nh]
    o_ref[0, 5 * nh:6 * nh] = totb[0:nh] - cb[0:nh] + ig[nh:]
    lane = lax.broadcasted_iota(jnp.int32, (nh * nc, L), 1)
    pf = (ig[0:nh] - cf[0:nh]).reshape(nh * nc, L)
    pb = (ig[nh:] - cb[0:nh]).reshape(nh * nc, L)
    sh = 1
    while sh < L:
        pf = jnp.where(lane >= sh, jnp.maximum(pf, pltpu.roll(pf, sh, 1)), pf)
        pb = jnp.where(lane < L - sh, jnp.maximum(pb, pltpu.roll(pb, L - sh, 1)), pb)
        sh *= 2
    o_ref[0, 6 * nh:7 * nh] = cf[0:nh] + pf.reshape(nh, nc, L)
    o_ref[0, 7 * nh:8 * nh] = cb[0:nh] + pb.reshape(nh, nc, L)
    H = SSM_HEADS
    af, ab, dtf, dtb = cf[nh:], cb[nh:], dt[0:H], dt[H:]
    s0 = PREP_SSM0
    o_ref[0, s0:s0 + H] = af
    o_ref[0, s0 + H:s0 + 2 * H] = ab
    o_ref[0, s0 + 2 * H:s0 + 3 * H] = af - jnp.log(dtf)
    o_ref[0, s0 + 3 * H:s0 + 4 * H] = ab - jnp.log(dtb)
    o_ref[0, s0 + 4 * H:s0 + 5 * H] = jnp.exp(af)
    o_ref[0, s0 + 5 * H:s0 + 6 * H] = jnp.exp(ab)
    o_ref[0, s0 + 6 * H:s0 + 7 * H] = dtf * jnp.exp(totf[nh:] - af)
    o_ref[0, s0 + 7 * H:s0 + 8 * H] = dtb * jnp.exp(totb[nh:] - ab)
    o_ref[0, s0 + 8 * H:s0 + 9 * H] = jnp.exp(totf[nh:])
    o_ref[0, s0 + 9 * H:s0 + 10 * H] = jnp.exp(totb[nh:])


def _gate_prep(g_rows, bias_rows, aneg_rows):
    B, R, nc, L = g_rows.shape
    return pl.pallas_call(
        functools.partial(_gate_kernel, nc=nc, L=L),
        grid=(B,),
        in_specs=[
            pl.BlockSpec((1, R, nc, L), lambda b: (b, 0, 0, 0)),
            pl.BlockSpec((R, 1, L), lambda b: (0, 0, 0)),
            pl.BlockSpec((2 * SSM_HEADS, 1, L), lambda b: (0, 0, 0)),
        ],
        out_specs=pl.BlockSpec((1, N_PREP_ROWS, nc, L), lambda b: (b, 0, 0, 0)),
        out_shape=jax.ShapeDtypeStruct((B, N_PREP_ROWS, nc, L), F32),
        compiler_params=_cparams(("parallel",)),
        name="gate_prep",
    )(g_rows, bias_rows, aneg_rows)


def _mlstm_kernel(q_ref, kt_ref, v_ref, o_ref, mg_ref, mc_ref, nw_ref, y_ref, park_scr, c_scr, n_scr, m_scr,
                  *, nc, L, nh):
    row_i = lax.broadcasted_iota(jnp.int32, (L, L), 0)
    col_i = lax.broadcasted_iota(jnp.int32, (L, L), 1)
    ones = jnp.ones((L, L), BF16)

    c_scr[...] = jnp.zeros_like(c_scr)
    n_scr[...] = jnp.zeros_like(n_scr)
    m_scr[...] = jnp.full_like(m_scr, NEG_INF)

    def chunk(c, d, hh):
        c_ref, n_ref, m_ref = c_scr.at[2 * hh + d], n_scr.at[2 * hh + d], m_scr.at[2 * hh + d]
        c_prev, n_prev, m_prev = c_ref[...], n_ref[...], m_ref[...]
        sl = pl.ds(pl.multiple_of(c * L, L), L)
        q = q_ref[0, sl, hh * ML_DQK:(hh + 1) * ML_DQK]
        kt = kt_ref[0, c, hh]
        vo = jnp.concatenate([v_ref[0, sl, hh * ML_DV:(hh + 1) * ML_DV], ones], axis=1)
        g8 = mg_ref[0, hh, c]
        ct = mc_ref[0, hh, sl, :]
        b_r, i_r, a_r = g8[2 * d:2 * d + 1], g8[2 * d + 1:2 * d + 2], g8[4 + d:5 + d]
        b_c = jnp.broadcast_to(ct[:, 2 * d:2 * d + 1], (L, L))
        mi_c = jnp.broadcast_to(ct[:, 6 + d:7 + d], (L, L))
        if d == 0:
            gtot = b_r[:, L - 1:L]
            mask = col_i <= row_i
        else:
            gtot = b_r[:, 0:1]
            mask = col_i >= row_i
        m_loc = jnp.max(a_r, axis=1, keepdims=True)
        kwt = (kt.astype(F32) * jnp.exp(a_r - m_loc)).astype(BF16)
        uu = _dot(kwt, vo)
        qq = _dot(q, jnp.concatenate([kt, c_prev.astype(BF16), n_prev.astype(BF16)], axis=1))
        s, q_c, q_n = qq[:, 0:L], qq[:, L:L + ML_DV], qq[:, L + ML_DV:]
        dmat = jnp.where(mask, b_c - b_r + i_r, NEG_INF)
        m_inter = b_c + m_prev
        m_t = jnp.maximum(m_inter, mi_c)
        s_mat = s * jnp.exp(dmat - m_t)
        inter_w = jnp.exp(m_inter - m_t)
        ss = _dot(s_mat.astype(BF16), vo)
        den = ss[:, ML_DV:] + inter_w * q_n
        r = 1.0 / jnp.maximum(jnp.abs(den), jnp.exp(-m_t))
        num = ss[:, 0:ML_DV] + jnp.concatenate([inter_w, inter_w], axis=1) * q_c
        h = num * jnp.concatenate([r, r], axis=1)
        m_new = jnp.maximum(gtot + m_prev, m_loc)
        decay = jnp.exp(gtot + m_prev - m_new)
        sc = jnp.exp(m_loc - m_new)
        c_ref[...] = decay * c_prev + sc * uu[:, 0:ML_DV]
        n_ref[...] = decay * n_prev + sc * uu[:, ML_DV:]
        m_ref[...] = m_new
        return sl, h

    def finish(sl, hh, h):
        cols = slice(hh * ML_DV, (hh + 1) * ML_DV)
        mu = jnp.mean(h, axis=1, keepdims=True)
        d = h - mu
        var = jnp.mean(d * d, axis=1, keepdims=True)
        y = d * lax.rsqrt(var + LN_EPS) * nw_ref[:, cols]
        y = y * _sigmoid(o_ref[0, sl, cols].astype(F32))
        y_ref[0, sl, cols] = y.astype(y_ref.dtype)

    def first_half(i, carry):
        for hh in range(nh):
            for d, c in ((0, i), (1, nc - 1 - i)):
                sl, h = chunk(c, d, hh)
                park_scr[hh, sl, :] = h
        return carry

    def second_half(i, carry):
        for hh in range(nh):
            for d, c in ((0, i), (1, nc - 1 - i)):
                sl, h = chunk(c, d, hh)
                finish(sl, hh, h + park_scr[hh, sl, :])
        return carry

    half = nc // 2
    lax.fori_loop(0, half, first_half, 0, unroll=8 // nh)
    lax.fori_loop(half, nc, second_half, 0, unroll=8 // nh)


def _mlstm(pa3, kt5, mg, mc, norm_w, nh=1):
    B, S, _ = pa3.shape
    nc = S // CHUNK
    H = ML_HEADS
    q_blk, v_blk, o_blk = PA_Q0 // (nh * ML_DQK), PA_V0 // (nh * ML_DV), PA_O0 // (nh * ML_DV)
    return pl.pallas_call(
        functools.partial(_mlstm_kernel, nc=nc, L=CHUNK, nh=nh),
        grid=(B, H // nh),
        in_specs=[
            pl.BlockSpec((1, S, nh * ML_DQK), lambda b, h: (b, 0, q_blk + h)),
            pl.BlockSpec((1, nc, nh, ML_DQK, CHUNK), lambda b, h: (b, 0, h, 0, 0)),
            pl.BlockSpec((1, S, nh * ML_DV), lambda b, h: (b, 0, v_blk + h)),
            pl.BlockSpec((1, S, nh * ML_DV), lambda b, h: (b, 0, o_blk + h)),
            pl.BlockSpec((1, nh, nc, 8, CHUNK), lambda b, h: (b, h, 0, 0, 0)),
            pl.BlockSpec((1, nh, S, 8), lambda b, h: (b, h, 0, 0)),
            pl.BlockSpec((1, nh * ML_DV), lambda b, h: (0, h)),
        ],
        out_specs=pl.BlockSpec((1, S, nh * ML_DV), lambda b, h: (b, 0, h)),
        out_shape=jax.ShapeDtypeStruct((B, S, ML_WIDTH), BF16),
        scratch_shapes=[
            pltpu.VMEM((nh, S, ML_DV), F32),
            pltpu.VMEM((2 * nh, ML_DQK, ML_DV), F32),
            pltpu.VMEM((2 * nh, ML_DQK, CHUNK), F32),
            pltpu.VMEM((2 * nh, 1, 1), F32),
        ],
        compiler_params=_cparams(("parallel", "parallel")),
        name="mlstm",
    )(pa3, kt5, pa3, pa3, mg, mc, norm_w)


CONV_EDGE = 16


def _conv_shift_matrix(L):
    pad, E = CONV_K // 2, CONV_EDGE
    t = jnp.arange(L, dtype=jnp.int32)[:, None]
    blocks = []
    u = jnp.arange(L, dtype=jnp.int32)[None, :]
    for j in range(CONV_K):
        blocks.append(u == t + j - pad)
    v = jnp.arange(E, dtype=jnp.int32)[None, :]
    for j in range(pad):
        blocks.append(v - E == t + j - pad)
    for j in range(pad + 1, CONV_K):
        blocks.append(v + L == t + j - pad)
    return jnp.concatenate(blocks, axis=1).astype(BF16)


def _conv_kernel(x_ref, top_ref, bot_ref, w_ref, b_ref, sh_ref, o_ref, ext_scr, *, Lc, L):
    pad, E = CONV_K // 2, CONV_EDGE
    i = pl.program_id(1)
    first, last = i == 0, i == pl.num_programs(1) - 1
    ext_scr[0:E, :] = jnp.where(first, jnp.zeros_like(top_ref[0]), top_ref[0])
    ext_scr[E:E + Lc, :] = x_ref[0]
    ext_scr[E + Lc:2 * E + Lc, :] = jnp.where(last, jnp.zeros_like(bot_ref[0]), bot_ref[0])
    taps = [w_ref[j:j + 1, :].astype(BF16) for j in range(CONV_K)]
    scaled = lambda rows, j: rows * taps[j]
    for r0 in range(0, Lc, L):
        cen = ext_scr[E + r0:E + r0 + L, :]
        top = ext_scr[r0:r0 + E, :]
        bot = ext_scr[E + r0 + L:2 * E + r0 + L, :]
        pieces = [scaled(cen, j) for j in range(CONV_K)]
        pieces += [scaled(top, j) for j in range(pad)] + [scaled(bot, j) for j in range(pad + 1, CONV_K)]
        acc = _dot(sh_ref[...], jnp.concatenate(pieces, axis=0)) + b_ref[...]
        o_ref[0, r0:r0 + L, :] = (acc * _sigmoid(acc)).astype(o_ref.dtype)


def _conv(pa3, conv_w8, conv_b, Lc):
    B, S, _ = pa3.shape
    C = SSM_CONV_DIM
    xbc_blk = PA_XBC0 // C
    L = CHUNK
    sh = _conv_shift_matrix(L)
    epb = Lc // CONV_EDGE
    n_edge = S // CONV_EDGE
    return pl.pallas_call(
        functools.partial(_conv_kernel, Lc=Lc, L=L),
        grid=(B, S // Lc),
        in_specs=[
            pl.BlockSpec((1, Lc, C), lambda b, i: (b, i, xbc_blk)),
            pl.BlockSpec((1, CONV_EDGE, C), lambda b, i: (b, jnp.maximum(i * epb - 1, 0), xbc_blk)),
            pl.BlockSpec((1, CONV_EDGE, C), lambda b, i: (b, jnp.minimum((i + 1) * epb, n_edge - 1), xbc_blk)),
            pl.BlockSpec((8, C), lambda b, i: (0, 0)),
            pl.BlockSpec((1, C), lambda b, i: (0, 0)),
            pl.BlockSpec(sh.shape, lambda b, i: (0, 0)),
        ],
        out_specs=pl.BlockSpec((1, Lc, C), lambda b, i: (b, i, 0)),
        out_shape=jax.ShapeDtypeStruct((B, S, C), BF16),
        scratch_shapes=[pltpu.VMEM((Lc + 2 * CONV_EDGE, C), BF16)],
        compiler_params=_cparams(("parallel", "parallel")),
        name="ssm_conv",
    )(pa3, pa3, pa3, conv_w8, conv_b, sh)


def _expansion_matrix(src_cols, width):
    src = jnp.repeat(jnp.asarray(src_cols, jnp.int32), width)
    e = (jnp.arange(128, dtype=jnp.int32)[:, None] == src[None, :]).astype(BF16)
    return jnp.concatenate([e, e], axis=0)


def _split2(t):
    hi = t.astype(BF16)
    return jnp.concatenate([hi, (t - hi.astype(F32)).astype(BF16)], axis=1)


def _ssd_bwd_kernel(x_ref, b_ref, tb_ref, dec_ref, ew_ref, hb_ref, h_scr, *, L):
    G, hg, P, N = SSM_GROUPS, SSM_HEADS // SSM_GROUPS, SSM_HEADDIM, SSM_STATE
    GW = hg * P

    @pl.when(pl.program_id(1) == 0)
    def _():
        h_scr[...] = jnp.zeros_like(h_scr)

    for s in reversed(range(x_ref.shape[1] // L)):
        rows = slice(s * L, (s + 1) * L)
        w_all = _dot(_split2(tb_ref[0, s].T), ew_ref[...])
        for g in range(G):
            w = w_all[:, g * GW:(g + 1) * GW]
            xs = (x_ref[0, rows, g * GW:(g + 1) * GW].astype(F32) * w).astype(BF16)
            st = _dot_tn(b_ref[0, rows, g * N:(g + 1) * N], xs)
            h_old = h_scr[g]
            hb_ref[0, s, g] = h_old.astype(hb_ref.dtype)
            h_scr[g] = dec_ref[0, s, 1:2, g * GW:(g + 1) * GW] * h_old + st


def _ssd_bwd(xc, tb, decs, cpb):
    B, S, _ = xc.shape
    nc = S // CHUNK
    nblk = nc // cpb
    R = cpb * CHUNK
    G, N, W = SSM_GROUPS, SSM_STATE, SSM_WIDTH
    return pl.pallas_call(
        functools.partial(_ssd_bwd_kernel, L=CHUNK),
        grid=(B, nblk),
        in_specs=[
            pl.BlockSpec((1, R, W), lambda b, c: (b, nblk - 1 - c, 0)),
            pl.BlockSpec((1, R, G * N), lambda b, c: (b, nblk - 1 - c, W // (G * N))),
            pl.BlockSpec((1, cpb, 128, CHUNK), lambda b, c: (b, nblk - 1 - c, 0, 0)),
            pl.BlockSpec((1, cpb, 2, W), lambda b, c: (b, nblk - 1 - c, 0, 0)),
            pl.BlockSpec((256, W), lambda b, c: (0, 0)),
        ],
        out_specs=pl.BlockSpec((1, cpb, G, N, W // G), lambda b, c: (b, nblk - 1 - c, 0, 0, 0)),
        out_shape=jax.ShapeDtypeStruct((B, nc, G, N, W // G), BF16),
        scratch_shapes=[pltpu.VMEM((G, N, W // G), F32)],
        compiler_params=_cparams(("parallel", "arbitrary")),
        name="ssd_bwd_states",
    )(xc, xc, tb, decs, _expansion_matrix([5 * SSM_HEADS + h for h in range(SSM_HEADS)], SSM_HEADDIM))


def _ssd_main_kernel(x_ref, b_ref, c_ref, z_ref, tb_ref, dec_ref, hb_ref, dsk_ref, nw_ref, ed_ref, ee_ref,
                     y_ref, h_scr, *, L):
    G, hg, P, N = SSM_GROUPS, SSM_HEADS // SSM_GROUPS, SSM_HEADDIM, SSM_STATE
    H = SSM_HEADS
    GW = hg * P

    @pl.when(pl.program_id(1) == 0)
    def _():
        h_scr[...] = jnp.zeros_like(h_scr)

    row_i = lax.broadcasted_iota(jnp.int32, (L, L), 0)
    col_i = lax.broadcasted_iota(jnp.int32, (L, L), 1)
    lower = col_i <= row_i
    upper = col_i >= row_i
    lane_g = lax.broadcasted_iota(jnp.int32, (L, GW), 1)

    for s, g in [(s, g) for s in range(x_ref.shape[1] // L) for g in range(G)]:
        rows = slice(s * L, (s + 1) * L)
        tb = tb_ref[0, s]
        sc2 = _split2(tb.T)
        xg = x_ref[0, rows, g * GW:(g + 1) * GW]
        bg = b_ref[0, rows, g * N:(g + 1) * N]
        cg = c_ref[0, rows, g * N:(g + 1) * N]
        cb = _dot_nt(cg, bg)
        acol = _dot(sc2, ed_ref[g])
        ecol = _dot(sc2, ee_ref[g])
        ms, xms = [], []
        for j in range(hg):
            hd = g * hg + j
            rf_r, rb_r = tb[6 * H + hd:6 * H + hd + 1], tb[7 * H + hd:7 * H + hd + 1]
            af_c, ab_c = acol[:, j * L:(j + 1) * L], acol[:, (hg + j) * L:(hg + j + 1) * L]
            fm = jnp.exp(jnp.where(lower, af_c - rf_r, NEG_INF))
            bm = jnp.exp(jnp.where(upper, ab_c - rb_r, NEG_INF))
            ms.append((cb * (fm + bm)).astype(BF16))
            xms.append(jnp.where((lane_g >= j * P) & (lane_g < (j + 1) * P), xg, jnp.zeros_like(xg)))
        efc, ebc, wfc = ecol[:, 0:GW], ecol[:, GW:2 * GW], ecol[:, 2 * GW:]
        y = _dot(jnp.concatenate(ms, axis=1), jnp.concatenate(xms, axis=0))
        h_old = h_scr[g]
        y = y + efc * _dot(cg, h_old.astype(BF16))
        y = y + ebc * _dot(cg, hb_ref[0, s, g])
        xgf = xg.astype(F32)
        y = y + dsk_ref[:, g * GW:(g + 1) * GW] * xgf
        xs = (xgf * wfc).astype(BF16)
        h_scr[g] = dec_ref[0, s, 0:1, g * GW:(g + 1) * GW] * h_old + _dot_tn(bg, xs)
        zg = z_ref[0, rows, g * GW:(g + 1) * GW].astype(F32)
        y = y * (zg * _sigmoid(zg))
        y = y * lax.rsqrt(jnp.mean(y * y, axis=1, keepdims=True) + RMS_EPS)
        y_ref[0, rows, g * GW:(g + 1) * GW] = (y * nw_ref[:, g * GW:(g + 1) * GW]).astype(y_ref.dtype)


def _ssd_main(xc, pa3, tb, decs, hb, dskip_row, norm_w, cpb):
    B, S, _ = xc.shape
    nc = S // CHUNK
    R = cpb * CHUNK
    G, N, W = SSM_GROUPS, SSM_STATE, SSM_WIDTH
    H, hg, P = SSM_HEADS, SSM_HEADS // SSM_GROUPS, SSM_HEADDIM
    z_blk = PA_Z0 // W
    e_dec = jnp.stack([_expansion_matrix([q * H + g * hg + j for q in (0, 1) for j in range(hg)], CHUNK)
                       for g in range(G)])
    e_ew = jnp.stack([_expansion_matrix([q * H + g * hg + j for q in (2, 3, 4) for j in range(hg)], P)
                      for g in range(G)])
    return pl.pallas_call(
        functools.partial(_ssd_main_kernel, L=CHUNK),
        grid=(B, nc // cpb),
        in_specs=[
            pl.BlockSpec((1, R, W), lambda b, c: (b, c, 0)),
            pl.BlockSpec((1, R, G * N), lambda b, c: (b, c, W // (G * N))),
            pl.BlockSpec((1, R, G * N), lambda b, c: (b, c, W // (G * N) + 1)),
            pl.BlockSpec((1, R, W), lambda b, c: (b, c, z_blk)),
            pl.BlockSpec((1, cpb, 128, CHUNK), lambda b, c: (b, c, 0, 0)),
            pl.BlockSpec((1, cpb, 2, W), lambda b, c: (b, c, 0, 0)),
            pl.BlockSpec((1, cpb, G, N, W // G), lambda b, c: (b, c, 0, 0, 0)),
            pl.BlockSpec((1, W), lambda b, c: (0, 0)),
            pl.BlockSpec((1, W), lambda b, c: (0, 0)),
            pl.BlockSpec((G, 256, 2 * hg * CHUNK), lambda b, c: (0, 0, 0)),
            pl.BlockSpec((G, 256, 3 * W // G), lambda b, c: (0, 0, 0)),
        ],
        out_specs=pl.BlockSpec((1, R, W), lambda b, c: (b, c, 0)),
        out_shape=jax.ShapeDtypeStruct((B, S, W), BF16),
        scratch_shapes=[pltpu.VMEM((G, N, W // G), F32)],
        compiler_params=_cparams(("parallel", "arbitrary")),
        name="ssd_main",
    )(xc, xc, xc, pa3, tb, decs, hb, dskip_row, norm_w, e_dec, e_ew)


def _pack_bf16_pairs(v):
    n = v.shape[1] // 2
    hi = lax.bitcast_convert_type(v[:, :n].astype(BF16).astype(F32), jnp.uint32)
    lo = lax.bitcast_convert_type(v[:, n:].astype(BF16).astype(F32), jnp.uint32)
    return (hi & jnp.uint32(0xFFFF0000)) | (lo >> 16)


def _unpack_bf16_pairs(u):
    hi = lax.bitcast_convert_type(u & jnp.uint32(0xFFFF0000), F32)
    lo = lax.bitcast_convert_type(u << 16, F32)
    return hi, lo


def _layer_norm(v, g, b):
    mu = jnp.mean(v, axis=1, keepdims=True)
    d = v - mu
    var = jnp.mean(d * d, axis=1, keepdims=True)
    return d * lax.rsqrt(var + LN_EPS) * g + b


def _outproj_kernel(yml_ref, yssm_ref, x_ref, wo_ref, g_ref, b_ref, wr_ref, h_ref, hpa_ref, hpb_ref, sc_ref):
    mix = _dot(yml_ref[...], wo_ref[0:ML_WIDTH, :]) + _dot(yssm_ref[...], wo_ref[ML_WIDTH:, :])
    h = _layer_norm(ALPHA * x_ref[...] + mix, g_ref[...], b_ref[...])
    h_ref[...] = h
    hp = _pack_bf16_pairs(h)
    q = hp.shape[1] // 2
    hpa_ref[...] = hp[:, :q]
    hpb_ref[...] = hp[:, q:]
    sc_ref[...] = _sigmoid(_dot(h.astype(BF16), wr_ref[...]))


def _outproj(yml, yssm, x2d, w_out, ln_g, ln_b, w_router, tm=1024):
    T, D = x2d.shape
    tm = min(tm, T)
    row = lambda i: (i, 0)
    fixed = lambda i: (0, 0)
    return pl.pallas_call(
        _outproj_kernel,
        grid=(T // tm,),
        in_specs=[
            pl.BlockSpec((tm, ML_WIDTH), row),
            pl.BlockSpec((tm, SSM_WIDTH), row),
            pl.BlockSpec((tm, D), row),
            pl.BlockSpec((ML_WIDTH + SSM_WIDTH, D), fixed),
            pl.BlockSpec((1, D), fixed),
            pl.BlockSpec((1, D), fixed),
            pl.BlockSpec((D, 128), fixed),
        ],
        out_specs=[pl.BlockSpec((tm, D), row), pl.BlockSpec((tm, D // 4), row), pl.BlockSpec((tm, D // 4), row),
                   pl.BlockSpec((tm, 128), row)],
        out_shape=[
            jax.ShapeDtypeStruct((T, D), F32),
            jax.ShapeDtypeStruct((T, D // 4), jnp.uint32),
            jax.ShapeDtypeStruct((T, D // 4), jnp.uint32),
            jax.ShapeDtypeStruct((T, 128), F32),
        ],
        compiler_params=_cparams(("parallel",)),
        name="out_proj_ln1_router",
    )(yml, yssm, x2d, w_out, ln_g, ln_b, w_router)


def _route_kernel(sc_ref, rb_ref, idx_ref, w_ref, rank_ref, cnt_ref, carry_scr, *, Tt):
    E, NG = N_EXPERTS, N_EXPERT_GROUPS
    EG = E // NG

    @pl.when(pl.program_id(0) == 0)
    def _():
        carry_scr[...] = jnp.zeros_like(carry_scr)

    s = sc_ref[...].T[0:E]
    s3 = s.reshape(EG, NG, Tt)
    ch3 = (s + rb_ref[:, 0:1]).reshape(EG, NG, Tt)
    e_in = lax.broadcasted_iota(jnp.int32, (EG, NG, Tt), 0)
    g_in = lax.broadcasted_iota(jnp.int32, (EG, NG, Tt), 1)
    f_i = g_in * EG + e_in
    m1 = jnp.max(ch3, axis=0, keepdims=True)
    i1 = jnp.min(jnp.where(ch3 == m1, e_in, EG), axis=0, keepdims=True)
    m2 = jnp.max(jnp.where(e_in == i1, NEG_INF, ch3), axis=0, keepdims=True)
    gs = m1 + m2
    g_i = lax.broadcasted_iota(jnp.int32, (1, NG, Tt), 1)
    gsel = jnp.zeros((1, NG, Tt), jnp.bool_)
    cur = gs
    for _ in range(TOPK_GROUPS):
        mx = jnp.max(cur, axis=1, keepdims=True)
        pick = jnp.min(jnp.where(cur == mx, g_i, NG), axis=1, keepdims=True)
        hit = g_i == pick
        gsel = gsel | hit
        cur = jnp.where(hit, NEG_INF, cur)
    cur = jnp.where(gsel, ch3, NEG_INF)
    selmask = jnp.zeros((EG, NG, Tt), jnp.bool_)
    idxs, ws = [], []
    for _ in range(TOP_K):
        mx = jnp.max(jnp.max(cur, axis=0, keepdims=True), axis=1, keepdims=True)
        cand = jnp.where(cur == mx, f_i, E)
        pick = jnp.min(jnp.min(cand, axis=0, keepdims=True), axis=1, keepdims=True)
        hit = f_i == pick
        wk = jnp.sum(jnp.sum(jnp.where(hit, s3, 0.0), axis=0, keepdims=True), axis=1, keepdims=True)
        idxs.append(pick.reshape(1, Tt))
        ws.append(wk.reshape(1, Tt))
        selmask = selmask | hit
        cur = jnp.where(hit, NEG_INF, cur)
    wsum = ws[0]
    for k in range(1, TOP_K):
        wsum = wsum + ws[k]
    r = lax.broadcasted_iota(jnp.int32, (Tt, Tt), 0)
    c = lax.broadcasted_iota(jnp.int32, (Tt, Tt), 1)
    ustrict = (r < c).astype(BF16)
    self32 = jnp.where(selmask, 1.0, 0.0).reshape(E, Tt)
    excl = _dot(self32.astype(BF16), ustrict) + carry_scr[:, 0:1]
    excl3 = excl.reshape(EG, NG, Tt)
    for k in range(TOP_K):
        hit = f_i == idxs[k].reshape(1, 1, Tt)
        rk = jnp.sum(jnp.sum(jnp.where(hit, excl3, 0.0), axis=0, keepdims=True), axis=1, keepdims=True)
        idx_ref[k:k + 1, :] = idxs[k]
        w_ref[k:k + 1, :] = ws[k] / wsum * ROUTED_SCALE
        rank_ref[k:k + 1, :] = rk.reshape(1, Tt).astype(jnp.int32)
    carry_scr[...] = carry_scr[...] + jnp.sum(self32, axis=1, keepdims=True)
    cnt_ref[...] = carry_scr[...]


def _route(scores, rbias, Tt=512):
    T = scores.shape[0]
    kt = lambda i: (0, i)
    return pl.pallas_call(
        functools.partial(_route_kernel, Tt=Tt),
        grid=(T // Tt,),
        in_specs=[pl.BlockSpec((Tt, 128), lambda i: (i, 0)), pl.BlockSpec((N_EXPERTS, 128), lambda i: (0, 0))],
        out_specs=[
            pl.BlockSpec((TOP_K, Tt), kt),
            pl.BlockSpec((TOP_K, Tt), kt),
            pl.BlockSpec((TOP_K, Tt), kt),
            pl.BlockSpec((N_EXPERTS, 128), lambda i: (0, 0)),
        ],
        out_shape=[
            jax.ShapeDtypeStruct((TOP_K, T), jnp.int32),
            jax.ShapeDtypeStruct((TOP_K, T), F32),
            jax.ShapeDtypeStruct((TOP_K, T), jnp.int32),
            jax.ShapeDtypeStruct((N_EXPERTS, 128), F32),
        ],
        scratch_shapes=[pltpu.VMEM((N_EXPERTS, 128), F32)],
        compiler_params=_cparams(("arbitrary",)),
        name="route_topk",
    )(scores, rbias)


def _dest_kernel(ps_ref, idx_ref, rank_ref, o_ref):
    idx = idx_ref[...]
    start = jnp.zeros(idx.shape, jnp.int32)
    for e in range(N_EXPERTS):
        start = jnp.where(idx == e, ps_ref[e], start)
    o_ref[...] = start + rank_ref[...]


def _dest(pstarts, idx, rank, Tt=2048):
    K, T = idx.shape
    Tt = min(Tt, T)
    blk = pl.BlockSpec((K, Tt), lambda i, ps: (0, i))
    return pl.pallas_call(
        _dest_kernel,
        grid_spec=pltpu.PrefetchScalarGridSpec(num_scalar_prefetch=1, grid=(T // Tt,), in_specs=[blk, blk],
                                               out_specs=blk),
        out_shape=jax.ShapeDtypeStruct((K, T), jnp.int32),
        compiler_params=_cparams(("parallel",)),
        name="dispatch_slots",
    )(pstarts, idx, rank)


def _sc_mesh():
    return plsc.VectorSubcoreMesh(core_axis_name="c", subcore_axis_name="s")


def _sc_scatter_rows(src, idx, n_out, win=SC_WINDOW):
    K, T = idx.shape
    W = src.shape[1]

    @functools.partial(pl.kernel, out_type=jax.ShapeDtypeStruct((n_out, W), src.dtype), mesh=_sc_mesh(),
                       scratch_types=[], name="sc_scatter_rows")
    def k(x_hbm, i_hbm, o_hbm):
        def body(x_vmem, i_vmem):
            for j in range(K):
                pltpu.sync_copy(x_vmem, o_hbm.at[i_vmem.at[j]])

        pltpu.emit_pipeline(
            body,
            grid=(T // win,),
            in_specs=[pl.BlockSpec((win, W), lambda i: (i, 0)), pl.BlockSpec((K, win), lambda i: (0, i))],
            out_specs=[],
            core_axis_name=("c", "s"),
            dimension_semantics=(pltpu.PARALLEL,),
        )(x_hbm, i_hbm)

    return k(src, idx)


def _sc_gather_rows(table, idx, win=SC_WINDOW):
    n = idx.shape[0]
    W = table.shape[1]

    @functools.partial(pl.kernel, out_type=jax.ShapeDtypeStruct((n, W), table.dtype), mesh=_sc_mesh(),
                       scratch_types=[], name="sc_gather_rows")
    def k(t_hbm, i_hbm, o_hbm):
        def body(i_vmem, o_vmem):
            pltpu.sync_copy(t_hbm.at[i_vmem.at[0]], o_vmem)

        pltpu.emit_pipeline(
            body,
            grid=(n // win,),
            in_specs=[pl.BlockSpec((1, win), lambda i: (0, i))],
            out_specs=[pl.BlockSpec((win, W), lambda i: (i, 0))],
            core_axis_name=("c", "s"),
            dimension_semantics=(pltpu.PARALLEL,),
        )(i_hbm, o_hbm)

    return k(table, idx.reshape(1, n))


def _expert_kernel(be_ref, nu_ref, xa_ref, xb_ref, wg_ref, wu_ref, wd_ref, oa_ref, ob_ref, wg_s, wu_s, wd_s):
    i = pl.program_id(0)
    active = i < nu_ref[0]

    @pl.when(active & ((i == 0) | (be_ref[i] != be_ref[jnp.maximum(i - 1, 0)])))
    def _():
        wg_s[...] = wg_ref[0].astype(BF16)
        wu_s[...] = wu_ref[0].astype(BF16)
        wd_s[...] = wd_ref[0].astype(BF16)

    @pl.when(active)
    def _():
        hi, lo = _unpack_bf16_pairs(jnp.concatenate([xa_ref[...], xb_ref[...]], axis=1))
        x = jnp.concatenate([hi.astype(BF16), lo.astype(BF16)], axis=1)
        gt = _dot(x, wg_s[...])
        up = _dot(x, wu_s[...])
        hmid = (gt * _sigmoid(gt) * up).astype(BF16)
        packed = _pack_bf16_pairs(_dot(hmid, wd_s[...]))
        q = packed.shape[1] // 2
        oa_ref[...] = packed[:, :q]
        ob_ref[...] = packed[:, q:]

    @pl.when(jnp.logical_not(active))
    def _():
        oa_ref[...] = jnp.zeros_like(oa_ref)
        ob_ref[...] = jnp.zeros_like(ob_ref)


def _experts(xs_a, xs_b, block_expert, n_used, w_gate, w_up, w_down, blk=MOE_BLOCK):
    P, quarter = xs_a.shape
    nb = P // blk
    D, F = D_MODEL, D_EXPERT
    cur = lambda i, be, nu: jnp.minimum(i, nu[0] - 1)
    return pl.pallas_call(
        _expert_kernel,
        grid_spec=pltpu.PrefetchScalarGridSpec(
            num_scalar_prefetch=2,
            grid=(nb,),
            in_specs=[
                pl.BlockSpec((blk, quarter), lambda i, be, nu: (cur(i, be, nu), 0)),
                pl.BlockSpec((blk, quarter), lambda i, be, nu: (cur(i, be, nu), 0)),
                pl.BlockSpec((1, D, F), lambda i, be, nu: (be[cur(i, be, nu)], 0, 0)),
                pl.BlockSpec((1, D, F), lambda i, be, nu: (be[cur(i, be, nu)], 0, 0)),
                pl.BlockSpec((1, F, D), lambda i, be, nu: (be[cur(i, be, nu)], 0, 0)),
            ],
            out_specs=[pl.BlockSpec((blk, quarter), lambda i, be, nu: (i, 0)),
                       pl.BlockSpec((blk, quarter), lambda i, be, nu: (i, 0))],
            scratch_shapes=[pltpu.VMEM((D, F), BF16), pltpu.VMEM((D, F), BF16), pltpu.VMEM((F, D), BF16)],
        ),
        out_shape=[jax.ShapeDtypeStruct((P, quarter), jnp.uint32), jax.ShapeDtypeStruct((P, quarter), jnp.uint32)],
        compiler_params=_cparams(("arbitrary",)),
        name="expert_mlp",
    )(block_expert, n_used, xs_a, xs_b, w_gate, w_up, w_down)


def _final_kernel(h_ref, ga_ref, gb_ref, wt_ref, wsg_ref, wsu_ref, wsd_ref, g_ref, b_ref, o_ref):
    h = h_ref[...]
    hb = h.astype(BF16)
    gt = _dot(hb, wsg_ref[...])
    up = _dot(hb, wsu_ref[...])
    shared = _dot((gt * _sigmoid(gt) * up).astype(BF16), wsd_ref[...])
    acc_hi = None
    for k in range(TOP_K):
        hi, lo = _unpack_bf16_pairs(jnp.concatenate([ga_ref[k], gb_ref[k]], axis=1))
        wk = wt_ref[:, k:k + 1]
        acc_hi = wk * hi if acc_hi is None else acc_hi + wk * hi
        acc_lo = wk * lo if k == 0 else acc_lo + wk * lo
    routed = jnp.concatenate([acc_hi, acc_lo], axis=1)
    o_ref[...] = _layer_norm(ALPHA * h + (routed + shared), g_ref[...], b_ref[...])


def _final(h, gk_a, gk_b, wt, ws_gate, ws_up, ws_down, ln_g, ln_b, tm=512):
    T, D = h.shape
    tm = min(tm, T)
    F = D_EXPERT
    fixed = lambda i: (0, 0)
    return pl.pallas_call(
        _final_kernel,
        grid=(T // tm,),
        in_specs=[
            pl.BlockSpec((tm, D), lambda i: (i, 0)),
            pl.BlockSpec((TOP_K, tm, D // 4), lambda i: (0, i, 0)),
            pl.BlockSpec((TOP_K, tm, D // 4), lambda i: (0, i, 0)),
            pl.BlockSpec((tm, TOP_K), lambda i: (i, 0)),
            pl.BlockSpec((D, F), fixed),
            pl.BlockSpec((D, F), fixed),
            pl.BlockSpec((F, D), fixed),
            pl.BlockSpec((1, D), fixed),
            pl.BlockSpec((1, D), fixed),
        ],
        out_specs=pl.BlockSpec((tm, D), lambda i: (i, 0)),
        out_shape=jax.ShapeDtypeStruct((T, D), F32),
        compiler_params=_cparams(("parallel",)),
        name="shared_combine_ln2",
    )(h, gk_a, gk_b, wt, ws_gate, ws_up, ws_down, ln_g, ln_b)


def _mixers(x, w_in, ml_i_bias, ml_f_bias, ml_norm_w, conv_w, conv_b, dt_bias, a_log, d_skip, ssm_norm_w):
    B, S, D = x.shape
    T = B * S
    nc = S // CHUNK
    H = ML_HEADS
    k0, v0 = ML_QK, 2 * ML_QK
    g0 = v0 + 2 * ML_WIDTH
    z0 = g0 + 4 * H
    x0 = z0 + SSM_WIDTH
    d0 = x0 + SSM_CONV_DIM
    w_big = jnp.concatenate([w_in[:, x0:d0], w_in[:, v0:g0], w_in[:, z0:x0], w_in[:, :k0]], axis=1).astype(BF16)
    w_small = jnp.concatenate(
        [w_in[:, g0:z0], w_in[:, d0:], jnp.zeros((D, PB_COLS - N_GATE_ROWS), w_in.dtype)], axis=1).astype(BF16)
    w_kt = (w_in[:, k0:v0] * (ML_DQK ** -0.5)).T.astype(BF16)
    pa, pb, kt = _proj(x.reshape(T, D), w_big, w_small, w_kt)
    pa3 = pa.reshape(B, S, PA_COLS)
    kt5 = kt.reshape(B, nc, H, ML_DQK, CHUNK)

    g_rows = pb.reshape(B, S, PB_COLS)[:, :, :N_GATE_ROWS].transpose(0, 2, 1).reshape(B, N_GATE_ROWS, nc, CHUNK)
    bias = jnp.concatenate([ml_i_bias, ml_f_bias, dt_bias]).astype(F32)
    bias_rows = jnp.broadcast_to(bias[:, None, None], (N_GATE_ROWS, 1, CHUNK))
    aneg_rows = jnp.broadcast_to((-jnp.exp(a_log.astype(F32)))[:, None, None], (2 * SSM_HEADS, 1, CHUNK))
    prep = _gate_prep(g_rows, bias_rows, aneg_rows)
    m8 = jnp.stack([prep[:, 0:H], prep[:, 2 * H:3 * H], prep[:, H:2 * H], prep[:, 3 * H:4 * H],
                    prep[:, 4 * H:5 * H], prep[:, 5 * H:6 * H], prep[:, 6 * H:7 * H], prep[:, 7 * H:8 * H]],
                   axis=2)
    mg = m8.transpose(0, 1, 3, 2, 4)
    mc = m8.transpose(0, 1, 3, 4, 2).reshape(B, H, S, 8)
    s0, SH = PREP_SSM0, SSM_HEADS
    tb = jnp.concatenate([prep[:, s0:s0 + 2 * SH], prep[:, s0 + 4 * SH:s0 + 8 * SH],
                          prep[:, s0 + 2 * SH:s0 + 4 * SH]], axis=1).transpose(0, 2, 1, 3)
    decs = prep[:, s0 + 8 * SH:s0 + 10 * SH, :, 0].reshape(B, 2, SH, nc).transpose(0, 3, 1, 2)
    decs = jnp.repeat(decs, SSM_HEADDIM, axis=3)

    y_ml = _mlstm(pa3, kt5, mg, mc, ml_norm_w.astype(F32).reshape(1, ML_WIDTH))

    Lc = next(t for t in (1024, 512, CHUNK) if S % t == 0)
    conv_w8 = jnp.concatenate([conv_w.astype(F32), jnp.zeros((8 - CONV_K, SSM_CONV_DIM), F32)], axis=0)
    xc = _conv(pa3, conv_w8, conv_b.astype(F32).reshape(1, SSM_CONV_DIM), Lc)

    cpb = next(c for c in (4, 2, 1) if nc % c == 0)
    hb = _ssd_bwd(xc, tb, decs, cpb)
    dsk = jnp.repeat(d_skip.astype(F32), SSM_HEADDIM).reshape(1, SSM_WIDTH)
    y_ssm = _ssd_main(xc, pa3, tb, decs, hb, dsk, ssm_norm_w.astype(F32).reshape(1, SSM_WIDTH), cpb)
    return y_ml.reshape(T, ML_WIDTH), y_ssm.reshape(T, SSM_WIDTH)


def _moe_dispatch(idx, rank, counts, T, blk):
    E = N_EXPERTS
    M = T * TOP_K
    nb = (M + E * (blk - 1)) // blk
    pcounts = (counts + blk - 1) // blk * blk
    pends = jnp.cumsum(pcounts)
    pstarts = (pends - pcounts).astype(jnp.int32)
    dest = _dest(pstarts, idx, rank)
    block_start = jnp.arange(nb, dtype=jnp.int32) * blk
    block_expert = jnp.minimum(jnp.sum(pends[None, :] <= block_start[:, None], axis=1), E - 1).astype(jnp.int32)
    n_used = (pends[-1] // blk).astype(jnp.int32).reshape(1)
    return dest, nb * blk, block_expert, n_used


def _layer(x, w_in, ml_i_bias, ml_f_bias, ml_norm_w, conv_w, conv_b, dt_bias, a_log, d_skip, ssm_norm_w,
           w_out, ln1_g, ln1_b, w_router, router_bias, w_exp_gate, w_exp_up, w_exp_down,
           w_sh_gate, w_sh_up, w_sh_down, ln2_g, ln2_b):
    B, S, D = x.shape
    T = B * S
    y_ml, y_ssm = _mixers(x, w_in, ml_i_bias, ml_f_bias, ml_norm_w, conv_w, conv_b, dt_bias, a_log, d_skip,
                          ssm_norm_w)
    EG = N_EXPERTS // N_EXPERT_GROUPS
    pos = jnp.arange(N_EXPERTS, dtype=jnp.int32)
    expert_at = (pos % N_EXPERT_GROUPS) * EG + pos // N_EXPERT_GROUPS
    pos_of = (pos % EG) * N_EXPERT_GROUPS + pos // EG
    w_r = jnp.concatenate([w_router[:, expert_at], jnp.zeros((D, 128 - N_EXPERTS), w_router.dtype)],
                          axis=1).astype(BF16)
    h, hp_a, hp_b, scores = _outproj(y_ml, y_ssm, x.reshape(T, D), w_out.astype(BF16),
                                     ln1_g.astype(F32).reshape(1, D), ln1_b.astype(F32).reshape(1, D), w_r)
    rbias = jnp.broadcast_to(router_bias.astype(F32)[expert_at, None], (N_EXPERTS, 128))
    idx, wgt, rank, cnt = _route(scores, rbias)
    counts = cnt[pos_of, 0].astype(jnp.int32)
    dest, n_slots, block_expert, n_used = _moe_dispatch(idx, rank, counts, T, MOE_BLOCK)
    dest_flat = dest.reshape(-1)
    xs_a = _sc_scatter_rows(hp_a, dest, n_slots)
    xs_b = _sc_scatter_rows(hp_b, dest, n_slots)
    os_a, os_b = _experts(xs_a, xs_b, block_expert, n_used, w_exp_gate, w_exp_up, w_exp_down)
    gk_a = _sc_gather_rows(os_a, dest_flat).reshape(TOP_K, T, D // 4)
    gk_b = _sc_gather_rows(os_b, dest_flat).reshape(TOP_K, T, D // 4)
    out = _final(h, gk_a, gk_b, wgt.T, w_sh_gate.astype(BF16), w_sh_up.astype(BF16),
                 w_sh_down.astype(BF16), ln2_g.astype(F32).reshape(1, D), ln2_b.astype(F32).reshape(1, D))
    return out.reshape(B, S, D)


def kernel(x, w_in, ml_i_bias, ml_f_bias, ml_norm_w, conv_w, conv_b, dt_bias, a_log, d_skip, ssm_norm_w, w_out,
           ln1_g, ln1_b, w_router, router_bias, w_exp_gate, w_exp_up, w_exp_down, w_sh_gate, w_sh_up, w_sh_down,
           ln2_g, ln2_b):
    for l in range(w_in.shape[0]):
        x = _layer(x, w_in[l], ml_i_bias[l], ml_f_bias[l], ml_norm_w[l], conv_w[l], conv_b[l], dt_bias[l],
                   a_log[l], d_skip[l], ssm_norm_w[l], w_out[l], ln1_g[l], ln1_b[l], w_router[l],
                   router_bias[l], w_exp_gate[l], w_exp_up[l], w_exp_down[l], w_sh_gate[l], w_sh_up[l],
                   w_sh_down[l], ln2_g[l], ln2_b[l])
    return x
```

```python
import functools

import jax
import jax.numpy as jnp
from jax import lax
from jax.experimental import pallas as pl
from jax.experimental.pallas import tpu as pltpu
from jax.experimental.pallas import tpu_sc as plsc

D_MODEL = 1024
ML_HEADS = 4
ML_DV = 256
ML_DQK = 128
ML_QK = ML_HEADS * ML_DQK
ML_WIDTH = ML_HEADS * ML_DV
SSM_WIDTH = 1024
SSM_HEADDIM = 64
SSM_HEADS = 16
SSM_GROUPS = 4
SSM_STATE = 128
SSM_CONV_DIM = SSM_WIDTH + 2 * SSM_GROUPS * SSM_STATE
CONV_K = 5
CHUNK = 128
N_EXPERTS = 64
TOP_K = 8
N_EXPERT_GROUPS = 8
TOPK_GROUPS = 4
D_EXPERT = 256
ROUTED_SCALE = 2.5
LN_EPS = 1e-5
RMS_EPS = 1e-5
DEPTH = 1
ALPHA = (2 * DEPTH) ** 0.25

PA_XBC0 = 0
PA_V0 = PA_XBC0 + SSM_CONV_DIM
PA_O0 = PA_V0 + ML_WIDTH
PA_Z0 = PA_O0 + ML_WIDTH
PA_Q0 = PA_Z0 + SSM_WIDTH
PA_COLS = PA_Q0 + ML_QK
PB_COLS = 128
N_GATE_ROWS = 2 * ML_HEADS + 2 * ML_HEADS + 2 * SSM_HEADS
PREP_SSM0 = 32
N_PREP_ROWS = PREP_SSM0 + 10 * SSM_HEADS

MOE_BLOCK = 1024
SC_WINDOW = 128
V7X_VMEM_BYTES = 64 * 1024 * 1024
VMEM_LIMIT = V7X_VMEM_BYTES * 7 // 8

BF16 = jnp.bfloat16
F32 = jnp.float32
NEG_INF = float("-inf")


def _cparams(sem, vmem=VMEM_LIMIT):
    return pltpu.CompilerParams(dimension_semantics=sem, vmem_limit_bytes=vmem)


def _dot(a, b):
    return jnp.dot(a, b, preferred_element_type=F32)


def _dot_nt(a, b):
    return lax.dot_general(a, b, (((1,), (1,)), ((), ())), preferred_element_type=F32)


def _dot_tn(a, b):
    return lax.dot_general(a, b, (((0,), (0,)), ((), ())), preferred_element_type=F32)


def _sigmoid(x):
    return 1.0 / (1.0 + jnp.exp(-x))


def _softplus(x):
    return jnp.maximum(x, 0.0) + jnp.log1p(jnp.exp(-jnp.abs(x)))


def _proj_kernel(x_ref, wb_ref, ws_ref, wkt_ref, oa_ref, ob_ref, kt_ref, xb_scr, *, L):
    @pl.when(pl.program_id(1) == 0)
    def _():
        xb = x_ref[...].astype(BF16)
        xb_scr[...] = xb
        ob_ref[...] = _dot(xb, ws_ref[...])
        kt = _dot_nt(wkt_ref[...], xb).astype(kt_ref.dtype)
        for c in range(kt_ref.shape[0]):
            kt_ref[c] = kt[:, c * L:(c + 1) * L]

    oa_ref[...] = _dot(xb_scr[...], wb_ref[...]).astype(oa_ref.dtype)


def _proj(x2d, w_big, w_small, w_kt, tm=1024, tn=2816):
    T, D = x2d.shape
    N = w_big.shape[1]
    tm = min(tm, T)
    L = CHUNK
    return pl.pallas_call(
        functools.partial(_proj_kernel, L=L),
        grid=(T // tm, N // tn),
        in_specs=[
            pl.BlockSpec((tm, D), lambda i, j: (i, 0)),
            pl.BlockSpec((D, tn), lambda i, j: (0, j)),
            pl.BlockSpec((D, PB_COLS), lambda i, j: (0, 0)),
            pl.BlockSpec((ML_QK, D), lambda i, j: (0, 0)),
        ],
        out_specs=[
            pl.BlockSpec((tm, tn), lambda i, j: (i, j)),
            pl.BlockSpec((tm, PB_COLS), lambda i, j: (i, 0)),
            pl.BlockSpec((tm // L, ML_QK, L), lambda i, j: (i, 0, 0)),
        ],
        out_shape=[jax.ShapeDtypeStruct((T, N), BF16), jax.ShapeDtypeStruct((T, PB_COLS), F32),
                   jax.ShapeDtypeStruct((T // L, ML_QK, L), BF16)],
        scratch_shapes=[pltpu.VMEM((tm, D), BF16)],
        compiler_params=_cparams(("parallel", "arbitrary")),
        name="in_proj",
    )(x2d, w_big, w_small, w_kt)


def _split3_dot(x, u):
    hi = x.astype(BF16)
    r1 = x - hi.astype(F32)
    mid = r1.astype(BF16)
    lo = (r1 - mid.astype(F32)).astype(BF16)
    return _dot(hi, u) + _dot(mid, u) + _dot(lo, u)


def _gate_kernel(g_ref, bias_ref, aneg_ref, o_ref, *, nc, L):
    g = g_ref[0] + bias_ref[...]
    nh = ML_HEADS
    ig = g[0:2 * nh]
    fpre = g[2 * nh:4 * nh]
    lf = -_softplus(-fpre)
    dt = _softplus(g[4 * nh:])
    dA = dt * aneg_ref[...]
    r = lax.broadcasted_iota(jnp.int32, (L, L), 0)
    c = lax.broadcasted_iota(jnp.int32, (L, L), 1)
    u_pre = (r <= c).astype(BF16)
    u_suf = (r >= c).astype(BF16)
    nf = nh + SSM_HEADS
    xf = jnp.concatenate([lf[0:nh], dA[0:SSM_HEADS]], axis=0).reshape(nf * nc, L)
    xb = jnp.concatenate([lf[nh:], dA[SSM_HEADS:]], axis=0).reshape(nf * nc, L)
    cf = _split3_dot(xf, u_pre).reshape(nf, nc, L)
    cb = _split3_dot(xb, u_suf).reshape(nf, nc, L)
    totf = jnp.broadcast_to(cf[:, :, L - 1:L], cf.shape)
    totb = jnp.broadcast_to(cb[:, :, 0:1], cb.shape)
    o_ref[0, 0:nh] = cf[0:nh]
    o_ref[0, nh:2 * nh] = cb[0:nh]
    o_ref[0, 2 * nh:4 * nh] = ig
    o_ref[0, 4 * nh:5 * nh] = totf[0:nh] - cf[0:nh] + ig[0:nh]
    o_ref[0, 5 * nh:6 * nh] = totb[0:nh] - cb[0:nh] + ig[nh:]
    lane = lax.broadcasted_iota(jnp.int32, (nh * nc, L), 1)
    pf = (ig[0:nh] - cf[0:nh]).reshape(nh * nc, L)
    pb = (ig[nh:] - cb[0:nh]).reshape(nh * nc, L)
    sh = 1
    while sh < L:
        pf = jnp.where(lane >= sh, jnp.maximum(pf, pltpu.roll(pf, sh, 1)), pf)
        pb = jnp.where(lane < L - sh, jnp.maximum(pb, pltpu.roll(pb, L - sh, 1)), pb)
        sh *= 2
    o_ref[0, 6 * nh:7 * nh] = cf[0:nh] + pf.reshape(nh, nc, L)
    o_ref[0, 7 * nh:8 * nh] = cb[0:nh] + pb.reshape(nh, nc, L)
    H = SSM_HEADS
    af, ab, dtf, dtb = cf[nh:], cb[nh:], dt[0:H], dt[H:]
    s0 = PREP_SSM0
    o_ref[0, s0:s0 + H] = af
    o_ref[0, s0 + H:s0 + 2 * H] = ab
    o_ref[0, s0 + 2 * H:s0 + 3 * H] = af - jnp.log(dtf)
    o_ref[0, s0 + 3 * H:s0 + 4 * H] = ab - jnp.log(dtb)
    o_ref[0, s0 + 4 * H:s0 + 5 * H] = jnp.exp(af)
    o_ref[0, s0 + 5 * H:s0 + 6 * H] = jnp.exp(ab)
    o_ref[0, s0 + 6 * H:s0 + 7 * H] = dtf * jnp.exp(totf[nh:] - af)
    o_ref[0, s0 + 7 * H:s0 + 8 * H] = dtb * jnp.exp(totb[nh:] - ab)
    o_ref[0, s0 + 8 * H:s0 + 9 * H] = jnp.exp(totf[nh:])
    o_ref[0, s0 + 9 * H:s0 + 10 * H] = jnp.exp(totb[nh:])


def _gate_prep(g_rows, bias_rows, aneg_rows):
    B, R, nc, L = g_rows.shape
    return pl.pallas_call(
        functools.partial(_gate_kernel, nc=nc, L=L),
        grid=(B,),
        in_specs=[
            pl.BlockSpec((1, R, nc, L), lambda b: (b, 0, 0, 0)),
            pl.BlockSpec((R, 1, L), lambda b: (0, 0, 0)),
            pl.BlockSpec((2 * SSM_HEADS, 1, L), lambda b: (0, 0, 0)),
        ],
        out_specs=pl.BlockSpec((1, N_PREP_ROWS, nc, L), lambda b: (b, 0, 0, 0)),
        out_shape=jax.ShapeDtypeStruct((B, N_PREP_ROWS, nc, L), F32),
        compiler_params=_cparams(("parallel",)),
        name="gate_prep",
    )(g_rows, bias_rows, aneg_rows)


def _mlstm_kernel(q_ref, kt_ref, v_ref, o_ref, mg_ref, mc_ref, nw_ref, y_ref, park_scr, c_scr, n_scr, m_scr,
                  *, nc, L, nh):
    row_i = lax.broadcasted_iota(jnp.int32, (L, L), 0)
    col_i = lax.broadcasted_iota(jnp.int32, (L, L), 1)
    ones = jnp.ones((L, L), BF16)

    c_scr[...] = jnp.zeros_like(c_scr)
    n_scr[...] = jnp.zeros_like(n_scr)
    m_scr[...] = jnp.full_like(m_scr, NEG_INF)

    def chunk(c, d, hh):
        c_ref, n_ref, m_ref = c_scr.at[2 * hh + d], n_scr.at[2 * hh + d], m_scr.at[2 * hh + d]
        c_prev, n_prev, m_prev = c_ref[...], n_ref[...], m_ref[...]
        sl = pl.ds(pl.multiple_of(c * L, L), L)
        q = q_ref[0, sl, hh * ML_DQK:(hh + 1) * ML_DQK]
        kt = kt_ref[0, c, hh]
        vo = jnp.concatenate([v_ref[0, sl, hh * ML_DV:(hh + 1) * ML_DV], ones], axis=1)
        g8 = mg_ref[0, hh, c]
        ct = mc_ref[0, hh, sl, :]
        b_r, i_r, a_r = g8[2 * d:2 * d + 1], g8[2 * d + 1:2 * d + 2], g8[4 + d:5 + d]
        b_c = jnp.broadcast_to(ct[:, 2 * d:2 * d + 1], (L, L))
        mi_c = jnp.broadcast_to(ct[:, 6 + d:7 + d], (L, L))
        if d == 0:
            gtot = b_r[:, L - 1:L]
            mask = col_i <= row_i
        else:
            gtot = b_r[:, 0:1]
            mask = col_i >= row_i
        m_loc = jnp.max(a_r, axis=1, keepdims=True)
        kwt = (kt.astype(F32) * jnp.exp(a_r - m_loc)).astype(BF16)
        uu = _dot(kwt, vo)
        qq = _dot(q, jnp.concatenate([kt, c_prev.astype(BF16), n_prev.astype(BF16)], axis=1))
        s, q_c, q_n = qq[:, 0:L], qq[:, L:L + ML_DV], qq[:, L + ML_DV:]
        dmat = jnp.where(mask, b_c - b_r + i_r, NEG_INF)
        m_inter = b_c + m_prev
        m_t = jnp.maximum(m_inter, mi_c)
        s_mat = s * jnp.exp(dmat - m_t)
        inter_w = jnp.exp(m_inter - m_t)
        ss = _dot(s_mat.astype(BF16), vo)
        den = ss[:, ML_DV:] + inter_w * q_n
        r = 1.0 / jnp.maximum(jnp.abs(den), jnp.exp(-m_t))
        num = ss[:, 0:ML_DV] + jnp.concatenate([inter_w, inter_w], axis=1) * q_c
        h = num * jnp.concatenate([r, r], axis=1)
        m_new = jnp.maximum(gtot + m_prev, m_loc)
        decay = jnp.exp(gtot + m_prev - m_new)
        sc = jnp.exp(m_loc - m_new)
        c_ref[...] = decay * c_prev + sc * uu[:, 0:ML_DV]
        n_ref[...] = decay * n_prev + sc * uu[:, ML_DV:]
        m_ref[...] = m_new
        return sl, h

    def finish(sl, hh, h):
        cols = slice(hh * ML_DV, (hh + 1) * ML_DV)
        mu = jnp.mean(h, axis=1, keepdims=True)
        d = h - mu
        var = jnp.mean(d * d, axis=1, keepdims=True)
        y = d * lax.rsqrt(var + LN_EPS) * nw_ref[:, cols]
        y = y * _sigmoid(o_ref[0, sl, cols].astype(F32))
        y_ref[0, sl, cols] = y.astype(y_ref.dtype)

    def first_half(i, carry):
        for hh in range(nh):
            for d, c in ((0, i), (1, nc - 1 - i)):
                sl, h = chunk(c, d, hh)
                park_scr[hh, sl, :] = h
        return carry

    def second_half(i, carry):
        for hh in range(nh):
            for d, c in ((0, i), (1, nc - 1 - i)):
                sl, h = chunk(c, d, hh)
                finish(sl, hh, h + park_scr[hh, sl, :])
        return carry

    half = nc // 2
    lax.fori_loop(0, half, first_half, 0, unroll=8 // nh)
    lax.fori_loop(half, nc, second_half, 0, unroll=8 // nh)


def _mlstm(pa3, kt5, mg, mc, norm_w, nh=1):
    B, S, _ = pa3.shape
    nc = S // CHUNK
    H = ML_HEADS
    q_blk, v_blk, o_blk = PA_Q0 // (nh * ML_DQK), PA_V0 // (nh * ML_DV), PA_O0 // (nh * ML_DV)
    return pl.pallas_call(
        functools.partial(_mlstm_kernel, nc=nc, L=CHUNK, nh=nh),
        grid=(B, H // nh),
        in_specs=[
            pl.BlockSpec((1, S, nh * ML_DQK), lambda b, h: (b, 0, q_blk + h)),
            pl.BlockSpec((1, nc, nh, ML_DQK, CHUNK), lambda b, h: (b, 0, h, 0, 0)),
            pl.BlockSpec((1, S, nh * ML_DV), lambda b, h: (b, 0, v_blk + h)),
            pl.BlockSpec((1, S, nh * ML_DV), lambda b, h: (b, 0, o_blk + h)),
            pl.BlockSpec((1, nh, nc, 8, CHUNK), lambda b, h: (b, h, 0, 0, 0)),
            pl.BlockSpec((1, nh, S, 8), lambda b, h: (b, h, 0, 0)),
            pl.BlockSpec((1, nh * ML_DV), lambda b, h: (0, h)),
        ],
        out_specs=pl.BlockSpec((1, S, nh * ML_DV), lambda b, h: (b, 0, h)),
        out_shape=jax.ShapeDtypeStruct((B, S, ML_WIDTH), BF16),
        scratch_shapes=[
            pltpu.VMEM((nh, S, ML_DV), F32),
            pltpu.VMEM((2 * nh, ML_DQK, ML_DV), F32),
            pltpu.VMEM((2 * nh, ML_DQK, CHUNK), F32),
            pltpu.VMEM((2 * nh, 1, 1), F32),
        ],
        compiler_params=_cparams(("parallel", "parallel")),
        name="mlstm",
    )(pa3, kt5, pa3, pa3, mg, mc, norm_w)


CONV_EDGE = 16


def _conv_shift_matrix(L):
    pad, E = CONV_K // 2, CONV_EDGE
    t = jnp.arange(L, dtype=jnp.int32)[:, None]
    blocks = []
    u = jnp.arange(L, dtype=jnp.int32)[None, :]
    for j in range(CONV_K):
        blocks.append(u == t + j - pad)
    v = jnp.arange(E, dtype=jnp.int32)[None, :]
    for j in range(pad):
        blocks.append(v - E == t + j - pad)
    for j in range(pad + 1, CONV_K):
        blocks.append(v + L == t + j - pad)
    return jnp.concatenate(blocks, axis=1).astype(BF16)


def _conv_kernel(x_ref, top_ref, bot_ref, w_ref, b_ref, sh_ref, o_ref, ext_scr, *, Lc, L):
    pad, E = CONV_K // 2, CONV_EDGE
    i = pl.program_id(1)
    first, last = i == 0, i == pl.num_programs(1) - 1
    ext_scr[0:E, :] = jnp.where(first, jnp.zeros_like(top_ref[0]), top_ref[0])
    ext_scr[E:E + Lc, :] = x_ref[0]
    ext_scr[E + Lc:2 * E + Lc, :] = jnp.where(last, jnp.zeros_like(bot_ref[0]), bot_ref[0])
    taps = [w_ref[j:j + 1, :].astype(BF16) for j in range(CONV_K)]
    scaled = lambda rows, j: rows * taps[j]
    for r0 in range(0, Lc, L):
        cen = ext_scr[E + r0:E + r0 + L, :]
        top = ext_scr[r0:r0 + E, :]
        bot = ext_scr[E + r0 + L:2 * E + r0 + L, :]
        pieces = [scaled(cen, j) for j in range(CONV_K)]
        pieces += [scaled(top, j) for j in range(pad)] + [scaled(bot, j) for j in range(pad + 1, CONV_K)]
        acc = _dot(sh_ref[...], jnp.concatenate(pieces, axis=0)) + b_ref[...]
        o_ref[0, r0:r0 + L, :] = (acc * _sigmoid(acc)).astype(o_ref.dtype)


def _conv(pa3, conv_w8, conv_b, Lc):
    B, S, _ = pa3.shape
    C = SSM_CONV_DIM
    xbc_blk = PA_XBC0 // C
    L = CHUNK
    sh = _conv_shift_matrix(L)
    epb = Lc // CONV_EDGE
    n_edge = S // CONV_EDGE
    return pl.pallas_call(
        functools.partial(_conv_kernel, Lc=Lc, L=L),
        grid=(B, S // Lc),
        in_specs=[
            pl.BlockSpec((1, Lc, C), lambda b, i: (b, i, xbc_blk)),
            pl.BlockSpec((1, CONV_EDGE, C), lambda b, i: (b, jnp.maximum(i * epb - 1, 0), xbc_blk)),
            pl.BlockSpec((1, CONV_EDGE, C), lambda b, i: (b, jnp.minimum((i + 1) * epb, n_edge - 1), xbc_blk)),
            pl.BlockSpec((8, C), lambda b, i: (0, 0)),
            pl.BlockSpec((1, C), lambda b, i: (0, 0)),
            pl.BlockSpec(sh.shape, lambda b, i: (0, 0)),
        ],
        out_specs=pl.BlockSpec((1, Lc, C), lambda b, i: (b, i, 0)),
        out_shape=jax.ShapeDtypeStruct((B, S, C), BF16),
        scratch_shapes=[pltpu.VMEM((Lc + 2 * CONV_EDGE, C), BF16)],
        compiler_params=_cparams(("parallel", "parallel")),
        name="ssm_conv",
    )(pa3, pa3, pa3, conv_w8, conv_b, sh)


def _expansion_matrix(src_cols, width):
    src = jnp.repeat(jnp.asarray(src_cols, jnp.int32), width)
    e = (jnp.arange(128, dtype=jnp.int32)[:, None] == src[None, :]).astype(BF16)
    return jnp.concatenate([e, e], axis=0)


def _split2(t):
    hi = t.astype(BF16)
    return jnp.concatenate([hi, (t - hi.astype(F32)).astype(BF16)], axis=1)


def _ssd_bwd_kernel(x_ref, b_ref, tb_ref, dec_ref, ew_ref, hb_ref, h_scr, *, L):
    G, hg, P, N = SSM_GROUPS, SSM_HEADS // SSM_GROUPS, SSM_HEADDIM, SSM_STATE
    GW = hg * P

    @pl.when(pl.program_id(1) == 0)
    def _():
        h_scr[...] = jnp.zeros_like(h_scr)

    for s in reversed(range(x_ref.shape[1] // L)):
        rows = slice(s * L, (s + 1) * L)
        w_all = _dot(_split2(tb_ref[0, s].T), ew_ref[...])
        for g in range(G):
            w = w_all[:, g * GW:(g + 1) * GW]
            xs = (x_ref[0, rows, g * GW:(g + 1) * GW].astype(F32) * w).astype(BF16)
            st = _dot_tn(b_ref[0, rows, g * N:(g + 1) * N], xs)
            h_old = h_scr[g]
            hb_ref[0, s, g] = h_old.astype(hb_ref.dtype)
            h_scr[g] = dec_ref[0, s, 1:2, g * GW:(g + 1) * GW] * h_old + st


def _ssd_bwd(xc, tb, decs, cpb):
    B, S, _ = xc.shape
    nc = S // CHUNK
    nblk = nc // cpb
    R = cpb * CHUNK
    G, N, W = SSM_GROUPS, SSM_STATE, SSM_WIDTH
    return pl.pallas_call(
        functools.partial(_ssd_bwd_kernel, L=CHUNK),
        grid=(B, nblk),
        in_specs=[
            pl.BlockSpec((1, R, W), lambda b, c: (b, nblk - 1 - c, 0)),
            pl.BlockSpec((1, R, G * N), lambda b, c: (b, nblk - 1 - c, W // (G * N))),
            pl.BlockSpec((1, cpb, 128, CHUNK), lambda b, c: (b, nblk - 1 - c, 0, 0)),
            pl.BlockSpec((1, cpb, 2, W), lambda b, c: (b, nblk - 1 - c, 0, 0)),
            pl.BlockSpec((256, W), lambda b, c: (0, 0)),
        ],
        out_specs=pl.BlockSpec((1, cpb, G, N, W // G), lambda b, c: (b, nblk - 1 - c, 0, 0, 0)),
        out_shape=jax.ShapeDtypeStruct((B, nc, G, N, W // G), BF16),
        scratch_shapes=[pltpu.VMEM((G, N, W // G), F32)],
        compiler_params=_cparams(("parallel", "arbitrary")),
        name="ssd_bwd_states",
    )(xc, xc, tb, decs, _expansion_matrix([5 * SSM_HEADS + h for h in range(SSM_HEADS)], SSM_HEADDIM))


def _ssd_main_kernel(x_ref, b_ref, c_ref, z_ref, tb_ref, dec_ref, hb_ref, dsk_ref, nw_ref, ed_ref, ee_ref,
                     y_ref, h_scr, *, L):
    G, hg, P, N = SSM_GROUPS, SSM_HEADS // SSM_GROUPS, SSM_HEADDIM, SSM_STATE
    H = SSM_HEADS
    GW = hg * P

    @pl.when(pl.program_id(1) == 0)
    def _():
        h_scr[...] = jnp.zeros_like(h_scr)

    row_i = lax.broadcasted_iota(jnp.int32, (L, L), 0)
    col_i = lax.broadcasted_iota(jnp.int32, (L, L), 1)
    lower = col_i <= row_i
    upper = col_i >= row_i
    lane_g = lax.broadcasted_iota(jnp.int32, (L, GW), 1)

    for s, g in [(s, g) for s in range(x_ref.shape[1] // L) for g in range(G)]:
        rows = slice(s * L, (s + 1) * L)
        tb = tb_ref[0, s]
        sc2 = _split2(tb.T)
        xg = x_ref[0, rows, g * GW:(g + 1) * GW]
        bg = b_ref[0, rows, g * N:(g + 1) * N]
        cg = c_ref[0, rows, g * N:(g + 1) * N]
        cb = _dot_nt(cg, bg)
        acol = _dot(sc2, ed_ref[g])
        ecol = _dot(sc2, ee_ref[g])
        ms, xms = [], []
        for j in range(hg):
            hd = g * hg + j
            rf_r, rb_r = tb[6 * H + hd:6 * H + hd + 1], tb[7 * H + hd:7 * H + hd + 1]
            af_c, ab_c = acol[:, j * L:(j + 1) * L], acol[:, (hg + j) * L:(hg + j + 1) * L]
            fm = jnp.exp(jnp.where(lower, af_c - rf_r, NEG_INF))
            bm = jnp.exp(jnp.where(upper, ab_c - rb_r, NEG_INF))
            ms.append((cb * (fm + bm)).astype(BF16))
            xms.append(jnp.where((lane_g >= j * P) & (lane_g < (j + 1) * P), xg, jnp.zeros_like(xg)))
        efc, ebc, wfc = ecol[:, 0:GW], ecol[:, GW:2 * GW], ecol[:, 2 * GW:]
        y = _dot(jnp.concatenate(ms, axis=1), jnp.concatenate(xms, axis=0))
        h_old = h_scr[g]
        y = y + efc * _dot(cg, h_old.astype(BF16))
        y = y + ebc * _dot(cg, hb_ref[0, s, g])
        xgf = xg.astype(F32)
        y = y + dsk_ref[:, g * GW:(g + 1) * GW] * xgf
        xs = (xgf * wfc).astype(BF16)
        h_scr[g] = dec_ref[0, s, 0:1, g * GW:(g + 1) * GW] * h_old + _dot_tn(bg, xs)
        zg = z_ref[0, rows, g * GW:(g + 1) * GW].astype(F32)
        y = y * (zg * _sigmoid(zg))
        y = y * lax.rsqrt(jnp.mean(y * y, axis=1, keepdims=True) + RMS_EPS)
        y_ref[0, rows, g * GW:(g + 1) * GW] = (y * nw_ref[:, g * GW:(g + 1) * GW]).astype(y_ref.dtype)


def _ssd_main(xc, pa3, tb, decs, hb, dskip_row, norm_w, cpb):
    B, S, _ = xc.shape
    nc = S // CHUNK
    R = cpb * CHUNK
    G, N, W = SSM_GROUPS, SSM_STATE, SSM_WIDTH
    H, hg, P = SSM_HEADS, SSM_HEADS // SSM_GROUPS, SSM_HEADDIM
    z_blk = PA_Z0 // W
    e_dec = jnp.stack([_expansion_matrix([q * H + g * hg + j for q in (0, 1) for j in range(hg)], CHUNK)
                       for g in range(G)])
    e_ew = jnp.stack([_expansion_matrix([q * H + g * hg + j for q in (2, 3, 4) for j in range(hg)], P)
                      for g in range(G)])
    return pl.pallas_call(
        functools.partial(_ssd_main_kernel, L=CHUNK),
        grid=(B, nc // cpb),
        in_specs=[
            pl.BlockSpec((1, R, W), lambda b, c: (b, c, 0)),
            pl.BlockSpec((1, R, G * N), lambda b, c: (b, c, W // (G * N))),
            pl.BlockSpec((1, R, G * N), lambda b, c: (b, c, W // (G * N) + 1)),
            pl.BlockSpec((1, R, W), lambda b, c: (b, c, z_blk)),
            pl.BlockSpec((1, cpb, 128, CHUNK), lambda b, c: (b, c, 0, 0)),
            pl.BlockSpec((1, cpb, 2, W), lambda b, c: (b, c, 0, 0)),
            pl.BlockSpec((1, cpb, G, N, W // G), lambda b, c: (b, c, 0, 0, 0)),
            pl.BlockSpec((1, W), lambda b, c: (0, 0)),
            pl.BlockSpec((1, W), lambda b, c: (0, 0)),
            pl.BlockSpec((G, 256, 2 * hg * CHUNK), lambda b, c: (0, 0, 0)),
            pl.BlockSpec((G, 256, 3 * W // G), lambda b, c: (0, 0, 0)),
        ],
        out_specs=pl.BlockSpec((1, R, W), lambda b, c: (b, c, 0)),
        out_shape=jax.ShapeDtypeStruct((B, S, W), BF16),
        scratch_shapes=[pltpu.VMEM((G, N, W // G), F32)],
        compiler_params=_cparams(("parallel", "arbitrary")),
        name="ssd_main",
    )(xc, xc, xc, pa3, tb, decs, hb, dskip_row, norm_w, e_dec, e_ew)


def _pack_bf16_pairs(v):
    n = v.shape[1] // 2
    hi = lax.bitcast_convert_type(v[:, :n].astype(BF16).astype(F32), jnp.uint32)
    lo = lax.bitcast_convert_type(v[:, n:].astype(BF16).astype(F32), jnp.uint32)
    return (hi & jnp.uint32(0xFFFF0000)) | (lo >> 16)


def _unpack_bf16_pairs(u):
    hi = lax.bitcast_convert_type(u & jnp.uint32(0xFFFF0000), F32)
    lo = lax.bitcast_convert_type(u << 16, F32)
    return hi, lo


def _layer_norm(v, g, b):
    mu = jnp.mean(v, axis=1, keepdims=True)
    d = v - mu
    var = jnp.mean(d * d, axis=1, keepdims=True)
    return d * lax.rsqrt(var + LN_EPS) * g + b


def _outproj_kernel(yml_ref, yssm_ref, x_ref, wo_ref, g_ref, b_ref, wr_ref, h_ref, hpa_ref, hpb_ref, sc_ref):
    mix = _dot(yml_ref[...], wo_ref[0:ML_WIDTH, :]) + _dot(yssm_ref[...], wo_ref[ML_WIDTH:, :])
    h = _layer_norm(ALPHA * x_ref[...] + mix, g_ref[...], b_ref[...])
    h_ref[...] = h
    hp = _pack_bf16_pairs(h)
    q = hp.shape[1] // 2
    hpa_ref[...] = hp[:, :q]
    hpb_ref[...] = hp[:, q:]
    sc_ref[...] = _sigmoid(_dot(h.astype(BF16), wr_ref[...]))


def _outproj(yml, yssm, x2d, w_out, ln_g, ln_b, w_router, tm=1024):
    T, D = x2d.shape
    tm = min(tm, T)
    row = lambda i: (i, 0)
    fixed = lambda i: (0, 0)
    return pl.pallas_call(
        _outproj_kernel,
        grid=(T // tm,),
        in_specs=[
            pl.BlockSpec((tm, ML_WIDTH), row),
            pl.BlockSpec((tm, SSM_WIDTH), row),
            pl.BlockSpec((tm, D), row),
            pl.BlockSpec((ML_WIDTH + SSM_WIDTH, D), fixed),
            pl.BlockSpec((1, D), fixed),
            pl.BlockSpec((1, D), fixed),
            pl.BlockSpec((D, 128), fixed),
        ],
        out_specs=[pl.BlockSpec((tm, D), row), pl.BlockSpec((tm, D // 4), row), pl.BlockSpec((tm, D // 4), row),
                   pl.BlockSpec((tm, 128), row)],
        out_shape=[
            jax.ShapeDtypeStruct((T, D), F32),
            jax.ShapeDtypeStruct((T, D // 4), jnp.uint32),
            jax.ShapeDtypeStruct((T, D // 4), jnp.uint32),
            jax.ShapeDtypeStruct((T, 128), F32),
        ],
        compiler_params=_cparams(("parallel",)),
        name="out_proj_ln1_router",
    )(yml, yssm, x2d, w_out, ln_g, ln_b, w_router)


def _route_kernel(sc_ref, rb_ref, idx_ref, w_ref, rank_ref, cnt_ref, carry_scr, *, Tt):
    E, NG = N_EXPERTS, N_EXPERT_GROUPS
    EG = E // NG

    @pl.when(pl.program_id(0) == 0)
    def _():
        carry_scr[...] = jnp.zeros_like(carry_scr)

    s = sc_ref[...].T[0:E]
    s3 = s.reshape(EG, NG, Tt)
    ch3 = (s + rb_ref[:, 0:1]).reshape(EG, NG, Tt)
    e_in = lax.broadcasted_iota(jnp.int32, (EG, NG, Tt), 0)
    g_in = lax.broadcasted_iota(jnp.int32, (EG, NG, Tt), 1)
    f_i = g_in * EG + e_in
    m1 = jnp.max(ch3, axis=0, keepdims=True)
    i1 = jnp.min(jnp.where(ch3 == m1, e_in, EG), axis=0, keepdims=True)
    m2 = jnp.max(jnp.where(e_in == i1, NEG_INF, ch3), axis=0, keepdims=True)
    gs = m1 + m2
    g_i = lax.broadcasted_iota(jnp.int32, (1, NG, Tt), 1)
    gsel = jnp.zeros((1, NG, Tt), jnp.bool_)
    cur = gs
    for _ in range(TOPK_GROUPS):
        mx = jnp.max(cur, axis=1, keepdims=True)
        pick = jnp.min(jnp.where(cur == mx, g_i, NG), axis=1, keepdims=True)
        hit = g_i == pick
        gsel = gsel | hit
        cur = jnp.where(hit, NEG_INF, cur)
    cur = jnp.where(gsel, ch3, NEG_INF)
    selmask = jnp.zeros((EG, NG, Tt), jnp.bool_)
    idxs, ws = [], []
    for _ in range(TOP_K):
        mx = jnp.max(jnp.max(cur, axis=0, keepdims=True), axis=1, keepdims=True)
        cand = jnp.where(cur == mx, f_i, E)
        pick = jnp.min(jnp.min(cand, axis=0, keepdims=True), axis=1, keepdims=True)
        hit = f_i == pick
        wk = jnp.sum(jnp.sum(jnp.where(hit, s3, 0.0), axis=0, keepdims=True), axis=1, keepdims=True)
        idxs.append(pick.reshape(1, Tt))
        ws.append(wk.reshape(1, Tt))
        selmask = selmask | hit
        cur = jnp.where(hit, NEG_INF, cur)
    wsum = ws[0]
    for k in range(1, TOP_K):
        wsum = wsum + ws[k]
    r = lax.broadcasted_iota(jnp.int32, (Tt, Tt), 0)
    c = lax.broadcasted_iota(jnp.int32, (Tt, Tt), 1)
    ustrict = (r < c).astype(BF16)
    self32 = jnp.where(selmask, 1.0, 0.0).reshape(E, Tt)
    excl = _dot(self32.astype(BF16), ustrict) + carry_scr[:, 0:1]
    excl3 = excl.reshape(EG, NG, Tt)
    for k in range(TOP_K):
        hit = f_i == idxs[k].reshape(1, 1, Tt)
        rk = jnp.sum(jnp.sum(jnp.where(hit, excl3, 0.0), axis=0, keepdims=True), axis=1, keepdims=True)
        idx_ref[k:k + 1, :] = idxs[k]
        w_ref[k:k + 1, :] = ws[k] / wsum * ROUTED_SCALE
        rank_ref[k:k + 1, :] = rk.reshape(1, Tt).astype(jnp.int32)
    carry_scr[...] = carry_scr[...] + jnp.sum(self32, axis=1, keepdims=True)
    cnt_ref[...] = carry_scr[...]


def _route(scores, rbias, Tt=512):
    T = scores.shape[0]
    kt = lambda i: (0, i)
    return pl.pallas_call(
        functools.partial(_route_kernel, Tt=Tt),
        grid=(T // Tt,),
        in_specs=[pl.BlockSpec((Tt, 128), lambda i: (i, 0)), pl.BlockSpec((N_EXPERTS, 128), lambda i: (0, 0))],
        out_specs=[
            pl.BlockSpec((TOP_K, Tt), kt),
            pl.BlockSpec((TOP_K, Tt), kt),
            pl.BlockSpec((TOP_K, Tt), kt),
            pl.BlockSpec((N_EXPERTS, 128), lambda i: (0, 0)),
        ],
        out_shape=[
            jax.ShapeDtypeStruct((TOP_K, T), jnp.int32),
            jax.ShapeDtypeStruct((TOP_K, T), F32),
            jax.ShapeDtypeStruct((TOP_K, T), jnp.int32),
            jax.ShapeDtypeStruct((N_EXPERTS, 128), F32),
        ],
        scratch_shapes=[pltpu.VMEM((N_EXPERTS, 128), F32)],
        compiler_params=_cparams(("arbitrary",)),
        name="route_topk",
    )(scores, rbias)


def _dest_kernel(ps_ref, idx_ref, rank_ref, o_ref):
    idx = idx_ref[...]
    start = jnp.zeros(idx.shape, jnp.int32)
    for e in range(N_EXPERTS):
        start = jnp.where(idx == e, ps_ref[e], start)
    o_ref[...] = start + rank_ref[...]


def _dest(pstarts, idx, rank, Tt=2048):
    K, T = idx.shape
    Tt = min(Tt, T)
    blk = pl.BlockSpec((K, Tt), lambda i, ps: (0, i))
    return pl.pallas_call(
        _dest_kernel,
        grid_spec=pltpu.PrefetchScalarGridSpec(num_scalar_prefetch=1, grid=(T // Tt,), in_specs=[blk, blk],
                                               out_specs=blk),
        out_shape=jax.ShapeDtypeStruct((K, T), jnp.int32),
        compiler_params=_cparams(("parallel",)),
        name="dispatch_slots",
    )(pstarts, idx, rank)


def _sc_mesh():
    return plsc.VectorSubcoreMesh(core_axis_name="c", subcore_axis_name="s")


def _sc_scatter_rows(src, idx, n_out, win=SC_WINDOW):
    K, T = idx.shape
    W = src.shape[1]

    @functools.partial(pl.kernel, out_type=jax.ShapeDtypeStruct((n_out, W), src.dtype), mesh=_sc_mesh(),
                       scratch_types=[], name="sc_scatter_rows")
    def k(x_hbm, i_hbm, o_hbm):
        def body(x_vmem, i_vmem):
            for j in range(K):
                pltpu.sync_copy(x_vmem, o_hbm.at[i_vmem.at[j]])

        pltpu.emit_pipeline(
            body,
            grid=(T // win,),
            in_specs=[pl.BlockSpec((win, W), lambda i: (i, 0)), pl.BlockSpec((K, win), lambda i: (0, i))],
            out_specs=[],
            core_axis_name=("c", "s"),
            dimension_semantics=(pltpu.PARALLEL,),
        )(x_hbm, i_hbm)

    return k(src, idx)


def _sc_gather_rows(table, idx, win=SC_WINDOW):
    n = idx.shape[0]
    W = table.shape[1]

    @functools.partial(pl.kernel, out_type=jax.ShapeDtypeStruct((n, W), table.dtype), mesh=_sc_mesh(),
                       scratch_types=[], name="sc_gather_rows")
    def k(t_hbm, i_hbm, o_hbm):
        def body(i_vmem, o_vmem):
            pltpu.sync_copy(t_hbm.at[i_vmem.at[0]], o_vmem)

        pltpu.emit_pipeline(
            body,
            grid=(n // win,),
            in_specs=[pl.BlockSpec((1, win), lambda i: (0, i))],
            out_specs=[pl.BlockSpec((win, W), lambda i: (i, 0))],
            core_axis_name=("c", "s"),
            dimension_semantics=(pltpu.PARALLEL,),
        )(i_hbm, o_hbm)

    return k(table, idx.reshape(1, n))


def _expert_kernel(be_ref, nu_ref, xa_ref, xb_ref, wg_ref, wu_ref, wd_ref, oa_ref, ob_ref, wg_s, wu_s, wd_s):
    i = pl.program_id(0)
    active = i < nu_ref[0]

    @pl.when(active & ((i == 0) | (be_ref[i] != be_ref[jnp.maximum(i - 1, 0)])))
    def _():
        wg_s[...] = wg_ref[0].astype(BF16)
        wu_s[...] = wu_ref[0].astype(BF16)
        wd_s[...] = wd_ref[0].astype(BF16)

    @pl.when(active)
    def _():
        hi, lo = _unpack_bf16_pairs(jnp.concatenate([xa_ref[...], xb_ref[...]], axis=1))
        x = jnp.concatenate([hi.astype(BF16), lo.astype(BF16)], axis=1)
        gt = _dot(x, wg_s[...])
        up = _dot(x, wu_s[...])
        hmid = (gt * _sigmoid(gt) * up).astype(BF16)
        packed = _pack_bf16_pairs(_dot(hmid, wd_s[...]))
        q = packed.shape[1] // 2
        oa_ref[...] = packed[:, :q]
        ob_ref[...] = packed[:, q:]

    @pl.when(jnp.logical_not(active))
    def _():
        oa_ref[...] = jnp.zeros_like(oa_ref)
        ob_ref[...] = jnp.zeros_like(ob_ref)


def _experts(xs_a, xs_b, block_expert, n_used, w_gate, w_up, w_down, blk=MOE_BLOCK):
    P, quarter = xs_a.shape
    nb = P // blk
    D, F = D_MODEL, D_EXPERT
    cur = lambda i, be, nu: jnp.minimum(i, nu[0] - 1)
    return pl.pallas_call(
        _expert_kernel,
        grid_spec=pltpu.PrefetchScalarGridSpec(
            num_scalar_prefetch=2,
            grid=(nb,),
            in_specs=[
                pl.BlockSpec((blk, quarter), lambda i, be, nu: (cur(i, be, nu), 0)),
                pl.BlockSpec((blk, quarter), lambda i, be, nu: (cur(i, be, nu), 0)),
                pl.BlockSpec((1, D, F), lambda i, be, nu: (be[cur(i, be, nu)], 0, 0)),
                pl.BlockSpec((1, D, F), lambda i, be, nu: (be[cur(i, be, nu)], 0, 0)),
                pl.BlockSpec((1, F, D), lambda i, be, nu: (be[cur(i, be, nu)], 0, 0)),
            ],
            out_specs=[pl.BlockSpec((blk, quarter), lambda i, be, nu: (i, 0)),
                       pl.BlockSpec((blk, quarter), lambda i, be, nu: (i, 0))],
            scratch_shapes=[pltpu.VMEM((D, F), BF16), pltpu.VMEM((D, F), BF16), pltpu.VMEM((F, D), BF16)],
        ),
        out_shape=[jax.ShapeDtypeStruct((P, quarter), jnp.uint32), jax.ShapeDtypeStruct((P, quarter), jnp.uint32)],
        compiler_params=_cparams(("arbitrary",)),
        name="expert_mlp",
    )(block_expert, n_used, xs_a, xs_b, w_gate, w_up, w_down)


def _shared_kernel(h_ref, wsg_ref, wsu_ref, wsd_ref, o_ref):
    h = h_ref[...]
    hb = h.astype(BF16)
    gt = _dot(hb, wsg_ref[...])
    up = _dot(hb, wsu_ref[...])
    o_ref[...] = ALPHA * h + _dot((gt * _sigmoid(gt) * up).astype(BF16), wsd_ref[...])


def _shared(h, ws_gate, ws_up, ws_down, tm=512):
    T, D = h.shape
    tm = min(tm, T)
    F = D_EXPERT
    fixed = lambda i: (0, 0)
    return pl.pallas_call(
        _shared_kernel,
        grid=(T // tm,),
        in_specs=[pl.BlockSpec((tm, D), lambda i: (i, 0)), pl.BlockSpec((D, F), fixed), pl.BlockSpec((D, F), fixed),
                  pl.BlockSpec((F, D), fixed)],
        out_specs=pl.BlockSpec((tm, D), lambda i: (i, 0)),
        out_shape=jax.ShapeDtypeStruct((T, D), F32),
        compiler_params=_cparams(("parallel",)),
        name="shared_expert_residual",
    )(h, ws_gate, ws_up, ws_down)


def _final_kernel(h_ref, ga_ref, gb_ref, wt_ref, g_ref, b_ref, o_ref):
    acc_hi = None
    for k in range(TOP_K):
        hi, lo = _unpack_bf16_pairs(jnp.concatenate([ga_ref[k], gb_ref[k]], axis=1))
        wk = wt_ref[:, k:k + 1]
        acc_hi = wk * hi if acc_hi is None else acc_hi + wk * hi
        acc_lo = wk * lo if k == 0 else acc_lo + wk * lo
    routed = jnp.concatenate([acc_hi, acc_lo], axis=1)
    o_ref[...] = _layer_norm(h_ref[...] + routed, g_ref[...], b_ref[...])


def _final(h, gk_a, gk_b, wt, ln_g, ln_b, tm=512):
    T, D = h.shape
    tm = min(tm, T)
    fixed = lambda i: (0, 0)
    return pl.pallas_call(
        _final_kernel,
        grid=(T // tm,),
        in_specs=[
            pl.BlockSpec((tm, D), lambda i: (i, 0)),
            pl.BlockSpec((TOP_K, tm, D // 4), lambda i: (0, i, 0)),
            pl.BlockSpec((TOP_K, tm, D // 4), lambda i: (0, i, 0)),
            pl.BlockSpec((tm, TOP_K), lambda i: (i, 0)),
            pl.BlockSpec((1, D), fixed),
            pl.BlockSpec((1, D), fixed),
        ],
        out_specs=pl.BlockSpec((tm, D), lambda i: (i, 0)),
        out_shape=jax.ShapeDtypeStruct((T, D), F32),
        compiler_params=_cparams(("parallel",)),
        name="combine_ln2",
    )(h, gk_a, gk_b, wt, ln_g, ln_b)


def _mixers(x, w_in, ml_i_bias, ml_f_bias, ml_norm_w, conv_w, conv_b, dt_bias, a_log, d_skip, ssm_norm_w):
    B, S, D = x.shape
    T = B * S
    nc = S // CHUNK
    H = ML_HEADS
    k0, v0 = ML_QK, 2 * ML_QK
    g0 = v0 + 2 * ML_WIDTH
    z0 = g0 + 4 * H
    x0 = z0 + SSM_WIDTH
    d0 = x0 + SSM_CONV_DIM
    w_big = jnp.concatenate([w_in[:, x0:d0], w_in[:, v0:g0], w_in[:, z0:x0], w_in[:, :k0]], axis=1).astype(BF16)
    w_small = jnp.concatenate(
        [w_in[:, g0:z0], w_in[:, d0:], jnp.zeros((D, PB_COLS - N_GATE_ROWS), w_in.dtype)], axis=1).astype(BF16)
    w_kt = (w_in[:, k0:v0] * (ML_DQK ** -0.5)).T.astype(BF16)
    pa, pb, kt = _proj(x.reshape(T, D), w_big, w_small, w_kt)
    pa3 = pa.reshape(B, S, PA_COLS)
    kt5 = kt.reshape(B, nc, H, ML_DQK, CHUNK)

    g_rows = pb.reshape(B, S, PB_COLS)[:, :, :N_GATE_ROWS].transpose(0, 2, 1).reshape(B, N_GATE_ROWS, nc, CHUNK)
    bias = jnp.concatenate([ml_i_bias, ml_f_bias, dt_bias]).astype(F32)
    bias_rows = jnp.broadcast_to(bias[:, None, None], (N_GATE_ROWS, 1, CHUNK))
    aneg_rows = jnp.broadcast_to((-jnp.exp(a_log.astype(F32)))[:, None, None], (2 * SSM_HEADS, 1, CHUNK))
    prep = _gate_prep(g_rows, bias_rows, aneg_rows)
    m8 = jnp.stack([prep[:, 0:H], prep[:, 2 * H:3 * H], prep[:, H:2 * H], prep[:, 3 * H:4 * H],
                    prep[:, 4 * H:5 * H], prep[:, 5 * H:6 * H], prep[:, 6 * H:7 * H], prep[:, 7 * H:8 * H]],
                   axis=2)
    mg = m8.transpose(0, 1, 3, 2, 4)
    mc = m8.transpose(0, 1, 3, 4, 2).reshape(B, H, S, 8)
    s0, SH = PREP_SSM0, SSM_HEADS
    tb = jnp.concatenate([prep[:, s0:s0 + 2 * SH], prep[:, s0 + 4 * SH:s0 + 8 * SH],
                          prep[:, s0 + 2 * SH:s0 + 4 * SH]], axis=1).transpose(0, 2, 1, 3)
    decs = prep[:, s0 + 8 * SH:s0 + 10 * SH, :, 0].reshape(B, 2, SH, nc).transpose(0, 3, 1, 2)
    decs = jnp.repeat(decs, SSM_HEADDIM, axis=3)

    y_ml = _mlstm(pa3, kt5, mg, mc, ml_norm_w.astype(F32).reshape(1, ML_WIDTH))

    Lc = next(t for t in (1024, 512, CHUNK) if S % t == 0)
    conv_w8 = jnp.concatenate([conv_w.astype(F32), jnp.zeros((8 - CONV_K, SSM_CONV_DIM), F32)], axis=0)
    xc = _conv(pa3, conv_w8, conv_b.astype(F32).reshape(1, SSM_CONV_DIM), Lc)

    cpb = next(c for c in (4, 2, 1) if nc % c == 0)
    hb = _ssd_bwd(xc, tb, decs, cpb)
    dsk = jnp.repeat(d_skip.astype(F32), SSM_HEADDIM).reshape(1, SSM_WIDTH)
    y_ssm = _ssd_main(xc, pa3, tb, decs, hb, dsk, ssm_norm_w.astype(F32).reshape(1, SSM_WIDTH), cpb)
    return y_ml.reshape(T, ML_WIDTH), y_ssm.reshape(T, SSM_WIDTH)


def _moe_dispatch(idx, rank, counts, T, blk):
    E = N_EXPERTS
    M = T * TOP_K
    nb = (M + E * (blk - 1)) // blk
    pcounts = (counts + blk - 1) // blk * blk
    pends = jnp.cumsum(pcounts)
    pstarts = (pends - pcounts).astype(jnp.int32)
    dest = _dest(pstarts, idx, rank)
    block_start = jnp.arange(nb, dtype=jnp.int32) * blk
    block_expert = jnp.minimum(jnp.sum(pends[None, :] <= block_start[:, None], axis=1), E - 1).astype(jnp.int32)
    n_used = (pends[-1] // blk).astype(jnp.int32).reshape(1)
    return dest, nb * blk, block_expert, n_used


def _layer(x, w_in, ml_i_bias, ml_f_bias, ml_norm_w, conv_w, conv_b, dt_bias, a_log, d_skip, ssm_norm_w,
           w_out, ln1_g, ln1_b, w_router, router_bias, w_exp_gate, w_exp_up, w_exp_down,
           w_sh_gate, w_sh_up, w_sh_down, ln2_g, ln2_b):
    B, S, D = x.shape
    T = B * S
    y_ml, y_ssm = _mixers(x, w_in, ml_i_bias, ml_f_bias, ml_norm_w, conv_w, conv_b, dt_bias, a_log, d_skip,
                          ssm_norm_w)
    EG = N_EXPERTS // N_EXPERT_GROUPS
    pos = jnp.arange(N_EXPERTS, dtype=jnp.int32)
    expert_at = (pos % N_EXPERT_GROUPS) * EG + pos // N_EXPERT_GROUPS
    pos_of = (pos % EG) * N_EXPERT_GROUPS + pos // EG
    w_r = jnp.concatenate([w_router[:, expert_at], jnp.zeros((D, 128 - N_EXPERTS), w_router.dtype)],
                          axis=1).astype(BF16)
    h, hp_a, hp_b, scores = _outproj(y_ml, y_ssm, x.reshape(T, D), w_out.astype(BF16),
                                     ln1_g.astype(F32).reshape(1, D), ln1_b.astype(F32).reshape(1, D), w_r)
    rbias = jnp.broadcast_to(router_bias.astype(F32)[expert_at, None], (N_EXPERTS, 128))
    idx, wgt, rank, cnt = _route(scores, rbias)
    counts = cnt[pos_of, 0].astype(jnp.int32)
    dest, n_slots, block_expert, n_used = _moe_dispatch(idx, rank, counts, T, MOE_BLOCK)
    dest_flat = dest.reshape(-1)
    xs_a = _sc_scatter_rows(hp_a, dest, n_slots)
    xs_b = _sc_scatter_rows(hp_b, dest, n_slots)
    os_a, os_b = _experts(xs_a, xs_b, block_expert, n_used, w_exp_gate, w_exp_up, w_exp_down)
    gk_a = _sc_gather_rows(os_a, dest_flat).reshape(TOP_K, T, D // 4)
    gk_b = _sc_gather_rows(os_b, dest_flat).reshape(TOP_K, T, D // 4)
    base = _shared(h, w_sh_gate.astype(BF16), w_sh_up.astype(BF16), w_sh_down.astype(BF16))
    out = _final(base, gk_a, gk_b, wgt.T, ln2_g.astype(F32).reshape(1, D), ln2_b.astype(F32).reshape(1, D))
    return out.reshape(B, S, D)


def kernel(x, w_in, ml_i_bias, ml_f_bias, ml_norm_w, conv_w, conv_b, dt_bias, a_log, d_skip, ssm_norm_w, w_out,
           ln1_g, ln1_b, w_router, router_bias, w_exp_gate, w_exp_up, w_exp_down, w_sh_gate, w_sh_up, w_sh_down,
           ln2_g, ln2_b):
    for l in range(w_in.shape[0]):
        x = _layer(x, w_in[l], ml_i_bias[l], ml_f_bias[l], ml_norm_w[l], conv_w[l], conv_b[l], dt_bias[l],
                   a_log[l], d_skip[l], ssm_norm_w[l], w_out[l], ln1_g[l], ln1_b[l], w_router[l],
                   router_bias[l], w_exp_gate[l], w_exp_up[l], w_exp_down[l], w_sh_gate[l], w_sh_up[l],
                   w_sh_down[l], ln2_g[l], ln2_b[l])
    return x
```
